```python
import math
import jax, jax.numpy as jnp
from jax import lax
import numpy as np

D_MODEL = 4096
BATCH = 4
SEQ = 2048
DEPTH = 1

HEAD_DIM = 128
N_HEADS_A = D_MODEL // (2 * HEAD_DIM)
N_KV_A = N_HEADS_A // 4
GROUP_A = N_HEADS_A // N_KV_A
N_HEADS_B = D_MODEL // (2 * HEAD_DIM)
CMP_LEN = 32
CMP_STRIDE = 16
CMP_HIDDEN = 256
SEL_LEN = 64
TOP_N = 16
WIN = 512
DILATIONS = ((128, 1), (512, 4), (2048, 16))
BLK = 128
SEL_CHUNK = 32
ROPE_THETA = 10000.0
EPS = 1e-6
FORCE_BONUS = 1e4
NEG = -1e30
C_IN = (2 * N_HEADS_A * HEAD_DIM + 3 * N_HEADS_A + 6 * N_KV_A * HEAD_DIM
        + 4 * N_HEADS_B * HEAD_DIM)

kernel_name = "hymba_nsa_dilated_hybrid_block"


def rms_norm(x, w):
    xf = x.astype(jnp.float32)
    y = xf * lax.rsqrt(jnp.mean(xf * xf, axis=-1, keepdims=True) + EPS)
    return (y * w.astype(jnp.float32)).astype(x.dtype)


def rope(x, pos):
    half = x.shape[-1] // 2
    inv = ROPE_THETA ** (-jnp.arange(half, dtype=jnp.float32) / half)
    ang = pos.astype(jnp.float32)[:, None, :, None] * inv
    cos, sin = jnp.cos(ang), jnp.sin(ang)
    x1 = x[..., :half].astype(jnp.float32)
    x2 = x[..., half:].astype(jnp.float32)
    return jnp.concatenate([x1 * cos - x2 * sin, x2 * cos + x1 * sin], axis=-1).astype(x.dtype)


def to_heads(t, h):
    b, s, _ = t.shape
    return t.reshape(b, s, h, HEAD_DIM).transpose(0, 2, 1, 3)


def band_attention(q, k, v, max_dist):
    n, hk, g, L, dh = q.shape
    nb = -(-L // BLK)
    lp = nb * BLK
    npv = -(-max_dist // BLK)
    kb_len = (npv + 1) * BLK
    qb = jnp.pad(q, ((0, 0), (0, 0), (0, 0), (0, lp - L), (0, 0))).reshape(n, hk, g, nb, BLK, dh)
    pad_kv = ((0, 0), (0, 0), (npv * BLK, lp - L), (0, 0))
    kp = jnp.pad(k, pad_kv)
    vp = jnp.pad(v, pad_kv)

    def blocks(t):
        return jnp.concatenate(
            [t[:, :, o * BLK:(o + nb) * BLK].reshape(n, hk, nb, BLK, dh) for o in range(npv + 1)], axis=3)

    kb, vb = blocks(kp), blocks(vp)
    s = jnp.einsum('nhgiqd,nhikd->nhgiqk', qb, kb).astype(jnp.float32)
    blk_i = jnp.arange(nb)
    qpos = (blk_i[:, None] * BLK + jnp.arange(BLK)[None, :])[:, :, None]
    kpos = ((blk_i - npv)[:, None] * BLK + jnp.arange(kb_len)[None, :])[:, None, :]
    dist = qpos - kpos
    mask = (dist >= 0) & (dist <= max_dist) & (kpos >= 0)
    s = jnp.where(mask, s, NEG)
    m = jnp.max(s, axis=-1)
    p = jnp.exp(s - m[..., None])
    den = jnp.sum(p, axis=-1)
    num = jnp.einsum('nhgiqk,nhikd->nhgiqd', p, vb.astype(jnp.float32))
    num = num.reshape(n, hk, g, lp, dh)[:, :, :, :L]
    m = m.reshape(n, hk, g, lp)[..., :L]
    den = den.reshape(n, hk, g, lp)[..., :L]
    return num, m, den


def compress(k, pe, w1, w2):
    b, hk, s, dh = k.shape
    r = CMP_LEN // CMP_STRIDE
    nchunk = s // CMP_STRIDE
    kc = k.reshape(b, hk, nchunk, CMP_STRIDE, dh)
    n_c = nchunk - r + 1
    blk = jnp.concatenate([kc[:, :, o:o + n_c] for o in range(r)], axis=3) + pe
    flat = blk.reshape(b, hk, n_c, CMP_LEN * dh)
    return jax.nn.gelu(flat @ w1) @ w2


def nsa_mixer(q, k_cmp, v_cmp, k_slc, v_slc, k_win, v_win, gates, pos,
              pe_k, pe_v, wk1, wk2, wv1, wv2):
    b, _, s_len, dh = q.shape
    scale = dh ** -0.5
    t = jnp.arange(s_len)
    q_raw = q.reshape(b, N_KV_A, GROUP_A, s_len, dh) * scale
    q_rot = rope(q, pos).reshape(b, N_KV_A, GROUP_A, s_len, dh) * scale

    kc = compress(k_cmp, pe_k, wk1, wk2)
    vc = compress(v_cmp, pe_v, wv1, wv2)
    n_c = kc.shape[2]
    sc = jnp.einsum('bhgtd,bhcd->bhgtc', q_raw, kc).astype(jnp.float32)
    c_end = jnp.arange(n_c) * CMP_STRIDE + CMP_LEN - 1
    c_valid = c_end[None, :] <= t[:, None]
    p_cmp = jax.nn.softmax(jnp.where(c_valid, sc, NEG), axis=-1) * c_valid.any(-1)[:, None]
    o_cmp = jnp.einsum('bhgtc,bhcd->bhgtd', p_cmp, vc.astype(jnp.float32))

    n_sel = s_len // SEL_LEN
    k_top = min(TOP_N, n_sel)
    ci = jnp.arange(n_c) * CMP_STRIDE
    sj = jnp.arange(n_sel) * SEL_LEN
    overlap = ((ci[:, None] < sj[None, :] + SEL_LEN) & (ci[:, None] + CMP_LEN > sj[None, :])).astype(jnp.float32)
    imp = jnp.einsum('bhgtc,cj->bhtj', p_cmp, overlap)
    cur = t // SEL_LEN
    jj = jnp.arange(n_sel)[None, :]
    forced = (jj == 0) | (jj == cur[:, None]) | (jj == cur[:, None] - 1)
    imp = jnp.where(forced, imp + FORCE_BONUS, imp)
    imp = jnp.where(sj[None, :] <= t[:, None], imp, -jnp.inf)
    _, idx = lax.top_k(imp, k_top)

    ks = rope(k_slc, pos).reshape(b, N_KV_A, n_sel, SEL_LEN, dh)
    vs = v_slc.reshape(b, N_KV_A, n_sel, SEL_LEN, dh)
    bi = jnp.arange(b)[:, None, None, None]
    hi = jnp.arange(N_KV_A)[None, :, None, None]
    nch = s_len // SEL_CHUNK

    def sel_chunk(args):
        qc, ic, tc = args
        kg = ks[bi, hi, ic].reshape(b, N_KV_A, SEL_CHUNK, k_top * SEL_LEN, dh)
        vg = vs[bi, hi, ic].reshape(b, N_KV_A, SEL_CHUNK, k_top * SEL_LEN, dh)
        kpos = (ic[..., None] * SEL_LEN + jnp.arange(SEL_LEN)).reshape(b, N_KV_A, SEL_CHUNK, k_top * SEL_LEN)
        mask = kpos <= tc[None, None, :, None]
        sc_ = jnp.einsum('bhgcd,bhckd->bhgck', qc, kg).astype(jnp.float32)
        p = jax.nn.softmax(jnp.where(mask[:, :, None], sc_, NEG), axis=-1)
        return jnp.einsum('bhgck,bhckd->bhgcd', p, vg.astype(jnp.float32))

    q_chunks = q_rot.reshape(b, N_KV_A, GROUP_A, nch, SEL_CHUNK, dh).transpose(3, 0, 1, 2, 4, 5)
    i_chunks = idx.reshape(b, N_KV_A, nch, SEL_CHUNK, k_top).transpose(2, 0, 1, 3, 4)
    t_chunks = t.reshape(nch, SEL_CHUNK)
    o_slc = lax.map(sel_chunk, (q_chunks, i_chunks, t_chunks))
    o_slc = o_slc.transpose(1, 2, 3, 0, 4, 5).reshape(b, N_KV_A, GROUP_A, s_len, dh)

    num, _, den = band_attention(q_rot, rope(k_win, pos), v_win, WIN - 1)
    o_win = num / den[..., None]

    o = gates[..., 0:1] * o_cmp + gates[..., 1:2] * o_slc + gates[..., 2:3] * o_win
    return o.reshape(b, N_HEADS_A, s_len, dh)


def dilated_mixer(q, k, v, pos):
    b, h, s_len, dh = q.shape
    q = rope(q, pos) * (dh ** -0.5)
    k = rope(k, pos)
    nums, ms, dens = [], [], []
    for window, d in DILATIONS:
        L = s_len // d

        def to_sub(t):
            return t.reshape(b, h, L, d, dh).transpose(0, 3, 1, 2, 4).reshape(b * d, h, L, dh)

        num, m, den = band_attention(to_sub(q)[:, :, None], to_sub(k), to_sub(v), window // d)
        nums.append(num.reshape(b, d, h, L, dh).transpose(0, 2, 3, 1, 4).reshape(b, h, s_len, dh))
        ms.append(m.reshape(b, d, h, L).transpose(0, 2, 3, 1).reshape(b, h, s_len))
        dens.append(den.reshape(b, d, h, L).transpose(0, 2, 3, 1).reshape(b, h, s_len))
    ms = jnp.stack(ms)
    w = jnp.exp(ms - jnp.max(ms, axis=0, keepdims=True))
    numer = jnp.sum(w[..., None] * jnp.stack(nums), axis=0)
    denom = jnp.sum(w * jnp.stack(dens), axis=0)
    return numer / denom[..., None]


def setup_inputs(seed: int = 0) -> dict:
    key = jax.random.key(seed)
    ks = jax.random.split(key, 14)
    f32 = jnp.float32
    x = jax.random.normal(ks[0], (BATCH, SEQ, D_MODEL), f32)
    offset = jax.random.randint(ks[1], (BATCH, 1), 0, 1024, dtype=jnp.int32)
    positions = (offset + jnp.arange(SEQ, dtype=jnp.int32)[None, :]).astype(jnp.int32)
    pre_norm_w = 1.0 + 0.05 * jax.random.normal(ks[2], (DEPTH, D_MODEL), f32)
    post_norm_w = 1.0 + 0.05 * jax.random.normal(ks[3], (DEPTH, D_MODEL), f32)
    w_in = jax.random.normal(ks[4], (DEPTH, D_MODEL, C_IN), f32) * D_MODEL ** -0.5
    b_gate = 0.01 * jax.random.normal(ks[5], (DEPTH, 3 * N_HEADS_A), f32)
    cmp_pe_k = 0.1 * jax.random.normal(ks[6], (DEPTH, CMP_LEN, HEAD_DIM), f32)
    cmp_pe_v = 0.1 * jax.random.normal(ks[7], (DEPTH, CMP_LEN, HEAD_DIM), f32)
    cmp_wk1 = jax.random.normal(ks[8], (DEPTH, CMP_LEN * HEAD_DIM, CMP_HIDDEN), f32) * (CMP_LEN * HEAD_DIM) ** -0.5
    cmp_wk2 = jax.random.normal(ks[9], (DEPTH, CMP_HIDDEN, HEAD_DIM), f32) * CMP_HIDDEN ** -0.5
    cmp_wv1 = jax.random.normal(ks[10], (DEPTH, CMP_LEN * HEAD_DIM, CMP_HIDDEN), f32) * (CMP_LEN * HEAD_DIM) ** -0.5
    cmp_wv2 = jax.random.normal(ks[11], (DEPTH, CMP_HIDDEN, HEAD_DIM), f32) * CMP_HIDDEN ** -0.5
    mix_width = (N_HEADS_A + N_HEADS_B) * HEAD_DIM
    w_out = jax.random.normal(ks[12], (DEPTH, mix_width, D_MODEL), f32) * mix_width ** -0.5
    return {"x": x, "positions": positions, "pre_norm_w": pre_norm_w, "post_norm_w": post_norm_w,
            "w_in": w_in, "b_gate": b_gate, "cmp_pe_k": cmp_pe_k, "cmp_pe_v": cmp_pe_v,
            "cmp_wk1": cmp_wk1, "cmp_wk2": cmp_wk2, "cmp_wv1": cmp_wv1, "cmp_wv2": cmp_wv2,
            "w_out": w_out}


def reference(x, positions, pre_norm_w, post_norm_w, w_in, b_gate, cmp_pe_k, cmp_pe_v,
              cmp_wk1, cmp_wk2, cmp_wv1, cmp_wv2, w_out):
    b, s_len, _ = x.shape
    wa = N_HEADS_A * HEAD_DIM
    wkv = N_KV_A * HEAD_DIM
    wb = N_HEADS_B * HEAD_DIM
    sizes = [wa, wa, 3 * N_HEADS_A, wkv, wkv, wkv, wkv, wkv, wkv, wb, wb, wb, wb]
    splits = [int(v) for v in np.cumsum(sizes)[:-1]]
    for l in range(DEPTH):
        h = rms_norm(x, pre_norm_w[l])
        proj = h @ w_in[l]
        qa, za, ga, kc, vc, ksl, vsl, kw, vw, qb, kb, vb, zb = jnp.split(proj, splits, axis=-1)
        gates = jax.nn.sigmoid((ga + b_gate[l]).astype(jnp.float32))
        gates = gates.reshape(b, s_len, N_HEADS_A, 3).transpose(0, 2, 1, 3).reshape(b, N_KV_A, GROUP_A, s_len, 3)
        oa = nsa_mixer(to_heads(qa, N_HEADS_A), to_heads(kc, N_KV_A), to_heads(vc, N_KV_A),
                       to_heads(ksl, N_KV_A), to_heads(vsl, N_KV_A), to_heads(kw, N_KV_A),
                       to_heads(vw, N_KV_A), gates, positions,
                       cmp_pe_k[l], cmp_pe_v[l], cmp_wk1[l], cmp_wk2[l], cmp_wv1[l], cmp_wv2[l])
        ob = dilated_mixer(to_heads(qb, N_HEADS_B), to_heads(kb, N_HEADS_B), to_heads(vb, N_HEADS_B), positions)
        oa = oa.transpose(0, 2, 1, 3).reshape(b, s_len, wa)
        ob = ob.transpose(0, 2, 1, 3).reshape(b, s_len, wb)
        mix = jnp.concatenate([oa * jax.nn.silu(za.astype(jnp.float32)),
                               ob * jax.nn.silu(zb.astype(jnp.float32))], axis=-1).astype(x.dtype)
        x = x + rms_norm(mix @ w_out[l], post_norm_w[l])
    return x
```

```python
import functools

import jax
import jax.numpy as jnp
from jax import lax
from jax.experimental import pallas as pl
from jax.experimental.pallas import tpu as pltpu

D_MODEL = 4096
HEAD_DIM = 128
HALF = HEAD_DIM // 2
N_HEADS_A = 16
N_KV_A = 4
GROUP_A = 4
N_HEADS_B = 16
CMP_LEN = 32
CMP_STRIDE = 16
CMP_HIDDEN = 256
SEL_LEN = 64
SEL_SHIFT = 6
TOP_N = 16
WIN = 512
ROPE_THETA = 10000.0
EPS = 1e-6
FORCE_BONUS = 1e4
NEG = -1e30
SCALE = HEAD_DIM ** -0.5

H_QA = 0
H_ZA = H_QA + N_HEADS_A
H_GATE = H_ZA + N_HEADS_A
H_KC = H_GATE + N_KV_A
H_VC = H_KC + N_KV_A
H_KSL = H_VC + N_KV_A
H_VSL = H_KSL + N_KV_A
H_KW = H_VSL + N_KV_A
H_VW = H_KW + N_KV_A
H_QB = H_VW + N_KV_A
H_KB = H_QB + N_HEADS_B
H_VB = H_KB + N_HEADS_B
H_ZB = H_VB + N_HEADS_B
N_PROJ_HEADS = H_ZB + N_HEADS_B

TQ = 256
TK = 256
VMEM_LIMIT = 56 * 1024 * 1024


def _cparams(sem):
    return pltpu.CompilerParams(dimension_semantics=sem, vmem_limit_bytes=VMEM_LIMIT)


def _rmsnorm_kernel(x_ref, w_ref, o_ref):
    x = x_ref[...]
    ms = jnp.mean(x * x, axis=-1, keepdims=True)
    o_ref[...] = (x * lax.rsqrt(ms + EPS) * w_ref[...]).astype(o_ref.dtype)


def _rmsnorm(x2d, w, tm=256):
    m, d = x2d.shape
    return pl.pallas_call(
        _rmsnorm_kernel,
        grid=(m // tm,),
        in_specs=[pl.BlockSpec((tm, d), lambda i: (i, 0)),
                  pl.BlockSpec((1, d), lambda i: (0, 0))],
        out_specs=pl.BlockSpec((tm, d), lambda i: (i, 0)),
        out_shape=jax.ShapeDtypeStruct((m, d), jnp.bfloat16),
        compiler_params=_cparams(("parallel",)),
        name="pre_rmsnorm",
    )(x2d, w.reshape(1, d))


def _matmul_kernel(x_ref, w_ref, o_ref):
    o_ref[...] = jnp.dot(x_ref[...], w_ref[...],
                         preferred_element_type=jnp.float32).astype(o_ref.dtype)


def _in_proj(h, w, tm=1024, tn=512):
    m, k = h.shape
    n = w.shape[1]
    return pl.pallas_call(
        _matmul_kernel,
        grid=(m // tm, n // tn),
        in_specs=[pl.BlockSpec((tm, k), lambda i, j: (i, 0)),
                  pl.BlockSpec((k, tn), lambda i, j: (0, j))],
        out_specs=pl.BlockSpec((tm, tn), lambda i, j: (i, j)),
        out_shape=jax.ShapeDtypeStruct((m, n), jnp.bfloat16),
        compiler_params=_cparams(("parallel", "arbitrary")),
        name="in_proj",
    )(h, w)


def _rope_table_kernel(pos_ref, inv_ref, cos_ref, sin_ref):
    ang = pos_ref[...].astype(jnp.float32) * inv_ref[...]
    lane = lax.broadcasted_iota(jnp.int32, ang.shape, 1)
    cos_ref[...] = jnp.cos(ang)
    sin_ref[...] = jnp.where(lane < HALF, -jnp.sin(ang), jnp.sin(ang))


def _rope_tables(positions):
    b, s = positions.shape
    inv = ROPE_THETA ** (-jnp.arange(HALF, dtype=jnp.float32) / HALF)
    inv_full = jnp.concatenate([inv, inv]).reshape(1, HEAD_DIM)
    out = jax.ShapeDtypeStruct((b, s, HEAD_DIM), jnp.float32)
    return pl.pallas_call(
        _rope_table_kernel,
        grid=(b,),
        in_specs=[pl.BlockSpec((None, s, 1), lambda i: (i, 0, 0)),
                  pl.BlockSpec((1, HEAD_DIM), lambda i: (0, 0))],
        out_specs=[pl.BlockSpec((None, s, HEAD_DIM), lambda i: (i, 0, 0)),
                   pl.BlockSpec((None, s, HEAD_DIM), lambda i: (i, 0, 0))],
        out_shape=[out, out],
        compiler_params=_cparams(("parallel",)),
        name="rope_tables",
    )(positions.reshape(b, s, 1), inv_full)


def _rope(xf, cos, sin_signed):
    return xf * cos + pltpu.roll(xf, HALF, 1) * sin_signed


def _compress_kernel(x_ref, pe_ref, w1_ref, w2_ref, o_ref, xs_ref):
    s = x_ref.shape[0]
    n_rows = s // CMP_STRIDE
    xs_ref[pl.ds(0, s), :] = x_ref[...].astype(jnp.float32)
    xs_ref[pl.ds(s, CMP_STRIDE), :] = jnp.zeros((CMP_STRIDE, HEAD_DIM), jnp.float32)
    acc = jnp.zeros((n_rows, CMP_HIDDEN), jnp.float32)
    for r in range(CMP_LEN):
        xr = xs_ref[pl.ds(r, n_rows, stride=CMP_STRIDE), :] + pe_ref[pl.ds(r, 1), :]
        acc = acc + jnp.dot(xr.astype(jnp.bfloat16), w1_ref[r],
                            preferred_element_type=jnp.float32)
    hid = jax.nn.gelu(acc)
    o_ref[...] = jnp.dot(hid.astype(jnp.bfloat16), w2_ref[...],
                         preferred_element_type=jnp.float32).astype(o_ref.dtype)


def _compress(proj, batch, s, pe, w1, w2):
    n_rows = s // CMP_STRIDE
    return pl.pallas_call(
        _compress_kernel,
        grid=(batch, 2 * N_KV_A),
        in_specs=[
            pl.BlockSpec((s, HEAD_DIM), lambda b, h: (b, H_KC + h)),
            pl.BlockSpec((None, CMP_LEN, HEAD_DIM), lambda b, h: (h // N_KV_A, 0, 0)),
            pl.BlockSpec((None, CMP_LEN, HEAD_DIM, CMP_HIDDEN), lambda b, h: (h // N_KV_A, 0, 0, 0)),
            pl.BlockSpec((None, CMP_HIDDEN, HEAD_DIM), lambda b, h: (h // N_KV_A, 0, 0)),
        ],
        out_specs=pl.BlockSpec((None, None, n_rows, HEAD_DIM), lambda b, h: (b, h, 0, 0)),
        out_shape=jax.ShapeDtypeStruct((batch, 2 * N_KV_A, n_rows, HEAD_DIM), jnp.bfloat16),
        scratch_shapes=[pltpu.VMEM((s + CMP_STRIDE, HEAD_DIM), jnp.float32)],
        compiler_params=_cparams(("parallel", "parallel")),
        name="compress",
    )(proj, pe, w1, w2)


def _online_update(carry, s, v_blk, weight=None):
    m, l, acc = carry
    m_new = jnp.maximum(m, jnp.max(s, axis=-1, keepdims=True))
    alpha = jnp.exp(m - m_new)
    p = jnp.exp(s - m_new)
    if weight is not None:
        p = p * weight
    l = alpha * l + jnp.sum(p, axis=-1, keepdims=True)
    acc = alpha * acc + jnp.dot(p.astype(jnp.bfloat16), v_blk,
                                preferred_element_type=jnp.float32)
    return m_new, l, acc


def _init_carry():
    return (jnp.full((TQ, 1), NEG, jnp.float32),
            jnp.zeros((TQ, 1), jnp.float32),
            jnp.zeros((TQ, HEAD_DIM), jnp.float32))


def _qk(q, k):
    return lax.dot_general(q, k, (((1,), (1,)), ((), ())),
                           preferred_element_type=jnp.float32)


def _nsa_kernel(q_ref, z_ref, g_ref, bg_ref, kc_ref, vc_ref, ksl_ref, vsl_ref, kw_ref, vw_ref,
                cos_ref, sin_ref, o_ref, ksl_rot, kw_rot):
    qi = pl.program_id(2)
    s_len = ksl_ref.shape[0]

    @pl.when(qi == 0)
    def _():
        cos_k = cos_ref[...]
        sin_k = sin_ref[...]
        ksl_rot[...] = _rope(ksl_ref[...].astype(jnp.float32), cos_k, sin_k).astype(jnp.bfloat16)
        kw_rot[...] = _rope(kw_ref[...].astype(jnp.float32), cos_k, sin_k).astype(jnp.bfloat16)

    q0 = pl.multiple_of(qi * TQ, TQ)
    cos_q = cos_ref[pl.ds(q0, TQ), :]
    sin_q = sin_ref[pl.ds(q0, TQ), :]

    q_raw, q_rot = [], []
    for g in range(GROUP_A):
        qf = q_ref[:, g * HEAD_DIM:(g + 1) * HEAD_DIM].astype(jnp.float32) * SCALE
        q_raw.append(qf.astype(jnp.bfloat16))
        q_rot.append(_rope(qf, cos_q, sin_q).astype(jnp.bfloat16))

    lane = lax.broadcasted_iota(jnp.int32, (TQ, HEAD_DIM), 1)
    t_row = q0 + lax.broadcasted_iota(jnp.int32, (TQ, HEAD_DIM), 0)

    kc = kc_ref[...]
    vc = vc_ref[...]
    c_valid = lane * CMP_STRIDE + (CMP_LEN - 1) <= t_row
    any_valid = jnp.where(t_row >= CMP_LEN - 1, 1.0, 0.0)
    o_cmp = []
    p_sum = jnp.zeros((TQ, HEAD_DIM), jnp.float32)
    for g in range(GROUP_A):
        sc = jnp.where(c_valid, _qk(q_raw[g], kc), NEG)
        mx = jnp.max(sc, axis=-1, keepdims=True)
        e = jnp.exp(sc - mx)
        p = e / jnp.sum(e, axis=-1, keepdims=True) * any_valid
        o_cmp.append(jnp.dot(p.astype(jnp.bfloat16), vc, preferred_element_type=jnp.float32))
        p_sum = p_sum + p

    rr = lax.broadcasted_iota(jnp.int32, (HEAD_DIM, HEAD_DIM), 0)
    cc = lax.broadcasted_iota(jnp.int32, (HEAD_DIM, HEAD_DIM), 1)
    ovl = jnp.where(rr * CMP_STRIDE < cc * SEL_LEN + SEL_LEN,
                    jnp.where(rr * CMP_STRIDE + CMP_LEN > cc * SEL_LEN, 1.0, 0.0), 0.0)
    n_cmp = s_len // CMP_STRIDE - CMP_LEN // CMP_STRIDE + 1
    ovl = jnp.where(rr < n_cmp, ovl, 0.0).astype(jnp.bfloat16)
    p_hi = p_sum.astype(jnp.bfloat16)
    p_lo = (p_sum - p_hi.astype(jnp.float32)).astype(jnp.bfloat16)
    imp = (jnp.dot(p_hi, ovl, preferred_element_type=jnp.float32)
           + jnp.dot(p_lo, ovl, preferred_element_type=jnp.float32))
    cur = lax.shift_right_arithmetic(t_row, SEL_SHIFT)
    forced = (lane == 0) | (lane == cur) | (lane == cur - 1)
    imp = jnp.where(forced, imp + FORCE_BONUS, imp)
    imp = jnp.where(lane * SEL_LEN <= t_row, imp, -jnp.inf)
    rank = jnp.zeros((TQ, HEAD_DIM), jnp.float32)
    for i in range(s_len // SEL_LEN):
        vi = imp[:, i:i + 1]
        rank = rank + jnp.where(lane > i, jnp.where(vi >= imp, 1.0, 0.0),
                                jnp.where(vi > imp, 1.0, 0.0))
    sel = jnp.where(rank < TOP_N, 1.0, 0.0).astype(jnp.bfloat16)

    col = lax.broadcasted_iota(jnp.int32, (TQ, TK), 1)
    t_blk = q0 + lax.broadcasted_iota(jnp.int32, (TQ, TK), 0)
    er = lax.broadcasted_iota(jnp.int32, (HEAD_DIM, TK), 0)
    ec = lax.broadcasted_iota(jnp.int32, (HEAD_DIM, TK), 1)

    def slc_body(j, carries):
        k0 = pl.multiple_of(j * TK, TK)
        k_blk = ksl_rot[pl.ds(k0, TK), :]
        v_blk = vsl_ref[pl.ds(k0, TK), :]
        expand = jnp.where(er == lax.shift_right_arithmetic(k0 + ec, SEL_SHIFT),
                           1.0, 0.0).astype(jnp.bfloat16)
        chosen = jnp.dot(sel, expand, preferred_element_type=jnp.float32)
        ok = jnp.where(k0 + col <= t_blk, chosen, 0.0) > 0.5
        return tuple(_online_update(carries[g], jnp.where(ok, _qk(q_rot[g], k_blk), NEG), v_blk)
                     for g in range(GROUP_A))

    slc = lax.fori_loop(0, qi + 1, slc_body, tuple(_init_carry() for _ in range(GROUP_A)))

    def win_body(j, carries):
        k0 = pl.multiple_of(j * TK, TK)
        k_blk = kw_rot[pl.ds(k0, TK), :]
        v_blk = vw_ref[pl.ds(k0, TK), :]
        dist = t_blk - (k0 + col)
        ok = (dist >= 0) & (dist <= WIN - 1)
        return tuple(_online_update(carries[g], jnp.where(ok, _qk(q_rot[g], k_blk), NEG), v_blk)
                     for g in range(GROUP_A))

    j_lo = jnp.maximum(qi - (WIN + TK - 2) // TK, 0)
    win = lax.fori_loop(j_lo, qi + 1, win_body, tuple(_init_carry() for _ in range(GROUP_A)))

    gate = jax.nn.sigmoid(g_ref[...].astype(jnp.float32) + bg_ref[...])
    for g in range(GROUP_A):
        o_slc = slc[g][2] / slc[g][1]
        o_win = win[g][2] / win[g][1]
        o = (gate[:, 3 * g:3 * g + 1] * o_cmp[g] + gate[:, 3 * g + 1:3 * g + 2] * o_slc
             + gate[:, 3 * g + 2:3 * g + 3] * o_win)
        z = z_ref[:, g * HEAD_DIM:(g + 1) * HEAD_DIM].astype(jnp.float32)
        o_ref[:, g * HEAD_DIM:(g + 1) * HEAD_DIM] = (o * (z * jax.nn.sigmoid(z))).astype(o_ref.dtype)


def _nsa(proj, cmp_kv, b_gate_blocks, cos, sin, batch, s):
    nq = s // TQ
    gw = GROUP_A * HEAD_DIM
    full = lambda col0: pl.BlockSpec((s, HEAD_DIM), lambda b, h, i: (b, col0 + h))
    return pl.pallas_call(
        _nsa_kernel,
        grid=(batch, N_KV_A, nq),
        in_specs=[
            pl.BlockSpec((TQ, gw), lambda b, h, i: (b * nq + i, H_QA // GROUP_A + h)),
            pl.BlockSpec((TQ, gw), lambda b, h, i: (b * nq + i, H_ZA // GROUP_A + h)),
            pl.BlockSpec((TQ, HEAD_DIM), lambda b, h, i: (b * nq + i, H_GATE + h)),
            pl.BlockSpec((None, 1, HEAD_DIM), lambda b, h, i: (h, 0, 0)),
            pl.BlockSpec((None, None, s // CMP_STRIDE, HEAD_DIM), lambda b, h, i: (b, h, 0, 0)),
            pl.BlockSpec((None, None, s // CMP_STRIDE, HEAD_DIM), lambda b, h, i: (b, N_KV_A + h, 0, 0)),
            full(H_KSL), full(H_VSL), full(H_KW), full(H_VW),
            pl.BlockSpec((None, s, HEAD_DIM), lambda b, h, i: (b, 0, 0)),
            pl.BlockSpec((None, s, HEAD_DIM), lambda b, h, i: (b, 0, 0)),
        ],
        out_specs=pl.BlockSpec((TQ, gw), lambda b, h, i: (b * nq + i, h)),
        out_shape=jax.ShapeDtypeStruct((batch * s, N_HEADS_A * HEAD_DIM), jnp.bfloat16),
        scratch_shapes=[pltpu.VMEM((s, HEAD_DIM), jnp.bfloat16),
                        pltpu.VMEM((s, HEAD_DIM), jnp.bfloat16)],
        compiler_params=_cparams(("parallel", "parallel", "arbitrary")),
        name="nsa_mixer",
    )(proj, proj, proj, b_gate_blocks, cmp_kv, cmp_kv, proj, proj, proj, proj, cos, sin)


def _dilated_kernel(q_ref, z_ref, k_ref, v_ref, cos_ref, sin_ref, o_ref, k_rot):
    qi = pl.program_id(2)

    @pl.when(qi == 0)
    def _():
        k_rot[...] = _rope(k_ref[...].astype(jnp.float32), cos_ref[...], sin_ref[...]).astype(jnp.bfloat16)

    q0 = pl.multiple_of(qi * TQ, TQ)
    qf = q_ref[...].astype(jnp.float32) * SCALE
    q_rot = _rope(qf, cos_ref[pl.ds(q0, TQ), :], sin_ref[pl.ds(q0, TQ), :]).astype(jnp.bfloat16)

    col = lax.broadcasted_iota(jnp.int32, (TQ, TK), 1)
    t_blk = q0 + lax.broadcasted_iota(jnp.int32, (TQ, TK), 0)

    def body(j, carry):
        k0 = pl.multiple_of(j * TK, TK)
        dist = t_blk - (k0 + col)
        cnt = (jnp.where(dist <= 128, 1.0, 0.0)
               + jnp.where((dist & 3) == 0, jnp.where(dist <= 512, 1.0, 0.0), 0.0)
               + jnp.where((dist & 15) == 0, 1.0, 0.0))
        cnt = jnp.where(dist >= 0, cnt, 0.0)
        s = jnp.where(cnt > 0.5, _qk(q_rot, k_rot[pl.ds(k0, TK), :]), NEG)
        return _online_update(carry, s, v_ref[pl.ds(k0, TK), :], weight=cnt)

    _, l, acc = lax.fori_loop(0, qi + 1, body, _init_carry())
    z = z_ref[...].astype(jnp.float32)
    o_ref[...] = (acc / l * (z * jax.nn.sigmoid(z))).astype(o_ref.dtype)


def _dilated(proj, cos, sin, batch, s):
    nq = s // TQ
    return pl.pallas_call(
        _dilated_kernel,
        grid=(batch, N_HEADS_B, nq),
        in_specs=[
            pl.BlockSpec((TQ, HEAD_DIM), lambda b, h, i: (b * nq + i, H_QB + h)),
            pl.BlockSpec((TQ, HEAD_DIM), lambda b, h, i: (b * nq + i, H_ZB + h)),
            pl.BlockSpec((s, HEAD_DIM), lambda b, h, i: (b, H_KB + h)),
            pl.BlockSpec((s, HEAD_DIM), lambda b, h, i: (b, H_VB + h)),
            pl.BlockSpec((None, s, HEAD_DIM), lambda b, h, i: (b, 0, 0)),
            pl.BlockSpec((None, s, HEAD_DIM), lambda b, h, i: (b, 0, 0)),
        ],
        out_specs=pl.BlockSpec((TQ, HEAD_DIM), lambda b, h, i: (b * nq + i, h)),
        out_shape=jax.ShapeDtypeStruct((batch * s, N_HEADS_B * HEAD_DIM), jnp.bfloat16),
        scratch_shapes=[pltpu.VMEM((s, HEAD_DIM), jnp.bfloat16)],
        compiler_params=_cparams(("parallel", "parallel", "arbitrary")),
        name="dilated_mixer",
    )(proj, proj, proj, proj, cos, sin)


def _out_proj_kernel(ma_ref, mb_ref, wa_ref, wb_ref, x_ref, nw_ref, o_ref, y_ref):
    j = pl.program_id(1)
    tn = wa_ref.shape[1]
    y = (jnp.dot(ma_ref[...], wa_ref[...], preferred_element_type=jnp.float32)
         + jnp.dot(mb_ref[...], wb_ref[...], preferred_element_type=jnp.float32))
    y_ref[:, pl.ds(pl.multiple_of(j * tn, tn), tn)] = y

    @pl.when(j == pl.num_programs(1) - 1)
    def _():
        yf = y_ref[...]
        ms = jnp.mean(yf * yf, axis=-1, keepdims=True)
        o_ref[...] = x_ref[...] + yf * lax.rsqrt(ms + EPS) * nw_ref[...]


def _out_proj(mix_a, mix_b, wa, wb, x2d, post_w, tm=256, tn=1024):
    m, d = x2d.shape
    ka, kb = mix_a.shape[1], mix_b.shape[1]
    return pl.pallas_call(
        _out_proj_kernel,
        grid=(m // tm, d // tn),
        in_specs=[
            pl.BlockSpec((tm, ka), lambda i, j: (i, 0)),
            pl.BlockSpec((tm, kb), lambda i, j: (i, 0)),
            pl.BlockSpec((ka, tn), lambda i, j: (0, j)),
            pl.BlockSpec((kb, tn), lambda i, j: (0, j)),
            pl.BlockSpec((tm, d), lambda i, j: (i, 0)),
            pl.BlockSpec((1, d), lambda i, j: (0, 0)),
        ],
        out_specs=pl.BlockSpec((tm, d), lambda i, j: (i, 0)),
        out_shape=jax.ShapeDtypeStruct((m, d), jnp.float32),
        scratch_shapes=[pltpu.VMEM((tm, d), jnp.float32)],
        compiler_params=_cparams(("parallel", "arbitrary")),
        name="out_proj_norm_residual",
    )(mix_a, mix_b, wa, wb, x2d, post_w.reshape(1, d))


def _pack_w_in(w):
    wa = N_HEADS_A * HEAD_DIM
    g0 = 2 * wa
    g1 = g0 + 3 * N_HEADS_A
    per = 3 * GROUP_A
    gate_blocks = [jnp.pad(w[:, g0 + h * per:g0 + (h + 1) * per], ((0, 0), (0, HEAD_DIM - per)))
                   for h in range(N_KV_A)]
    return jnp.concatenate([w[:, :g0]] + gate_blocks + [w[:, g1:]], axis=1).astype(jnp.bfloat16)


def _layer(x, cos, sin, pre_w, post_w, w_in, b_gate, pe_k, pe_v, wk1, wk2, wv1, wv2, w_out):
    b, s, d = x.shape
    x2d = x.reshape(b * s, d)
    h = _rmsnorm(x2d, pre_w)
    proj = _in_proj(h, _pack_w_in(w_in))

    pe = jnp.stack([pe_k, pe_v])
    w1 = jnp.stack([wk1, wv1]).reshape(2, CMP_LEN, HEAD_DIM, CMP_HIDDEN).astype(jnp.bfloat16)
    w2 = jnp.stack([wk2, wv2]).astype(jnp.bfloat16)
    cmp_kv = _compress(proj, b, s, pe, w1, w2)

    per = 3 * GROUP_A
    bg = jnp.pad(b_gate.reshape(N_KV_A, 1, per), ((0, 0), (0, 0), (0, HEAD_DIM - per)))
    mix_a = _nsa(proj, cmp_kv, bg, cos, sin, b, s)
    mix_b = _dilated(proj, cos, sin, b, s)

    wa = N_HEADS_A * HEAD_DIM
    w_out_b = w_out.astype(jnp.bfloat16)
    out = _out_proj(mix_a, mix_b, w_out_b[:wa], w_out_b[wa:], x2d, post_w)
    return out.reshape(b, s, d)


def kernel(x, positions, pre_norm_w, post_norm_w, w_in, b_gate, cmp_pe_k, cmp_pe_v,
           cmp_wk1, cmp_wk2, cmp_wv1, cmp_wv2, w_out):
    cos, sin = _rope_tables(positions)
    for l in range(pre_norm_w.shape[0]):
        x = _layer(x, cos, sin, pre_norm_w[l], post_norm_w[l], w_in[l], b_gate[l],
                   cmp_pe_k[l], cmp_pe_v[l], cmp_wk1[l], cmp_wk2[l], cmp_wv1[l], cmp_wv2[l],
                   w_out[l])
    return x
```

```python
import functools
import math

import jax
import jax.numpy as jnp
from jax import lax
from jax.experimental import pallas as pl
from jax.experimental.pallas import tpu as pltpu

D_MODEL = 4096
HEAD_DIM = 128
HALF = HEAD_DIM // 2
N_HEADS_A = 16
N_KV_A = 4
GROUP_A = 4
N_HEADS_B = 16
CMP_LEN = 32
CMP_STRIDE = 16
CMP_HIDDEN = 256
SEL_LEN = 64
TOP_N = 16
WIN = 512
DILATIONS = ((128, 1), (512, 4), (2048, 16))
ROPE_THETA = 10000.0
EPS = 1e-6
FORCE_BONUS = 1e4
NEG = -1e30
SCALE = HEAD_DIM ** -0.5
QSCALE = SCALE * math.log2(math.e)

H_QA = 0
H_ZA = H_QA + N_HEADS_A
H_GATE = H_ZA + N_HEADS_A
H_KC = H_GATE + N_KV_A
H_VC = H_KC + N_KV_A
H_KSL = H_VC + N_KV_A
H_VSL = H_KSL + N_KV_A
H_KW = H_VSL + N_KV_A
H_VW = H_KW + N_KV_A
H_QB = H_VW + N_KV_A
H_KB = H_QB + N_HEADS_B
H_VB = H_KB + N_HEADS_B
H_ZB = H_VB + N_HEADS_B
N_PROJ_HEADS = H_ZB + N_HEADS_B

TQ = 256
TK = 256
SEL_PER_TK = TK // SEL_LEN
WIN_BACK = (WIN + TK - 2) // TK
DIL_NEAR = max((w // d) * d for w, d in DILATIONS[:-1]) // TK + 1
VMEM_LIMIT = 56 * 1024 * 1024


def _cparams(sem):
    return pltpu.CompilerParams(dimension_semantics=sem, vmem_limit_bytes=VMEM_LIMIT)


def _rmsnorm_kernel(x_ref, w_ref, o_ref):
    x = x_ref[...]
    ms = jnp.mean(x * x, axis=-1, keepdims=True)
    o_ref[...] = (x * lax.rsqrt(ms + EPS) * w_ref[...]).astype(o_ref.dtype)


def _rmsnorm(x2d, w, tm=256):
    m, d = x2d.shape
    return pl.pallas_call(
        _rmsnorm_kernel,
        grid=(m // tm,),
        in_specs=[pl.BlockSpec((tm, d), lambda i: (i, 0)),
                  pl.BlockSpec((1, d), lambda i: (0, 0))],
        out_specs=pl.BlockSpec((tm, d), lambda i: (i, 0)),
        out_shape=jax.ShapeDtypeStruct((m, d), jnp.bfloat16),
        compiler_params=_cparams(("parallel",)),
        name="pre_rmsnorm",
    )(x2d, w.reshape(1, d))


def _matmul_kernel(x_ref, w_ref, o_ref):
    o_ref[...] = jnp.dot(x_ref[...], w_ref[...],
                         preferred_element_type=jnp.float32).astype(o_ref.dtype)


def _in_proj(h, w, tm=1024, tn=512):
    m, k = h.shape
    n = w.shape[1]
    return pl.pallas_call(
        _matmul_kernel,
        grid=(m // tm, n // tn),
        in_specs=[pl.BlockSpec((tm, k), lambda i, j: (i, 0)),
                  pl.BlockSpec((k, tn), lambda i, j: (0, j))],
        out_specs=pl.BlockSpec((tm, tn), lambda i, j: (i, j)),
        out_shape=jax.ShapeDtypeStruct((m, n), jnp.bfloat16),
        compiler_params=_cparams(("parallel", "arbitrary")),
        name="in_proj",
    )(h, w)


def _rope_table_kernel(pos_ref, inv_ref, cos_ref, sin_ref):
    ang = pos_ref[...].astype(jnp.float32) * inv_ref[...]
    lane = lax.broadcasted_iota(jnp.int32, ang.shape, 1)
    cos_ref[...] = jnp.cos(ang)
    sin_ref[...] = jnp.where(lane < HALF, -jnp.sin(ang), jnp.sin(ang))


def _rope_tables(positions):
    b, s = positions.shape
    inv = ROPE_THETA ** (-jnp.arange(HALF, dtype=jnp.float32) / HALF)
    inv_full = jnp.concatenate([inv, inv]).reshape(1, HEAD_DIM)
    out = jax.ShapeDtypeStruct((b, s, HEAD_DIM), jnp.float32)
    return pl.pallas_call(
        _rope_table_kernel,
        grid=(b,),
        in_specs=[pl.BlockSpec((None, s, 1), lambda i: (i, 0, 0)),
                  pl.BlockSpec((1, HEAD_DIM), lambda i: (0, 0))],
        out_specs=[pl.BlockSpec((None, s, HEAD_DIM), lambda i: (i, 0, 0)),
                   pl.BlockSpec((None, s, HEAD_DIM), lambda i: (i, 0, 0))],
        out_shape=[out, out],
        compiler_params=_cparams(("parallel",)),
        name="rope_tables",
    )(positions.reshape(b, s, 1), inv_full)


def _rope(xf, cos, sin_signed):
    return xf * cos + pltpu.roll(xf, HALF, 1) * sin_signed


def _compress_kernel(x_ref, pe_ref, w1_ref, w2_ref, o_ref, xs_ref):
    s = x_ref.shape[0]
    n_rows = s // CMP_STRIDE
    xs_ref[pl.ds(0, s), :] = x_ref[...].astype(jnp.float32)
    xs_ref[pl.ds(s, CMP_STRIDE), :] = jnp.zeros((CMP_STRIDE, HEAD_DIM), jnp.float32)
    acc = jnp.zeros((n_rows, CMP_HIDDEN), jnp.float32)
    for r in range(CMP_LEN):
        xr = xs_ref[pl.ds(r, n_rows, stride=CMP_STRIDE), :] + pe_ref[pl.ds(r, 1), :]
        acc = acc + jnp.dot(xr.astype(jnp.bfloat16), w1_ref[r],
                            preferred_element_type=jnp.float32)
    hid = jax.nn.gelu(acc)
    o_ref[...] = jnp.dot(hid.astype(jnp.bfloat16), w2_ref[...],
                         preferred_element_type=jnp.float32).astype(o_ref.dtype)


def _compress(proj, batch, s, pe, w1, w2):
    n_rows = s // CMP_STRIDE
    return pl.pallas_call(
        _compress_kernel,
        grid=(batch, 2 * N_KV_A),
        in_specs=[
            pl.BlockSpec((s, HEAD_DIM), lambda b, h: (b, H_KC + h)),
            pl.BlockSpec((None, CMP_LEN, HEAD_DIM), lambda b, h: (h // N_KV_A, 0, 0)),
            pl.BlockSpec((None, CMP_LEN, HEAD_DIM, CMP_HIDDEN), lambda b, h: (h // N_KV_A, 0, 0, 0)),
            pl.BlockSpec((None, CMP_HIDDEN, HEAD_DIM), lambda b, h: (h // N_KV_A, 0, 0)),
        ],
        out_specs=pl.BlockSpec((None, None, n_rows, HEAD_DIM), lambda b, h: (b, h, 0, 0)),
        out_shape=jax.ShapeDtypeStruct((batch, 2 * N_KV_A, n_rows, HEAD_DIM), jnp.bfloat16),
        scratch_shapes=[pltpu.VMEM((s + CMP_STRIDE, HEAD_DIM), jnp.float32)],
        compiler_params=_cparams(("parallel", "parallel")),
        name="compress",
    )(proj, pe, w1, w2)


def _kq(k, q):
    return lax.dot_general(k, q, (((1,), (1,)), ((), ())),
                           preferred_element_type=jnp.float32)


def _softmax_pv(s, vt, near_weight=None):
    p = jnp.exp2(s - jnp.max(s, axis=0, keepdims=True))
    if near_weight is not None:
        nk, nw = s.shape[0], near_weight.shape[0]
        p_near = p[nk - nw:, :] * near_weight
        p = p_near if nw == nk else jnp.concatenate([p[:nk - nw, :], p_near], axis=0)
    inv_l = 1.0 / jnp.sum(p, axis=0, keepdims=True)
    return jnp.dot(vt, p.astype(jnp.bfloat16), preferred_element_type=jnp.float32) * inv_l


def _store_transposed(src_ref, dst_ref):
    for c in range(src_ref.shape[0] // TK):
        blk = src_ref[c * TK:(c + 1) * TK, :].astype(jnp.float32)
        dst_ref[:, c * TK:(c + 1) * TK] = blk.T.astype(jnp.bfloat16)


def _silu(z):
    return z * jax.nn.sigmoid(z)


def _nsa_kernel(q_ref, z_ref, g_ref, bg_ref, kc_ref, vc_ref, ksl_ref, vsl_ref, kw_ref, vw_ref,
                cos_ref, sin_ref, o_ref, ksl_rot, kw_rot, vsl_t, vw_t, oslc_ref, owin_ref):
    qi = pl.program_id(2)
    s_len = ksl_ref.shape[0]
    n_sel = s_len // SEL_LEN
    n_cmp = s_len // CMP_STRIDE - CMP_LEN // CMP_STRIDE + 1

    @pl.when(qi == 0)
    def _():
        cos_k = cos_ref[...]
        sin_k = sin_ref[...]
        ksl_rot[...] = _rope(ksl_ref[...].astype(jnp.float32), cos_k, sin_k).astype(jnp.bfloat16)
        kw_rot[...] = _rope(kw_ref[...].astype(jnp.float32), cos_k, sin_k).astype(jnp.bfloat16)
        _store_transposed(vsl_ref, vsl_t)
        _store_transposed(vw_ref, vw_t)

    q0 = pl.multiple_of(qi * TQ, TQ)
    cos_q = cos_ref[pl.ds(q0, TQ), :]
    sin_q = sin_ref[pl.ds(q0, TQ), :]
    q_raw, q_rot = [], []
    for g in range(GROUP_A):
        qf = q_ref[:, g * HEAD_DIM:(g + 1) * HEAD_DIM].astype(jnp.float32) * QSCALE
        q_raw.append(qf.astype(jnp.bfloat16))
        q_rot.append(_rope(qf, cos_q, sin_q).astype(jnp.bfloat16))
    q4_raw = jnp.concatenate(q_raw, axis=0)
    q4_rot = jnp.concatenate(q_rot, axis=0)

    c_row = lax.broadcasted_iota(jnp.int32, (HEAD_DIM, TQ), 0)
    t_col = q0 + lax.broadcasted_iota(jnp.int32, (HEAD_DIM, TQ), 1)
    c_valid = c_row * CMP_STRIDE + (CMP_LEN - 1) <= t_col
    t_one = q0 + lax.broadcasted_iota(jnp.int32, (1, TQ), 1)
    any_valid = jnp.where(t_one >= CMP_LEN - 1, 1.0, 0.0)
    sc4 = _kq(kc_ref[...], q4_raw)
    p_heads = []
    p_sum = jnp.zeros((HEAD_DIM, TQ), jnp.float32)
    for g in range(GROUP_A):
        sc = jnp.where(c_valid, sc4[:, g * TQ:(g + 1) * TQ], NEG)
        e = jnp.exp2(sc - jnp.max(sc, axis=0, keepdims=True))
        p = e * (any_valid / jnp.sum(e, axis=0, keepdims=True))
        p_heads.append(p.astype(jnp.bfloat16))
        p_sum = p_sum + p
    vc_t = vc_ref[...].astype(jnp.float32).T.astype(jnp.bfloat16)
    o_cmp4 = jnp.dot(vc_t, jnp.concatenate(p_heads, axis=1),
                     preferred_element_type=jnp.float32)

    oj = lax.broadcasted_iota(jnp.int32, (n_sel, HEAD_DIM), 0)
    oc = lax.broadcasted_iota(jnp.int32, (n_sel, HEAD_DIM), 1)
    ovl = jnp.where(oc * CMP_STRIDE < oj * SEL_LEN + SEL_LEN,
                    jnp.where(oc * CMP_STRIDE + CMP_LEN > oj * SEL_LEN, 1.0, 0.0), 0.0)
    ovl = jnp.where(oc < n_cmp, ovl, 0.0).astype(jnp.bfloat16)
    p_hi = p_sum.astype(jnp.bfloat16)
    p_lo = (p_sum - p_hi.astype(jnp.float32)).astype(jnp.bfloat16)
    imp = (jnp.dot(ovl, p_hi, preferred_element_type=jnp.float32)
           + jnp.dot(ovl, p_lo, preferred_element_type=jnp.float32))
    j_row = lax.broadcasted_iota(jnp.int32, (n_sel, TQ), 0)
    t_sel = q0 + lax.broadcasted_iota(jnp.int32, (n_sel, TQ), 1)
    cur = lax.shift_right_arithmetic(t_sel, int(math.log2(SEL_LEN)))
    forced = (j_row == 0) | (j_row == cur) | (j_row == cur - 1)
    imp = jnp.where(forced, imp + FORCE_BONUS, imp)
    imp = jnp.where(j_row * SEL_LEN <= t_sel, imp, -jnp.inf)
    rank = jnp.zeros((n_sel, TQ), jnp.float32)
    for i in range(n_sel):
        vi = imp[i:i + 1, :]
        rank = rank + jnp.where(j_row > i, jnp.where(vi >= imp, 1.0, 0.0),
                                jnp.where(vi > imp, 1.0, 0.0))
    selb = jnp.where(rank < TOP_N, 0.0, NEG)

    n_q = s_len // TQ
    diag_rc = (lax.broadcasted_iota(jnp.int32, (TK, TQ), 0)
               - lax.broadcasted_iota(jnp.int32, (TK, TQ), 1))

    def slc_variant(n):
        nk = n * TK
        s4 = _kq(ksl_rot[0:nk, :], q4_rot)
        vt = vsl_t[:, 0:nk]
        bias = [jnp.broadcast_to(selb[a:a + 1, :], (SEL_LEN, TQ)) for a in range(n * SEL_PER_TK)]
        diag = jnp.where(diag_rc <= 0, jnp.concatenate(bias[-SEL_PER_TK:], axis=0), NEG)
        bias = jnp.concatenate(bias[:-SEL_PER_TK] + [diag], axis=0)
        for g in range(GROUP_A):
            oslc_ref[g] = _softmax_pv(s4[:, g * TQ:(g + 1) * TQ] + bias, vt)

    for n in range(1, n_q + 1):
        pl.when(qi == n - 1)(functools.partial(slc_variant, n))

    def win_variant(n):
        nk = n * TK
        k0 = pl.multiple_of((qi + 1 - n) * TK, TK)
        dist = ((n - 1) * TK + lax.broadcasted_iota(jnp.int32, (nk, TQ), 1)
                - lax.broadcasted_iota(jnp.int32, (nk, TQ), 0))
        ok = (dist >= 0) & (dist <= WIN - 1)
        s4 = _kq(kw_rot[pl.ds(k0, nk), :], q4_rot)
        vt = vw_t[:, pl.ds(k0, nk)]
        for g in range(GROUP_A):
            owin_ref[g] = _softmax_pv(jnp.where(ok, s4[:, g * TQ:(g + 1) * TQ], NEG), vt)

    for n in range(1, WIN_BACK + 1):
        pl.when(qi == n - 1)(functools.partial(win_variant, n))
    pl.when(qi >= WIN_BACK)(functools.partial(win_variant, WIN_BACK + 1))

    gate_t = jax.nn.sigmoid(g_ref[...].astype(jnp.float32) + bg_ref[...]).T
    for g in range(GROUP_A):
        o_t = (gate_t[3 * g:3 * g + 1, :] * o_cmp4[:, g * TQ:(g + 1) * TQ]
               + gate_t[3 * g + 1:3 * g + 2, :] * oslc_ref[g]
               + gate_t[3 * g + 2:3 * g + 3, :] * owin_ref[g])
        z = z_ref[:, g * HEAD_DIM:(g + 1) * HEAD_DIM].astype(jnp.float32)
        o_ref[:, g * HEAD_DIM:(g + 1) * HEAD_DIM] = (o_t.T * _silu(z)).astype(o_ref.dtype)


def _nsa(proj, cmp_kv, b_gate_blocks, cos, sin, batch, s):
    nq = s // TQ
    gw = GROUP_A * HEAD_DIM
    full = lambda col0: pl.BlockSpec((s, HEAD_DIM), lambda b, h, i: (b, col0 + h))
    return pl.pallas_call(
        _nsa_kernel,
        grid=(batch, N_KV_A, nq),
        in_specs=[
            pl.BlockSpec((TQ, gw), lambda b, h, i: (b * nq + i, H_QA // GROUP_A + h)),
            pl.BlockSpec((TQ, gw), lambda b, h, i: (b * nq + i, H_ZA // GROUP_A + h)),
            pl.BlockSpec((TQ, HEAD_DIM), lambda b, h, i: (b * nq + i, H_GATE + h)),
            pl.BlockSpec((None, 1, HEAD_DIM), lambda b, h, i: (h, 0, 0)),
            pl.BlockSpec((None, None, s // CMP_STRIDE, HEAD_DIM), lambda b, h, i: (b, h, 0, 0)),
            pl.BlockSpec((None, None, s // CMP_STRIDE, HEAD_DIM), lambda b, h, i: (b, N_KV_A + h, 0, 0)),
            full(H_KSL), full(H_VSL), full(H_KW), full(H_VW),
            pl.BlockSpec((None, s, HEAD_DIM), lambda b, h, i: (b, 0, 0)),
            pl.BlockSpec((None, s, HEAD_DIM), lambda b, h, i: (b, 0, 0)),
        ],
        out_specs=pl.BlockSpec((TQ, gw), lambda b, h, i: (b * nq + i, h)),
        out_shape=jax.ShapeDtypeStruct((batch * s, N_HEADS_A * HEAD_DIM), jnp.bfloat16),
        scratch_shapes=[pltpu.VMEM((s, HEAD_DIM), jnp.bfloat16),
                        pltpu.VMEM((s, HEAD_DIM), jnp.bfloat16),
                        pltpu.VMEM((HEAD_DIM, s), jnp.bfloat16),
                        pltpu.VMEM((HEAD_DIM, s), jnp.bfloat16),
                        pltpu.VMEM((GROUP_A, HEAD_DIM, TQ), jnp.float32),
                        pltpu.VMEM((GROUP_A, HEAD_DIM, TQ), jnp.float32)],
        compiler_params=_cparams(("parallel", "parallel", "arbitrary")),
        name="nsa_mixer",
    )(proj, proj, proj, b_gate_blocks, cmp_kv, cmp_kv, proj, proj, proj, proj, cos, sin)


def _dilated_multiplicity(dist):
    cnt = jnp.zeros(dist.shape, jnp.float32)
    for window, dil in DILATIONS:
        cnt = cnt + jnp.where((dist & (dil - 1)) == 0,
                              jnp.where(dist <= (window // dil) * dil, 1.0, 0.0), 0.0)
    return jnp.where(dist >= 0, cnt, 0.0)


def _dilated_table_kernel(bias_ref, cnt_ref):
    s_len, near = bias_ref.shape[0], cnt_ref.shape[0]
    dist = (s_len - TQ + lax.broadcasted_iota(jnp.int32, (s_len, TQ), 1)
            - lax.broadcasted_iota(jnp.int32, (s_len, TQ), 0))
    bias_ref[...] = jnp.where(_dilated_multiplicity(dist) > 0.5, 0.0, NEG)
    near_dist = (near - TQ + lax.broadcasted_iota(jnp.int32, (near, TQ), 1)
                 - lax.broadcasted_iota(jnp.int32, (near, TQ), 0))
    cnt_ref[...] = _dilated_multiplicity(near_dist)


def _dilated_tables(s):
    return pl.pallas_call(
        _dilated_table_kernel,
        out_shape=[jax.ShapeDtypeStruct((s, TQ), jnp.float32),
                   jax.ShapeDtypeStruct((DIL_NEAR * TK, TQ), jnp.float32)],
        compiler_params=pltpu.CompilerParams(vmem_limit_bytes=VMEM_LIMIT),
        name="dilated_tables",
    )()


def _dilated_kernel(q_ref, z_ref, k_ref, v_ref, cos_ref, sin_ref, bias_ref, cnt_ref, o_ref,
                    k_rot, v_t):
    qi = pl.program_id(2)
    s_len = k_ref.shape[0]

    @pl.when(qi == 0)
    def _():
        k_rot[...] = _rope(k_ref[...].astype(jnp.float32), cos_ref[...], sin_ref[...]).astype(jnp.bfloat16)
        _store_transposed(v_ref, v_t)

    q0 = pl.multiple_of(qi * TQ, TQ)
    qf = q_ref[...].astype(jnp.float32) * QSCALE
    q_rot = _rope(qf, cos_ref[pl.ds(q0, TQ), :], sin_ref[pl.ds(q0, TQ), :]).astype(jnp.bfloat16)
    out_gate = _silu(z_ref[...].astype(jnp.float32))

    def variant(n):
        nk = n * TK
        near = min(nk, DIL_NEAR * TK)
        s = _kq(k_rot[0:nk, :], q_rot) + bias_ref[s_len - nk:, :]
        o_t = _softmax_pv(s, v_t[:, 0:nk], near_weight=cnt_ref[DIL_NEAR * TK - near:, :])
        o_ref[...] = (o_t.T * out_gate).astype(o_ref.dtype)

    for n in range(1, s_len // TQ + 1):
        pl.when(qi == n - 1)(functools.partial(variant, n))


def _dilated(proj, cos, sin, batch, s):
    nq = s // TQ
    bias_tab, cnt_tab = _dilated_tables(s)
    return pl.pallas_call(
        _dilated_kernel,
        grid=(batch, N_HEADS_B, nq),
        in_specs=[
            pl.BlockSpec((TQ, HEAD_DIM), lambda b, h, i: (b * nq + i, H_QB + h)),
            pl.BlockSpec((TQ, HEAD_DIM), lambda b, h, i: (b * nq + i, H_ZB + h)),
            pl.BlockSpec((s, HEAD_DIM), lambda b, h, i: (b, H_KB + h)),
            pl.BlockSpec((s, HEAD_DIM), lambda b, h, i: (b, H_VB + h)),
            pl.BlockSpec((None, s, HEAD_DIM), lambda b, h, i: (b, 0, 0)),
            pl.BlockSpec((None, s, HEAD_DIM), lambda b, h, i: (b, 0, 0)),
            pl.BlockSpec((s, TQ), lambda b, h, i: (0, 0)),
            pl.BlockSpec((DIL_NEAR * TK, TQ), lambda b, h, i: (0, 0)),
        ],
        out_specs=pl.BlockSpec((TQ, HEAD_DIM), lambda b, h, i: (b * nq + i, h)),
        out_shape=jax.ShapeDtypeStruct((batch * s, N_HEADS_B * HEAD_DIM), jnp.bfloat16),
        scratch_shapes=[pltpu.VMEM((s, HEAD_DIM), jnp.bfloat16),
                        pltpu.VMEM((HEAD_DIM, s), jnp.bfloat16)],
        compiler_params=_cparams(("parallel", "parallel", "arbitrary")),
        name="dilated_mixer",
    )(proj, proj, proj, proj, cos, sin, bias_tab, cnt_tab)


def _out_proj_kernel(ma_ref, mb_ref, wa_ref, wb_ref, x_ref, nw_ref, o_ref, y_ref):
    j = pl.program_id(1)
    tn = wa_ref.shape[1]
    y = (jnp.dot(ma_ref[...], wa_ref[...], preferred_element_type=jnp.float32)
         + jnp.dot(mb_ref[...], wb_ref[...], preferred_element_type=jnp.float32))
    y_ref[:, pl.ds(pl.multiple_of(j * tn, tn), tn)] = y

    @pl.when(j == pl.num_programs(1) - 1)
    def _():
        yf = y_ref[...]
        ms = jnp.mean(yf * yf, axis=-1, keepdims=True)
        o_ref[...] = x_ref[...] + yf * lax.rsqrt(ms + EPS) * nw_ref[...]


def _out_proj(mix_a, mix_b, wa, wb, x2d, post_w, tm=256, tn=1024):
    m, d = x2d.shape
    ka, kb = mix_a.shape[1], mix_b.shape[1]
    return pl.pallas_call(
        _out_proj_kernel,
        grid=(m // tm, d // tn),
        in_specs=[
            pl.BlockSpec((tm, ka), lambda i, j: (i, 0)),
            pl.BlockSpec((tm, kb), lambda i, j: (i, 0)),
            pl.BlockSpec((ka, tn), lambda i, j: (0, j)),
            pl.BlockSpec((kb, tn), lambda i, j: (0, j)),
            pl.BlockSpec((tm, d), lambda i, j: (i, 0)),
            pl.BlockSpec((1, d), lambda i, j: (0, 0)),
        ],
        out_specs=pl.BlockSpec((tm, d), lambda i, j: (i, 0)),
        out_shape=jax.ShapeDtypeStruct((m, d), jnp.float32),
        scratch_shapes=[pltpu.VMEM((tm, d), jnp.float32)],
        compiler_params=_cparams(("parallel", "arbitrary")),
        name="out_proj_norm_residual",
    )(mix_a, mix_b, wa, wb, x2d, post_w.reshape(1, d))


def _pack_w_in(w):
    wa = N_HEADS_A * HEAD_DIM
    g0 = 2 * wa
    g1 = g0 + 3 * N_HEADS_A
    per = 3 * GROUP_A
    gate_blocks = [jnp.pad(w[:, g0 + h * per:g0 + (h + 1) * per], ((0, 0), (0, HEAD_DIM - per)))
                   for h in range(N_KV_A)]
    return jnp.concatenate([w[:, :g0]] + gate_blocks + [w[:, g1:]], axis=1).astype(jnp.bfloat16)


def _layer(x, cos, sin, pre_w, post_w, w_in, b_gate, pe_k, pe_v, wk1, wk2, wv1, wv2, w_out):
    b, s, d = x.shape
    x2d = x.reshape(b * s, d)
    h = _rmsnorm(x2d, pre_w)
    proj = _in_proj(h, _pack_w_in(w_in))

    pe = jnp.stack([pe_k, pe_v])
    w1 = jnp.stack([wk1, wv1]).reshape(2, CMP_LEN, HEAD_DIM, CMP_HIDDEN).astype(jnp.bfloat16)
    w2 = jnp.stack([wk2, wv2]).astype(jnp.bfloat16)
    cmp_kv = _compress(proj, b, s, pe, w1, w2)

    per = 3 * GROUP_A
    bg = jnp.pad(b_gate.reshape(N_KV_A, 1, per), ((0, 0), (0, 0), (0, HEAD_DIM - per)))
    mix_a = _nsa(proj, cmp_kv, bg, cos, sin, b, s)
    mix_b = _dilated(proj, cos, sin, b, s)

    wa = N_HEADS_A * HEAD_DIM
    w_out_b = w_out.astype(jnp.bfloat16)
    out = _out_proj(mix_a, mix_b, w_out_b[:wa], w_out_b[wa:], x2d, post_w)
    return out.reshape(b, s, d)


def kernel(x, positions, pre_norm_w, post_norm_w, w_in, b_gate, cmp_pe_k, cmp_pe_v,
           cmp_wk1, cmp_wk2, cmp_wv1, cmp_wv2, w_out):
    cos, sin = _rope_tables(positions)
    for l in range(pre_norm_w.shape[0]):
        x = _layer(x, cos, sin, pre_norm_w[l], post_norm_w[l], w_in[l], b_gate[l],
                   cmp_pe_k[l], cmp_pe_v[l], cmp_wk1[l], cmp_wk2[l], cmp_wv1[l], cmp_wv2[l],
                   w_out[l])
    return x
```

```python
import functools
import math

import jax
import jax.numpy as jnp
from jax import lax
from jax.experimental import pallas as pl
from jax.experimental.pallas import tpu as pltpu

D_MODEL = 4096
HEAD_DIM = 128
HALF = HEAD_DIM // 2
N_HEADS_A = 16
N_KV_A = 4
GROUP_A = 4
N_HEADS_B = 16
CMP_LEN = 32
CMP_STRIDE = 16
CMP_HIDDEN = 256
SEL_LEN = 64
TOP_N = 16
WIN = 512
DILATIONS = ((128, 1), (512, 4), (2048, 16))
ROPE_THETA = 10000.0
EPS = 1e-6
FORCE_BONUS = 1e4
NEG = -1e30
SCALE = HEAD_DIM ** -0.5
QSCALE = SCALE * math.log2(math.e)

H_QA = 0
H_ZA = H_QA + N_HEADS_A
H_GATE = H_ZA + N_HEADS_A
H_KC = H_GATE + N_KV_A
H_VC = H_KC + N_KV_A
H_KSL = H_VC + N_KV_A
H_VSL = H_KSL + N_KV_A
H_KW = H_VSL + N_KV_A
H_VW = H_KW + N_KV_A
H_QB = H_VW + N_KV_A
H_KB = H_QB + N_HEADS_B
H_VB = H_KB + N_HEADS_B
H_ZB = H_VB + N_HEADS_B
N_PROJ_HEADS = H_ZB + N_HEADS_B

TQ = 256
TK = 256
SEL_PER_TK = TK // SEL_LEN
BF16_SUBLANES = 16
V_ROWS = HEAD_DIM + BF16_SUBLANES
VMEM_LIMIT = 56 * 1024 * 1024
OUT_PROJ_VMEM_LIMIT = 60 * 1024 * 1024


def _cparams(sem):
    return pltpu.CompilerParams(dimension_semantics=sem, vmem_limit_bytes=VMEM_LIMIT)


def _rmsnorm_kernel(x_ref, w_ref, o_ref):
    x = x_ref[...]
    ms = jnp.mean(x * x, axis=-1, keepdims=True)
    o_ref[...] = (x * lax.rsqrt(ms + EPS) * w_ref[...]).astype(o_ref.dtype)


def _rmsnorm(x2d, w, tm=256):
    m, d = x2d.shape
    return pl.pallas_call(
        _rmsnorm_kernel,
        grid=(m // tm,),
        in_specs=[pl.BlockSpec((tm, d), lambda i: (i, 0)),
                  pl.BlockSpec((1, d), lambda i: (0, 0))],
        out_specs=pl.BlockSpec((tm, d), lambda i: (i, 0)),
        out_shape=jax.ShapeDtypeStruct((m, d), jnp.bfloat16),
        compiler_params=_cparams(("parallel",)),
        name="pre_rmsnorm",
    )(x2d, w.reshape(1, d))


def _matmul_kernel(x_ref, w_ref, o_ref):
    o_ref[...] = jnp.dot(x_ref[...], w_ref[...],
                         preferred_element_type=jnp.float32).astype(o_ref.dtype)


def _in_proj(h, w, tm=1024, tn=512):
    m, k = h.shape
    n = w.shape[1]
    return pl.pallas_call(
        _matmul_kernel,
        grid=(m // tm, n // tn),
        in_specs=[pl.BlockSpec((tm, k), lambda i, j: (i, 0)),
                  pl.BlockSpec((k, tn), lambda i, j: (0, j))],
        out_specs=pl.BlockSpec((tm, tn), lambda i, j: (i, j)),
        out_shape=jax.ShapeDtypeStruct((m, n), jnp.bfloat16),
        compiler_params=_cparams(("parallel", "arbitrary")),
        name="in_proj",
    )(h, w)


def _rope_table_kernel(pos_ref, inv_ref, cos_ref, sin_ref):
    ang = pos_ref[...].astype(jnp.float32) * inv_ref[...]
    lane = lax.broadcasted_iota(jnp.int32, ang.shape, 1)
    cos_ref[...] = jnp.cos(ang)
    sin_ref[...] = jnp.where(lane < HALF, -jnp.sin(ang), jnp.sin(ang))


def _rope_tables(positions):
    b, s = positions.shape
    inv = ROPE_THETA ** (-jnp.arange(HALF, dtype=jnp.float32) / HALF)
    inv_full = jnp.concatenate([inv, inv]).reshape(1, HEAD_DIM)
    out = jax.ShapeDtypeStruct((b, s, HEAD_DIM), jnp.float32)
    return pl.pallas_call(
        _rope_table_kernel,
        grid=(b,),
        in_specs=[pl.BlockSpec((None, s, 1), lambda i: (i, 0, 0)),
                  pl.BlockSpec((1, HEAD_DIM), lambda i: (0, 0))],
        out_specs=[pl.BlockSpec((None, s, HEAD_DIM), lambda i: (i, 0, 0)),
                   pl.BlockSpec((None, s, HEAD_DIM), lambda i: (i, 0, 0))],
        out_shape=[out, out],
        compiler_params=_cparams(("parallel",)),
        name="rope_tables",
    )(positions.reshape(b, s, 1), inv_full)


def _rope(xf, cos, sin_signed):
    return xf * cos + pltpu.roll(xf, HALF, 1) * sin_signed


def _compress_kernel(x_ref, pe_ref, w1_ref, w2_ref, o_ref, xs_ref):
    s = x_ref.shape[0]
    n_rows = s // CMP_STRIDE
    xs_ref[pl.ds(0, s), :] = x_ref[...].astype(jnp.float32)
    xs_ref[pl.ds(s, CMP_STRIDE), :] = jnp.zeros((CMP_STRIDE, HEAD_DIM), jnp.float32)
    acc = jnp.zeros((n_rows, CMP_HIDDEN), jnp.float32)
    for r in range(CMP_LEN):
        xr = xs_ref[pl.ds(r, n_rows, stride=CMP_STRIDE), :] + pe_ref[pl.ds(r, 1), :]
        acc = acc + jnp.dot(xr.astype(jnp.bfloat16), w1_ref[r],
                            preferred_element_type=jnp.float32)
    hid = jax.nn.gelu(acc)
    o_ref[...] = jnp.dot(hid.astype(jnp.bfloat16), w2_ref[...],
                         preferred_element_type=jnp.float32).astype(o_ref.dtype)


def _compress(proj, batch, s, pe, w1, w2):
    n_rows = s // CMP_STRIDE
    return pl.pallas_call(
        _compress_kernel,
        grid=(batch, 2 * N_KV_A),
        in_specs=[
            pl.BlockSpec((s, HEAD_DIM), lambda b, h: (b, H_KC + h)),
            pl.BlockSpec((None, CMP_LEN, HEAD_DIM), lambda b, h: (h // N_KV_A, 0, 0)),
            pl.BlockSpec((None, CMP_LEN, HEAD_DIM, CMP_HIDDEN), lambda b, h: (h // N_KV_A, 0, 0, 0)),
            pl.BlockSpec((None, CMP_HIDDEN, HEAD_DIM), lambda b, h: (h // N_KV_A, 0, 0)),
        ],
        out_specs=pl.BlockSpec((None, None, n_rows, HEAD_DIM), lambda b, h: (b, h, 0, 0)),
        out_shape=jax.ShapeDtypeStruct((batch, 2 * N_KV_A, n_rows, HEAD_DIM), jnp.bfloat16),
        scratch_shapes=[pltpu.VMEM((s + CMP_STRIDE, HEAD_DIM), jnp.float32)],
        compiler_params=_cparams(("parallel", "parallel")),
        name="compress",
    )(proj, pe, w1, w2)


def _kq(k, q):
    return lax.dot_general(k, q, (((1,), (1,)), ((), ())),
                           preferred_element_type=jnp.float32)


class _AttnJob:
    def __init__(self, q, chunks):
        self.q, self.chunks = q, chunks
        self.s, self.m, self.acc = [], None, None

    def _score(self, c):
        k_fn, _, mask_fn, _ = self.chunks[c]
        s = _kq(k_fn(), self.q)
        if mask_fn is not None:
            s = mask_fn(s)
        self.s.append(s)
        cm = jnp.max(s, axis=0, keepdims=True)
        self.m = cm if self.m is None else jnp.maximum(self.m, cm)

    def _value(self, c):
        _, vt_fn, _, weight_fn = self.chunks[c]
        p = jnp.exp2(self.s[c] - self.m)
        if weight_fn is not None:
            p = p * weight_fn()
        part = jnp.dot(vt_fn(), p.astype(jnp.bfloat16), preferred_element_type=jnp.float32)
        self.acc = part if self.acc is None else self.acc + part

    def score_tasks(self):
        return [functools.partial(self._score, c) for c in range(len(self.chunks))]

    def value_tasks(self):
        return [functools.partial(self._value, c) for c in range(len(self.chunks))]

    def result(self):
        return self.acc[:HEAD_DIM, :] * (1.0 / self.acc[HEAD_DIM:HEAD_DIM + 1, :])


def _run_pipelined(jobs, on_done):
    for t in jobs[0].score_tasks():
        t()
    for k, job in enumerate(jobs):
        a = job.value_tasks()
        b = jobs[k + 1].score_tasks() if k + 1 < len(jobs) else []
        for i in range(max(len(a), len(b))):
            if i < len(a):
                a[i]()
            if i < len(b):
                b[i]()
        on_done(k, job)


def _store_values_transposed(src_ref, dst_ref):
    s_len = src_ref.shape[0]
    for c in range(s_len // TK):
        blk = src_ref[c * TK:(c + 1) * TK, :].astype(jnp.float32)
        dst_ref[0:HEAD_DIM, c * TK:(c + 1) * TK] = blk.T.astype(jnp.bfloat16)
    row = lax.broadcasted_iota(jnp.int32, (BF16_SUBLANES, s_len), 0)
    dst_ref[HEAD_DIM:, :] = jnp.where(row == 0, 1.0, 0.0).astype(jnp.bfloat16)


def _silu(z):
    return z * jax.nn.sigmoid(z)


def _topk_active(q0):
    return q0 + TQ > TOP_N * SEL_LEN


def _nsa_selection_bias(p_sum, q0, s_len):
    n_sel = s_len // SEL_LEN
    n_cmp = s_len // CMP_STRIDE - CMP_LEN // CMP_STRIDE + 1
    oj = lax.broadcasted_iota(jnp.int32, (n_sel, HEAD_DIM), 0)
    oc = lax.broadcasted_iota(jnp.int32, (n_sel, HEAD_DIM), 1)
    ovl = jnp.where(oc * CMP_STRIDE < oj * SEL_LEN + SEL_LEN,
                    jnp.where(oc * CMP_STRIDE + CMP_LEN > oj * SEL_LEN, 1.0, 0.0), 0.0)
    ovl = jnp.where(oc < n_cmp, ovl, 0.0).astype(jnp.bfloat16)
    p_hi = p_sum.astype(jnp.bfloat16)
    p_lo = (p_sum - p_hi.astype(jnp.float32)).astype(jnp.bfloat16)
    imp = (jnp.dot(ovl, p_hi, preferred_element_type=jnp.float32)
           + jnp.dot(ovl, p_lo, preferred_element_type=jnp.float32))
    j_row = lax.broadcasted_iota(jnp.int32, (n_sel, TQ), 0)
    t_sel = q0 + lax.broadcasted_iota(jnp.int32, (n_sel, TQ), 1)
    cur = lax.shift_right_arithmetic(t_sel, int(math.log2(SEL_LEN)))
    forced = (j_row == 0) | (j_row == cur) | (j_row == cur - 1)
    imp = jnp.where(forced, imp + FORCE_BONUS, imp)
    imp = jnp.where(j_row * SEL_LEN <= t_sel, imp, -jnp.inf)
    rank = jnp.zeros((n_sel, TQ), jnp.float32)
    for i in range(n_sel):
        vi = imp[i:i + 1, :]
        rank = rank + jnp.where(j_row > i, jnp.where(vi >= imp, 1.0, 0.0),
                                jnp.where(vi > imp, 1.0, 0.0))
    return jnp.where(rank < TOP_N, 0.0, NEG)


def _nsa_kernel(q_ref, z_ref, g_ref, bg_ref, kc_ref, vc_ref, ksl_ref, vsl_ref, kw_ref, vw_ref,
                cos_ref, sin_ref, o_ref,
                ksl_rot, kw_rot, vsl_t, vw_t, qrot_s, ocmp_s, selb_s, gate_s):
    s_len = ksl_ref.shape[0]
    n_q = s_len // TQ

    ksl_rot[...] = _rope(ksl_ref[...].astype(jnp.float32), cos_ref[...], sin_ref[...]).astype(jnp.bfloat16)
    kw_rot[...] = _rope(kw_ref[...].astype(jnp.float32), cos_ref[...], sin_ref[...]).astype(jnp.bfloat16)
    _store_values_transposed(vsl_ref, vsl_t)
    _store_values_transposed(vw_ref, vw_t)
    kc = kc_ref[...]
    vc_t = vc_ref[...].astype(jnp.float32).T.astype(jnp.bfloat16)
    bg = bg_ref[...]

    c_row = lax.broadcasted_iota(jnp.int32, (HEAD_DIM, TQ), 0)
    lane_q = lax.broadcasted_iota(jnp.int32, (HEAD_DIM, TQ), 1)
    lane_one = lax.broadcasted_iota(jnp.int32, (1, TQ), 1)
    for qi in range(n_q):
        q0 = qi * TQ
        rows = slice(q0, q0 + TQ)
        cos_q = cos_ref[rows, :]
        sin_q = sin_ref[rows, :]
        q_raw = []
        for g in range(GROUP_A):
            qf = q_ref[rows, g * HEAD_DIM:(g + 1) * HEAD_DIM].astype(jnp.float32) * QSCALE
            q_raw.append(qf.astype(jnp.bfloat16))
            qrot_s[g, rows, :] = _rope(qf, cos_q, sin_q).astype(jnp.bfloat16)
        c_valid = c_row * CMP_STRIDE + (CMP_LEN - 1) <= q0 + lane_q
        any_valid = jnp.where(q0 + lane_one >= CMP_LEN - 1, 1.0, 0.0)
        sc4 = _kq(kc, jnp.concatenate(q_raw, axis=0))
        p_heads = []
        p_sum = jnp.zeros((HEAD_DIM, TQ), jnp.float32)
        for g in range(GROUP_A):
            sc = jnp.where(c_valid, sc4[:, g * TQ:(g + 1) * TQ], NEG)
            e = jnp.exp2(sc - jnp.max(sc, axis=0, keepdims=True))
            p = e * (any_valid / jnp.sum(e, axis=0, keepdims=True))
            p_heads.append(p.astype(jnp.bfloat16))
            p_sum = p_sum + p
        o_cmp4 = jnp.dot(vc_t, jnp.concatenate(p_heads, axis=1),
                         preferred_element_type=jnp.float32)
        for g in range(GROUP_A):
            ocmp_s[g, :, rows] = o_cmp4[:, g * TQ:(g + 1) * TQ]
        if _topk_active(q0):
            selb_s[qi] = _nsa_selection_bias(p_sum, q0, s_len)
        gate_s[:, rows] = jax.nn.sigmoid(g_ref[rows, :].astype(jnp.float32) + bg).T

    diag_rc = (lax.broadcasted_iota(jnp.int32, (TK, TQ), 0)
               - lax.broadcasted_iota(jnp.int32, (TK, TQ), 1))

    def slc_mask(qi, c, s):
        if _topk_active(qi * TQ):
            s = s + jnp.concatenate(
                [jnp.broadcast_to(selb_s[qi, c * SEL_PER_TK + a:c * SEL_PER_TK + a + 1, :],
                                  (SEL_LEN, TQ)) for a in range(SEL_PER_TK)], axis=0)
        if c == qi:
            s = jnp.where(diag_rc <= 0, s, NEG)
        return s

    def win_mask(qi, c, s):
        if c == qi:
            return jnp.where(diag_rc <= 0, s, NEG)
        if (qi - c) * TK + TQ - 1 > WIN - 1:
            return jnp.where(diag_rc >= (qi - c) * TK - (WIN - 1), s, NEG)
        return s

    def head_body(g, carry):
        col = pl.ds(pl.multiple_of(g * HEAD_DIM, HEAD_DIM), HEAD_DIM)
        jobs = []
        for qi in range(n_q):
            q = qrot_s[g, qi * TQ:(qi + 1) * TQ, :]
            first_win = max(0, (qi * TQ - (WIN - 1)) // TK)
            for k_ref_, v_ref_, mask, c_lo in ((ksl_rot, vsl_t, slc_mask, 0),
                                               (kw_rot, vw_t, win_mask, first_win)):
                jobs.append(_AttnJob(q, [
                    (functools.partial(lambda r, c: r[c * TK:(c + 1) * TK, :], k_ref_, c),
                     functools.partial(lambda r, c: r[:, c * TK:(c + 1) * TK], v_ref_, c),
                     functools.partial(mask, qi, c), None)
                    for c in range(c_lo, qi + 1)]))

        def on_done(k, job):
            if k % 2 == 0:
                return
            qi = k // 2
            rows = slice(qi * TQ, (qi + 1) * TQ)
            o_t = (gate_s[pl.ds(3 * g, 1), rows] * ocmp_s[g, :, rows]
                   + gate_s[pl.ds(3 * g + 1, 1), rows] * jobs[k - 1].result()
                   + gate_s[pl.ds(3 * g + 2, 1), rows] * job.result())
            z = z_ref[rows, col].astype(jnp.float32)
            o_ref[rows, col] = (o_t.T * _silu(z)).astype(o_ref.dtype)

        _run_pipelined(jobs, on_done)
        return carry

    lax.fori_loop(0, GROUP_A, head_body, 0)


def _nsa(proj, cmp_kv, b_gate_blocks, cos, sin, batch, s):
    gw = GROUP_A * HEAD_DIM
    full = lambda col0: pl.BlockSpec((s, HEAD_DIM), lambda b, h: (b, col0 + h))
    return pl.pallas_call(
        _nsa_kernel,
        grid=(batch, N_KV_A),
        in_specs=[
            pl.BlockSpec((s, gw), lambda b, h: (b, H_QA // GROUP_A + h)),
            pl.BlockSpec((s, gw), lambda b, h: (b, H_ZA // GROUP_A + h)),
            full(H_GATE),
            pl.BlockSpec((None, 1, HEAD_DIM), lambda b, h: (h, 0, 0)),
            pl.BlockSpec((None, None, s // CMP_STRIDE, HEAD_DIM), lambda b, h: (b, h, 0, 0)),
            pl.BlockSpec((None, None, s // CMP_STRIDE, HEAD_DIM), lambda b, h: (b, N_KV_A + h, 0, 0)),
            full(H_KSL), full(H_VSL), full(H_KW), full(H_VW),
            pl.BlockSpec((None, s, HEAD_DIM), lambda b, h: (b, 0, 0)),
            pl.BlockSpec((None, s, HEAD_DIM), lambda b, h: (b, 0, 0)),
        ],
        out_specs=pl.BlockSpec((s, gw), lambda b, h: (b, h)),
        out_shape=jax.ShapeDtypeStruct((batch * s, N_HEADS_A * HEAD_DIM), jnp.bfloat16),
        scratch_shapes=[pltpu.VMEM((s, HEAD_DIM), jnp.bfloat16),
                        pltpu.VMEM((s, HEAD_DIM), jnp.bfloat16),
                        pltpu.VMEM((V_ROWS, s), jnp.bfloat16),
                        pltpu.VMEM((V_ROWS, s), jnp.bfloat16),
                        pltpu.VMEM((GROUP_A, s, HEAD_DIM), jnp.bfloat16),
                        pltpu.VMEM((GROUP_A, HEAD_DIM, s), jnp.float32),
                        pltpu.VMEM((s // TQ, s // SEL_LEN, TQ), jnp.float32),
                        pltpu.VMEM((HEAD_DIM, s), jnp.float32)],
        compiler_params=_cparams(("parallel", "parallel")),
        name="nsa_mixer",
    )(proj, proj, proj, b_gate_blocks, cmp_kv, cmp_kv, proj, proj, proj, proj, cos, sin)


def _dilated_multiplicity(dist):
    cnt = jnp.zeros(dist.shape, jnp.float32)
    for window, dil in DILATIONS:
        cnt = cnt + jnp.where((dist & (dil - 1)) == 0,
                              jnp.where(dist <= (window // dil) * dil, 1.0, 0.0), 0.0)
    return jnp.where(dist >= 0, cnt, 0.0)


def _dilated_table_kernel(bias_ref, cnt_ref):
    s_len = bias_ref.shape[0]
    dist = (s_len - TQ + lax.broadcasted_iota(jnp.int32, (s_len, TQ), 1)
            - lax.broadcasted_iota(jnp.int32, (s_len, TQ), 0))
    cnt = _dilated_multiplicity(dist)
    cnt_ref[...] = cnt
    bias_ref[...] = jnp.where(cnt > 0.5, 0.0, NEG)


def _dilated_tables(s):
    out = jax.ShapeDtypeStruct((s, TQ), jnp.float32)
    return pl.pallas_call(
        _dilated_table_kernel,
        out_shape=[out, out],
        compiler_params=pltpu.CompilerParams(vmem_limit_bytes=VMEM_LIMIT),
        name="dilated_tables",
    )()


def _dilated_single_count(delta):
    nearest = delta * TK - (TK - 1)
    return all(nearest > (w // d) * d for w, d in DILATIONS[:-1])


def _dilated_kernel(q_ref, z_ref, k_ref, v_ref, cos_ref, sin_ref, bias_ref, cnt_ref, o_ref,
                    k_rot, v_t):
    s_len = k_ref.shape[0]
    n_q = s_len // TQ
    k_rot[...] = _rope(k_ref[...].astype(jnp.float32), cos_ref[...], sin_ref[...]).astype(jnp.bfloat16)
    _store_values_transposed(v_ref, v_t)

    def strip(ref, delta):
        r0 = (n_q - 1 - delta) * TK
        return ref[r0:r0 + TK, :]

    jobs = []
    for qi in range(n_q):
        rows = slice(qi * TQ, (qi + 1) * TQ)
        qf = q_ref[rows, :].astype(jnp.float32) * QSCALE
        q_rot = _rope(qf, cos_ref[rows, :], sin_ref[rows, :]).astype(jnp.bfloat16)
        jobs.append(_AttnJob(q_rot, [
            (functools.partial(lambda c: k_rot[c * TK:(c + 1) * TK, :], c),
             functools.partial(lambda c: v_t[:, c * TK:(c + 1) * TK], c),
             functools.partial(lambda d, s: s + strip(bias_ref, d), qi - c),
             None if _dilated_single_count(qi - c) else functools.partial(strip, cnt_ref, qi - c))
            for c in range(qi + 1)]))

    def on_done(qi, job):
        rows = slice(qi * TQ, (qi + 1) * TQ)
        z = z_ref[rows, :].astype(jnp.float32)
        o_ref[rows, :] = (job.result().T * _silu(z)).astype(o_ref.dtype)

    _run_pipelined(jobs, on_done)


def _dilated(proj, cos, sin, batch, s):
    bias_tab, cnt_tab = _dilated_tables(s)
    return pl.pallas_call(
        _dilated_kernel,
        grid=(batch, N_HEADS_B),
        in_specs=[
            pl.BlockSpec((s, HEAD_DIM), lambda b, h: (b, H_QB + h)),
            pl.BlockSpec((s, HEAD_DIM), lambda b, h: (b, H_ZB + h)),
            pl.BlockSpec((s, HEAD_DIM), lambda b, h: (b, H_KB + h)),
            pl.BlockSpec((s, HEAD_DIM), lambda b, h: (b, H_VB + h)),
            pl.BlockSpec((None, s, HEAD_DIM), lambda b, h: (b, 0, 0)),
            pl.BlockSpec((None, s, HEAD_DIM), lambda b, h: (b, 0, 0)),
            pl.BlockSpec((s, TQ), lambda b, h: (0, 0)),
            pl.BlockSpec((s, TQ), lambda b, h: (0, 0)),
        ],
        out_specs=pl.BlockSpec((s, HEAD_DIM), lambda b, h: (b, h)),
        out_shape=jax.ShapeDtypeStruct((batch * s, N_HEADS_B * HEAD_DIM), jnp.bfloat16),
        scratch_shapes=[pltpu.VMEM((s, HEAD_DIM), jnp.bfloat16),
                        pltpu.VMEM((V_ROWS, s), jnp.bfloat16)],
        compiler_params=_cparams(("parallel", "parallel")),
        name="dilated_mixer",
    )(proj, proj, proj, proj, cos, sin, bias_tab, cnt_tab)


def _out_proj_kernel(ma_ref, mb_ref, wa_ref, wb_ref, x_ref, nw_ref, o_ref):
    y = (jnp.dot(ma_ref[...], wa_ref[...], preferred_element_type=jnp.float32)
         + jnp.dot(mb_ref[...], wb_ref[...], preferred_element_type=jnp.float32))
    ms = jnp.mean(y * y, axis=-1, keepdims=True)
    o_ref[...] = x_ref[...] + y * lax.rsqrt(ms + EPS) * nw_ref[...]


def _out_proj(mix_a, mix_b, wa, wb, x2d, post_w, tm=256):
    m, d = x2d.shape
    ka, kb = mix_a.shape[1], mix_b.shape[1]
    resident = dict(pipeline_mode=pl.Buffered(1))
    return pl.pallas_call(
        _out_proj_kernel,
        grid=(m // tm,),
        in_specs=[
            pl.BlockSpec((tm, ka), lambda i: (i, 0)),
            pl.BlockSpec((tm, kb), lambda i: (i, 0)),
            pl.BlockSpec((ka, d), lambda i: (0, 0), **resident),
            pl.BlockSpec((kb, d), lambda i: (0, 0), **resident),
            pl.BlockSpec((tm, d), lambda i: (i, 0)),
            pl.BlockSpec((1, d), lambda i: (0, 0)),
        ],
        out_specs=pl.BlockSpec((tm, d), lambda i: (i, 0)),
        out_shape=jax.ShapeDtypeStruct((m, d), jnp.float32),
        compiler_params=pltpu.CompilerParams(dimension_semantics=("parallel",),
                                             vmem_limit_bytes=OUT_PROJ_VMEM_LIMIT),
        name="out_proj_norm_residual",
    )(mix_a, mix_b, wa, wb, x2d, post_w.reshape(1, d))


def _pack_w_in(w):
    wa = N_HEADS_A * HEAD_DIM
    g0 = 2 * wa
    g1 = g0 + 3 * N_HEADS_A
    per = 3 * GROUP_A
    gate_blocks = [jnp.pad(w[:, g0 + h * per:g0 + (h + 1) * per], ((0, 0), (0, HEAD_DIM - per)))
                   for h in range(N_KV_A)]
    return jnp.concatenate([w[:, :g0]] + gate_blocks + [w[:, g1:]], axis=1).astype(jnp.bfloat16)


def _layer(x, cos, sin, pre_w, post_w, w_in, b_gate, pe_k, pe_v, wk1, wk2, wv1, wv2, w_out):
    b, s, d = x.shape
    x2d = x.reshape(b * s, d)
    h = _rmsnorm(x2d, pre_w)
    proj = _in_proj(h, _pack_w_in(w_in))

    pe = jnp.stack([pe_k, pe_v])
    w1 = jnp.stack([wk1, wv1]).reshape(2, CMP_LEN, HEAD_DIM, CMP_HIDDEN).astype(jnp.bfloat16)
    w2 = jnp.stack([wk2, wv2]).astype(jnp.bfloat16)
    cmp_kv = _compress(proj, b, s, pe, w1, w2)

    per = 3 * GROUP_A
    bg = jnp.pad(b_gate.reshape(N_KV_A, 1, per), ((0, 0), (0, 0), (0, HEAD_DIM - per)))
    mix_a = _nsa(proj, cmp_kv, bg, cos, sin, b, s)
    mix_b = _dilated(proj, cos, sin, b, s)

    wa = N_HEADS_A * HEAD_DIM
    w_out_b = w_out.astype(jnp.bfloat16)
    out = _out_proj(mix_a, mix_b, w_out_b[:wa], w_out_b[wa:], x2d, post_w)
    return out.reshape(b, s, d)


def kernel(x, positions, pre_norm_w, post_norm_w, w_in, b_gate, cmp_pe_k, cmp_pe_v,
           cmp_wk1, cmp_wk2, cmp_wv1, cmp_wv2, w_out):
    cos, sin = _rope_tables(positions)
    for l in range(pre_norm_w.shape[0]):
        x = _layer(x, cos, sin, pre_norm_w[l], post_norm_w[l], w_in[l], b_gate[l],
                   cmp_pe_k[l], cmp_pe_v[l], cmp_wk1[l], cmp_wk2[l], cmp_wv1[l], cmp_wv2[l],
                   w_out[l])
    return x
```

```python
import functools
import math

import jax
import jax.numpy as jnp
from jax import lax
from jax.experimental import pallas as pl
from jax.experimental.pallas import tpu as pltpu

D_MODEL = 4096
HEAD_DIM = 128
HALF = HEAD_DIM // 2
N_HEADS_A = 16
N_KV_A = 4
GROUP_A = 4
N_HEADS_B = 16
CMP_LEN = 32
CMP_STRIDE = 16
CMP_HIDDEN = 256
SEL_LEN = 64
TOP_N = 16
WIN = 512
DILATIONS = ((128, 1), (512, 4), (2048, 16))
ROPE_THETA = 10000.0
EPS = 1e-6
FORCE_BONUS = 1e4
NEG = -1e30
SCALE = HEAD_DIM ** -0.5
QSCALE = SCALE * math.log2(math.e)

H_QA = 0
H_ZA = H_QA + N_HEADS_A
H_GATE = H_ZA + N_HEADS_A
H_KC = H_GATE + N_KV_A
H_VC = H_KC + N_KV_A
H_KSL = H_VC + N_KV_A
H_VSL = H_KSL + N_KV_A
H_KW = H_VSL + N_KV_A
H_VW = H_KW + N_KV_A
H_QB = H_VW + N_KV_A
H_KB = H_QB + N_HEADS_B
H_VB = H_KB + N_HEADS_B
H_ZB = H_VB + N_HEADS_B
N_PROJ_HEADS = H_ZB + N_HEADS_B

TQ = 256
TK = 256
SEL_PER_TK = TK // SEL_LEN
BF16_SUBLANES = 16
V_ROWS = HEAD_DIM + BF16_SUBLANES
VMEM_LIMIT = 56 * 1024 * 1024
OUT_PROJ_VMEM_LIMIT = 60 * 1024 * 1024


def _cparams(sem):
    return pltpu.CompilerParams(dimension_semantics=sem, vmem_limit_bytes=VMEM_LIMIT)


def _rmsnorm_kernel(x_ref, w_ref, o_ref):
    x = x_ref[...]
    ms = jnp.mean(x * x, axis=-1, keepdims=True)
    o_ref[...] = (x * lax.rsqrt(ms + EPS) * w_ref[...]).astype(o_ref.dtype)


def _rmsnorm(x2d, w, tm=256):
    m, d = x2d.shape
    return pl.pallas_call(
        _rmsnorm_kernel,
        grid=(m // tm,),
        in_specs=[pl.BlockSpec((tm, d), lambda i: (i, 0)),
                  pl.BlockSpec((1, d), lambda i: (0, 0))],
        out_specs=pl.BlockSpec((tm, d), lambda i: (i, 0)),
        out_shape=jax.ShapeDtypeStruct((m, d), jnp.bfloat16),
        compiler_params=_cparams(("parallel",)),
        name="pre_rmsnorm",
    )(x2d, w.reshape(1, d))


def _matmul_kernel(x_ref, w_ref, o_ref):
    o_ref[...] = jnp.dot(x_ref[...], w_ref[...],
                         preferred_element_type=jnp.float32).astype(o_ref.dtype)


def _in_proj(h, w, tm=1024, tn=512):
    m, k = h.shape
    n = w.shape[1]
    return pl.pallas_call(
        _matmul_kernel,
        grid=(m // tm, n // tn),
        in_specs=[pl.BlockSpec((tm, k), lambda i, j: (i, 0)),
                  pl.BlockSpec((k, tn), lambda i, j: (0, j))],
        out_specs=pl.BlockSpec((tm, tn), lambda i, j: (i, j)),
        out_shape=jax.ShapeDtypeStruct((m, n), jnp.bfloat16),
        compiler_params=_cparams(("parallel", "arbitrary")),
        name="in_proj",
    )(h, w)


def _rope_table_kernel(pos_ref, inv_ref, cos_ref, sin_ref):
    ang = pos_ref[...].astype(jnp.float32) * inv_ref[...]
    lane = lax.broadcasted_iota(jnp.int32, ang.shape, 1)
    cos_ref[...] = jnp.cos(ang)
    sin_ref[...] = jnp.where(lane < HALF, -jnp.sin(ang), jnp.sin(ang))


def _rope_tables(positions):
    b, s = positions.shape
    inv = ROPE_THETA ** (-jnp.arange(HALF, dtype=jnp.float32) / HALF)
    inv_full = jnp.concatenate([inv, inv]).reshape(1, HEAD_DIM)
    out = jax.ShapeDtypeStruct((b, s, HEAD_DIM), jnp.float32)
    return pl.pallas_call(
        _rope_table_kernel,
        grid=(b,),
        in_specs=[pl.BlockSpec((None, s, 1), lambda i: (i, 0, 0)),
                  pl.BlockSpec((1, HEAD_DIM), lambda i: (0, 0))],
        out_specs=[pl.BlockSpec((None, s, HEAD_DIM), lambda i: (i, 0, 0)),
                   pl.BlockSpec((None, s, HEAD_DIM), lambda i: (i, 0, 0))],
        out_shape=[out, out],
        compiler_params=_cparams(("parallel",)),
        name="rope_tables",
    )(positions.reshape(b, s, 1), inv_full)


def _rope(xf, cos, sin_signed):
    return xf * cos + pltpu.roll(xf, HALF, 1) * sin_signed


def _compress_kernel(x_ref, pe_ref, w1_ref, w2_ref, o_ref, xs_ref):
    s = x_ref.shape[0]
    n_rows = s // CMP_STRIDE
    xs_ref[pl.ds(0, s), :] = x_ref[...].astype(jnp.float32)
    xs_ref[pl.ds(s, CMP_STRIDE), :] = jnp.zeros((CMP_STRIDE, HEAD_DIM), jnp.float32)
    acc = jnp.zeros((n_rows, CMP_HIDDEN), jnp.float32)
    for r in range(CMP_LEN):
        xr = xs_ref[pl.ds(r, n_rows, stride=CMP_STRIDE), :] + pe_ref[pl.ds(r, 1), :]
        acc = acc + jnp.dot(xr.astype(jnp.bfloat16), w1_ref[r],
                            preferred_element_type=jnp.float32)
    hid = jax.nn.gelu(acc)
    o_ref[...] = jnp.dot(hid.astype(jnp.bfloat16), w2_ref[...],
                         preferred_element_type=jnp.float32).astype(o_ref.dtype)


def _compress(proj, batch, s, pe, w1, w2):
    n_rows = s // CMP_STRIDE
    return pl.pallas_call(
        _compress_kernel,
        grid=(batch, 2 * N_KV_A),
        in_specs=[
            pl.BlockSpec((s, HEAD_DIM), lambda b, h: (b, H_KC + h)),
            pl.BlockSpec((None, CMP_LEN, HEAD_DIM), lambda b, h: (h // N_KV_A, 0, 0)),
            pl.BlockSpec((None, CMP_LEN, HEAD_DIM, CMP_HIDDEN), lambda b, h: (h // N_KV_A, 0, 0, 0)),
            pl.BlockSpec((None, CMP_HIDDEN, HEAD_DIM), lambda b, h: (h // N_KV_A, 0, 0)),
        ],
        out_specs=pl.BlockSpec((None, None, n_rows, HEAD_DIM), lambda b, h: (b, h, 0, 0)),
        out_shape=jax.ShapeDtypeStruct((batch, 2 * N_KV_A, n_rows, HEAD_DIM), jnp.bfloat16),
        scratch_shapes=[pltpu.VMEM((s + CMP_STRIDE, HEAD_DIM), jnp.float32)],
        compiler_params=_cparams(("parallel", "parallel")),
        name="compress",
    )(proj, pe, w1, w2)


def _kq(k, q):
    return lax.dot_general(k, q, (((1,), (1,)), ((), ())),
                           preferred_element_type=jnp.float32)


class _AttnJob:
    def __init__(self, q, chunks):
        self.q, self.chunks = q, chunks
        self.s, self.m, self.acc = [], None, None

    def _score(self, c):
        k_fn, _, mask_fn, _ = self.chunks[c]
        s = _kq(k_fn(), self.q)
        if mask_fn is not None:
            s = mask_fn(s)
        self.s.append(s)
        cm = jnp.max(s, axis=0, keepdims=True)
        self.m = cm if self.m is None else jnp.maximum(self.m, cm)

    def _value(self, c):
        _, vt_fn, _, weight_fn = self.chunks[c]
        p = jnp.exp2(self.s[c] - self.m)
        if weight_fn is not None:
            p = p * weight_fn()
        part = jnp.dot(vt_fn(), p.astype(jnp.bfloat16), preferred_element_type=jnp.float32)
        self.acc = part if self.acc is None else self.acc + part

    def score_tasks(self):
        return [functools.partial(self._score, c) for c in range(len(self.chunks))]

    def value_tasks(self):
        return [functools.partial(self._value, c) for c in range(len(self.chunks))]

    def result(self):
        return self.acc[:HEAD_DIM, :] * (1.0 / self.acc[HEAD_DIM:HEAD_DIM + 1, :])


def _round_robin(task_lists):
    for i in range(max(len(t) for t in task_lists)):
        for tasks in task_lists:
            if i < len(tasks):
                tasks[i]()


def _run_pipelined(streams, on_done):
    n = len(streams[0])
    _round_robin([jobs[0].score_tasks() for jobs in streams])
    for k in range(n):
        lists = []
        for jobs in streams:
            lists.append(jobs[k].value_tasks())
            if k + 1 < n:
                lists.append(jobs[k + 1].score_tasks())
        _round_robin(lists)
        for si, jobs in enumerate(streams):
            on_done(si, k, jobs[k])


def _store_values_transposed(src_ref, dst_ref):
    s_len = src_ref.shape[0]
    for c in range(s_len // TK):
        blk = src_ref[c * TK:(c + 1) * TK, :].astype(jnp.float32)
        dst_ref[0:HEAD_DIM, c * TK:(c + 1) * TK] = blk.T.astype(jnp.bfloat16)
    row = lax.broadcasted_iota(jnp.int32, (BF16_SUBLANES, s_len), 0)
    dst_ref[HEAD_DIM:, :] = jnp.where(row == 0, 1.0, 0.0).astype(jnp.bfloat16)


def _silu(z):
    return z * jax.nn.sigmoid(z)


def _topk_active(q0):
    return q0 + TQ > TOP_N * SEL_LEN


def _nsa_selection_bias(p_sum, q0, s_len):
    n_sel = s_len // SEL_LEN
    n_cmp = s_len // CMP_STRIDE - CMP_LEN // CMP_STRIDE + 1
    oj = lax.broadcasted_iota(jnp.int32, (n_sel, HEAD_DIM), 0)
    oc = lax.broadcasted_iota(jnp.int32, (n_sel, HEAD_DIM), 1)
    ovl = jnp.where(oc * CMP_STRIDE < oj * SEL_LEN + SEL_LEN,
                    jnp.where(oc * CMP_STRIDE + CMP_LEN > oj * SEL_LEN, 1.0, 0.0), 0.0)
    ovl = jnp.where(oc < n_cmp, ovl, 0.0).astype(jnp.bfloat16)
    p_hi = p_sum.astype(jnp.bfloat16)
    p_lo = (p_sum - p_hi.astype(jnp.float32)).astype(jnp.bfloat16)
    imp = (jnp.dot(ovl, p_hi, preferred_element_type=jnp.float32)
           + jnp.dot(ovl, p_lo, preferred_element_type=jnp.float32))
    j_row = lax.broadcasted_iota(jnp.int32, (n_sel, TQ), 0)
    t_sel = q0 + lax.broadcasted_iota(jnp.int32, (n_sel, TQ), 1)
    cur = lax.shift_right_arithmetic(t_sel, int(math.log2(SEL_LEN)))
    forced = (j_row == 0) | (j_row == cur) | (j_row == cur - 1)
    imp = jnp.where(forced, imp + FORCE_BONUS, imp)
    imp = jnp.where(j_row * SEL_LEN <= t_sel, imp, -jnp.inf)
    rank = jnp.zeros((n_sel, TQ), jnp.float32)
    for i in range(n_sel):
        vi = imp[i:i + 1, :]
        rank = rank + jnp.where(j_row > i, jnp.where(vi >= imp, 1.0, 0.0),
                                jnp.where(vi > imp, 1.0, 0.0))
    return jnp.where(rank < TOP_N, 0.0, NEG)


def _nsa_kernel(q_ref, z_ref, g_ref, bg_ref, kc_ref, vc_ref, ksl_ref, vsl_ref, kw_ref, vw_ref,
                cos_ref, sin_ref, o_ref,
                ksl_rot, kw_rot, vsl_t, vw_t, qrot_s, ocmp_s, selb_s, gate_s):
    s_len = ksl_ref.shape[0]
    n_q = s_len // TQ

    ksl_rot[...] = _rope(ksl_ref[...].astype(jnp.float32), cos_ref[...], sin_ref[...]).astype(jnp.bfloat16)
    kw_rot[...] = _rope(kw_ref[...].astype(jnp.float32), cos_ref[...], sin_ref[...]).astype(jnp.bfloat16)
    _store_values_transposed(vsl_ref, vsl_t)
    _store_values_transposed(vw_ref, vw_t)
    kc = kc_ref[...]
    vc_t = vc_ref[...].astype(jnp.float32).T.astype(jnp.bfloat16)
    bg = bg_ref[...]

    c_row = lax.broadcasted_iota(jnp.int32, (HEAD_DIM, TQ), 0)
    lane_q = lax.broadcasted_iota(jnp.int32, (HEAD_DIM, TQ), 1)
    lane_one = lax.broadcasted_iota(jnp.int32, (1, TQ), 1)
    for qi in range(n_q):
        q0 = qi * TQ
        rows = slice(q0, q0 + TQ)
        cos_q = cos_ref[rows, :]
        sin_q = sin_ref[rows, :]
        q_raw = []
        for g in range(GROUP_A):
            qf = q_ref[rows, g * HEAD_DIM:(g + 1) * HEAD_DIM].astype(jnp.float32) * QSCALE
            q_raw.append(qf.astype(jnp.bfloat16))
            qrot_s[g, rows, :] = _rope(qf, cos_q, sin_q).astype(jnp.bfloat16)
        c_valid = c_row * CMP_STRIDE + (CMP_LEN - 1) <= q0 + lane_q
        any_valid = jnp.where(q0 + lane_one >= CMP_LEN - 1, 1.0, 0.0)
        sc4 = _kq(kc, jnp.concatenate(q_raw, axis=0))
        p_heads = []
        p_sum = jnp.zeros((HEAD_DIM, TQ), jnp.float32)
        for g in range(GROUP_A):
            sc = jnp.where(c_valid, sc4[:, g * TQ:(g + 1) * TQ], NEG)
            e = jnp.exp2(sc - jnp.max(sc, axis=0, keepdims=True))
            p = e * (any_valid / jnp.sum(e, axis=0, keepdims=True))
            p_heads.append(p.astype(jnp.bfloat16))
            p_sum = p_sum + p
        o_cmp4 = jnp.dot(vc_t, jnp.concatenate(p_heads, axis=1),
                         preferred_element_type=jnp.float32)
        for g in range(GROUP_A):
            ocmp_s[g, :, rows] = o_cmp4[:, g * TQ:(g + 1) * TQ]
        if _topk_active(q0):
            selb_s[qi] = _nsa_selection_bias(p_sum, q0, s_len)
        gate_s[:, rows] = jax.nn.sigmoid(g_ref[rows, :].astype(jnp.float32) + bg).T

    diag_rc = (lax.broadcasted_iota(jnp.int32, (TK, TQ), 0)
               - lax.broadcasted_iota(jnp.int32, (TK, TQ), 1))

    def slc_mask(qi, c, s):
        if _topk_active(qi * TQ):
            s = s + jnp.concatenate(
                [jnp.broadcast_to(selb_s[qi, c * SEL_PER_TK + a:c * SEL_PER_TK + a + 1, :],
                                  (SEL_LEN, TQ)) for a in range(SEL_PER_TK)], axis=0)
        if c == qi:
            s = jnp.where(diag_rc <= 0, s, NEG)
        return s

    def win_mask(qi, c, s):
        if c == qi:
            return jnp.where(diag_rc <= 0, s, NEG)
        if (qi - c) * TK + TQ - 1 > WIN - 1:
            return jnp.where(diag_rc >= (qi - c) * TK - (WIN - 1), s, NEG)
        return s

    def head_body(g, carry):
        col = pl.ds(pl.multiple_of(g * HEAD_DIM, HEAD_DIM), HEAD_DIM)
        slc_jobs, win_jobs = [], []
        for qi in range(n_q):
            q = qrot_s[g, qi * TQ:(qi + 1) * TQ, :]
            first_win = max(0, (qi * TQ - (WIN - 1)) // TK)
            for jobs, k_ref_, v_ref_, mask, c_lo in ((slc_jobs, ksl_rot, vsl_t, slc_mask, 0),
                                                     (win_jobs, kw_rot, vw_t, win_mask, first_win)):
                jobs.append(_AttnJob(q, [
                    (functools.partial(lambda r, c: r[c * TK:(c + 1) * TK, :], k_ref_, c),
                     functools.partial(lambda r, c: r[:, c * TK:(c + 1) * TK], v_ref_, c),
                     functools.partial(mask, qi, c), None)
                    for c in range(c_lo, qi + 1)]))

        def on_done(si, qi, job):
            if si == 0:
                return
            rows = slice(qi * TQ, (qi + 1) * TQ)
            o_t = (gate_s[pl.ds(3 * g, 1), rows] * ocmp_s[g, :, rows]
                   + gate_s[pl.ds(3 * g + 1, 1), rows] * slc_jobs[qi].result()
                   + gate_s[pl.ds(3 * g + 2, 1), rows] * job.result())
            z = z_ref[rows, col].astype(jnp.float32)
            o_ref[rows, col] = (o_t.T * _silu(z)).astype(o_ref.dtype)

        _run_pipelined([slc_jobs, win_jobs], on_done)
        return carry

    lax.fori_loop(0, GROUP_A, head_body, 0)


def _nsa(proj, cmp_kv, b_gate_blocks, cos, sin, batch, s):
    gw = GROUP_A * HEAD_DIM
    full = lambda col0: pl.BlockSpec((s, HEAD_DIM), lambda b, h: (b, col0 + h))
    return pl.pallas_call(
        _nsa_kernel,
        grid=(batch, N_KV_A),
        in_specs=[
            pl.BlockSpec((s, gw), lambda b, h: (b, H_QA // GROUP_A + h)),
            pl.BlockSpec((s, gw), lambda b, h: (b, H_ZA // GROUP_A + h)),
            full(H_GATE),
            pl.BlockSpec((None, 1, HEAD_DIM), lambda b, h: (h, 0, 0)),
            pl.BlockSpec((None, None, s // CMP_STRIDE, HEAD_DIM), lambda b, h: (b, h, 0, 0)),
            pl.BlockSpec((None, None, s // CMP_STRIDE, HEAD_DIM), lambda b, h: (b, N_KV_A + h, 0, 0)),
            full(H_KSL), full(H_VSL), full(H_KW), full(H_VW),
            pl.BlockSpec((None, s, HEAD_DIM), lambda b, h: (b, 0, 0)),
            pl.BlockSpec((None, s, HEAD_DIM), lambda b, h: (b, 0, 0)),
        ],
        out_specs=pl.BlockSpec((s, gw), lambda b, h: (b, h)),
        out_shape=jax.ShapeDtypeStruct((batch * s, N_HEADS_A * HEAD_DIM), jnp.bfloat16),
        scratch_shapes=[pltpu.VMEM((s, HEAD_DIM), jnp.bfloat16),
                        pltpu.VMEM((s, HEAD_DIM), jnp.bfloat16),
                        pltpu.VMEM((V_ROWS, s), jnp.bfloat16),
                        pltpu.VMEM((V_ROWS, s), jnp.bfloat16),
                        pltpu.VMEM((GROUP_A, s, HEAD_DIM), jnp.bfloat16),
                        pltpu.VMEM((GROUP_A, HEAD_DIM, s), jnp.float32),
                        pltpu.VMEM((s // TQ, s // SEL_LEN, TQ), jnp.float32),
                        pltpu.VMEM((HEAD_DIM, s), jnp.float32)],
        compiler_params=_cparams(("parallel", "parallel")),
        name="nsa_mixer",
    )(proj, proj, proj, b_gate_blocks, cmp_kv, cmp_kv, proj, proj, proj, proj, cos, sin)


def _dilated_multiplicity(dist):
    cnt = jnp.zeros(dist.shape, jnp.float32)
    for window, dil in DILATIONS:
        cnt = cnt + jnp.where((dist & (dil - 1)) == 0,
                              jnp.where(dist <= (window // dil) * dil, 1.0, 0.0), 0.0)
    return jnp.where(dist >= 0, cnt, 0.0)


def _dilated_table_kernel(bias_ref, cnt_ref):
    s_len = bias_ref.shape[0]
    dist = (s_len - TQ + lax.broadcasted_iota(jnp.int32, (s_len, TQ), 1)
            - lax.broadcasted_iota(jnp.int32, (s_len, TQ), 0))
    cnt = _dilated_multiplicity(dist)
    cnt_ref[...] = cnt
    bias_ref[...] = jnp.where(cnt > 0.5, 0.0, NEG)


def _dilated_tables(s):
    out = jax.ShapeDtypeStruct((s, TQ), jnp.float32)
    return pl.pallas_call(
        _dilated_table_kernel,
        out_shape=[out, out],
        compiler_params=pltpu.CompilerParams(vmem_limit_bytes=VMEM_LIMIT),
        name="dilated_tables",
    )()


def _dilated_single_count(delta):
    nearest = delta * TK - (TK - 1)
    return all(nearest > (w // d) * d for w, d in DILATIONS[:-1])


DIL_HEADS = 2


def _dilated_kernel(q_ref, z_ref, k_ref, v_ref, cos_ref, sin_ref, bias_ref, cnt_ref, o_ref,
                    k_rot, v_t):
    s_len = k_ref.shape[0]
    n_q = s_len // TQ

    def strip(ref, delta):
        r0 = (n_q - 1 - delta) * TK
        return ref[r0:r0 + TK, :]

    streams = []
    for hh in range(DIL_HEADS):
        cols = slice(hh * HEAD_DIM, (hh + 1) * HEAD_DIM)
        k_rot[hh] = _rope(k_ref[:, cols].astype(jnp.float32), cos_ref[...], sin_ref[...]).astype(jnp.bfloat16)
        _store_values_transposed(v_ref.at[:, cols], v_t.at[hh])
        jobs = []
        for qi in range(n_q):
            rows = slice(qi * TQ, (qi + 1) * TQ)
            qf = q_ref[rows, cols].astype(jnp.float32) * QSCALE
            q_rot = _rope(qf, cos_ref[rows, :], sin_ref[rows, :]).astype(jnp.bfloat16)
            jobs.append(_AttnJob(q_rot, [
                (functools.partial(lambda hh, c: k_rot[hh, c * TK:(c + 1) * TK, :], hh, c),
                 functools.partial(lambda hh, c: v_t[hh, :, c * TK:(c + 1) * TK], hh, c),
                 functools.partial(lambda d, s: s + strip(bias_ref, d), qi - c),
                 None if _dilated_single_count(qi - c) else functools.partial(strip, cnt_ref, qi - c))
                for c in range(qi + 1)]))
        streams.append(jobs)

    def on_done(hh, qi, job):
        rows = slice(qi * TQ, (qi + 1) * TQ)
        cols = slice(hh * HEAD_DIM, (hh + 1) * HEAD_DIM)
        z = z_ref[rows, cols].astype(jnp.float32)
        o_ref[rows, cols] = (job.result().T * _silu(z)).astype(o_ref.dtype)

    _run_pipelined(streams, on_done)


def _dilated(proj, cos, sin, batch, s):
    bias_tab, cnt_tab = _dilated_tables(s)
    w = DIL_HEADS * HEAD_DIM
    assert all(h % DIL_HEADS == 0 for h in (H_QB, H_ZB, H_KB, H_VB, N_HEADS_B))
    heads = lambda h0: pl.BlockSpec((s, w), lambda b, h: (b, h0 // DIL_HEADS + h))
    return pl.pallas_call(
        _dilated_kernel,
        grid=(batch, N_HEADS_B // DIL_HEADS),
        in_specs=[
            heads(H_QB), heads(H_ZB), heads(H_KB), heads(H_VB),
            pl.BlockSpec((None, s, HEAD_DIM), lambda b, h: (b, 0, 0)),
            pl.BlockSpec((None, s, HEAD_DIM), lambda b, h: (b, 0, 0)),
            pl.BlockSpec((s, TQ), lambda b, h: (0, 0)),
            pl.BlockSpec((s, TQ), lambda b, h: (0, 0)),
        ],
        out_specs=pl.BlockSpec((s, w), lambda b, h: (b, h)),
        out_shape=jax.ShapeDtypeStruct((batch * s, N_HEADS_B * HEAD_DIM), jnp.bfloat16),
        scratch_shapes=[pltpu.VMEM((DIL_HEADS, s, HEAD_DIM), jnp.bfloat16),
                        pltpu.VMEM((DIL_HEADS, V_ROWS, s), jnp.bfloat16)],
        compiler_params=_cparams(("parallel", "parallel")),
        name="dilated_mixer",
    )(proj, proj, proj, proj, cos, sin, bias_tab, cnt_tab)


def _out_proj_kernel(ma_ref, mb_ref, wa_ref, wb_ref, x_ref, nw_ref, o_ref):
    y = (jnp.dot(ma_ref[...], wa_ref[...], preferred_element_type=jnp.float32)
         + jnp.dot(mb_ref[...], wb_ref[...], preferred_element_type=jnp.float32))
    ms = jnp.mean(y * y, axis=-1, keepdims=True)
    o_ref[...] = x_ref[...] + y * lax.rsqrt(ms + EPS) * nw_ref[...]


def _out_proj(mix_a, mix_b, w, x2d, post_w, tm=256):
    m, d = x2d.shape
    ka, kb = mix_a.shape[1], mix_b.shape[1]
    assert ka == kb and w.shape[0] == ka + kb
    resident = dict(pipeline_mode=pl.Buffered(1))
    return pl.pallas_call(
        _out_proj_kernel,
        grid=(m // tm,),
        in_specs=[
            pl.BlockSpec((tm, ka), lambda i: (i, 0)),
            pl.BlockSpec((tm, kb), lambda i: (i, 0)),
            pl.BlockSpec((ka, d), lambda i: (0, 0), **resident),
            pl.BlockSpec((kb, d), lambda i: (1, 0), **resident),
            pl.BlockSpec((tm, d), lambda i: (i, 0)),
            pl.BlockSpec((1, d), lambda i: (0, 0)),
        ],
        out_specs=pl.BlockSpec((tm, d), lambda i: (i, 0)),
        out_shape=jax.ShapeDtypeStruct((m, d), jnp.float32),
        compiler_params=pltpu.CompilerParams(dimension_semantics=("parallel",),
                                             vmem_limit_bytes=OUT_PROJ_VMEM_LIMIT),
        name="out_proj_norm_residual",
    )(mix_a, mix_b, w, w, x2d, post_w.reshape(1, d))


PACK_COLS = GROUP_A * HEAD_DIM
GATE_COLS = 3 * N_HEADS_A
GATE_PER_KV = 3 * GROUP_A
PACK_PLAIN = (H_GATE * HEAD_DIM) // PACK_COLS


def _pack_kernel(a_ref, b_ref, o_ref):
    j = pl.program_id(1)

    @pl.when(j < PACK_PLAIN)
    def _():
        o_ref[...] = a_ref[...].astype(o_ref.dtype)

    @pl.when(j == PACK_PLAIN)
    def _():
        r = lax.broadcasted_iota(jnp.int32, (HEAD_DIM, PACK_COLS), 0)
        c = lax.broadcasted_iota(jnp.int32, (HEAD_DIM, PACK_COLS), 1)
        h, i = c // HEAD_DIM, c % HEAD_DIM
        place = jnp.where(i < GATE_PER_KV, jnp.where(r == GATE_PER_KV * h + i, 1.0, 0.0), 0.0)
        o_ref[...] = jnp.dot(a_ref[:, :HEAD_DIM].astype(jnp.bfloat16), place.astype(jnp.bfloat16),
                             preferred_element_type=jnp.float32).astype(o_ref.dtype)

    @pl.when(j > PACK_PLAIN)
    def _():
        x = jnp.concatenate([a_ref[...], b_ref[...]], axis=1)
        o_ref[...] = x[:, GATE_COLS:GATE_COLS + PACK_COLS].astype(o_ref.dtype)


def _pack_w_in(w, rows=2048):
    d = w.shape[0]
    n_out = N_PROJ_HEADS * HEAD_DIM
    a_idx = lambda i, j: (i, jnp.where(j <= PACK_PLAIN, j, j - 1))
    b_idx = lambda i, j: (i, (PACK_COLS // HEAD_DIM) * j)
    return pl.pallas_call(
        _pack_kernel,
        grid=(d // rows, n_out // PACK_COLS),
        in_specs=[pl.BlockSpec((rows, PACK_COLS), a_idx),
                  pl.BlockSpec((rows, HEAD_DIM), b_idx)],
        out_specs=pl.BlockSpec((rows, PACK_COLS), lambda i, j: (i, j)),
        out_shape=jax.ShapeDtypeStruct((d, n_out), jnp.bfloat16),
        compiler_params=_cparams(("parallel", "arbitrary")),
        name="pack_w_in",
    )(w, w)


def _layer(x, cos, sin, pre_w, post_w, w_in, b_gate, pe_k, pe_v, wk1, wk2, wv1, wv2, w_out):
    b, s, d = x.shape
    x2d = x.reshape(b * s, d)
    h = _rmsnorm(x2d, pre_w)
    proj = _in_proj(h, _pack_w_in(w_in))

    pe = jnp.stack([pe_k, pe_v])
    w1 = jnp.stack([wk1, wv1]).reshape(2, CMP_LEN, HEAD_DIM, CMP_HIDDEN).astype(jnp.bfloat16)
    w2 = jnp.stack([wk2, wv2]).astype(jnp.bfloat16)
    cmp_kv = _compress(proj, b, s, pe, w1, w2)

    per = 3 * GROUP_A
    bg = jnp.pad(b_gate.reshape(N_KV_A, 1, per), ((0, 0), (0, 0), (0, HEAD_DIM - per)))
    mix_a = _nsa(proj, cmp_kv, bg, cos, sin, b, s)
    mix_b = _dilated(proj, cos, sin, b, s)

    out = _out_proj(mix_a, mix_b, w_out.astype(jnp.bfloat16), x2d, post_w)
    return out.reshape(b, s, d)


def kernel(x, positions, pre_norm_w, post_norm_w, w_in, b_gate, cmp_pe_k, cmp_pe_v,
           cmp_wk1, cmp_wk2, cmp_wv1, cmp_wv2, w_out):
    cos, sin = _rope_tables(positions)
    for l in range(pre_norm_w.shape[0]):
        x = _layer(x, cos, sin, pre_norm_w[l], post_norm_w[l], w_in[l], b_gate[l],
                   cmp_pe_k[l], cmp_pe_v[l], cmp_wk1[l], cmp_wk2[l], cmp_wv1[l], cmp_wv2[l],
                   w_out[l])
    return x
```

```python
import functools
import math

import jax
import jax.numpy as jnp
from jax import lax
from jax.experimental import pallas as pl
from jax.experimental.pallas import tpu as pltpu

D_MODEL = 4096
HEAD_DIM = 128
HALF = HEAD_DIM // 2
N_HEADS_A = 16
N_KV_A = 4
GROUP_A = 4
N_HEADS_B = 16
CMP_LEN = 32
CMP_STRIDE = 16
CMP_HIDDEN = 256
SEL_LEN = 64
TOP_N = 16
WIN = 512
DILATIONS = ((128, 1), (512, 4), (2048, 16))
ROPE_THETA = 10000.0
EPS = 1e-6
FORCE_BONUS = 1e4
NEG = -1e30
SCALE = HEAD_DIM ** -0.5
QSCALE = SCALE * math.log2(math.e)

H_QA = 0
H_ZA = H_QA + N_HEADS_A
H_GATE = H_ZA + N_HEADS_A
N_FRONT_HEADS = H_GATE + N_KV_A
H_KC = 0
H_VC = H_KC + N_KV_A
H_KSL = H_VC + N_KV_A
H_VSL = H_KSL + N_KV_A
H_KW = H_VSL + N_KV_A
H_VW = H_KW + N_KV_A
H_QB = H_VW + N_KV_A
H_KB = H_QB + N_HEADS_B
H_VB = H_KB + N_HEADS_B
H_ZB = H_VB + N_HEADS_B
N_BACK_HEADS = H_ZB + N_HEADS_B

TQ = 256
TK = 256
SEL_PER_TK = TK // SEL_LEN
F32_SUBLANES = 8
BF16_SUBLANES = 16
V_ROWS = HEAD_DIM + BF16_SUBLANES
VMEM_LIMIT = 56 * 1024 * 1024
OUT_PROJ_VMEM_LIMIT = 60 * 1024 * 1024


def _cparams(sem):
    return pltpu.CompilerParams(dimension_semantics=sem, vmem_limit_bytes=VMEM_LIMIT)


def _rmsnorm_kernel(x_ref, w_ref, o_ref):
    x = x_ref[...]
    ms = jnp.mean(x * x, axis=-1, keepdims=True)
    o_ref[...] = (x * lax.rsqrt(ms + EPS) * w_ref[...]).astype(o_ref.dtype)


def _rmsnorm(x2d, w, tm=256):
    m, d = x2d.shape
    return pl.pallas_call(
        _rmsnorm_kernel,
        grid=(m // tm,),
        in_specs=[pl.BlockSpec((tm, d), lambda i: (i, 0)),
                  pl.BlockSpec((1, d), lambda i: (0, 0))],
        out_specs=pl.BlockSpec((tm, d), lambda i: (i, 0)),
        out_shape=jax.ShapeDtypeStruct((m, d), jnp.bfloat16),
        compiler_params=_cparams(("parallel",)),
        name="pre_rmsnorm",
    )(x2d, w.reshape(1, d))


def _matmul_nt_kernel(x_ref, w_ref, o_ref):
    o_ref[...] = lax.dot_general(x_ref[...], w_ref[...], (((1,), (1,)), ((), ())),
                                 preferred_element_type=jnp.float32).astype(o_ref.dtype)


def _in_proj(h, w_t, tm=1024, tn=512):
    m, k = h.shape
    n = w_t.shape[0]
    return pl.pallas_call(
        _matmul_nt_kernel,
        grid=(m // tm, n // tn),
        in_specs=[pl.BlockSpec((tm, k), lambda i, j: (i, 0)),
                  pl.BlockSpec((tn, k), lambda i, j: (j, 0))],
        out_specs=pl.BlockSpec((tm, tn), lambda i, j: (i, j)),
        out_shape=jax.ShapeDtypeStruct((m, n), jnp.bfloat16),
        compiler_params=_cparams(("parallel", "arbitrary")),
        name="in_proj",
    )(h, w_t)


def _rope_table_kernel(pos_ref, inv_ref, cos_ref, sin_ref):
    ang = pos_ref[...].astype(jnp.float32) * inv_ref[...]
    lane = lax.broadcasted_iota(jnp.int32, ang.shape, 1)
    cos_ref[...] = jnp.cos(ang)
    sin_ref[...] = jnp.where(lane < HALF, -jnp.sin(ang), jnp.sin(ang))


def _rope_tables(positions):
    b, s = positions.shape
    inv = ROPE_THETA ** (-jnp.arange(HALF, dtype=jnp.float32) / HALF)
    inv_full = jnp.concatenate([inv, inv]).reshape(1, HEAD_DIM)
    out = jax.ShapeDtypeStruct((b, s, HEAD_DIM), jnp.float32)
    return pl.pallas_call(
        _rope_table_kernel,
        grid=(b,),
        in_specs=[pl.BlockSpec((None, s, 1), lambda i: (i, 0, 0)),
                  pl.BlockSpec((1, HEAD_DIM), lambda i: (0, 0))],
        out_specs=[pl.BlockSpec((None, s, HEAD_DIM), lambda i: (i, 0, 0)),
                   pl.BlockSpec((None, s, HEAD_DIM), lambda i: (i, 0, 0))],
        out_shape=[out, out],
        compiler_params=_cparams(("parallel",)),
        name="rope_tables",
    )(positions.reshape(b, s, 1), inv_full)


def _rope(xf, cos, sin_signed):
    return xf * cos + pltpu.roll(xf, HALF, 1) * sin_signed


def _compress_kernel(x_ref, pe_ref, w1_ref, w2_ref, o_ref, xs_ref):
    s = x_ref.shape[0]
    n_rows = s // CMP_STRIDE
    xs_ref[pl.ds(0, s), :] = x_ref[...].astype(jnp.float32)
    xs_ref[pl.ds(s, CMP_STRIDE), :] = jnp.zeros((CMP_STRIDE, HEAD_DIM), jnp.float32)
    acc = jnp.zeros((n_rows, CMP_HIDDEN), jnp.float32)
    for r in range(CMP_LEN):
        xr = xs_ref[pl.ds(r, n_rows, stride=CMP_STRIDE), :] + pe_ref[pl.ds(r, 1), :]
        acc = acc + jnp.dot(xr.astype(jnp.bfloat16), w1_ref[r],
                            preferred_element_type=jnp.float32)
    hid = jax.nn.gelu(acc)
    o_ref[...] = jnp.dot(hid.astype(jnp.bfloat16), w2_ref[...],
                         preferred_element_type=jnp.float32).astype(o_ref.dtype)


def _compress(proj, batch, s, pe, w1, w2):
    n_rows = s // CMP_STRIDE
    return pl.pallas_call(
        _compress_kernel,
        grid=(batch, 2 * N_KV_A),
        in_specs=[
            pl.BlockSpec((s, HEAD_DIM), lambda b, h: (b, H_KC + h)),
            pl.BlockSpec((None, CMP_LEN, HEAD_DIM), lambda b, h: (h // N_KV_A, 0, 0)),
            pl.BlockSpec((None, CMP_LEN, HEAD_DIM, CMP_HIDDEN), lambda b, h: (h // N_KV_A, 0, 0, 0)),
            pl.BlockSpec((None, CMP_HIDDEN, HEAD_DIM), lambda b, h: (h // N_KV_A, 0, 0)),
        ],
        out_specs=pl.BlockSpec((None, None, n_rows, HEAD_DIM), lambda b, h: (b, h, 0, 0)),
        out_shape=jax.ShapeDtypeStruct((batch, 2 * N_KV_A, n_rows, HEAD_DIM), jnp.bfloat16),
        scratch_shapes=[pltpu.VMEM((s + CMP_STRIDE, HEAD_DIM), jnp.float32)],
        compiler_params=_cparams(("parallel", "parallel")),
        name="compress",
    )(proj, pe, w1, w2)


def _kq(k, q):
    return lax.dot_general(k, q, (((1,), (1,)), ((), ())),
                           preferred_element_type=jnp.float32)


class _AttnJob:
    def __init__(self, q, chunks):
        self.q, self.chunks = q, chunks
        self.s, self.m, self.acc = [], None, None

    def _score(self, c):
        k_fn, _, mask_fn, _ = self.chunks[c]
        s = _kq(k_fn(), self.q)
        if mask_fn is not None:
            s = mask_fn(s)
        self.s.append(s)
        cm = jnp.max(s, axis=0, keepdims=True)
        self.m = cm if self.m is None else jnp.maximum(self.m, cm)

    def _value(self, c):
        _, vt_fn, _, weight_fn = self.chunks[c]
        p = jnp.exp2(self.s[c] - self.m)
        if weight_fn is not None:
            p = p * weight_fn()
        part = jnp.dot(vt_fn(), p.astype(jnp.bfloat16), preferred_element_type=jnp.float32)
        self.acc = part if self.acc is None else self.acc + part

    def score_tasks(self):
        return [functools.partial(self._score, c) for c in range(len(self.chunks))]

    def value_tasks(self):
        return [functools.partial(self._value, c) for c in range(len(self.chunks))]

    def result(self):
        return self.acc[:HEAD_DIM, :] * (1.0 / self.acc[HEAD_DIM:HEAD_DIM + 1, :])


def _round_robin(task_lists):
    for i in range(max(len(t) for t in task_lists)):
        for tasks in task_lists:
            if i < len(tasks):
                tasks[i]()


def _run_pipelined(streams, on_done):
    n = len(streams[0])
    _round_robin([jobs[0].score_tasks() for jobs in streams])
    for k in range(n):
        lists = []
        for jobs in streams:
            lists.append(jobs[k].value_tasks())
            if k + 1 < n:
                lists.append(jobs[k + 1].score_tasks())
        _round_robin(lists)
        for si, jobs in enumerate(streams):
            on_done(si, k, jobs[k])


def _store_values_transposed(src_ref, dst_ref):
    s_len = src_ref.shape[0]
    for c in range(s_len // TK):
        blk = src_ref[c * TK:(c + 1) * TK, :].astype(jnp.float32)
        dst_ref[0:HEAD_DIM, c * TK:(c + 1) * TK] = blk.T.astype(jnp.bfloat16)
    row = lax.broadcasted_iota(jnp.int32, (BF16_SUBLANES, s_len), 0)
    dst_ref[HEAD_DIM:, :] = jnp.where(row == 0, 1.0, 0.0).astype(jnp.bfloat16)


def _silu(z):
    return z * jax.nn.sigmoid(z)


def _topk_active(q0):
    return q0 + TQ > TOP_N * SEL_LEN


def _nsa_selection_bias(p_sum, q0, s_len):
    n_sel = s_len // SEL_LEN
    n_cmp = s_len // CMP_STRIDE - CMP_LEN // CMP_STRIDE + 1
    oj = lax.broadcasted_iota(jnp.int32, (n_sel, HEAD_DIM), 0)
    oc = lax.broadcasted_iota(jnp.int32, (n_sel, HEAD_DIM), 1)
    ovl = jnp.where(oc * CMP_STRIDE < oj * SEL_LEN + SEL_LEN,
                    jnp.where(oc * CMP_STRIDE + CMP_LEN > oj * SEL_LEN, 1.0, 0.0), 0.0)
    ovl = jnp.where(oc < n_cmp, ovl, 0.0).astype(jnp.bfloat16)
    p_hi = p_sum.astype(jnp.bfloat16)
    p_lo = (p_sum - p_hi.astype(jnp.float32)).astype(jnp.bfloat16)
    imp = (jnp.dot(ovl, p_hi, preferred_element_type=jnp.float32)
           + jnp.dot(ovl, p_lo, preferred_element_type=jnp.float32))
    j_row = lax.broadcasted_iota(jnp.int32, (n_sel, TQ), 0)
    t_sel = q0 + lax.broadcasted_iota(jnp.int32, (n_sel, TQ), 1)
    cur = lax.shift_right_arithmetic(t_sel, int(math.log2(SEL_LEN)))
    forced = (j_row == 0) | (j_row == cur) | (j_row == cur - 1)
    imp = jnp.where(forced, imp + FORCE_BONUS, imp)
    imp = jnp.where(j_row * SEL_LEN <= t_sel, imp, -jnp.inf)
    rank = jnp.zeros((n_sel, TQ), jnp.float32)
    for i in range(n_sel):
        vi = imp[i:i + 1, :]
        rank = rank + jnp.where(j_row > i, jnp.where(vi >= imp, 1.0, 0.0),
                                jnp.where(vi > imp, 1.0, 0.0))
    return jnp.where(rank < TOP_N, 0.0, NEG)


def _nsa_kernel(q_ref, z_ref, g_ref, bg_ref, kc_ref, vc_ref, ksl_ref, vsl_ref, kw_ref, vw_ref,
                cos_ref, sin_ref, o_ref,
                ksl_rot, kw_rot, vsl_t, vw_t, qrot_s, ocmp_s, selb_s, gate_s):
    s_len = ksl_ref.shape[0]
    n_q = s_len // TQ

    ksl_rot[...] = _rope(ksl_ref[...].astype(jnp.float32), cos_ref[...], sin_ref[...]).astype(jnp.bfloat16)
    kw_rot[...] = _rope(kw_ref[...].astype(jnp.float32), cos_ref[...], sin_ref[...]).astype(jnp.bfloat16)
    _store_values_transposed(vsl_ref, vsl_t)
    _store_values_transposed(vw_ref, vw_t)
    kc = kc_ref[...]
    vc_t = vc_ref[...].astype(jnp.float32).T.astype(jnp.bfloat16)
    bg = bg_ref[...]

    c_row = lax.broadcasted_iota(jnp.int32, (HEAD_DIM, TQ), 0)
    lane_q = lax.broadcasted_iota(jnp.int32, (HEAD_DIM, TQ), 1)
    lane_one = lax.broadcasted_iota(jnp.int32, (1, TQ), 1)
    for qi in range(n_q):
        q0 = qi * TQ
        rows = slice(q0, q0 + TQ)
        cos_q = cos_ref[rows, :]
        sin_q = sin_ref[rows, :]
        q_raw = []
        for g in range(GROUP_A):
            qf = q_ref[rows, g * HEAD_DIM:(g + 1) * HEAD_DIM].astype(jnp.float32) * QSCALE
            q_raw.append(qf.astype(jnp.bfloat16))
            qrot_s[g, rows, :] = _rope(qf, cos_q, sin_q).astype(jnp.bfloat16)
        c_valid = c_row * CMP_STRIDE + (CMP_LEN - 1) <= q0 + lane_q
        any_valid = jnp.where(q0 + lane_one >= CMP_LEN - 1, 1.0, 0.0)
        sc4 = _kq(kc, jnp.concatenate(q_raw, axis=0))
        p_heads = []
        p_sum = jnp.zeros((HEAD_DIM, TQ), jnp.float32)
        for g in range(GROUP_A):
            sc = jnp.where(c_valid, sc4[:, g * TQ:(g + 1) * TQ], NEG)
            e = jnp.exp2(sc - jnp.max(sc, axis=0, keepdims=True))
            p = e * (any_valid / jnp.sum(e, axis=0, keepdims=True))
            p_heads.append(p.astype(jnp.bfloat16))
            p_sum = p_sum + p
        o_cmp4 = jnp.dot(vc_t, jnp.concatenate(p_heads, axis=1),
                         preferred_element_type=jnp.float32)
        for g in range(GROUP_A):
            ocmp_s[g, :, rows] = o_cmp4[:, g * TQ:(g + 1) * TQ]
        if _topk_active(q0):
            selb_s[qi] = _nsa_selection_bias(p_sum, q0, s_len)
        gate_s[:, rows] = jax.nn.sigmoid(g_ref[rows, :].astype(jnp.float32) + bg).T

    diag_rc = (lax.broadcasted_iota(jnp.int32, (TK, TQ), 0)
               - lax.broadcasted_iota(jnp.int32, (TK, TQ), 1))

    def slc_mask(qi, c, s):
        if _topk_active(qi * TQ):
            s = s + jnp.concatenate(
                [jnp.broadcast_to(selb_s[qi, c * SEL_PER_TK + a:c * SEL_PER_TK + a + 1, :],
                                  (SEL_LEN, TQ)) for a in range(SEL_PER_TK)], axis=0)
        if c == qi:
            s = jnp.where(diag_rc <= 0, s, NEG)
        return s

    def win_mask(qi, c, s):
        if c == qi:
            return jnp.where(diag_rc <= 0, s, NEG)
        if (qi - c) * TK + TQ - 1 > WIN - 1:
            return jnp.where(diag_rc >= (qi - c) * TK - (WIN - 1), s, NEG)
        return s

    def head_body(g, carry):
        col = pl.ds(pl.multiple_of(g * HEAD_DIM, HEAD_DIM), HEAD_DIM)
        slc_jobs, win_jobs = [], []
        for qi in range(n_q):
            q = qrot_s[g, qi * TQ:(qi + 1) * TQ, :]
            first_win = max(0, (qi * TQ - (WIN - 1)) // TK)
            for jobs, k_ref_, v_ref_, mask, c_lo in ((slc_jobs, ksl_rot, vsl_t, slc_mask, 0),
                                                     (win_jobs, kw_rot, vw_t, win_mask, first_win)):
                jobs.append(_AttnJob(q, [
                    (functools.partial(lambda r, c: r[c * TK:(c + 1) * TK, :], k_ref_, c),
                     functools.partial(lambda r, c: r[:, c * TK:(c + 1) * TK], v_ref_, c),
                     functools.partial(mask, qi, c), None)
                    for c in range(c_lo, qi + 1)]))

        def on_done(si, qi, job):
            if si == 0:
                return
            rows = slice(qi * TQ, (qi + 1) * TQ)
            o_t = (gate_s[pl.ds(3 * g, 1), rows] * ocmp_s[g, :, rows]
                   + gate_s[pl.ds(3 * g + 1, 1), rows] * slc_jobs[qi].result()
                   + gate_s[pl.ds(3 * g + 2, 1), rows] * job.result())
            z = z_ref[rows, col].astype(jnp.float32)
            o_ref[rows, col] = (o_t.T * _silu(z)).astype(o_ref.dtype)

        _run_pipelined([slc_jobs, win_jobs], on_done)
        return carry

    lax.fori_loop(0, GROUP_A, head_body, 0)


def _nsa(front, back, cmp_kv, b_gate_blocks, cos, sin, batch, s):
    gw = GROUP_A * HEAD_DIM
    full = lambda col0: pl.BlockSpec((s, HEAD_DIM), lambda b, h: (b, col0 + h))
    return pl.pallas_call(
        _nsa_kernel,
        grid=(batch, N_KV_A),
        in_specs=[
            pl.BlockSpec((s, gw), lambda b, h: (b, H_QA // GROUP_A + h)),
            pl.BlockSpec((s, gw), lambda b, h: (b, H_ZA // GROUP_A + h)),
            full(H_GATE),
            pl.BlockSpec((None, 1, HEAD_DIM), lambda b, h: (h, 0, 0)),
            pl.BlockSpec((None, None, s // CMP_STRIDE, HEAD_DIM), lambda b, h: (b, h, 0, 0)),
            pl.BlockSpec((None, None, s // CMP_STRIDE, HEAD_DIM), lambda b, h: (b, N_KV_A + h, 0, 0)),
            full(H_KSL), full(H_VSL), full(H_KW), full(H_VW),
            pl.BlockSpec((None, s, HEAD_DIM), lambda b, h: (b, 0, 0)),
            pl.BlockSpec((None, s, HEAD_DIM), lambda b, h: (b, 0, 0)),
        ],
        out_specs=pl.BlockSpec((s, gw), lambda b, h: (b, h)),
        out_shape=jax.ShapeDtypeStruct((batch * s, N_HEADS_A * HEAD_DIM), jnp.bfloat16),
        scratch_shapes=[pltpu.VMEM((s, HEAD_DIM), jnp.bfloat16),
                        pltpu.VMEM((s, HEAD_DIM), jnp.bfloat16),
                        pltpu.VMEM((V_ROWS, s), jnp.bfloat16),
                        pltpu.VMEM((V_ROWS, s), jnp.bfloat16),
                        pltpu.VMEM((GROUP_A, s, HEAD_DIM), jnp.bfloat16),
                        pltpu.VMEM((GROUP_A, HEAD_DIM, s), jnp.float32),
                        pltpu.VMEM((s // TQ, s // SEL_LEN, TQ), jnp.float32),
                        pltpu.VMEM((HEAD_DIM, s), jnp.float32)],
        compiler_params=_cparams(("parallel", "parallel")),
        name="nsa_mixer",
    )(front, front, front, b_gate_blocks, cmp_kv, cmp_kv, back, back, back, back, cos, sin)


def _dilated_multiplicity(dist):
    cnt = jnp.zeros(dist.shape, jnp.float32)
    for window, dil in DILATIONS:
        cnt = cnt + jnp.where((dist & (dil - 1)) == 0,
                              jnp.where(dist <= (window // dil) * dil, 1.0, 0.0), 0.0)
    return jnp.where(dist >= 0, cnt, 0.0)


def _dilated_table_kernel(bias_ref, cnt_ref):
    s_len = bias_ref.shape[0]
    dist = (s_len - TQ + lax.broadcasted_iota(jnp.int32, (s_len, TQ), 1)
            - lax.broadcasted_iota(jnp.int32, (s_len, TQ), 0))
    cnt = _dilated_multiplicity(dist)
    cnt_ref[...] = cnt
    bias_ref[...] = jnp.where(cnt > 0.5, 0.0, NEG)


def _dilated_tables(s):
    out = jax.ShapeDtypeStruct((s, TQ), jnp.float32)
    return pl.pallas_call(
        _dilated_table_kernel,
        out_shape=[out, out],
        compiler_params=pltpu.CompilerParams(vmem_limit_bytes=VMEM_LIMIT),
        name="dilated_tables",
    )()


def _dilated_single_count(delta):
    nearest = delta * TK - (TK - 1)
    return all(nearest > (w // d) * d for w, d in DILATIONS[:-1])


DIL_HEADS = 2


def _dilated_kernel(q_ref, z_ref, k_ref, v_ref, cos_ref, sin_ref, bias_ref, cnt_ref, o_ref,
                    k_rot, v_t):
    s_len = k_ref.shape[0]
    n_q = s_len // TQ

    def strip(ref, delta):
        r0 = (n_q - 1 - delta) * TK
        return ref[r0:r0 + TK, :]

    streams = []
    for hh in range(DIL_HEADS):
        cols = slice(hh * HEAD_DIM, (hh + 1) * HEAD_DIM)
        k_rot[hh] = _rope(k_ref[:, cols].astype(jnp.float32), cos_ref[...], sin_ref[...]).astype(jnp.bfloat16)
        _store_values_transposed(v_ref.at[:, cols], v_t.at[hh])
        jobs = []
        for qi in range(n_q):
            rows = slice(qi * TQ, (qi + 1) * TQ)
            qf = q_ref[rows, cols].astype(jnp.float32) * QSCALE
            q_rot = _rope(qf, cos_ref[rows, :], sin_ref[rows, :]).astype(jnp.bfloat16)
            jobs.append(_AttnJob(q_rot, [
                (functools.partial(lambda hh, c: k_rot[hh, c * TK:(c + 1) * TK, :], hh, c),
                 functools.partial(lambda hh, c: v_t[hh, :, c * TK:(c + 1) * TK], hh, c),
                 functools.partial(lambda d, s: s + strip(bias_ref, d), qi - c),
                 None if _dilated_single_count(qi - c) else functools.partial(strip, cnt_ref, qi - c))
                for c in range(qi + 1)]))
        streams.append(jobs)

    def on_done(hh, qi, job):
        rows = slice(qi * TQ, (qi + 1) * TQ)
        cols = slice(hh * HEAD_DIM, (hh + 1) * HEAD_DIM)
        z = z_ref[rows, cols].astype(jnp.float32)
        o_ref[rows, cols] = (job.result().T * _silu(z)).astype(o_ref.dtype)

    _run_pipelined(streams, on_done)


def _dilated(proj, cos, sin, batch, s):
    bias_tab, cnt_tab = _dilated_tables(s)
    w = DIL_HEADS * HEAD_DIM
    assert all(h % DIL_HEADS == 0 for h in (H_QB, H_ZB, H_KB, H_VB, N_HEADS_B))
    heads = lambda h0: pl.BlockSpec((s, w), lambda b, h: (b, h0 // DIL_HEADS + h))
    return pl.pallas_call(
        _dilated_kernel,
        grid=(batch, N_HEADS_B // DIL_HEADS),
        in_specs=[
            heads(H_QB), heads(H_ZB), heads(H_KB), heads(H_VB),
            pl.BlockSpec((None, s, HEAD_DIM), lambda b, h: (b, 0, 0)),
            pl.BlockSpec((None, s, HEAD_DIM), lambda b, h: (b, 0, 0)),
            pl.BlockSpec((s, TQ), lambda b, h: (0, 0)),
            pl.BlockSpec((s, TQ), lambda b, h: (0, 0)),
        ],
        out_specs=pl.BlockSpec((s, w), lambda b, h: (b, h)),
        out_shape=jax.ShapeDtypeStruct((batch * s, N_HEADS_B * HEAD_DIM), jnp.bfloat16),
        scratch_shapes=[pltpu.VMEM((DIL_HEADS, s, HEAD_DIM), jnp.bfloat16),
                        pltpu.VMEM((DIL_HEADS, V_ROWS, s), jnp.bfloat16)],
        compiler_params=_cparams(("parallel", "parallel")),
        name="dilated_mixer",
    )(proj, proj, proj, proj, cos, sin, bias_tab, cnt_tab)


def _out_proj_kernel(ma_ref, mb_ref, wa_ref, wb_ref, x_ref, nw_ref, o_ref):
    y = (jnp.dot(ma_ref[...], wa_ref[...], preferred_element_type=jnp.float32)
         + jnp.dot(mb_ref[...], wb_ref[...], preferred_element_type=jnp.float32))
    ms = jnp.mean(y * y, axis=-1, keepdims=True)
    o_ref[...] = x_ref[...] + y * lax.rsqrt(ms + EPS) * nw_ref[...]


def _out_proj(mix_a, mix_b, w, x2d, post_w, tm=256):
    m, d = x2d.shape
    ka, kb = mix_a.shape[1], mix_b.shape[1]
    assert ka == kb and w.shape[0] == ka + kb
    resident = dict(pipeline_mode=pl.Buffered(1))
    return pl.pallas_call(
        _out_proj_kernel,
        grid=(m // tm,),
        in_specs=[
            pl.BlockSpec((tm, ka), lambda i: (i, 0)),
            pl.BlockSpec((tm, kb), lambda i: (i, 0)),
            pl.BlockSpec((ka, d), lambda i: (0, 0), **resident),
            pl.BlockSpec((kb, d), lambda i: (1, 0), **resident),
            pl.BlockSpec((tm, d), lambda i: (i, 0)),
            pl.BlockSpec((1, d), lambda i: (0, 0)),
        ],
        out_specs=pl.BlockSpec((tm, d), lambda i: (i, 0)),
        out_shape=jax.ShapeDtypeStruct((m, d), jnp.float32),
        compiler_params=pltpu.CompilerParams(dimension_semantics=("parallel",),
                                             vmem_limit_bytes=OUT_PROJ_VMEM_LIMIT),
        name="out_proj_norm_residual",
    )(mix_a, mix_b, w, w, x2d, post_w.reshape(1, d))


PACK_ROWS = GROUP_A * HEAD_DIM
GATE_ROW0 = 2 * N_HEADS_A * HEAD_DIM
GATE_PER_KV = 3 * GROUP_A
BACK_ROW0 = GATE_ROW0 + 3 * N_HEADS_A
PACK_PLAIN = GATE_ROW0 // PACK_ROWS


def _pack_front_kernel(a_ref, g_ref, o_ref):
    j = pl.program_id(0)

    @pl.when(j < PACK_PLAIN)
    def _():
        o_ref[...] = a_ref[...].astype(o_ref.dtype)

    @pl.when(j == PACK_PLAIN)
    def _():
        r = lax.broadcasted_iota(jnp.int32, (PACK_ROWS, HEAD_DIM), 0)
        c = lax.broadcasted_iota(jnp.int32, (PACK_ROWS, HEAD_DIM), 1)
        h, i = r // HEAD_DIM, r % HEAD_DIM
        place = jnp.where(i < GATE_PER_KV, jnp.where(c == GATE_PER_KV * h + i, 1.0, 0.0), 0.0)
        o_ref[...] = jnp.dot(place.astype(jnp.bfloat16), g_ref[...].astype(jnp.bfloat16),
                             preferred_element_type=jnp.float32).astype(o_ref.dtype)


def _pack_back_kernel(a_ref, o_ref):
    o_ref[...] = a_ref[...].astype(o_ref.dtype)


def _pack_w_in(w_t):
    d = w_t.shape[1]
    front = pl.pallas_call(
        _pack_front_kernel,
        grid=(PACK_PLAIN + 1,),
        in_specs=[pl.BlockSpec((PACK_ROWS, d), lambda j: (jnp.minimum(j, PACK_PLAIN - 1), 0)),
                  pl.BlockSpec((HEAD_DIM, d), lambda j: (GATE_ROW0 // HEAD_DIM, 0))],
        out_specs=pl.BlockSpec((PACK_ROWS, d), lambda j: (j, 0)),
        out_shape=jax.ShapeDtypeStruct((N_FRONT_HEADS * HEAD_DIM, d), jnp.bfloat16),
        compiler_params=_cparams(("arbitrary",)),
        name="pack_w_front",
    )(w_t, w_t)
    n_back = N_BACK_HEADS * HEAD_DIM
    back = pl.pallas_call(
        _pack_back_kernel,
        grid=(n_back // PACK_ROWS,),
        in_specs=[pl.BlockSpec((pl.Element(PACK_ROWS), pl.Element(d)),
                               lambda j: (pl.multiple_of(BACK_ROW0 + PACK_ROWS * j, F32_SUBLANES), 0))],
        out_specs=pl.BlockSpec((PACK_ROWS, d), lambda j: (j, 0)),
        out_shape=jax.ShapeDtypeStruct((n_back, d), jnp.bfloat16),
        compiler_params=_cparams(("arbitrary",)),
        name="pack_w_back",
    )(w_t)
    return front, back


def _layer(x, cos, sin, pre_w, post_w, w_in, b_gate, pe_k, pe_v, wk1, wk2, wv1, wv2, w_out):
    b, s, d = x.shape
    x2d = x.reshape(b * s, d)
    h = _rmsnorm(x2d, pre_w)
    w_front, w_back = _pack_w_in(w_in.T)
    front = _in_proj(h, w_front)
    back = _in_proj(h, w_back)

    pe = jnp.stack([pe_k, pe_v])
    w1 = jnp.stack([wk1, wv1]).reshape(2, CMP_LEN, HEAD_DIM, CMP_HIDDEN).astype(jnp.bfloat16)
    w2 = jnp.stack([wk2, wv2]).astype(jnp.bfloat16)
    cmp_kv = _compress(back, b, s, pe, w1, w2)

    per = 3 * GROUP_A
    bg = jnp.pad(b_gate.reshape(N_KV_A, 1, per), ((0, 0), (0, 0), (0, HEAD_DIM - per)))
    mix_a = _nsa(front, back, cmp_kv, bg, cos, sin, b, s)
    mix_b = _dilated(back, cos, sin, b, s)

    out = _out_proj(mix_a, mix_b, w_out.astype(jnp.bfloat16), x2d, post_w)
    return out.reshape(b, s, d)


def kernel(x, positions, pre_norm_w, post_norm_w, w_in, b_gate, cmp_pe_k, cmp_pe_v,
           cmp_wk1, cmp_wk2, cmp_wv1, cmp_wv2, w_out):
    cos, sin = _rope_tables(positions)
    for l in range(pre_norm_w.shape[0]):
        x = _layer(x, cos, sin, pre_norm_w[l], post_norm_w[l], w_in[l], b_gate[l],
                   cmp_pe_k[l], cmp_pe_v[l], cmp_wk1[l], cmp_wk2[l], cmp_wv1[l], cmp_wv2[l],
                   w_out[l])
    return x
```

```python
import functools
import math

import jax
import jax.numpy as jnp
from jax import lax
from jax.experimental import pallas as pl
from jax.experimental.pallas import tpu as pltpu

D_MODEL = 4096
HEAD_DIM = 128
HALF = HEAD_DIM // 2
N_HEADS_A = 16
N_KV_A = 4
GROUP_A = 4
N_HEADS_B = 16
CMP_LEN = 32
CMP_STRIDE = 16
CMP_HIDDEN = 256
SEL_LEN = 64
TOP_N = 16
WIN = 512
DILATIONS = ((128, 1), (512, 4), (2048, 16))
ROPE_THETA = 10000.0
EPS = 1e-6
FORCE_BONUS = 1e4
NEG = -1e30
SCALE = HEAD_DIM ** -0.5
QSCALE = SCALE * math.log2(math.e)

H_QA = 0
H_ZA = H_QA + N_HEADS_A
H_GATE = H_ZA + N_HEADS_A
N_FRONT_HEADS = H_GATE + N_KV_A
H_KC = 0
H_VC = H_KC + N_KV_A
H_KSL = H_VC + N_KV_A
H_VSL = H_KSL + N_KV_A
H_KW = H_VSL + N_KV_A
H_VW = H_KW + N_KV_A
H_QB = H_VW + N_KV_A
H_KB = H_QB + N_HEADS_B
H_VB = H_KB + N_HEADS_B
H_ZB = H_VB + N_HEADS_B
N_BACK_HEADS = H_ZB + N_HEADS_B

TQ = 256
TK = 256
SEL_PER_TK = TK // SEL_LEN
F32_SUBLANES = 8
BF16_SUBLANES = 16
V_ROWS = HEAD_DIM + BF16_SUBLANES
VMEM_LIMIT = 56 * 1024 * 1024
OUT_PROJ_VMEM_LIMIT = 60 * 1024 * 1024


def _cparams(sem):
    return pltpu.CompilerParams(dimension_semantics=sem, vmem_limit_bytes=VMEM_LIMIT)


def _rmsnorm_kernel(x_ref, w_ref, o_ref):
    x = x_ref[...]
    ms = jnp.mean(x * x, axis=-1, keepdims=True)
    o_ref[...] = (x * lax.rsqrt(ms + EPS) * w_ref[...]).astype(o_ref.dtype)


def _rmsnorm(x2d, w, tm=256):
    m, d = x2d.shape
    return pl.pallas_call(
        _rmsnorm_kernel,
        grid=(m // tm,),
        in_specs=[pl.BlockSpec((tm, d), lambda i: (i, 0)),
                  pl.BlockSpec((1, d), lambda i: (0, 0))],
        out_specs=pl.BlockSpec((tm, d), lambda i: (i, 0)),
        out_shape=jax.ShapeDtypeStruct((m, d), jnp.bfloat16),
        compiler_params=_cparams(("parallel",)),
        name="pre_rmsnorm",
    )(x2d, w.reshape(1, d))


PROJ_TN = GROUP_A * HEAD_DIM
GATE_ROW0 = 2 * N_HEADS_A * HEAD_DIM
GATE_PER_KV = 3 * GROUP_A
BACK_ROW0 = GATE_ROW0 + 3 * N_HEADS_A
PROJ_PLAIN = GATE_ROW0 // PROJ_TN


def _nt_dot(x, w):
    return lax.dot_general(x, w, (((1,), (1,)), ((), ())), preferred_element_type=jnp.float32)


def _in_proj_front_kernel(x_ref, w_ref, g_ref, o_ref):
    j = pl.program_id(1)

    @pl.when(j < PROJ_PLAIN)
    def _():
        o_ref[...] = _nt_dot(x_ref[...], w_ref[...].astype(jnp.bfloat16)).astype(o_ref.dtype)

    @pl.when(j == PROJ_PLAIN)
    def _():
        r = lax.broadcasted_iota(jnp.int32, (PROJ_TN, HEAD_DIM), 0)
        c = lax.broadcasted_iota(jnp.int32, (PROJ_TN, HEAD_DIM), 1)
        h, i = r // HEAD_DIM, r % HEAD_DIM
        place = jnp.where(i < GATE_PER_KV, jnp.where(c == GATE_PER_KV * h + i, 1.0, 0.0), 0.0)
        w = jnp.dot(place.astype(jnp.bfloat16), g_ref[...].astype(jnp.bfloat16),
                    preferred_element_type=jnp.float32).astype(jnp.bfloat16)
        o_ref[...] = _nt_dot(x_ref[...], w).astype(o_ref.dtype)


def _in_proj_back_kernel(x_ref, w_ref, o_ref):
    o_ref[...] = _nt_dot(x_ref[...], w_ref[...].astype(jnp.bfloat16)).astype(o_ref.dtype)


def _in_proj(h, w_t, tm=1024):
    m, k = h.shape
    x_spec = pl.BlockSpec((tm, k), lambda i, j: (i, 0))
    out_spec = pl.BlockSpec((tm, PROJ_TN), lambda i, j: (i, j))
    front = pl.pallas_call(
        _in_proj_front_kernel,
        grid=(m // tm, PROJ_PLAIN + 1),
        in_specs=[x_spec,
                  pl.BlockSpec((PROJ_TN, k), lambda i, j: (jnp.minimum(j, PROJ_PLAIN - 1), 0)),
                  pl.BlockSpec((HEAD_DIM, k), lambda i, j: (GATE_ROW0 // HEAD_DIM, 0))],
        out_specs=out_spec,
        out_shape=jax.ShapeDtypeStruct((m, N_FRONT_HEADS * HEAD_DIM), jnp.bfloat16),
        compiler_params=_cparams(("parallel", "arbitrary")),
        name="in_proj_front",
    )(h, w_t, w_t)
    n_back = N_BACK_HEADS * HEAD_DIM
    back = pl.pallas_call(
        _in_proj_back_kernel,
        grid=(m // tm, n_back // PROJ_TN),
        in_specs=[x_spec,
                  pl.BlockSpec((pl.Element(PROJ_TN), pl.Element(k)),
                               lambda i, j: (pl.multiple_of(BACK_ROW0 + PROJ_TN * j, F32_SUBLANES), 0))],
        out_specs=out_spec,
        out_shape=jax.ShapeDtypeStruct((m, n_back), jnp.bfloat16),
        compiler_params=_cparams(("parallel", "arbitrary")),
        name="in_proj_back",
    )(h, w_t)
    return front, back


def _rope_table_kernel(pos_ref, inv_ref, cos_ref, sin_ref):
    ang = pos_ref[...].astype(jnp.float32) * inv_ref[...]
    lane = lax.broadcasted_iota(jnp.int32, ang.shape, 1)
    cos_ref[...] = jnp.cos(ang)
    sin_ref[...] = jnp.where(lane < HALF, -jnp.sin(ang), jnp.sin(ang))


def _rope_tables(positions):
    b, s = positions.shape
    inv = ROPE_THETA ** (-jnp.arange(HALF, dtype=jnp.float32) / HALF)
    inv_full = jnp.concatenate([inv, inv]).reshape(1, HEAD_DIM)
    out = jax.ShapeDtypeStruct((b, s, HEAD_DIM), jnp.float32)
    return pl.pallas_call(
        _rope_table_kernel,
        grid=(b,),
        in_specs=[pl.BlockSpec((None, s, 1), lambda i: (i, 0, 0)),
                  pl.BlockSpec((1, HEAD_DIM), lambda i: (0, 0))],
        out_specs=[pl.BlockSpec((None, s, HEAD_DIM), lambda i: (i, 0, 0)),
                   pl.BlockSpec((None, s, HEAD_DIM), lambda i: (i, 0, 0))],
        out_shape=[out, out],
        compiler_params=_cparams(("parallel",)),
        name="rope_tables",
    )(positions.reshape(b, s, 1), inv_full)


def _rope(xf, cos, sin_signed):
    return xf * cos + pltpu.roll(xf, HALF, 1) * sin_signed


def _compress_kernel(x_ref, pe_ref, w1_ref, w2_ref, o_ref, xs_ref):
    s = x_ref.shape[0]
    n_rows = s // CMP_STRIDE
    xs_ref[pl.ds(0, s), :] = x_ref[...].astype(jnp.float32)
    xs_ref[pl.ds(s, CMP_STRIDE), :] = jnp.zeros((CMP_STRIDE, HEAD_DIM), jnp.float32)
    acc = jnp.zeros((n_rows, CMP_HIDDEN), jnp.float32)
    for r in range(CMP_LEN):
        xr = xs_ref[pl.ds(r, n_rows, stride=CMP_STRIDE), :] + pe_ref[pl.ds(r, 1), :]
        acc = acc + jnp.dot(xr.astype(jnp.bfloat16), w1_ref[r],
                            preferred_element_type=jnp.float32)
    hid = jax.nn.gelu(acc)
    o_ref[...] = jnp.dot(hid.astype(jnp.bfloat16), w2_ref[...],
                         preferred_element_type=jnp.float32).astype(o_ref.dtype)


def _compress(proj, batch, s, pe, w1, w2):
    n_rows = s // CMP_STRIDE
    return pl.pallas_call(
        _compress_kernel,
        grid=(batch, 2 * N_KV_A),
        in_specs=[
            pl.BlockSpec((s, HEAD_DIM), lambda b, h: (b, H_KC + h)),
            pl.BlockSpec((None, CMP_LEN, HEAD_DIM), lambda b, h: (h // N_KV_A, 0, 0)),
            pl.BlockSpec((None, CMP_LEN, HEAD_DIM, CMP_HIDDEN), lambda b, h: (h // N_KV_A, 0, 0, 0)),
            pl.BlockSpec((None, CMP_HIDDEN, HEAD_DIM), lambda b, h: (h // N_KV_A, 0, 0)),
        ],
        out_specs=pl.BlockSpec((None, None, n_rows, HEAD_DIM), lambda b, h: (b, h, 0, 0)),
        out_shape=jax.ShapeDtypeStruct((batch, 2 * N_KV_A, n_rows, HEAD_DIM), jnp.bfloat16),
        scratch_shapes=[pltpu.VMEM((s + CMP_STRIDE, HEAD_DIM), jnp.float32)],
        compiler_params=_cparams(("parallel", "parallel")),
        name="compress",
    )(proj, pe, w1, w2)


def _kq(k, q):
    return lax.dot_general(k, q, (((1,), (1,)), ((), ())),
                           preferred_element_type=jnp.float32)


class _AttnJob:
    def __init__(self, q, chunks):
        self.q, self.chunks = q, chunks
        self.s, self.m, self.acc = [], None, None

    def _score(self, c):
        k_fn, _, mask_fn, _ = self.chunks[c]
        s = _kq(k_fn(), self.q)
        if mask_fn is not None:
            s = mask_fn(s)
        self.s.append(s)
        cm = jnp.max(s, axis=0, keepdims=True)
        self.m = cm if self.m is None else jnp.maximum(self.m, cm)

    def _value(self, c):
        _, vt_fn, _, weight_fn = self.chunks[c]
        p = jnp.exp2(self.s[c] - self.m)
        if weight_fn is not None:
            p = p * weight_fn()
        part = jnp.dot(vt_fn(), p.astype(jnp.bfloat16), preferred_element_type=jnp.float32)
        self.acc = part if self.acc is None else self.acc + part

    def score_tasks(self):
        return [functools.partial(self._score, c) for c in range(len(self.chunks))]

    def value_tasks(self):
        return [functools.partial(self._value, c) for c in range(len(self.chunks))]

    def result(self):
        return self.acc[:HEAD_DIM, :] * (1.0 / self.acc[HEAD_DIM:HEAD_DIM + 1, :])


def _round_robin(task_lists):
    for i in range(max(len(t) for t in task_lists)):
        for tasks in task_lists:
            if i < len(tasks):
                tasks[i]()


def _run_pipelined(streams, on_done):
    n = len(streams[0])
    _round_robin([jobs[0].score_tasks() for jobs in streams])
    for k in range(n):
        lists = []
        for jobs in streams:
            lists.append(jobs[k].value_tasks())
            if k + 1 < n:
                lists.append(jobs[k + 1].score_tasks())
        _round_robin(lists)
        for si, jobs in enumerate(streams):
            on_done(si, k, jobs[k])


def _store_values_transposed(src_ref, dst_ref):
    s_len = src_ref.shape[0]
    for c in range(s_len // TK):
        blk = src_ref[c * TK:(c + 1) * TK, :].astype(jnp.float32)
        dst_ref[0:HEAD_DIM, c * TK:(c + 1) * TK] = blk.T.astype(jnp.bfloat16)
    row = lax.broadcasted_iota(jnp.int32, (BF16_SUBLANES, s_len), 0)
    dst_ref[HEAD_DIM:, :] = jnp.where(row == 0, 1.0, 0.0).astype(jnp.bfloat16)


def _silu(z):
    return z * jax.nn.sigmoid(z)


def _topk_active(q0):
    return q0 + TQ > TOP_N * SEL_LEN


def _nsa_selection_bias(p_sum, q0, s_len):
    n_sel = s_len // SEL_LEN
    n_cmp = s_len // CMP_STRIDE - CMP_LEN // CMP_STRIDE + 1
    oj = lax.broadcasted_iota(jnp.int32, (n_sel, HEAD_DIM), 0)
    oc = lax.broadcasted_iota(jnp.int32, (n_sel, HEAD_DIM), 1)
    ovl = jnp.where(oc * CMP_STRIDE < oj * SEL_LEN + SEL_LEN,
                    jnp.where(oc * CMP_STRIDE + CMP_LEN > oj * SEL_LEN, 1.0, 0.0), 0.0)
    ovl = jnp.where(oc < n_cmp, ovl, 0.0).astype(jnp.bfloat16)
    p_hi = p_sum.astype(jnp.bfloat16)
    p_lo = (p_sum - p_hi.astype(jnp.float32)).astype(jnp.bfloat16)
    imp = (jnp.dot(ovl, p_hi, preferred_element_type=jnp.float32)
           + jnp.dot(ovl, p_lo, preferred_element_type=jnp.float32))
    j_row = lax.broadcasted_iota(jnp.int32, (n_sel, TQ), 0)
    t_sel = q0 + lax.broadcasted_iota(jnp.int32, (n_sel, TQ), 1)
    cur = lax.shift_right_arithmetic(t_sel, int(math.log2(SEL_LEN)))
    forced = (j_row == 0) | (j_row == cur) | (j_row == cur - 1)
    imp = jnp.where(forced, imp + FORCE_BONUS, imp)
    imp = jnp.where(j_row * SEL_LEN <= t_sel, imp, -jnp.inf)
    rank = jnp.zeros((n_sel, TQ), jnp.float32)
    for i in range(n_sel):
        vi = imp[i:i + 1, :]
        rank = rank + jnp.where(j_row > i, jnp.where(vi >= imp, 1.0, 0.0),
                                jnp.where(vi > imp, 1.0, 0.0))
    return jnp.where(rank < TOP_N, 0.0, NEG)


def _nsa_kernel(q_ref, z_ref, g_ref, bg_ref, kc_ref, vc_ref, ksl_ref, vsl_ref, kw_ref, vw_ref,
                cos_ref, sin_ref, o_ref,
                ksl_rot, kw_rot, vsl_t, vw_t, qrot_s, ocmp_s, selb_s, gate_s):
    s_len = ksl_ref.shape[0]
    n_q = s_len // TQ

    ksl_rot[...] = _rope(ksl_ref[...].astype(jnp.float32), cos_ref[...], sin_ref[...]).astype(jnp.bfloat16)
    kw_rot[...] = _rope(kw_ref[...].astype(jnp.float32), cos_ref[...], sin_ref[...]).astype(jnp.bfloat16)
    _store_values_transposed(vsl_ref, vsl_t)
    _store_values_transposed(vw_ref, vw_t)
    kc = kc_ref[...]
    vc_t = vc_ref[...].astype(jnp.float32).T.astype(jnp.bfloat16)
    bg = bg_ref[...]

    c_row = lax.broadcasted_iota(jnp.int32, (HEAD_DIM, TQ), 0)
    lane_q = lax.broadcasted_iota(jnp.int32, (HEAD_DIM, TQ), 1)
    lane_one = lax.broadcasted_iota(jnp.int32, (1, TQ), 1)
    for qi in range(n_q):
        q0 = qi * TQ
        rows = slice(q0, q0 + TQ)
        cos_q = cos_ref[rows, :]
        sin_q = sin_ref[rows, :]
        q_raw = []
        for g in range(GROUP_A):
            qf = q_ref[rows, g * HEAD_DIM:(g + 1) * HEAD_DIM].astype(jnp.float32) * QSCALE
            q_raw.append(qf.astype(jnp.bfloat16))
            qrot_s[g, rows, :] = _rope(qf, cos_q, sin_q).astype(jnp.bfloat16)
        c_valid = c_row * CMP_STRIDE + (CMP_LEN - 1) <= q0 + lane_q
        any_valid = jnp.where(q0 + lane_one >= CMP_LEN - 1, 1.0, 0.0)
        sc4 = _kq(kc, jnp.concatenate(q_raw, axis=0))
        p_heads = []
        p_sum = jnp.zeros((HEAD_DIM, TQ), jnp.float32)
        for g in range(GROUP_A):
            sc = jnp.where(c_valid, sc4[:, g * TQ:(g + 1) * TQ], NEG)
            e = jnp.exp2(sc - jnp.max(sc, axis=0, keepdims=True))
            p = e * (any_valid / jnp.sum(e, axis=0, keepdims=True))
            p_heads.append(p.astype(jnp.bfloat16))
            p_sum = p_sum + p
        o_cmp4 = jnp.dot(vc_t, jnp.concatenate(p_heads, axis=1),
                         preferred_element_type=jnp.float32)
        for g in range(GROUP_A):
            ocmp_s[g, :, rows] = o_cmp4[:, g * TQ:(g + 1) * TQ]
        if _topk_active(q0):
            selb_s[qi] = _nsa_selection_bias(p_sum, q0, s_len)
        gate_s[:, rows] = jax.nn.sigmoid(g_ref[rows, :].astype(jnp.float32) + bg).T

    diag_rc = (lax.broadcasted_iota(jnp.int32, (TK, TQ), 0)
               - lax.broadcasted_iota(jnp.int32, (TK, TQ), 1))

    def slc_mask(qi, c, s):
        if _topk_active(qi * TQ):
            s = s + jnp.concatenate(
                [jnp.broadcast_to(selb_s[qi, c * SEL_PER_TK + a:c * SEL_PER_TK + a + 1, :],
                                  (SEL_LEN, TQ)) for a in range(SEL_PER_TK)], axis=0)
        if c == qi:
            s = jnp.where(diag_rc <= 0, s, NEG)
        return s

    def win_mask(qi, c, s):
        if c == qi:
            return jnp.where(diag_rc <= 0, s, NEG)
        if (qi - c) * TK + TQ - 1 > WIN - 1:
            return jnp.where(diag_rc >= (qi - c) * TK - (WIN - 1), s, NEG)
        return s

    def head_body(g, carry):
        col = pl.ds(pl.multiple_of(g * HEAD_DIM, HEAD_DIM), HEAD_DIM)
        slc_jobs, win_jobs = [], []
        for qi in range(n_q):
            q = qrot_s[g, qi * TQ:(qi + 1) * TQ, :]
            first_win = max(0, (qi * TQ - (WIN - 1)) // TK)
            for jobs, k_ref_, v_ref_, mask, c_lo in ((slc_jobs, ksl_rot, vsl_t, slc_mask, 0),
                                                     (win_jobs, kw_rot, vw_t, win_mask, first_win)):
                jobs.append(_AttnJob(q, [
                    (functools.partial(lambda r, c: r[c * TK:(c + 1) * TK, :], k_ref_, c),
                     functools.partial(lambda r, c: r[:, c * TK:(c + 1) * TK], v_ref_, c),
                     functools.partial(mask, qi, c), None)
                    for c in range(c_lo, qi + 1)]))

        def on_done(si, qi, job):
            if si == 0:
                return
            rows = slice(qi * TQ, (qi + 1) * TQ)
            o_t = (gate_s[pl.ds(3 * g, 1), rows] * ocmp_s[g, :, rows]
                   + gate_s[pl.ds(3 * g + 1, 1), rows] * slc_jobs[qi].result()
                   + gate_s[pl.ds(3 * g + 2, 1), rows] * job.result())
            z = z_ref[rows, col].astype(jnp.float32)
            o_ref[rows, col] = (o_t.T * _silu(z)).astype(o_ref.dtype)

        _run_pipelined([slc_jobs, win_jobs], on_done)
        return carry

    lax.fori_loop(0, GROUP_A, head_body, 0)


def _nsa(front, back, cmp_kv, b_gate_blocks, cos, sin, batch, s):
    gw = GROUP_A * HEAD_DIM
    full = lambda col0: pl.BlockSpec((s, HEAD_DIM), lambda b, h: (b, col0 + h))
    return pl.pallas_call(
        _nsa_kernel,
        grid=(batch, N_KV_A),
        in_specs=[
            pl.BlockSpec((s, gw), lambda b, h: (b, H_QA // GROUP_A + h)),
            pl.BlockSpec((s, gw), lambda b, h: (b, H_ZA // GROUP_A + h)),
            full(H_GATE),
            pl.BlockSpec((None, 1, HEAD_DIM), lambda b, h: (h, 0, 0)),
            pl.BlockSpec((None, None, s // CMP_STRIDE, HEAD_DIM), lambda b, h: (b, h, 0, 0)),
            pl.BlockSpec((None, None, s // CMP_STRIDE, HEAD_DIM), lambda b, h: (b, N_KV_A + h, 0, 0)),
            full(H_KSL), full(H_VSL), full(H_KW), full(H_VW),
            pl.BlockSpec((None, s, HEAD_DIM), lambda b, h: (b, 0, 0)),
            pl.BlockSpec((None, s, HEAD_DIM), lambda b, h: (b, 0, 0)),
        ],
        out_specs=pl.BlockSpec((s, gw), lambda b, h: (b, h)),
        out_shape=jax.ShapeDtypeStruct((batch * s, N_HEADS_A * HEAD_DIM), jnp.bfloat16),
        scratch_shapes=[pltpu.VMEM((s, HEAD_DIM), jnp.bfloat16),
                        pltpu.VMEM((s, HEAD_DIM), jnp.bfloat16),
                        pltpu.VMEM((V_ROWS, s), jnp.bfloat16),
                        pltpu.VMEM((V_ROWS, s), jnp.bfloat16),
                        pltpu.VMEM((GROUP_A, s, HEAD_DIM), jnp.bfloat16),
                        pltpu.VMEM((GROUP_A, HEAD_DIM, s), jnp.float32),
                        pltpu.VMEM((s // TQ, s // SEL_LEN, TQ), jnp.float32),
                        pltpu.VMEM((HEAD_DIM, s), jnp.float32)],
        compiler_params=_cparams(("parallel", "parallel")),
        name="nsa_mixer",
    )(front, front, front, b_gate_blocks, cmp_kv, cmp_kv, back, back, back, back, cos, sin)


def _dilated_multiplicity(dist):
    cnt = jnp.zeros(dist.shape, jnp.float32)
    for window, dil in DILATIONS:
        cnt = cnt + jnp.where((dist & (dil - 1)) == 0,
                              jnp.where(dist <= (window // dil) * dil, 1.0, 0.0), 0.0)
    return jnp.where(dist >= 0, cnt, 0.0)


def _dilated_table_kernel(bias_ref, cnt_ref):
    s_len = bias_ref.shape[0]
    dist = (s_len - TQ + lax.broadcasted_iota(jnp.int32, (s_len, TQ), 1)
            - lax.broadcasted_iota(jnp.int32, (s_len, TQ), 0))
    cnt = _dilated_multiplicity(dist)
    cnt_ref[...] = cnt
    bias_ref[...] = jnp.where(cnt > 0.5, 0.0, NEG)


def _dilated_tables(s):
    out = jax.ShapeDtypeStruct((s, TQ), jnp.float32)
    return pl.pallas_call(
        _dilated_table_kernel,
        out_shape=[out, out],
        compiler_params=pltpu.CompilerParams(vmem_limit_bytes=VMEM_LIMIT),
        name="dilated_tables",
    )()


def _dilated_single_count(delta):
    nearest = delta * TK - (TK - 1)
    return all(nearest > (w // d) * d for w, d in DILATIONS[:-1])


DIL_HEADS = 2


def _dilated_kernel(q_ref, z_ref, k_ref, v_ref, cos_ref, sin_ref, bias_ref, cnt_ref, o_ref,
                    k_rot, v_t):
    s_len = k_ref.shape[0]
    n_q = s_len // TQ

    def strip(ref, delta):
        r0 = (n_q - 1 - delta) * TK
        return ref[r0:r0 + TK, :]

    streams = []
    for hh in range(DIL_HEADS):
        cols = slice(hh * HEAD_DIM, (hh + 1) * HEAD_DIM)
        k_rot[hh] = _rope(k_ref[:, cols].astype(jnp.float32), cos_ref[...], sin_ref[...]).astype(jnp.bfloat16)
        _store_values_transposed(v_ref.at[:, cols], v_t.at[hh])
        jobs = []
        for qi in range(n_q):
            rows = slice(qi * TQ, (qi + 1) * TQ)
            qf = q_ref[rows, cols].astype(jnp.float32) * QSCALE
            q_rot = _rope(qf, cos_ref[rows, :], sin_ref[rows, :]).astype(jnp.bfloat16)
            jobs.append(_AttnJob(q_rot, [
                (functools.partial(lambda hh, c: k_rot[hh, c * TK:(c + 1) * TK, :], hh, c),
                 functools.partial(lambda hh, c: v_t[hh, :, c * TK:(c + 1) * TK], hh, c),
                 functools.partial(lambda d, s: s + strip(bias_ref, d), qi - c),
                 None if _dilated_single_count(qi - c) else functools.partial(strip, cnt_ref, qi - c))
                for c in range(qi + 1)]))
        streams.append(jobs)

    def on_done(hh, qi, job):
        rows = slice(qi * TQ, (qi + 1) * TQ)
        cols = slice(hh * HEAD_DIM, (hh + 1) * HEAD_DIM)
        z = z_ref[rows, cols].astype(jnp.float32)
        o_ref[rows, cols] = (job.result().T * _silu(z)).astype(o_ref.dtype)

    _run_pipelined(streams, on_done)


def _dilated(proj, cos, sin, batch, s):
    bias_tab, cnt_tab = _dilated_tables(s)
    w = DIL_HEADS * HEAD_DIM
    assert all(h % DIL_HEADS == 0 for h in (H_QB, H_ZB, H_KB, H_VB, N_HEADS_B))
    heads = lambda h0: pl.BlockSpec((s, w), lambda b, h: (b, h0 // DIL_HEADS + h))
    return pl.pallas_call(
        _dilated_kernel,
        grid=(batch, N_HEADS_B // DIL_HEADS),
        in_specs=[
            heads(H_QB), heads(H_ZB), heads(H_KB), heads(H_VB),
            pl.BlockSpec((None, s, HEAD_DIM), lambda b, h: (b, 0, 0)),
            pl.BlockSpec((None, s, HEAD_DIM), lambda b, h: (b, 0, 0)),
            pl.BlockSpec((s, TQ), lambda b, h: (0, 0)),
            pl.BlockSpec((s, TQ), lambda b, h: (0, 0)),
        ],
        out_specs=pl.BlockSpec((s, w), lambda b, h: (b, h)),
        out_shape=jax.ShapeDtypeStruct((batch * s, N_HEADS_B * HEAD_DIM), jnp.bfloat16),
        scratch_shapes=[pltpu.VMEM((DIL_HEADS, s, HEAD_DIM), jnp.bfloat16),
                        pltpu.VMEM((DIL_HEADS, V_ROWS, s), jnp.bfloat16)],
        compiler_params=_cparams(("parallel", "parallel")),
        name="dilated_mixer",
    )(proj, proj, proj, proj, cos, sin, bias_tab, cnt_tab)


def _out_proj_kernel(ma_ref, mb_ref, wa_ref, wb_ref, x_ref, nw_ref, o_ref):
    y = (jnp.dot(ma_ref[...], wa_ref[...], preferred_element_type=jnp.float32)
         + jnp.dot(mb_ref[...], wb_ref[...], preferred_element_type=jnp.float32))
    ms = jnp.mean(y * y, axis=-1, keepdims=True)
    o_ref[...] = x_ref[...] + y * lax.rsqrt(ms + EPS) * nw_ref[...]


def _out_proj(mix_a, mix_b, w, x2d, post_w, tm=256):
    m, d = x2d.shape
    ka, kb = mix_a.shape[1], mix_b.shape[1]
    assert ka == kb and w.shape[0] == ka + kb
    resident = dict(pipeline_mode=pl.Buffered(1))
    return pl.pallas_call(
        _out_proj_kernel,
        grid=(m // tm,),
        in_specs=[
            pl.BlockSpec((tm, ka), lambda i: (i, 0)),
            pl.BlockSpec((tm, kb), lambda i: (i, 0)),
            pl.BlockSpec((ka, d), lambda i: (0, 0), **resident),
            pl.BlockSpec((kb, d), lambda i: (1, 0), **resident),
            pl.BlockSpec((tm, d), lambda i: (i, 0)),
            pl.BlockSpec((1, d), lambda i: (0, 0)),
        ],
        out_specs=pl.BlockSpec((tm, d), lambda i: (i, 0)),
        out_shape=jax.ShapeDtypeStruct((m, d), jnp.float32),
        compiler_params=pltpu.CompilerParams(dimension_semantics=("parallel",),
                                             vmem_limit_bytes=OUT_PROJ_VMEM_LIMIT),
        name="out_proj_norm_residual",
    )(mix_a, mix_b, w, w, x2d, post_w.reshape(1, d))


def _layer(x, cos, sin, pre_w, post_w, w_in, b_gate, pe_k, pe_v, wk1, wk2, wv1, wv2, w_out):
    b, s, d = x.shape
    x2d = x.reshape(b * s, d)
    h = _rmsnorm(x2d, pre_w)
    front, back = _in_proj(h, w_in.T)

    pe = jnp.stack([pe_k, pe_v])
    w1 = jnp.stack([wk1, wv1]).reshape(2, CMP_LEN, HEAD_DIM, CMP_HIDDEN).astype(jnp.bfloat16)
    w2 = jnp.stack([wk2, wv2]).astype(jnp.bfloat16)
    cmp_kv = _compress(back, b, s, pe, w1, w2)

    per = 3 * GROUP_A
    bg = jnp.pad(b_gate.reshape(N_KV_A, 1, per), ((0, 0), (0, 0), (0, HEAD_DIM - per)))
    mix_a = _nsa(front, back, cmp_kv, bg, cos, sin, b, s)
    mix_b = _dilated(back, cos, sin, b, s)

    out = _out_proj(mix_a, mix_b, w_out.astype(jnp.bfloat16), x2d, post_w)
    return out.reshape(b, s, d)


def kernel(x, positions, pre_norm_w, post_norm_w, w_in, b_gate, cmp_pe_k, cmp_pe_v,
           cmp_wk1, cmp_wk2, cmp_wv1, cmp_wv2, w_out):
    cos, sin = _rope_tables(positions)
    for l in range(pre_norm_w.shape[0]):
        x = _layer(x, cos, sin, pre_norm_w[l], post_norm_w[l], w_in[l], b_gate[l],
                   cmp_pe_k[l], cmp_pe_v[l], cmp_wk1[l], cmp_wk2[l], cmp_wv1[l], cmp_wv2[l],
                   w_out[l])
    return x
```

```python
import functools
import math

import jax
import jax.numpy as jnp
from jax import lax
from jax.experimental import pallas as pl
from jax.experimental.pallas import tpu as pltpu

D_MODEL = 4096
HEAD_DIM = 128
HALF = HEAD_DIM // 2
N_HEADS_A = 16
N_KV_A = 4
GROUP_A = 4
N_HEADS_B = 16
CMP_LEN = 32
CMP_STRIDE = 16
CMP_HIDDEN = 256
SEL_LEN = 64
TOP_N = 16
WIN = 512
DILATIONS = ((128, 1), (512, 4), (2048, 16))
ROPE_THETA = 10000.0
EPS = 1e-6
FORCE_BONUS = 1e4
NEG = -1e30
SCALE = HEAD_DIM ** -0.5
QSCALE = SCALE * math.log2(math.e)

H_QA = 0
H_ZA = H_QA + N_HEADS_A
H_GATE = H_ZA + N_HEADS_A
N_FRONT_HEADS = H_GATE + N_KV_A
H_KC = 0
H_VC = H_KC + N_KV_A
H_KSL = H_VC + N_KV_A
H_VSL = H_KSL + N_KV_A
H_KW = H_VSL + N_KV_A
H_VW = H_KW + N_KV_A
H_QB = H_VW + N_KV_A
H_KB = H_QB + N_HEADS_B
H_VB = H_KB + N_HEADS_B
H_ZB = H_VB + N_HEADS_B
N_BACK_HEADS = H_ZB + N_HEADS_B

TQ = 256
TK = 256
SEL_PER_TK = TK // SEL_LEN
F32_SUBLANES = 8
BF16_SUBLANES = 16
V_ROWS = HEAD_DIM + BF16_SUBLANES
NSA_SPLIT = 4
VMEM_LIMIT = 56 * 1024 * 1024
OUT_PROJ_VMEM_LIMIT = 60 * 1024 * 1024


def _cparams(sem):
    return pltpu.CompilerParams(dimension_semantics=sem, vmem_limit_bytes=VMEM_LIMIT)


def _rmsnorm_kernel(x_ref, w_ref, o_ref):
    x = x_ref[...]
    ms = jnp.mean(x * x, axis=-1, keepdims=True)
    o_ref[...] = (x * lax.rsqrt(ms + EPS) * w_ref[...]).astype(o_ref.dtype)


def _rmsnorm(x2d, w, tm=256):
    m, d = x2d.shape
    return pl.pallas_call(
        _rmsnorm_kernel,
        grid=(m // tm,),
        in_specs=[pl.BlockSpec((tm, d), lambda i: (i, 0)),
                  pl.BlockSpec((1, d), lambda i: (0, 0))],
        out_specs=pl.BlockSpec((tm, d), lambda i: (i, 0)),
        out_shape=jax.ShapeDtypeStruct((m, d), jnp.bfloat16),
        compiler_params=_cparams(("parallel",)),
        name="pre_rmsnorm",
    )(x2d, w.reshape(1, d))


PROJ_TN = GROUP_A * HEAD_DIM
GATE_ROW0 = 2 * N_HEADS_A * HEAD_DIM
GATE_PER_KV = 3 * GROUP_A
BACK_ROW0 = GATE_ROW0 + 3 * N_HEADS_A
PROJ_PLAIN = GATE_ROW0 // PROJ_TN


def _nt_dot(x, w):
    return lax.dot_general(x, w, (((1,), (1,)), ((), ())), preferred_element_type=jnp.float32)


def _in_proj_front_kernel(x_ref, w_ref, g_ref, o_ref):
    j = pl.program_id(1)

    @pl.when(j < PROJ_PLAIN)
    def _():
        o_ref[...] = _nt_dot(x_ref[...], w_ref[...].astype(jnp.bfloat16)).astype(o_ref.dtype)

    @pl.when(j == PROJ_PLAIN)
    def _():
        r = lax.broadcasted_iota(jnp.int32, (PROJ_TN, HEAD_DIM), 0)
        c = lax.broadcasted_iota(jnp.int32, (PROJ_TN, HEAD_DIM), 1)
        h, i = r // HEAD_DIM, r % HEAD_DIM
        place = jnp.where(i < GATE_PER_KV, jnp.where(c == GATE_PER_KV * h + i, 1.0, 0.0), 0.0)
        w = jnp.dot(place.astype(jnp.bfloat16), g_ref[...].astype(jnp.bfloat16),
                    preferred_element_type=jnp.float32).astype(jnp.bfloat16)
        o_ref[...] = _nt_dot(x_ref[...], w).astype(o_ref.dtype)


def _in_proj_back_kernel(x_ref, w_ref, o_ref):
    o_ref[...] = _nt_dot(x_ref[...], w_ref[...].astype(jnp.bfloat16)).astype(o_ref.dtype)


def _in_proj(h, w_t, tm=2048):
    m, k = h.shape
    x_spec = pl.BlockSpec((tm, k), lambda i, j: (i, 0), pipeline_mode=pl.Buffered(1))
    out_spec = pl.BlockSpec((tm, PROJ_TN), lambda i, j: (i, j))
    front = pl.pallas_call(
        _in_proj_front_kernel,
        grid=(m // tm, PROJ_PLAIN + 1),
        in_specs=[x_spec,
                  pl.BlockSpec((PROJ_TN, k), lambda i, j: (jnp.minimum(j, PROJ_PLAIN - 1), 0)),
                  pl.BlockSpec((HEAD_DIM, k), lambda i, j: (GATE_ROW0 // HEAD_DIM, 0))],
        out_specs=out_spec,
        out_shape=jax.ShapeDtypeStruct((m, N_FRONT_HEADS * HEAD_DIM), jnp.bfloat16),
        compiler_params=_cparams(("parallel", "arbitrary")),
        name="in_proj_front",
    )(h, w_t, w_t)
    n_back = N_BACK_HEADS * HEAD_DIM
    back = pl.pallas_call(
        _in_proj_back_kernel,
        grid=(m // tm, n_back // PROJ_TN),
        in_specs=[x_spec,
                  pl.BlockSpec((pl.Element(PROJ_TN), pl.Element(k)),
                               lambda i, j: (pl.multiple_of(BACK_ROW0 + PROJ_TN * j, F32_SUBLANES), 0))],
        out_specs=out_spec,
        out_shape=jax.ShapeDtypeStruct((m, n_back), jnp.bfloat16),
        compiler_params=_cparams(("parallel", "arbitrary")),
        name="in_proj_back",
    )(h, w_t)
    return front, back


def _rope_table_kernel(pos_ref, inv_ref, cos_ref, sin_ref):
    ang = pos_ref[...].astype(jnp.float32) * inv_ref[...]
    lane = lax.broadcasted_iota(jnp.int32, ang.shape, 1)
    cos_ref[...] = jnp.cos(ang)
    sin_ref[...] = jnp.where(lane < HALF, -jnp.sin(ang), jnp.sin(ang))


def _rope_tables(positions):
    b, s = positions.shape
    inv = ROPE_THETA ** (-jnp.arange(HALF, dtype=jnp.float32) / HALF)
    inv_full = jnp.concatenate([inv, inv]).reshape(1, HEAD_DIM)
    out = jax.ShapeDtypeStruct((b, s, HEAD_DIM), jnp.float32)
    return pl.pallas_call(
        _rope_table_kernel,
        grid=(b,),
        in_specs=[pl.BlockSpec((None, s, 1), lambda i: (i, 0, 0)),
                  pl.BlockSpec((1, HEAD_DIM), lambda i: (0, 0))],
        out_specs=[pl.BlockSpec((None, s, HEAD_DIM), lambda i: (i, 0, 0)),
                   pl.BlockSpec((None, s, HEAD_DIM), lambda i: (i, 0, 0))],
        out_shape=[out, out],
        compiler_params=_cparams(("parallel",)),
        name="rope_tables",
    )(positions.reshape(b, s, 1), inv_full)


def _rope(xf, cos, sin_signed):
    return xf * cos + pltpu.roll(xf, HALF, 1) * sin_signed


def _compress_kernel(x_ref, pe_ref, w1_ref, w2_ref, o_ref, xs_ref):
    s = x_ref.shape[0]
    n_rows = s // CMP_STRIDE
    xs_ref[pl.ds(0, s), :] = x_ref[...].astype(jnp.float32)
    xs_ref[pl.ds(s, CMP_STRIDE), :] = jnp.zeros((CMP_STRIDE, HEAD_DIM), jnp.float32)
    acc = jnp.zeros((n_rows, CMP_HIDDEN), jnp.float32)
    for r in range(CMP_LEN):
        xr = xs_ref[pl.ds(r, n_rows, stride=CMP_STRIDE), :] + pe_ref[pl.ds(r, 1), :]
        acc = acc + jnp.dot(xr.astype(jnp.bfloat16), w1_ref[r],
                            preferred_element_type=jnp.float32)
    hid = jax.nn.gelu(acc)
    o_ref[...] = jnp.dot(hid.astype(jnp.bfloat16), w2_ref[...],
                         preferred_element_type=jnp.float32).astype(o_ref.dtype)


def _compress(proj, batch, s, pe, w1, w2):
    n_rows = s // CMP_STRIDE
    return pl.pallas_call(
        _compress_kernel,
        grid=(batch, 2 * N_KV_A),
        in_specs=[
            pl.BlockSpec((s, HEAD_DIM), lambda b, h: (b, H_KC + h)),
            pl.BlockSpec((None, CMP_LEN, HEAD_DIM), lambda b, h: (h // N_KV_A, 0, 0)),
            pl.BlockSpec((None, CMP_LEN, HEAD_DIM, CMP_HIDDEN), lambda b, h: (h // N_KV_A, 0, 0, 0)),
            pl.BlockSpec((None, CMP_HIDDEN, HEAD_DIM), lambda b, h: (h // N_KV_A, 0, 0)),
        ],
        out_specs=pl.BlockSpec((None, None, n_rows, HEAD_DIM), lambda b, h: (b, h, 0, 0)),
        out_shape=jax.ShapeDtypeStruct((batch, 2 * N_KV_A, n_rows, HEAD_DIM), jnp.bfloat16),
        scratch_shapes=[pltpu.VMEM((s + CMP_STRIDE, HEAD_DIM), jnp.float32)],
        compiler_params=_cparams(("parallel", "parallel")),
        name="compress",
    )(proj, pe, w1, w2)


def _kq(k, q):
    return lax.dot_general(k, q, (((1,), (1,)), ((), ())),
                           preferred_element_type=jnp.float32)


class _AttnJob:
    def __init__(self, q, chunks):
        self.q, self.chunks = q, chunks
        self.s, self.m, self.acc = [], None, None

    def _score(self, c):
        k_fn, _, mask_fn, _ = self.chunks[c]
        s = _kq(k_fn(), self.q)
        if mask_fn is not None:
            s = mask_fn(s)
        self.s.append(s)
        cm = jnp.max(s, axis=0, keepdims=True)
        self.m = cm if self.m is None else jnp.maximum(self.m, cm)

    def _value(self, c):
        _, vt_fn, _, weight_fn = self.chunks[c]
        p = jnp.exp2(self.s[c] - self.m)
        if weight_fn is not None:
            p = p * weight_fn()
        part = jnp.dot(vt_fn(), p.astype(jnp.bfloat16), preferred_element_type=jnp.float32)
        self.acc = part if self.acc is None else self.acc + part

    def score_tasks(self):
        return [functools.partial(self._score, c) for c in range(len(self.chunks))]

    def value_tasks(self):
        return [functools.partial(self._value, c) for c in range(len(self.chunks))]

    def result(self):
        return self.acc[:HEAD_DIM, :] * (1.0 / self.acc[HEAD_DIM:HEAD_DIM + 1, :])


def _round_robin(task_lists):
    for i in range(max(len(t) for t in task_lists)):
        for tasks in task_lists:
            if i < len(tasks):
                tasks[i]()


def _run_pipelined(streams, on_done):
    n = len(streams[0])
    _round_robin([jobs[0].score_tasks() for jobs in streams])
    for k in range(n):
        lists = []
        for jobs in streams:
            lists.append(jobs[k].value_tasks())
            if k + 1 < n:
                lists.append(jobs[k + 1].score_tasks())
        _round_robin(lists)
        for si, jobs in enumerate(streams):
            on_done(si, k, jobs[k])


def _store_values_transposed(src_ref, dst_ref):
    s_len = src_ref.shape[0]
    for c in range(s_len // TK):
        blk = src_ref[c * TK:(c + 1) * TK, :].astype(jnp.float32)
        dst_ref[0:HEAD_DIM, c * TK:(c + 1) * TK] = blk.T.astype(jnp.bfloat16)
    row = lax.broadcasted_iota(jnp.int32, (BF16_SUBLANES, s_len), 0)
    dst_ref[HEAD_DIM:, :] = jnp.where(row == 0, 1.0, 0.0).astype(jnp.bfloat16)


def _silu(z):
    return z * jax.nn.sigmoid(z)


def _topk_active(q0):
    return q0 + TQ > TOP_N * SEL_LEN


def _nsa_selection_bias(p_sum, q0, s_len):
    n_sel = s_len // SEL_LEN
    n_cmp = s_len // CMP_STRIDE - CMP_LEN // CMP_STRIDE + 1
    oj = lax.broadcasted_iota(jnp.int32, (n_sel, HEAD_DIM), 0)
    oc = lax.broadcasted_iota(jnp.int32, (n_sel, HEAD_DIM), 1)
    ovl = jnp.where(oc * CMP_STRIDE < oj * SEL_LEN + SEL_LEN,
                    jnp.where(oc * CMP_STRIDE + CMP_LEN > oj * SEL_LEN, 1.0, 0.0), 0.0)
    ovl = jnp.where(oc < n_cmp, ovl, 0.0).astype(jnp.bfloat16)
    p_hi = p_sum.astype(jnp.bfloat16)
    p_lo = (p_sum - p_hi.astype(jnp.float32)).astype(jnp.bfloat16)
    imp = (jnp.dot(ovl, p_hi, preferred_element_type=jnp.float32)
           + jnp.dot(ovl, p_lo, preferred_element_type=jnp.float32))
    j_row = lax.broadcasted_iota(jnp.int32, (n_sel, TQ), 0)
    t_sel = q0 + lax.broadcasted_iota(jnp.int32, (n_sel, TQ), 1)
    cur = lax.shift_right_arithmetic(t_sel, int(math.log2(SEL_LEN)))
    forced = (j_row == 0) | (j_row == cur) | (j_row == cur - 1)
    imp = jnp.where(forced, imp + FORCE_BONUS, imp)
    imp = jnp.where(j_row * SEL_LEN <= t_sel, imp, -jnp.inf)
    rank = jnp.zeros((n_sel, TQ), jnp.float32)
    for i in range(n_sel):
        vi = imp[i:i + 1, :]
        rank = rank + jnp.where(j_row > i, jnp.where(vi >= imp, 1.0, 0.0),
                                jnp.where(vi > imp, 1.0, 0.0))
    return jnp.where(rank < TOP_N, 0.0, NEG)


def _nsa_kernel(q_ref, z_ref, g_ref, bg_ref, kc_ref, vc_ref, ksl_ref, vsl_ref, kw_ref, vw_ref,
                cos_ref, sin_ref, o_ref,
                ksl_rot, kw_rot, vsl_t, vw_t, qrot_s, ocmp_s, selb_s, gate_s):
    s_len = ksl_ref.shape[0]
    n_q = s_len // TQ

    ksl_rot[...] = _rope(ksl_ref[...].astype(jnp.float32), cos_ref[...], sin_ref[...]).astype(jnp.bfloat16)
    kw_rot[...] = _rope(kw_ref[...].astype(jnp.float32), cos_ref[...], sin_ref[...]).astype(jnp.bfloat16)
    _store_values_transposed(vsl_ref, vsl_t)
    _store_values_transposed(vw_ref, vw_t)
    kc = kc_ref[...]
    vc_t = vc_ref[...].astype(jnp.float32).T.astype(jnp.bfloat16)
    bg = bg_ref[...]

    c_row = lax.broadcasted_iota(jnp.int32, (HEAD_DIM, TQ), 0)
    lane_q = lax.broadcasted_iota(jnp.int32, (HEAD_DIM, TQ), 1)
    lane_one = lax.broadcasted_iota(jnp.int32, (1, TQ), 1)
    for qi in range(n_q):
        q0 = qi * TQ
        rows = slice(q0, q0 + TQ)
        cos_q = cos_ref[rows, :]
        sin_q = sin_ref[rows, :]
        q_raw = []
        for g in range(GROUP_A):
            qf = q_ref[rows, g * HEAD_DIM:(g + 1) * HEAD_DIM].astype(jnp.float32) * QSCALE
            q_raw.append(qf.astype(jnp.bfloat16))
            qrot_s[g, rows, :] = _rope(qf, cos_q, sin_q).astype(jnp.bfloat16)
        c_valid = c_row * CMP_STRIDE + (CMP_LEN - 1) <= q0 + lane_q
        any_valid = jnp.where(q0 + lane_one >= CMP_LEN - 1, 1.0, 0.0)
        sc4 = _kq(kc, jnp.concatenate(q_raw, axis=0))
        p_heads = []
        p_sum = jnp.zeros((HEAD_DIM, TQ), jnp.float32)
        for g in range(GROUP_A):
            sc = jnp.where(c_valid, sc4[:, g * TQ:(g + 1) * TQ], NEG)
            e = jnp.exp2(sc - jnp.max(sc, axis=0, keepdims=True))
            p = e * (any_valid / jnp.sum(e, axis=0, keepdims=True))
            p_heads.append(p.astype(jnp.bfloat16))
            p_sum = p_sum + p
        o_cmp4 = jnp.dot(vc_t, jnp.concatenate(p_heads, axis=1),
                         preferred_element_type=jnp.float32)
        for g in range(GROUP_A):
            ocmp_s[g, :, rows] = o_cmp4[:, g * TQ:(g + 1) * TQ]
        if _topk_active(q0):
            selb_s[qi] = _nsa_selection_bias(p_sum, q0, s_len)
        gate_s[:, rows] = jax.nn.sigmoid(g_ref[rows, :].astype(jnp.float32) + bg).T

    diag_rc = (lax.broadcasted_iota(jnp.int32, (TK, TQ), 0)
               - lax.broadcasted_iota(jnp.int32, (TK, TQ), 1))

    def slc_mask(qi, c, s):
        if _topk_active(qi * TQ):
            s = s + jnp.concatenate(
                [jnp.broadcast_to(selb_s[qi, c * SEL_PER_TK + a:c * SEL_PER_TK + a + 1, :],
                                  (SEL_LEN, TQ)) for a in range(SEL_PER_TK)], axis=0)
        if c == qi:
            s = jnp.where(diag_rc <= 0, s, NEG)
        return s

    def win_mask(qi, c, s):
        if c == qi:
            return jnp.where(diag_rc <= 0, s, NEG)
        if (qi - c) * TK + TQ - 1 > WIN - 1:
            return jnp.where(diag_rc >= (qi - c) * TK - (WIN - 1), s, NEG)
        return s

    def head_body(g, carry):
        col = pl.ds(pl.multiple_of(g * HEAD_DIM, HEAD_DIM), HEAD_DIM)
        slc_jobs, win_jobs = [], []
        for qi in range(n_q):
            q = qrot_s[g, qi * TQ:(qi + 1) * TQ, :]
            first_win = max(0, (qi * TQ - (WIN - 1)) // TK)
            for jobs, k_ref_, v_ref_, mask, c_lo in ((slc_jobs, ksl_rot, vsl_t, slc_mask, 0),
                                                     (win_jobs, kw_rot, vw_t, win_mask, first_win)):
                jobs.append(_AttnJob(q, [
                    (functools.partial(lambda r, c: r[c * TK:(c + 1) * TK, :], k_ref_, c),
                     functools.partial(lambda r, c: r[:, c * TK:(c + 1) * TK], v_ref_, c),
                     functools.partial(mask, qi, c), None)
                    for c in range(c_lo, qi + 1)]))

        def on_done(si, k, job):
            if si % 2 == 0:
                return
            qi = NSA_SPLIT * k + si // 2
            rows = slice(qi * TQ, (qi + 1) * TQ)
            o_t = (gate_s[pl.ds(3 * g, 1), rows] * ocmp_s[g, :, rows]
                   + gate_s[pl.ds(3 * g + 1, 1), rows] * slc_jobs[qi].result()
                   + gate_s[pl.ds(3 * g + 2, 1), rows] * job.result())
            z = z_ref[rows, col].astype(jnp.float32)
            o_ref[rows, col] = (o_t.T * _silu(z)).astype(o_ref.dtype)

        streams = []
        for r in range(NSA_SPLIT):
            streams += [slc_jobs[r::NSA_SPLIT], win_jobs[r::NSA_SPLIT]]
        _run_pipelined(streams, on_done)
        return carry

    lax.fori_loop(0, GROUP_A, head_body, 0)


def _nsa(front, back, cmp_kv, b_gate_blocks, cos, sin, batch, s):
    gw = GROUP_A * HEAD_DIM
    full = lambda col0: pl.BlockSpec((s, HEAD_DIM), lambda b, h: (b, col0 + h))
    return pl.pallas_call(
        _nsa_kernel,
        grid=(batch, N_KV_A),
        in_specs=[
            pl.BlockSpec((s, gw), lambda b, h: (b, H_QA // GROUP_A + h)),
            pl.BlockSpec((s, gw), lambda b, h: (b, H_ZA // GROUP_A + h)),
            full(H_GATE),
            pl.BlockSpec((None, 1, HEAD_DIM), lambda b, h: (h, 0, 0)),
            pl.BlockSpec((None, None, s // CMP_STRIDE, HEAD_DIM), lambda b, h: (b, h, 0, 0)),
            pl.BlockSpec((None, None, s // CMP_STRIDE, HEAD_DIM), lambda b, h: (b, N_KV_A + h, 0, 0)),
            full(H_KSL), full(H_VSL), full(H_KW), full(H_VW),
            pl.BlockSpec((None, s, HEAD_DIM), lambda b, h: (b, 0, 0)),
            pl.BlockSpec((None, s, HEAD_DIM), lambda b, h: (b, 0, 0)),
        ],
        out_specs=pl.BlockSpec((s, gw), lambda b, h: (b, h)),
        out_shape=jax.ShapeDtypeStruct((batch * s, N_HEADS_A * HEAD_DIM), jnp.bfloat16),
        scratch_shapes=[pltpu.VMEM((s, HEAD_DIM), jnp.bfloat16),
                        pltpu.VMEM((s, HEAD_DIM), jnp.bfloat16),
                        pltpu.VMEM((V_ROWS, s), jnp.bfloat16),
                        pltpu.VMEM((V_ROWS, s), jnp.bfloat16),
                        pltpu.VMEM((GROUP_A, s, HEAD_DIM), jnp.bfloat16),
                        pltpu.VMEM((GROUP_A, HEAD_DIM, s), jnp.float32),
                        pltpu.VMEM((s // TQ, s // SEL_LEN, TQ), jnp.float32),
                        pltpu.VMEM((HEAD_DIM, s), jnp.float32)],
        compiler_params=_cparams(("parallel", "parallel")),
        name="nsa_mixer",
    )(front, front, front, b_gate_blocks, cmp_kv, cmp_kv, back, back, back, back, cos, sin)


def _dilated_multiplicity(dist):
    cnt = jnp.zeros(dist.shape, jnp.float32)
    for window, dil in DILATIONS:
        cnt = cnt + jnp.where((dist & (dil - 1)) == 0,
                              jnp.where(dist <= (window // dil) * dil, 1.0, 0.0), 0.0)
    return jnp.where(dist >= 0, cnt, 0.0)


def _dilated_table_kernel(bias_ref, cnt_ref):
    s_len = bias_ref.shape[0]
    dist = (s_len - TQ + lax.broadcasted_iota(jnp.int32, (s_len, TQ), 1)
            - lax.broadcasted_iota(jnp.int32, (s_len, TQ), 0))
    cnt = _dilated_multiplicity(dist)
    cnt_ref[...] = cnt
    bias_ref[...] = jnp.where(cnt > 0.5, 0.0, NEG)


def _dilated_tables(s):
    out = jax.ShapeDtypeStruct((s, TQ), jnp.float32)
    return pl.pallas_call(
        _dilated_table_kernel,
        out_shape=[out, out],
        compiler_params=pltpu.CompilerParams(vmem_limit_bytes=VMEM_LIMIT),
        name="dilated_tables",
    )()


def _dilated_single_count(delta):
    nearest = delta * TK - (TK - 1)
    return all(nearest > (w // d) * d for w, d in DILATIONS[:-1])


DIL_HEADS = 2


def _dilated_kernel(q_ref, z_ref, k_ref, v_ref, cos_ref, sin_ref, bias_ref, cnt_ref, o_ref,
                    k_rot, v_t):
    s_len = k_ref.shape[0]
    n_q = s_len // TQ

    def strip(ref, delta):
        r0 = (n_q - 1 - delta) * TK
        return ref[r0:r0 + TK, :]

    streams = []
    for hh in range(DIL_HEADS):
        cols = slice(hh * HEAD_DIM, (hh + 1) * HEAD_DIM)
        k_rot[hh] = _rope(k_ref[:, cols].astype(jnp.float32), cos_ref[...], sin_ref[...]).astype(jnp.bfloat16)
        _store_values_transposed(v_ref.at[:, cols], v_t.at[hh])
        jobs = []
        for qi in range(n_q):
            rows = slice(qi * TQ, (qi + 1) * TQ)
            qf = q_ref[rows, cols].astype(jnp.float32) * QSCALE
            q_rot = _rope(qf, cos_ref[rows, :], sin_ref[rows, :]).astype(jnp.bfloat16)
            jobs.append(_AttnJob(q_rot, [
                (functools.partial(lambda hh, c: k_rot[hh, c * TK:(c + 1) * TK, :], hh, c),
                 functools.partial(lambda hh, c: v_t[hh, :, c * TK:(c + 1) * TK], hh, c),
                 functools.partial(lambda d, s: s + strip(bias_ref, d), qi - c),
                 None if _dilated_single_count(qi - c) else functools.partial(strip, cnt_ref, qi - c))
                for c in range(qi + 1)]))
        streams.append(jobs)

    def on_done(hh, qi, job):
        rows = slice(qi * TQ, (qi + 1) * TQ)
        cols = slice(hh * HEAD_DIM, (hh + 1) * HEAD_DIM)
        z = z_ref[rows, cols].astype(jnp.float32)
        o_ref[rows, cols] = (job.result().T * _silu(z)).astype(o_ref.dtype)

    _run_pipelined(streams, on_done)


def _dilated(proj, cos, sin, batch, s):
    bias_tab, cnt_tab = _dilated_tables(s)
    w = DIL_HEADS * HEAD_DIM
    assert all(h % DIL_HEADS == 0 for h in (H_QB, H_ZB, H_KB, H_VB, N_HEADS_B))
    heads = lambda h0: pl.BlockSpec((s, w), lambda b, h: (b, h0 // DIL_HEADS + h))
    return pl.pallas_call(
        _dilated_kernel,
        grid=(batch, N_HEADS_B // DIL_HEADS),
        in_specs=[
            heads(H_QB), heads(H_ZB), heads(H_KB), heads(H_VB),
            pl.BlockSpec((None, s, HEAD_DIM), lambda b, h: (b, 0, 0)),
            pl.BlockSpec((None, s, HEAD_DIM), lambda b, h: (b, 0, 0)),
            pl.BlockSpec((s, TQ), lambda b, h: (0, 0)),
            pl.BlockSpec((s, TQ), lambda b, h: (0, 0)),
        ],
        out_specs=pl.BlockSpec((s, w), lambda b, h: (b, h)),
        out_shape=jax.ShapeDtypeStruct((batch * s, N_HEADS_B * HEAD_DIM), jnp.bfloat16),
        scratch_shapes=[pltpu.VMEM((DIL_HEADS, s, HEAD_DIM), jnp.bfloat16),
                        pltpu.VMEM((DIL_HEADS, V_ROWS, s), jnp.bfloat16)],
        compiler_params=_cparams(("parallel", "parallel")),
        name="dilated_mixer",
    )(proj, proj, proj, proj, cos, sin, bias_tab, cnt_tab)


def _out_proj_kernel(ma_ref, mb_ref, wa_ref, wb_ref, x_ref, nw_ref, o_ref):
    y = (jnp.dot(ma_ref[...], wa_ref[...], preferred_element_type=jnp.float32)
         + jnp.dot(mb_ref[...], wb_ref[...], preferred_element_type=jnp.float32))
    ms = jnp.mean(y * y, axis=-1, keepdims=True)
    o_ref[...] = x_ref[...] + y * lax.rsqrt(ms + EPS) * nw_ref[...]


def _out_proj(mix_a, mix_b, w, x2d, post_w, tm=256):
    m, d = x2d.shape
    ka, kb = mix_a.shape[1], mix_b.shape[1]
    assert ka == kb and w.shape[0] == ka + kb
    resident = dict(pipeline_mode=pl.Buffered(1))
    return pl.pallas_call(
        _out_proj_kernel,
        grid=(m // tm,),
        in_specs=[
            pl.BlockSpec((tm, ka), lambda i: (i, 0)),
            pl.BlockSpec((tm, kb), lambda i: (i, 0)),
            pl.BlockSpec((ka, d), lambda i: (0, 0), **resident),
            pl.BlockSpec((kb, d), lambda i: (1, 0), **resident),
            pl.BlockSpec((tm, d), lambda i: (i, 0)),
            pl.BlockSpec((1, d), lambda i: (0, 0)),
        ],
        out_specs=pl.BlockSpec((tm, d), lambda i: (i, 0)),
        out_shape=jax.ShapeDtypeStruct((m, d), jnp.float32),
        compiler_params=pltpu.CompilerParams(dimension_semantics=("parallel",),
                                             vmem_limit_bytes=OUT_PROJ_VMEM_LIMIT),
        name="out_proj_norm_residual",
    )(mix_a, mix_b, w, w, x2d, post_w.reshape(1, d))


def _layer(x, cos, sin, pre_w, post_w, w_in, b_gate, pe_k, pe_v, wk1, wk2, wv1, wv2, w_out):
    b, s, d = x.shape
    x2d = x.reshape(b * s, d)
    h = _rmsnorm(x2d, pre_w)
    front, back = _in_proj(h, w_in.T)

    pe = jnp.stack([pe_k, pe_v])
    w1 = jnp.stack([wk1, wv1]).reshape(2, CMP_LEN, HEAD_DIM, CMP_HIDDEN).astype(jnp.bfloat16)
    w2 = jnp.stack([wk2, wv2]).astype(jnp.bfloat16)
    cmp_kv = _compress(back, b, s, pe, w1, w2)

    per = 3 * GROUP_A
    bg = jnp.pad(b_gate.reshape(N_KV_A, 1, per), ((0, 0), (0, 0), (0, HEAD_DIM - per)))
    mix_a = _nsa(front, back, cmp_kv, bg, cos, sin, b, s)
    mix_b = _dilated(back, cos, sin, b, s)

    out = _out_proj(mix_a, mix_b, w_out.astype(jnp.bfloat16), x2d, post_w)
    return out.reshape(b, s, d)


def kernel(x, positions, pre_norm_w, post_norm_w, w_in, b_gate, cmp_pe_k, cmp_pe_v,
           cmp_wk1, cmp_wk2, cmp_wv1, cmp_wv2, w_out):
    cos, sin = _rope_tables(positions)
    for l in range(pre_norm_w.shape[0]):
        x = _layer(x, cos, sin, pre_norm_w[l], post_norm_w[l], w_in[l], b_gate[l],
                   cmp_pe_k[l], cmp_pe_v[l], cmp_wk1[l], cmp_wk2[l], cmp_wv1[l], cmp_wv2[l],
                   w_out[l])
    return x
```

```python
import functools
import math

import jax
import jax.numpy as jnp
from jax import lax
from jax.experimental import pallas as pl
from jax.experimental.pallas import tpu as pltpu

D_MODEL = 4096
HEAD_DIM = 128
HALF = HEAD_DIM // 2
N_HEADS_A = 16
N_KV_A = 4
GROUP_A = 4
N_HEADS_B = 16
CMP_LEN = 32
CMP_STRIDE = 16
CMP_HIDDEN = 256
SEL_LEN = 64
TOP_N = 16
WIN = 512
DILATIONS = ((128, 1), (512, 4), (2048, 16))
ROPE_THETA = 10000.0
EPS = 1e-6
FORCE_BONUS = 1e4
NEG = -1e30
SCALE = HEAD_DIM ** -0.5
QSCALE = SCALE * math.log2(math.e)

H_QA = 0
H_ZA = H_QA + N_HEADS_A
H_GATE = H_ZA + N_HEADS_A
N_FRONT_HEADS = H_GATE + N_KV_A
H_KC = 0
H_VC = H_KC + N_KV_A
H_KSL = H_VC + N_KV_A
H_VSL = H_KSL + N_KV_A
H_KW = H_VSL + N_KV_A
H_VW = H_KW + N_KV_A
H_QB = H_VW + N_KV_A
H_KB = H_QB + N_HEADS_B
H_VB = H_KB + N_HEADS_B
H_ZB = H_VB + N_HEADS_B
N_BACK_HEADS = H_ZB + N_HEADS_B

TQ = 256
TK = 256
SEL_PER_TK = TK // SEL_LEN
F32_SUBLANES = 8
BF16_SUBLANES = 16
V_ROWS = HEAD_DIM + BF16_SUBLANES
PV_GROUP = 2
NSA_SPLIT = 4
VMEM_LIMIT = 56 * 1024 * 1024
OUT_PROJ_VMEM_LIMIT = 60 * 1024 * 1024


def _cparams(sem):
    return pltpu.CompilerParams(dimension_semantics=sem, vmem_limit_bytes=VMEM_LIMIT)


def _rmsnorm_kernel(x_ref, w_ref, o_ref):
    x = x_ref[...]
    ms = jnp.mean(x * x, axis=-1, keepdims=True)
    o_ref[...] = (x * lax.rsqrt(ms + EPS) * w_ref[...]).astype(o_ref.dtype)


def _rmsnorm(x2d, w, tm=256):
    m, d = x2d.shape
    return pl.pallas_call(
        _rmsnorm_kernel,
        grid=(m // tm,),
        in_specs=[pl.BlockSpec((tm, d), lambda i: (i, 0)),
                  pl.BlockSpec((1, d), lambda i: (0, 0))],
        out_specs=pl.BlockSpec((tm, d), lambda i: (i, 0)),
        out_shape=jax.ShapeDtypeStruct((m, d), jnp.bfloat16),
        compiler_params=_cparams(("parallel",)),
        name="pre_rmsnorm",
    )(x2d, w.reshape(1, d))


PROJ_TN = GROUP_A * HEAD_DIM
GATE_ROW0 = 2 * N_HEADS_A * HEAD_DIM
GATE_PER_KV = 3 * GROUP_A
BACK_ROW0 = GATE_ROW0 + 3 * N_HEADS_A
PROJ_PLAIN = GATE_ROW0 // PROJ_TN


def _nt_dot(x, w):
    return lax.dot_general(x, w, (((1,), (1,)), ((), ())), preferred_element_type=jnp.float32)


def _in_proj_kernel(x_ref, w_ref, front_ref, back_ref):
    j = pl.program_id(1)

    @pl.when(j < PROJ_PLAIN)
    def _():
        front_ref[...] = _nt_dot(x_ref[...], w_ref[...].astype(jnp.bfloat16)).astype(front_ref.dtype)

    @pl.when(j == PROJ_PLAIN)
    def _():
        r = lax.broadcasted_iota(jnp.int32, (PROJ_TN, HEAD_DIM), 0)
        c = lax.broadcasted_iota(jnp.int32, (PROJ_TN, HEAD_DIM), 1)
        h, i = r // HEAD_DIM, r % HEAD_DIM
        place = jnp.where(i < GATE_PER_KV, jnp.where(c == GATE_PER_KV * h + i, 1.0, 0.0), 0.0)
        w = jnp.dot(place.astype(jnp.bfloat16), w_ref[0:HEAD_DIM, :].astype(jnp.bfloat16),
                    preferred_element_type=jnp.float32).astype(jnp.bfloat16)
        front_ref[...] = _nt_dot(x_ref[...], w).astype(front_ref.dtype)

    @pl.when(j > PROJ_PLAIN)
    def _():
        back_ref[...] = _nt_dot(x_ref[...], w_ref[...].astype(jnp.bfloat16)).astype(back_ref.dtype)


def _in_proj(h, w_t, tm=2048):
    m, k = h.shape
    n_back = N_BACK_HEADS * HEAD_DIM
    n_steps = PROJ_PLAIN + 1 + n_back // PROJ_TN

    def w_row(i, j):
        row = PROJ_TN * j - jnp.where(j > PROJ_PLAIN, (PROJ_PLAIN + 1) * PROJ_TN - BACK_ROW0, 0)
        return pl.multiple_of(row, F32_SUBLANES), 0

    return pl.pallas_call(
        _in_proj_kernel,
        grid=(m // tm, n_steps),
        in_specs=[
            pl.BlockSpec((tm, k), lambda i, j: (i, 0), pipeline_mode=pl.Buffered(1)),
            pl.BlockSpec((pl.Element(PROJ_TN), pl.Element(k)), w_row)],
        out_specs=[
            pl.BlockSpec((tm, PROJ_TN), lambda i, j: (i, jnp.minimum(j, PROJ_PLAIN))),
            pl.BlockSpec((tm, PROJ_TN), lambda i, j: (i, jnp.maximum(j - PROJ_PLAIN - 1, 0)))],
        out_shape=[jax.ShapeDtypeStruct((m, N_FRONT_HEADS * HEAD_DIM), jnp.bfloat16),
                   jax.ShapeDtypeStruct((m, n_back), jnp.bfloat16)],
        compiler_params=_cparams(("parallel", "arbitrary")),
        name="in_proj",
    )(h, w_t)


def _rope_table_kernel(pos_ref, inv_ref, cos_ref, sin_ref):
    ang = pos_ref[...].astype(jnp.float32) * inv_ref[...]
    lane = lax.broadcasted_iota(jnp.int32, ang.shape, 1)
    cos_ref[...] = jnp.cos(ang)
    sin_ref[...] = jnp.where(lane < HALF, -jnp.sin(ang), jnp.sin(ang))


def _rope_tables(positions):
    b, s = positions.shape
    inv = ROPE_THETA ** (-jnp.arange(HALF, dtype=jnp.float32) / HALF)
    inv_full = jnp.concatenate([inv, inv]).reshape(1, HEAD_DIM)
    out = jax.ShapeDtypeStruct((b, s, HEAD_DIM), jnp.float32)
    return pl.pallas_call(
        _rope_table_kernel,
        grid=(b,),
        in_specs=[pl.BlockSpec((None, s, 1), lambda i: (i, 0, 0)),
                  pl.BlockSpec((1, HEAD_DIM), lambda i: (0, 0))],
        out_specs=[pl.BlockSpec((None, s, HEAD_DIM), lambda i: (i, 0, 0)),
                   pl.BlockSpec((None, s, HEAD_DIM), lambda i: (i, 0, 0))],
        out_shape=[out, out],
        compiler_params=_cparams(("parallel",)),
        name="rope_tables",
    )(positions.reshape(b, s, 1), inv_full)


def _rope(xf, cos, sin_signed):
    return xf * cos + pltpu.roll(xf, HALF, 1) * sin_signed


def _compress_kernel(x_ref, pe_ref, w1_ref, w2_ref, o_ref, xs_ref):
    s = x_ref.shape[0]
    n_rows = s // CMP_STRIDE
    xs_ref[pl.ds(0, s), :] = x_ref[...].astype(jnp.float32)
    xs_ref[pl.ds(s, CMP_STRIDE), :] = jnp.zeros((CMP_STRIDE, HEAD_DIM), jnp.float32)
    acc = jnp.zeros((n_rows, CMP_HIDDEN), jnp.float32)
    for r in range(CMP_LEN):
        xr = xs_ref[pl.ds(r, n_rows, stride=CMP_STRIDE), :] + pe_ref[pl.ds(r, 1), :]
        acc = acc + jnp.dot(xr.astype(jnp.bfloat16), w1_ref[r],
                            preferred_element_type=jnp.float32)
    hid = jax.nn.gelu(acc)
    o_ref[...] = jnp.dot(hid.astype(jnp.bfloat16), w2_ref[...],
                         preferred_element_type=jnp.float32).astype(o_ref.dtype)


def _compress(proj, batch, s, pe, w1, w2):
    n_rows = s // CMP_STRIDE
    return pl.pallas_call(
        _compress_kernel,
        grid=(batch, 2 * N_KV_A),
        in_specs=[
            pl.BlockSpec((s, HEAD_DIM), lambda b, h: (b, H_KC + h)),
            pl.BlockSpec((None, CMP_LEN, HEAD_DIM), lambda b, h: (h // N_KV_A, 0, 0)),
            pl.BlockSpec((None, CMP_LEN, HEAD_DIM, CMP_HIDDEN), lambda b, h: (h // N_KV_A, 0, 0, 0)),
            pl.BlockSpec((None, CMP_HIDDEN, HEAD_DIM), lambda b, h: (h // N_KV_A, 0, 0)),
        ],
        out_specs=pl.BlockSpec((None, None, n_rows, HEAD_DIM), lambda b, h: (b, h, 0, 0)),
        out_shape=jax.ShapeDtypeStruct((batch, 2 * N_KV_A, n_rows, HEAD_DIM), jnp.bfloat16),
        scratch_shapes=[pltpu.VMEM((s + CMP_STRIDE, HEAD_DIM), jnp.float32)],
        compiler_params=_cparams(("parallel", "parallel")),
        name="compress",
    )(proj, pe, w1, w2)


def _kq(k, q):
    return lax.dot_general(k, q, (((1,), (1,)), ((), ())),
                           preferred_element_type=jnp.float32)


class _AttnJob:
    def __init__(self, q, chunks):
        self.q, self.chunks = q, chunks
        self.s, self.m, self.acc = [], None, None

    def _score(self, c):
        k_fn, _, mask_fn = self.chunks[c]
        s = mask_fn(_kq(k_fn(), self.q))
        self.s.append(s)
        cm = jnp.max(s, axis=0, keepdims=True)
        self.m = cm if self.m is None else jnp.maximum(self.m, cm)

    def _value(self, cs):
        p = jnp.concatenate([jnp.exp2(self.s[c] - self.m).astype(jnp.bfloat16) for c in cs], axis=0)
        vt = jnp.concatenate([self.chunks[c][1]() for c in cs], axis=1)
        part = jnp.dot(vt, p, preferred_element_type=jnp.float32)
        self.acc = part if self.acc is None else self.acc + part

    def score_tasks(self):
        return [functools.partial(self._score, c) for c in range(len(self.chunks))]

    def value_tasks(self):
        n = len(self.chunks)
        return [functools.partial(self._value, range(c, min(c + PV_GROUP, n)))
                for c in range(0, n, PV_GROUP)]

    def result(self):
        return self.acc[:HEAD_DIM, :] * (1.0 / self.acc[HEAD_DIM:HEAD_DIM + 1, :])


def _round_robin(task_lists):
    for i in range(max(len(t) for t in task_lists)):
        for tasks in task_lists:
            if i < len(tasks):
                tasks[i]()


def _run_pipelined(streams, on_done):
    n = len(streams[0])
    _round_robin([jobs[0].score_tasks() for jobs in streams])
    for k in range(n):
        lists = []
        for jobs in streams:
            lists.append(jobs[k].value_tasks())
            if k + 1 < n:
                lists.append(jobs[k + 1].score_tasks())
        _round_robin(lists)
        for si, jobs in enumerate(streams):
            on_done(si, k, jobs[k])


def _store_values_transposed(src_ref, dst_ref):
    s_len = src_ref.shape[0]
    for c in range(s_len // TK):
        blk = src_ref[c * TK:(c + 1) * TK, :].astype(jnp.float32)
        dst_ref[0:HEAD_DIM, c * TK:(c + 1) * TK] = blk.T.astype(jnp.bfloat16)
    row = lax.broadcasted_iota(jnp.int32, (BF16_SUBLANES, s_len), 0)
    dst_ref[HEAD_DIM:, :] = jnp.where(row == 0, 1.0, 0.0).astype(jnp.bfloat16)


def _silu(z):
    return z * jax.nn.sigmoid(z)


def _topk_active(q0):
    return q0 + TQ > TOP_N * SEL_LEN


def _nsa_selection_bias(p_sum, q0, s_len):
    n_sel = s_len // SEL_LEN
    n_cmp = s_len // CMP_STRIDE - CMP_LEN // CMP_STRIDE + 1
    oj = lax.broadcasted_iota(jnp.int32, (n_sel, HEAD_DIM), 0)
    oc = lax.broadcasted_iota(jnp.int32, (n_sel, HEAD_DIM), 1)
    ovl = jnp.where(oc * CMP_STRIDE < oj * SEL_LEN + SEL_LEN,
                    jnp.where(oc * CMP_STRIDE + CMP_LEN > oj * SEL_LEN, 1.0, 0.0), 0.0)
    ovl = jnp.where(oc < n_cmp, ovl, 0.0).astype(jnp.bfloat16)
    p_hi = p_sum.astype(jnp.bfloat16)
    p_lo = (p_sum - p_hi.astype(jnp.float32)).astype(jnp.bfloat16)
    imp = (jnp.dot(ovl, p_hi, preferred_element_type=jnp.float32)
           + jnp.dot(ovl, p_lo, preferred_element_type=jnp.float32))
    j_row = lax.broadcasted_iota(jnp.int32, (n_sel, TQ), 0)
    t_sel = q0 + lax.broadcasted_iota(jnp.int32, (n_sel, TQ), 1)
    cur = lax.shift_right_arithmetic(t_sel, int(math.log2(SEL_LEN)))
    forced = (j_row == 0) | (j_row == cur) | (j_row == cur - 1)
    imp = jnp.where(forced, imp + FORCE_BONUS, imp)
    imp = jnp.where(j_row * SEL_LEN <= t_sel, imp, -jnp.inf)
    rank = jnp.zeros((n_sel, TQ), jnp.float32)
    for i in range(n_sel):
        vi = imp[i:i + 1, :]
        rank = rank + jnp.where(j_row > i, jnp.where(vi >= imp, 1.0, 0.0),
                                jnp.where(vi > imp, 1.0, 0.0))
    return jnp.where(rank < TOP_N, 0.0, NEG)


def _nsa_kernel(q_ref, z_ref, g_ref, bg_ref, kc_ref, vc_ref, ksl_ref, vsl_ref, kw_ref, vw_ref,
                cos_ref, sin_ref, o_ref,
                ksl_rot, kw_rot, vsl_t, vw_t, qrot_s, ocmp_s, selb_s, gate_s):
    s_len = ksl_ref.shape[0]
    n_q = s_len // TQ

    ksl_rot[...] = _rope(ksl_ref[...].astype(jnp.float32), cos_ref[...], sin_ref[...]).astype(jnp.bfloat16)
    kw_rot[...] = _rope(kw_ref[...].astype(jnp.float32), cos_ref[...], sin_ref[...]).astype(jnp.bfloat16)
    _store_values_transposed(vsl_ref, vsl_t)
    _store_values_transposed(vw_ref, vw_t)
    kc = kc_ref[...]
    vc_t = vc_ref[...].astype(jnp.float32).T.astype(jnp.bfloat16)
    bg = bg_ref[...]

    c_row = lax.broadcasted_iota(jnp.int32, (HEAD_DIM, TQ), 0)
    lane_q = lax.broadcasted_iota(jnp.int32, (HEAD_DIM, TQ), 1)
    lane_one = lax.broadcasted_iota(jnp.int32, (1, TQ), 1)
    for qi in range(n_q):
        q0 = qi * TQ
        rows = slice(q0, q0 + TQ)
        cos_q = cos_ref[rows, :]
        sin_q = sin_ref[rows, :]
        q_raw = []
        for g in range(GROUP_A):
            qf = q_ref[rows, g * HEAD_DIM:(g + 1) * HEAD_DIM].astype(jnp.float32) * QSCALE
            q_raw.append(qf.astype(jnp.bfloat16))
            qrot_s[g, rows, :] = _rope(qf, cos_q, sin_q).astype(jnp.bfloat16)
        c_valid = c_row * CMP_STRIDE + (CMP_LEN - 1) <= q0 + lane_q
        any_valid = jnp.where(q0 + lane_one >= CMP_LEN - 1, 1.0, 0.0)
        sc4 = _kq(kc, jnp.concatenate(q_raw, axis=0))
        p_heads = []
        p_sum = jnp.zeros((HEAD_DIM, TQ), jnp.float32)
        for g in range(GROUP_A):
            sc = jnp.where(c_valid, sc4[:, g * TQ:(g + 1) * TQ], NEG)
            e = jnp.exp2(sc - jnp.max(sc, axis=0, keepdims=True))
            p = e * (any_valid / jnp.sum(e, axis=0, keepdims=True))
            p_heads.append(p.astype(jnp.bfloat16))
            p_sum = p_sum + p
        o_cmp4 = jnp.dot(vc_t, jnp.concatenate(p_heads, axis=1),
                         preferred_element_type=jnp.float32)
        for g in range(GROUP_A):
            ocmp_s[g, :, rows] = o_cmp4[:, g * TQ:(g + 1) * TQ]
        if _topk_active(q0):
            selb_s[qi] = _nsa_selection_bias(p_sum, q0, s_len)
        gate_s[:, rows] = jax.nn.sigmoid(g_ref[rows, :].astype(jnp.float32) + bg).T

    diag_rc = (lax.broadcasted_iota(jnp.int32, (TK, TQ), 0)
               - lax.broadcasted_iota(jnp.int32, (TK, TQ), 1))

    def slc_mask(qi, c, s):
        if _topk_active(qi * TQ):
            s = s + jnp.concatenate(
                [jnp.broadcast_to(selb_s[qi, c * SEL_PER_TK + a:c * SEL_PER_TK + a + 1, :],
                                  (SEL_LEN, TQ)) for a in range(SEL_PER_TK)], axis=0)
        if c == qi:
            s = jnp.where(diag_rc <= 0, s, NEG)
        return s

    def win_mask(qi, c, s):
        if c == qi:
            return jnp.where(diag_rc <= 0, s, NEG)
        if (qi - c) * TK + TQ - 1 > WIN - 1:
            return jnp.where(diag_rc >= (qi - c) * TK - (WIN - 1), s, NEG)
        return s

    def head_body(g, carry):
        col = pl.ds(pl.multiple_of(g * HEAD_DIM, HEAD_DIM), HEAD_DIM)
        slc_jobs, win_jobs = [], []
        for qi in range(n_q):
            q = qrot_s[g, qi * TQ:(qi + 1) * TQ, :]
            first_win = max(0, (qi * TQ - (WIN - 1)) // TK)
            for jobs, k_ref_, v_ref_, mask, c_lo in ((slc_jobs, ksl_rot, vsl_t, slc_mask, 0),
                                                     (win_jobs, kw_rot, vw_t, win_mask, first_win)):
                jobs.append(_AttnJob(q, [
                    (functools.partial(lambda r, c: r[c * TK:(c + 1) * TK, :], k_ref_, c),
                     functools.partial(lambda r, c: r[:, c * TK:(c + 1) * TK], v_ref_, c),
                     functools.partial(mask, qi, c))
                    for c in range(c_lo, qi + 1)]))

        def on_done(si, k, job):
            if si % 2 == 0:
                return
            qi = NSA_SPLIT * k + si // 2
            rows = slice(qi * TQ, (qi + 1) * TQ)
            o_t = (gate_s[pl.ds(3 * g, 1), rows] * ocmp_s[g, :, rows]
                   + gate_s[pl.ds(3 * g + 1, 1), rows] * slc_jobs[qi].result()
                   + gate_s[pl.ds(3 * g + 2, 1), rows] * job.result())
            z = z_ref[rows, col].astype(jnp.float32)
            o_ref[rows, col] = (o_t.T * _silu(z)).astype(o_ref.dtype)

        streams = []
        for r in range(NSA_SPLIT):
            streams += [slc_jobs[r::NSA_SPLIT], win_jobs[r::NSA_SPLIT]]
        _run_pipelined(streams, on_done)
        return carry

    lax.fori_loop(0, GROUP_A, head_body, 0)


def _nsa(front, back, cmp_kv, b_gate_blocks, cos, sin, batch, s):
    gw = GROUP_A * HEAD_DIM
    full = lambda col0: pl.BlockSpec((s, HEAD_DIM), lambda b, h: (b, col0 + h))
    return pl.pallas_call(
        _nsa_kernel,
        grid=(batch, N_KV_A),
        in_specs=[
            pl.BlockSpec((s, gw), lambda b, h: (b, H_QA // GROUP_A + h)),
            pl.BlockSpec((s, gw), lambda b, h: (b, H_ZA // GROUP_A + h)),
            full(H_GATE),
            pl.BlockSpec((None, 1, HEAD_DIM), lambda b, h: (h, 0, 0)),
            pl.BlockSpec((None, None, s // CMP_STRIDE, HEAD_DIM), lambda b, h: (b, h, 0, 0)),
            pl.BlockSpec((None, None, s // CMP_STRIDE, HEAD_DIM), lambda b, h: (b, N_KV_A + h, 0, 0)),
            full(H_KSL), full(H_VSL), full(H_KW), full(H_VW),
            pl.BlockSpec((None, s, HEAD_DIM), lambda b, h: (b, 0, 0)),
            pl.BlockSpec((None, s, HEAD_DIM), lambda b, h: (b, 0, 0)),
        ],
        out_specs=pl.BlockSpec((s, gw), lambda b, h: (b, h)),
        out_shape=jax.ShapeDtypeStruct((batch * s, N_HEADS_A * HEAD_DIM), jnp.bfloat16),
        scratch_shapes=[pltpu.VMEM((s, HEAD_DIM), jnp.bfloat16),
                        pltpu.VMEM((s, HEAD_DIM), jnp.bfloat16),
                        pltpu.VMEM((V_ROWS, s), jnp.bfloat16),
                        pltpu.VMEM((V_ROWS, s), jnp.bfloat16),
                        pltpu.VMEM((GROUP_A, s, HEAD_DIM), jnp.bfloat16),
                        pltpu.VMEM((GROUP_A, HEAD_DIM, s), jnp.float32),
                        pltpu.VMEM((s // TQ, s // SEL_LEN, TQ), jnp.float32),
                        pltpu.VMEM((HEAD_DIM, s), jnp.float32)],
        compiler_params=_cparams(("parallel", "parallel")),
        name="nsa_mixer",
    )(front, front, front, b_gate_blocks, cmp_kv, cmp_kv, back, back, back, back, cos, sin)


def _dilated_multiplicity(dist):
    cnt = jnp.zeros(dist.shape, jnp.float32)
    for window, dil in DILATIONS:
        cnt = cnt + jnp.where((dist & (dil - 1)) == 0,
                              jnp.where(dist <= (window // dil) * dil, 1.0, 0.0), 0.0)
    return jnp.where(dist >= 0, cnt, 0.0)


def _dilated_table_kernel(bias_ref):
    s_len = bias_ref.shape[0]
    dist = (s_len - TQ + lax.broadcasted_iota(jnp.int32, (s_len, TQ), 1)
            - lax.broadcasted_iota(jnp.int32, (s_len, TQ), 0))
    cnt = _dilated_multiplicity(dist)
    bias_ref[...] = jnp.where(cnt > 0.5, jnp.log2(jnp.maximum(cnt, 1.0)), NEG)


def _dilated_tables(s):
    return pl.pallas_call(
        _dilated_table_kernel,
        out_shape=jax.ShapeDtypeStruct((s, TQ), jnp.float32),
        compiler_params=pltpu.CompilerParams(vmem_limit_bytes=VMEM_LIMIT),
        name="dilated_tables",
    )()


DIL_HEADS = 2
DIL_SPLIT = 2


def _dilated_kernel(q_ref, z_ref, k_ref, v_ref, cos_ref, sin_ref, bias_ref, o_ref, k_rot, v_t):
    s_len = k_ref.shape[0]
    n_q = s_len // TQ

    def add_bias(delta, s):
        r0 = (n_q - 1 - delta) * TK
        return s + bias_ref[r0:r0 + TK, :]

    streams = []
    for hh in range(DIL_HEADS):
        cols = slice(hh * HEAD_DIM, (hh + 1) * HEAD_DIM)
        k_rot[hh] = _rope(k_ref[:, cols].astype(jnp.float32), cos_ref[...], sin_ref[...]).astype(jnp.bfloat16)
        _store_values_transposed(v_ref.at[:, cols], v_t.at[hh])
        jobs = []
        for qi in range(n_q):
            rows = slice(qi * TQ, (qi + 1) * TQ)
            qf = q_ref[rows, cols].astype(jnp.float32) * QSCALE
            q_rot = _rope(qf, cos_ref[rows, :], sin_ref[rows, :]).astype(jnp.bfloat16)
            jobs.append(_AttnJob(q_rot, [
                (functools.partial(lambda hh, c: k_rot[hh, c * TK:(c + 1) * TK, :], hh, c),
                 functools.partial(lambda hh, c: v_t[hh, :, c * TK:(c + 1) * TK], hh, c),
                 functools.partial(add_bias, qi - c))
                for c in range(qi + 1)]))
        streams += [jobs[r::DIL_SPLIT] for r in range(DIL_SPLIT)]

    def on_done(si, k, job):
        hh, qi = si // DIL_SPLIT, DIL_SPLIT * k + si % DIL_SPLIT
        rows = slice(qi * TQ, (qi + 1) * TQ)
        cols = slice(hh * HEAD_DIM, (hh + 1) * HEAD_DIM)
        z = z_ref[rows, cols].astype(jnp.float32)
        o_ref[rows, cols] = (job.result().T * _silu(z)).astype(o_ref.dtype)

    _run_pipelined(streams, on_done)


def _dilated(proj, cos, sin, batch, s):
    bias_tab = _dilated_tables(s)
    w = DIL_HEADS * HEAD_DIM
    assert all(h % DIL_HEADS == 0 for h in (H_QB, H_ZB, H_KB, H_VB, N_HEADS_B))
    heads = lambda h0: pl.BlockSpec((s, w), lambda b, h: (b, h0 // DIL_HEADS + h))
    return pl.pallas_call(
        _dilated_kernel,
        grid=(batch, N_HEADS_B // DIL_HEADS),
        in_specs=[
            heads(H_QB), heads(H_ZB), heads(H_KB), heads(H_VB),
            pl.BlockSpec((None, s, HEAD_DIM), lambda b, h: (b, 0, 0)),
            pl.BlockSpec((None, s, HEAD_DIM), lambda b, h: (b, 0, 0)),
            pl.BlockSpec((s, TQ), lambda b, h: (0, 0)),
        ],
        out_specs=pl.BlockSpec((s, w), lambda b, h: (b, h)),
        out_shape=jax.ShapeDtypeStruct((batch * s, N_HEADS_B * HEAD_DIM), jnp.bfloat16),
        scratch_shapes=[pltpu.VMEM((DIL_HEADS, s, HEAD_DIM), jnp.bfloat16),
                        pltpu.VMEM((DIL_HEADS, V_ROWS, s), jnp.bfloat16)],
        compiler_params=_cparams(("parallel", "parallel")),
        name="dilated_mixer",
    )(proj, proj, proj, proj, cos, sin, bias_tab)


def _out_proj_kernel(ma_ref, mb_ref, wa_ref, wb_ref, x_ref, nw_ref, o_ref):
    y = (jnp.dot(ma_ref[...], wa_ref[...], preferred_element_type=jnp.float32)
         + jnp.dot(mb_ref[...], wb_ref[...], preferred_element_type=jnp.float32))
    ms = jnp.mean(y * y, axis=-1, keepdims=True)
    o_ref[...] = x_ref[...] + y * lax.rsqrt(ms + EPS) * nw_ref[...]


def _out_proj(mix_a, mix_b, w, x2d, post_w, tm=256):
    m, d = x2d.shape
    ka, kb = mix_a.shape[1], mix_b.shape[1]
    assert ka == kb and w.shape[0] == ka + kb
    resident = dict(pipeline_mode=pl.Buffered(1))
    return pl.pallas_call(
        _out_proj_kernel,
        grid=(m // tm,),
        in_specs=[
            pl.BlockSpec((tm, ka), lambda i: (i, 0)),
            pl.BlockSpec((tm, kb), lambda i: (i, 0)),
            pl.BlockSpec((ka, d), lambda i: (0, 0), **resident),
            pl.BlockSpec((kb, d), lambda i: (1, 0), **resident),
            pl.BlockSpec((tm, d), lambda i: (i, 0)),
            pl.BlockSpec((1, d), lambda i: (0, 0)),
        ],
        out_specs=pl.BlockSpec((tm, d), lambda i: (i, 0)),
        out_shape=jax.ShapeDtypeStruct((m, d), jnp.float32),
        compiler_params=pltpu.CompilerParams(dimension_semantics=("parallel",),
                                             vmem_limit_bytes=OUT_PROJ_VMEM_LIMIT),
        name="out_proj_norm_residual",
    )(mix_a, mix_b, w, w, x2d, post_w.reshape(1, d))


def _layer(x, cos, sin, pre_w, post_w, w_in, b_gate, pe_k, pe_v, wk1, wk2, wv1, wv2, w_out):
    b, s, d = x.shape
    x2d = x.reshape(b * s, d)
    h = _rmsnorm(x2d, pre_w)
    front, back = _in_proj(h, w_in.T)

    pe = jnp.stack([pe_k, pe_v])
    w1 = jnp.stack([wk1, wv1]).reshape(2, CMP_LEN, HEAD_DIM, CMP_HIDDEN).astype(jnp.bfloat16)
    w2 = jnp.stack([wk2, wv2]).astype(jnp.bfloat16)
    cmp_kv = _compress(back, b, s, pe, w1, w2)

    per = 3 * GROUP_A
    bg = jnp.pad(b_gate.reshape(N_KV_A, 1, per), ((0, 0), (0, 0), (0, HEAD_DIM - per)))
    mix_a = _nsa(front, back, cmp_kv, bg, cos, sin, b, s)
    mix_b = _dilated(back, cos, sin, b, s)

    out = _out_proj(mix_a, mix_b, w_out.astype(jnp.bfloat16), x2d, post_w)
    return out.reshape(b, s, d)


def kernel(x, positions, pre_norm_w, post_norm_w, w_in, b_gate, cmp_pe_k, cmp_pe_v,
           cmp_wk1, cmp_wk2, cmp_wv1, cmp_wv2, w_out):
    cos, sin = _rope_tables(positions)
    for l in range(pre_norm_w.shape[0]):
        x = _layer(x, cos, sin, pre_norm_w[l], post_norm_w[l], w_in[l], b_gate[l],
                   cmp_pe_k[l], cmp_pe_v[l], cmp_wk1[l], cmp_wk2[l], cmp_wv1[l], cmp_wv2[l],
                   w_out[l])
    return x
```

```python
import functools
import math

import jax
import jax.numpy as jnp
from jax import lax
from jax.experimental import pallas as pl
from jax.experimental.pallas import tpu as pltpu

D_MODEL = 4096
HEAD_DIM = 128
HALF = HEAD_DIM // 2
N_HEADS_A = 16
N_KV_A = 4
GROUP_A = 4
N_HEADS_B = 16
CMP_LEN = 32
CMP_STRIDE = 16
CMP_HIDDEN = 256
SEL_LEN = 64
TOP_N = 16
WIN = 512
DILATIONS = ((128, 1), (512, 4), (2048, 16))
ROPE_THETA = 10000.0
EPS = 1e-6
FORCE_BONUS = 1e4
NEG = -1e30
SCALE = HEAD_DIM ** -0.5
QSCALE = SCALE * math.log2(math.e)

H_QA = 0
H_ZA = H_QA + N_HEADS_A
N_FRONT_HEADS = H_ZA + N_HEADS_A
GATE_PER_KV = 3 * GROUP_A
H_KC = 0
H_VC = H_KC + N_KV_A
H_KSL = H_VC + N_KV_A
H_VSL = H_KSL + N_KV_A
H_KW = H_VSL + N_KV_A
H_VW = H_KW + N_KV_A
H_QB = H_VW + N_KV_A
H_KB = H_QB + N_HEADS_B
H_VB = H_KB + N_HEADS_B
H_ZB = H_VB + N_HEADS_B
N_BACK_HEADS = H_ZB + N_HEADS_B

TQ = 256
TK = 256
SEL_PER_TK = TK // SEL_LEN
F32_SUBLANES = 8
BF16_SUBLANES = 16
V_ROWS = HEAD_DIM + BF16_SUBLANES
PV_GROUP = 2
NSA_SPLIT = 4
VMEM_LIMIT = 56 * 1024 * 1024
OUT_PROJ_VMEM_LIMIT = 60 * 1024 * 1024


def _cparams(sem):
    return pltpu.CompilerParams(dimension_semantics=sem, vmem_limit_bytes=VMEM_LIMIT)


GATE_ROW0 = 2 * N_HEADS_A * HEAD_DIM
BACK_ROW0 = GATE_ROW0 + 3 * N_HEADS_A


def _nt_dot(x, w):
    return lax.dot_general(x, w, (((1,), (1,)), ((), ())), preferred_element_type=jnp.float32)


def _rmsnorm_kernel(x_ref, w_ref, wg_ref, o_ref, g_ref):
    x = x_ref[...]
    ms = jnp.mean(x * x, axis=-1, keepdims=True)
    h = (x * lax.rsqrt(ms + EPS) * w_ref[...]).astype(o_ref.dtype)
    o_ref[...] = h
    g_ref[...] = _nt_dot(h, wg_ref[...].astype(jnp.bfloat16))


def _rmsnorm(x2d, w, w_t, tm=256):
    m, d = x2d.shape
    return pl.pallas_call(
        _rmsnorm_kernel,
        grid=(m // tm,),
        in_specs=[pl.BlockSpec((tm, d), lambda i: (i, 0)),
                  pl.BlockSpec((1, d), lambda i: (0, 0)),
                  pl.BlockSpec((HEAD_DIM, d), lambda i: (GATE_ROW0 // HEAD_DIM, 0))],
        out_specs=[pl.BlockSpec((tm, d), lambda i: (i, 0)),
                   pl.BlockSpec((tm, HEAD_DIM), lambda i: (i, 0))],
        out_shape=[jax.ShapeDtypeStruct((m, d), jnp.bfloat16),
                   jax.ShapeDtypeStruct((m, HEAD_DIM), jnp.float32)],
        compiler_params=_cparams(("parallel",)),
        name="pre_rmsnorm_gates",
    )(x2d, w.reshape(1, d), w_t)


PROJ_TN = GROUP_A * HEAD_DIM
PROJ_FRONT = GATE_ROW0 // PROJ_TN


def _in_proj_kernel(x_ref, w_ref, front_ref, back_ref):
    j = pl.program_id(1)

    @pl.when(j < PROJ_FRONT)
    def _():
        front_ref[...] = _nt_dot(x_ref[...], w_ref[...].astype(jnp.bfloat16)).astype(front_ref.dtype)

    @pl.when(j >= PROJ_FRONT)
    def _():
        back_ref[...] = _nt_dot(x_ref[...], w_ref[...].astype(jnp.bfloat16)).astype(back_ref.dtype)


def _in_proj(h, w_t, tm=2048):
    m, k = h.shape
    n_back = N_BACK_HEADS * HEAD_DIM
    n_steps = PROJ_FRONT + n_back // PROJ_TN

    def w_row(i, j):
        row = PROJ_TN * j + jnp.where(j >= PROJ_FRONT, BACK_ROW0 - GATE_ROW0, 0)
        return pl.multiple_of(row, F32_SUBLANES), 0

    return pl.pallas_call(
        _in_proj_kernel,
        grid=(m // tm, n_steps),
        in_specs=[
            pl.BlockSpec((tm, k), lambda i, j: (i, 0), pipeline_mode=pl.Buffered(1)),
            pl.BlockSpec((pl.Element(PROJ_TN), pl.Element(k)), w_row)],
        out_specs=[
            pl.BlockSpec((tm, PROJ_TN), lambda i, j: (i, jnp.minimum(j, PROJ_FRONT - 1))),
            pl.BlockSpec((tm, PROJ_TN), lambda i, j: (i, jnp.maximum(j - PROJ_FRONT, 0)))],
        out_shape=[jax.ShapeDtypeStruct((m, N_FRONT_HEADS * HEAD_DIM), jnp.bfloat16),
                   jax.ShapeDtypeStruct((m, n_back), jnp.bfloat16)],
        compiler_params=_cparams(("parallel", "arbitrary")),
        name="in_proj",
    )(h, w_t)


def _rope_table_kernel(pos_ref, inv_ref, cos_ref, sin_ref):
    ang = pos_ref[...].astype(jnp.float32) * inv_ref[...]
    lane = lax.broadcasted_iota(jnp.int32, ang.shape, 1)
    cos_ref[...] = jnp.cos(ang)
    sin_ref[...] = jnp.where(lane < HALF, -jnp.sin(ang), jnp.sin(ang))


def _rope_tables(positions):
    b, s = positions.shape
    inv = ROPE_THETA ** (-jnp.arange(HALF, dtype=jnp.float32) / HALF)
    inv_full = jnp.concatenate([inv, inv]).reshape(1, HEAD_DIM)
    out = jax.ShapeDtypeStruct((b, s, HEAD_DIM), jnp.float32)
    return pl.pallas_call(
        _rope_table_kernel,
        grid=(b,),
        in_specs=[pl.BlockSpec((None, s, 1), lambda i: (i, 0, 0)),
                  pl.BlockSpec((1, HEAD_DIM), lambda i: (0, 0))],
        out_specs=[pl.BlockSpec((None, s, HEAD_DIM), lambda i: (i, 0, 0)),
                   pl.BlockSpec((None, s, HEAD_DIM), lambda i: (i, 0, 0))],
        out_shape=[out, out],
        compiler_params=_cparams(("parallel",)),
        name="rope_tables",
    )(positions.reshape(b, s, 1), inv_full)


def _rope(xf, cos, sin_signed):
    return xf * cos + pltpu.roll(xf, HALF, 1) * sin_signed


def _compress_kernel(x_ref, pe_ref, w1_ref, w2_ref, o_ref, xs_ref):
    s = x_ref.shape[0]
    n_rows = s // CMP_STRIDE
    xs_ref[pl.ds(0, s), :] = x_ref[...].astype(jnp.float32)
    xs_ref[pl.ds(s, CMP_STRIDE), :] = jnp.zeros((CMP_STRIDE, HEAD_DIM), jnp.float32)
    flat = jnp.concatenate(
        [(xs_ref[pl.ds(r, n_rows, stride=CMP_STRIDE), :] + pe_ref[pl.ds(r, 1), :]).astype(jnp.bfloat16)
         for r in range(CMP_LEN)], axis=1)
    hid = jax.nn.gelu(jnp.dot(flat, w1_ref[...], preferred_element_type=jnp.float32))
    o_ref[...] = jnp.dot(hid.astype(jnp.bfloat16), w2_ref[...],
                         preferred_element_type=jnp.float32).astype(o_ref.dtype)


def _compress(proj, batch, s, pe, w1, w2):
    n_rows = s // CMP_STRIDE
    return pl.pallas_call(
        _compress_kernel,
        grid=(batch, 2 * N_KV_A),
        in_specs=[
            pl.BlockSpec((s, HEAD_DIM), lambda b, h: (b, H_KC + h)),
            pl.BlockSpec((None, CMP_LEN, HEAD_DIM), lambda b, h: (h // N_KV_A, 0, 0)),
            pl.BlockSpec((None, CMP_LEN * HEAD_DIM, CMP_HIDDEN), lambda b, h: (h // N_KV_A, 0, 0)),
            pl.BlockSpec((None, CMP_HIDDEN, HEAD_DIM), lambda b, h: (h // N_KV_A, 0, 0)),
        ],
        out_specs=pl.BlockSpec((None, None, n_rows, HEAD_DIM), lambda b, h: (b, h, 0, 0)),
        out_shape=jax.ShapeDtypeStruct((batch, 2 * N_KV_A, n_rows, HEAD_DIM), jnp.bfloat16),
        scratch_shapes=[pltpu.VMEM((s + CMP_STRIDE, HEAD_DIM), jnp.float32)],
        compiler_params=_cparams(("parallel", "parallel")),
        name="compress",
    )(proj, pe, w1, w2)


def _kq(k, q):
    return lax.dot_general(k, q, (((1,), (1,)), ((), ())),
                           preferred_element_type=jnp.float32)


def _memo(fn):
    cache = []

    def get():
        if not cache:
            cache.append(fn())
        return cache[0]
    return get


class _AttnJob:
    def __init__(self, q_fn, chunks):
        self.q_fn, self.chunks = q_fn, chunks
        self.s, self.m, self.acc = [], None, None

    def _score(self, c):
        k_fn, _, mask_fn = self.chunks[c]
        s = mask_fn(_kq(k_fn(), self.q_fn()))
        self.s.append(s)
        cm = jnp.max(s, axis=0, keepdims=True)
        self.m = cm if self.m is None else jnp.maximum(self.m, cm)

    def _value(self, cs):
        p = jnp.concatenate([jnp.exp2(self.s[c] - self.m).astype(jnp.bfloat16) for c in cs], axis=0)
        vt = jnp.concatenate([self.chunks[c][1]() for c in cs], axis=1)
        part = jnp.dot(vt, p, preferred_element_type=jnp.float32)
        self.acc = part if self.acc is None else self.acc + part

    def score_tasks(self):
        return [functools.partial(self._score, c) for c in range(len(self.chunks))]

    def value_tasks(self):
        n = len(self.chunks)
        return [functools.partial(self._value, range(c, min(c + PV_GROUP, n)))
                for c in range(0, n, PV_GROUP)]

    def result(self):
        return self.acc[:HEAD_DIM, :] * (1.0 / self.acc[HEAD_DIM:HEAD_DIM + 1, :])


def _round_robin(task_lists):
    for i in range(max(len(t) for t in task_lists)):
        for tasks in task_lists:
            if i < len(tasks):
                tasks[i]()


def _run_pipelined(streams, on_done):
    n = len(streams[0])
    _round_robin([jobs[0].score_tasks() for jobs in streams])
    for k in range(n):
        lists = []
        for jobs in streams:
            lists.append(jobs[k].value_tasks())
            if k + 1 < n:
                lists.append(jobs[k + 1].score_tasks())
        _round_robin(lists)
        for si, jobs in enumerate(streams):
            on_done(si, k, jobs[k])


def _store_values_transposed(src_ref, dst_ref):
    s_len = src_ref.shape[0]
    for c in range(s_len // TK):
        blk = src_ref[c * TK:(c + 1) * TK, :].astype(jnp.float32)
        dst_ref[0:HEAD_DIM, c * TK:(c + 1) * TK] = blk.T.astype(jnp.bfloat16)
    row = lax.broadcasted_iota(jnp.int32, (BF16_SUBLANES, s_len), 0)
    dst_ref[HEAD_DIM:, :] = jnp.where(row == 0, 1.0, 0.0).astype(jnp.bfloat16)


def _silu(z):
    return z * jax.nn.sigmoid(z)


def _topk_active(q0):
    return q0 + TQ > TOP_N * SEL_LEN


def _nsa_selection_bias(p_sum, q0, s_len):
    n_sel = s_len // SEL_LEN
    n_cmp = s_len // CMP_STRIDE - CMP_LEN // CMP_STRIDE + 1
    oj = lax.broadcasted_iota(jnp.int32, (n_sel, HEAD_DIM), 0)
    oc = lax.broadcasted_iota(jnp.int32, (n_sel, HEAD_DIM), 1)
    ovl = jnp.where(oc * CMP_STRIDE < oj * SEL_LEN + SEL_LEN,
                    jnp.where(oc * CMP_STRIDE + CMP_LEN > oj * SEL_LEN, 1.0, 0.0), 0.0)
    ovl = jnp.where(oc < n_cmp, ovl, 0.0).astype(jnp.bfloat16)
    p_hi = p_sum.astype(jnp.bfloat16)
    p_lo = (p_sum - p_hi.astype(jnp.float32)).astype(jnp.bfloat16)
    imp = (jnp.dot(ovl, p_hi, preferred_element_type=jnp.float32)
           + jnp.dot(ovl, p_lo, preferred_element_type=jnp.float32))
    j_row = lax.broadcasted_iota(jnp.int32, (n_sel, TQ), 0)
    t_sel = q0 + lax.broadcasted_iota(jnp.int32, (n_sel, TQ), 1)
    cur = lax.shift_right_arithmetic(t_sel, int(math.log2(SEL_LEN)))
    forced = (j_row == 0) | (j_row == cur) | (j_row == cur - 1)
    imp = jnp.where(forced, imp + FORCE_BONUS, imp)
    imp = jnp.where(j_row * SEL_LEN <= t_sel, imp, -jnp.inf)
    rank = jnp.zeros((n_sel, TQ), jnp.float32)
    for i in range(n_sel):
        vi = imp[i:i + 1, :]
        rank = rank + jnp.where(j_row > i, jnp.where(vi >= imp, 1.0, 0.0),
                                jnp.where(vi > imp, 1.0, 0.0))
    return jnp.where(rank < TOP_N, 0.0, NEG)


def _nsa_kernel(q_ref, z_ref, g_ref, bg_ref, kc_ref, vc_ref, ksl_ref, vsl_ref, kw_ref, vw_ref,
                cos_ref, sin_ref, o_ref,
                ksl_rot, kw_rot, vsl_t, vw_t, ocmp_s, selb_s, gate_s):
    s_len = ksl_ref.shape[0]
    n_q = s_len // TQ

    ksl_rot[...] = _rope(ksl_ref[...].astype(jnp.float32), cos_ref[...], sin_ref[...]).astype(jnp.bfloat16)
    kw_rot[...] = _rope(kw_ref[...].astype(jnp.float32), cos_ref[...], sin_ref[...]).astype(jnp.bfloat16)
    _store_values_transposed(vsl_ref, vsl_t)
    _store_values_transposed(vw_ref, vw_t)
    kc = kc_ref[...]
    vc_t = vc_ref[...].astype(jnp.float32).T.astype(jnp.bfloat16)
    bg = bg_ref[...]
    kv_head = pl.program_id(1)

    c_row = lax.broadcasted_iota(jnp.int32, (HEAD_DIM, TQ), 0)
    lane_q = lax.broadcasted_iota(jnp.int32, (HEAD_DIM, TQ), 1)
    lane_one = lax.broadcasted_iota(jnp.int32, (1, TQ), 1)
    for qi in range(n_q):
        q0 = qi * TQ
        rows = slice(q0, q0 + TQ)
        q_raw = [(q_ref[rows, g * HEAD_DIM:(g + 1) * HEAD_DIM].astype(jnp.float32) * QSCALE
                  ).astype(jnp.bfloat16) for g in range(GROUP_A)]
        c_valid = c_row * CMP_STRIDE + (CMP_LEN - 1) <= q0 + lane_q
        any_valid = jnp.where(q0 + lane_one >= CMP_LEN - 1, 1.0, 0.0)
        sc4 = _kq(kc, jnp.concatenate(q_raw, axis=0))
        p_heads = []
        p_sum = jnp.zeros((HEAD_DIM, TQ), jnp.float32)
        for g in range(GROUP_A):
            sc = jnp.where(c_valid, sc4[:, g * TQ:(g + 1) * TQ], NEG)
            e = jnp.exp2(sc - jnp.max(sc, axis=0, keepdims=True))
            p = e * (any_valid / jnp.sum(e, axis=0, keepdims=True))
            p_heads.append(p.astype(jnp.bfloat16))
            p_sum = p_sum + p
        o_cmp4 = jnp.dot(vc_t, jnp.concatenate(p_heads, axis=1),
                         preferred_element_type=jnp.float32)
        for g in range(GROUP_A):
            ocmp_s[g, :, rows] = o_cmp4[:, g * TQ:(g + 1) * TQ]
        if _topk_active(q0):
            selb_s[qi] = _nsa_selection_bias(p_sum, q0, s_len)
        gate_s[:, rows] = jax.nn.sigmoid(g_ref[rows, :] + bg).T

    diag_rc = (lax.broadcasted_iota(jnp.int32, (TK, TQ), 0)
               - lax.broadcasted_iota(jnp.int32, (TK, TQ), 1))

    def slc_mask(qi, c, s):
        if _topk_active(qi * TQ):
            s = s + jnp.concatenate(
                [jnp.broadcast_to(selb_s[qi, c * SEL_PER_TK + a:c * SEL_PER_TK + a + 1, :],
                                  (SEL_LEN, TQ)) for a in range(SEL_PER_TK)], axis=0)
        if c == qi:
            s = jnp.where(diag_rc <= 0, s, NEG)
        return s

    def win_mask(qi, c, s):
        if c == qi:
            return jnp.where(diag_rc <= 0, s, NEG)
        if (qi - c) * TK + TQ - 1 > WIN - 1:
            return jnp.where(diag_rc >= (qi - c) * TK - (WIN - 1), s, NEG)
        return s

    def roped_query(rows, col):
        qf = q_ref[rows, col].astype(jnp.float32) * QSCALE
        return _rope(qf, cos_ref[rows, :], sin_ref[rows, :]).astype(jnp.bfloat16)

    def head_body(g, carry):
        col = pl.ds(pl.multiple_of(g * HEAD_DIM, HEAD_DIM), HEAD_DIM)
        gate_row = GATE_PER_KV * kv_head + 3 * g
        slc_jobs, win_jobs = [], []
        for qi in range(n_q):
            q = _memo(functools.partial(roped_query, slice(qi * TQ, (qi + 1) * TQ), col))
            first_win = max(0, (qi * TQ - (WIN - 1)) // TK)
            for jobs, k_ref_, v_ref_, mask, c_lo in ((slc_jobs, ksl_rot, vsl_t, slc_mask, 0),
                                                     (win_jobs, kw_rot, vw_t, win_mask, first_win)):
                jobs.append(_AttnJob(q, [
                    (functools.partial(lambda r, c: r[c * TK:(c + 1) * TK, :], k_ref_, c),
                     functools.partial(lambda r, c: r[:, c * TK:(c + 1) * TK], v_ref_, c),
                     functools.partial(mask, qi, c))
                    for c in range(c_lo, qi + 1)]))

        def on_done(si, k, job):
            if si % 2 == 0:
                return
            qi = NSA_SPLIT * k + si // 2
            rows = slice(qi * TQ, (qi + 1) * TQ)
            o_t = (gate_s[pl.ds(gate_row, 1), rows] * ocmp_s[g, :, rows]
                   + gate_s[pl.ds(gate_row + 1, 1), rows] * slc_jobs[qi].result()
                   + gate_s[pl.ds(gate_row + 2, 1), rows] * job.result())
            z = z_ref[rows, col].astype(jnp.float32)
            o_ref[rows, col] = (o_t.T * _silu(z)).astype(o_ref.dtype)

        streams = []
        for r in range(NSA_SPLIT):
            streams += [slc_jobs[r::NSA_SPLIT], win_jobs[r::NSA_SPLIT]]
        _run_pipelined(streams, on_done)
        return carry

    lax.fori_loop(0, GROUP_A, head_body, 0)


def _nsa(front, back, gates, cmp_kv, b_gate_row, cos, sin, batch, s):
    gw = GROUP_A * HEAD_DIM
    full = lambda col0: pl.BlockSpec((s, HEAD_DIM), lambda b, h: (b, col0 + h))
    return pl.pallas_call(
        _nsa_kernel,
        grid=(batch, N_KV_A),
        in_specs=[
            pl.BlockSpec((s, gw), lambda b, h: (b, H_QA // GROUP_A + h)),
            pl.BlockSpec((s, gw), lambda b, h: (b, H_ZA // GROUP_A + h)),
            pl.BlockSpec((s, HEAD_DIM), lambda b, h: (b, 0)),
            pl.BlockSpec((1, HEAD_DIM), lambda b, h: (0, 0)),
            pl.BlockSpec((None, None, s // CMP_STRIDE, HEAD_DIM), lambda b, h: (b, h, 0, 0)),
            pl.BlockSpec((None, None, s // CMP_STRIDE, HEAD_DIM), lambda b, h: (b, N_KV_A + h, 0, 0)),
            full(H_KSL), full(H_VSL), full(H_KW), full(H_VW),
            pl.BlockSpec((None, s, HEAD_DIM), lambda b, h: (b, 0, 0)),
            pl.BlockSpec((None, s, HEAD_DIM), lambda b, h: (b, 0, 0)),
        ],
        out_specs=pl.BlockSpec((s, gw), lambda b, h: (b, h)),
        out_shape=jax.ShapeDtypeStruct((batch * s, N_HEADS_A * HEAD_DIM), jnp.bfloat16),
        scratch_shapes=[pltpu.VMEM((s, HEAD_DIM), jnp.bfloat16),
                        pltpu.VMEM((s, HEAD_DIM), jnp.bfloat16),
                        pltpu.VMEM((V_ROWS, s), jnp.bfloat16),
                        pltpu.VMEM((V_ROWS, s), jnp.bfloat16),
                        pltpu.VMEM((GROUP_A, HEAD_DIM, s), jnp.float32),
                        pltpu.VMEM((s // TQ, s // SEL_LEN, TQ), jnp.float32),
                        pltpu.VMEM((HEAD_DIM, s), jnp.float32)],
        compiler_params=_cparams(("parallel", "parallel")),
        name="nsa_mixer",
    )(front, front, gates, b_gate_row, cmp_kv, cmp_kv, back, back, back, back, cos, sin)


def _dilated_multiplicity(dist):
    cnt = jnp.zeros(dist.shape, jnp.float32)
    for window, dil in DILATIONS:
        cnt = cnt + jnp.where((dist & (dil - 1)) == 0,
                              jnp.where(dist <= (window // dil) * dil, 1.0, 0.0), 0.0)
    return jnp.where(dist >= 0, cnt, 0.0)


def _dilated_table_kernel(bias_ref):
    s_len = bias_ref.shape[0]
    dist = (s_len - TQ + lax.broadcasted_iota(jnp.int32, (s_len, TQ), 1)
            - lax.broadcasted_iota(jnp.int32, (s_len, TQ), 0))
    cnt = _dilated_multiplicity(dist)
    bias_ref[...] = jnp.where(cnt > 0.5, jnp.log2(jnp.maximum(cnt, 1.0)), NEG)


def _dilated_tables(s):
    return pl.pallas_call(
        _dilated_table_kernel,
        out_shape=jax.ShapeDtypeStruct((s, TQ), jnp.float32),
        compiler_params=pltpu.CompilerParams(vmem_limit_bytes=VMEM_LIMIT),
        name="dilated_tables",
    )()


DIL_HEADS = 2
DIL_SPLIT = 2


def _dilated_kernel(q_ref, z_ref, k_ref, v_ref, cos_ref, sin_ref, bias_ref, o_ref, k_rot, v_t):
    s_len = k_ref.shape[0]
    n_q = s_len // TQ

    def add_bias(delta, s):
        r0 = (n_q - 1 - delta) * TK
        return s + bias_ref[r0:r0 + TK, :]

    def roped_query(rows, cols):
        qf = q_ref[rows, cols].astype(jnp.float32) * QSCALE
        return _rope(qf, cos_ref[rows, :], sin_ref[rows, :]).astype(jnp.bfloat16)

    streams = []
    for hh in range(DIL_HEADS):
        cols = slice(hh * HEAD_DIM, (hh + 1) * HEAD_DIM)
        k_rot[hh] = _rope(k_ref[:, cols].astype(jnp.float32), cos_ref[...], sin_ref[...]).astype(jnp.bfloat16)
        _store_values_transposed(v_ref.at[:, cols], v_t.at[hh])
        jobs = []
        for qi in range(n_q):
            rows = slice(qi * TQ, (qi + 1) * TQ)
            jobs.append(_AttnJob(_memo(functools.partial(roped_query, rows, cols)), [
                (functools.partial(lambda hh, c: k_rot[hh, c * TK:(c + 1) * TK, :], hh, c),
                 functools.partial(lambda hh, c: v_t[hh, :, c * TK:(c + 1) * TK], hh, c),
                 functools.partial(add_bias, qi - c))
                for c in range(qi + 1)]))
        streams += [jobs[r::DIL_SPLIT] for r in range(DIL_SPLIT)]

    def on_done(si, k, job):
        hh, qi = si // DIL_SPLIT, DIL_SPLIT * k + si % DIL_SPLIT
        rows = slice(qi * TQ, (qi + 1) * TQ)
        cols = slice(hh * HEAD_DIM, (hh + 1) * HEAD_DIM)
        z = z_ref[rows, cols].astype(jnp.float32)
        o_ref[rows, cols] = (job.result().T * _silu(z)).astype(o_ref.dtype)

    _run_pipelined(streams, on_done)


def _dilated(proj, cos, sin, batch, s):
    bias_tab = _dilated_tables(s)
    w = DIL_HEADS * HEAD_DIM
    assert all(h % DIL_HEADS == 0 for h in (H_QB, H_ZB, H_KB, H_VB, N_HEADS_B))
    heads = lambda h0: pl.BlockSpec((s, w), lambda b, h: (b, h0 // DIL_HEADS + h))
    return pl.pallas_call(
        _dilated_kernel,
        grid=(batch, N_HEADS_B // DIL_HEADS),
        in_specs=[
            heads(H_QB), heads(H_ZB), heads(H_KB), heads(H_VB),
            pl.BlockSpec((None, s, HEAD_DIM), lambda b, h: (b, 0, 0)),
            pl.BlockSpec((None, s, HEAD_DIM), lambda b, h: (b, 0, 0)),
            pl.BlockSpec((s, TQ), lambda b, h: (0, 0)),
        ],
        out_specs=pl.BlockSpec((s, w), lambda b, h: (b, h)),
        out_shape=jax.ShapeDtypeStruct((batch * s, N_HEADS_B * HEAD_DIM), jnp.bfloat16),
        scratch_shapes=[pltpu.VMEM((DIL_HEADS, s, HEAD_DIM), jnp.bfloat16),
                        pltpu.VMEM((DIL_HEADS, V_ROWS, s), jnp.bfloat16)],
        compiler_params=_cparams(("parallel", "parallel")),
        name="dilated_mixer",
    )(proj, proj, proj, proj, cos, sin, bias_tab)


OUT_PANELS = 4


def _out_proj_kernel(ma_ref, mb_ref, wa_ref, wb_ref, x_ref, nw_ref, o_ref):
    d = o_ref.shape[1]
    pn = d // OUT_PANELS
    ma, mb = ma_ref[...], mb_ref[...]
    ys, ssq = [], None
    for p in range(OUT_PANELS):
        cols = slice(p * pn, (p + 1) * pn)
        y = (jnp.dot(ma, wa_ref[:, cols], preferred_element_type=jnp.float32)
             + jnp.dot(mb, wb_ref[:, cols], preferred_element_type=jnp.float32))
        part = jnp.sum(y * y, axis=-1, keepdims=True)
        ssq = part if ssq is None else ssq + part
        ys.append(y)
    scale = lax.rsqrt(ssq * (1.0 / d) + EPS)
    for p in range(OUT_PANELS):
        cols = slice(p * pn, (p + 1) * pn)
        o_ref[:, cols] = x_ref[:, cols] + ys[p] * scale * nw_ref[:, cols]


def _out_proj(mix_a, mix_b, w, x2d, post_w, tm=256):
    m, d = x2d.shape
    ka, kb = mix_a.shape[1], mix_b.shape[1]
    assert ka == kb and w.shape[0] == ka + kb
    resident = dict(pipeline_mode=pl.Buffered(1))
    return pl.pallas_call(
        _out_proj_kernel,
        grid=(m // tm,),
        in_specs=[
            pl.BlockSpec((tm, ka), lambda i: (i, 0)),
            pl.BlockSpec((tm, kb), lambda i: (i, 0)),
            pl.BlockSpec((ka, d), lambda i: (0, 0), **resident),
            pl.BlockSpec((kb, d), lambda i: (1, 0), **resident),
            pl.BlockSpec((tm, d), lambda i: (i, 0)),
            pl.BlockSpec((1, d), lambda i: (0, 0)),
        ],
        out_specs=pl.BlockSpec((tm, d), lambda i: (i, 0)),
        out_shape=jax.ShapeDtypeStruct((m, d), jnp.float32),
        compiler_params=pltpu.CompilerParams(dimension_semantics=("parallel",),
                                             vmem_limit_bytes=OUT_PROJ_VMEM_LIMIT),
        name="out_proj_norm_residual",
    )(mix_a, mix_b, w, w, x2d, post_w.reshape(1, d))


def _layer(x, cos, sin, pre_w, post_w, w_in, b_gate, pe_k, pe_v, wk1, wk2, wv1, wv2, w_out):
    b, s, d = x.shape
    x2d = x.reshape(b * s, d)
    w_t = w_in.T
    h, gates = _rmsnorm(x2d, pre_w, w_t)
    front, back = _in_proj(h, w_t)

    pe = jnp.stack([pe_k, pe_v])
    w1 = jnp.stack([wk1, wv1]).astype(jnp.bfloat16)
    w2 = jnp.stack([wk2, wv2]).astype(jnp.bfloat16)
    cmp_kv = _compress(back, b, s, pe, w1, w2)

    bg = jnp.pad(b_gate.reshape(1, -1), ((0, 0), (0, HEAD_DIM - b_gate.size)))
    mix_a = _nsa(front, back, gates, cmp_kv, bg, cos, sin, b, s)
    mix_b = _dilated(back, cos, sin, b, s)

    out = _out_proj(mix_a, mix_b, w_out.astype(jnp.bfloat16), x2d, post_w)
    return out.reshape(b, s, d)


def kernel(x, positions, pre_norm_w, post_norm_w, w_in, b_gate, cmp_pe_k, cmp_pe_v,
           cmp_wk1, cmp_wk2, cmp_wv1, cmp_wv2, w_out):
    cos, sin = _rope_tables(positions)
    for l in range(pre_norm_w.shape[0]):
        x = _layer(x, cos, sin, pre_norm_w[l], post_norm_w[l], w_in[l], b_gate[l],
                   cmp_pe_k[l], cmp_pe_v[l], cmp_wk1[l], cmp_wk2[l], cmp_wv1[l], cmp_wv2[l],
                   w_out[l])
    return x
```

```python
import functools
import math

import jax
import jax.numpy as jnp
from jax import lax
from jax.experimental import pallas as pl
from jax.experimental.pallas import tpu as pltpu

D_MODEL = 4096
HEAD_DIM = 128
HALF = HEAD_DIM // 2
N_HEADS_A = 16
N_KV_A = 4
GROUP_A = 4
N_HEADS_B = 16
CMP_LEN = 32
CMP_STRIDE = 16
CMP_HIDDEN = 256
SEL_LEN = 64
TOP_N = 16
WIN = 512
DILATIONS = ((128, 1), (512, 4), (2048, 16))
ROPE_THETA = 10000.0
EPS = 1e-6
FORCE_BONUS = 1e4
NEG = -1e30
SCALE = HEAD_DIM ** -0.5
QSCALE = SCALE * math.log2(math.e)

H_QA = 0
H_ZA = H_QA + N_HEADS_A
N_FRONT_HEADS = H_ZA + N_HEADS_A
GATE_PER_KV = 3 * GROUP_A
H_KC = 0
H_VC = H_KC + N_KV_A
H_KSL = H_VC + N_KV_A
H_VSL = H_KSL + N_KV_A
H_KW = H_VSL + N_KV_A
H_VW = H_KW + N_KV_A
H_QB = H_VW + N_KV_A
H_KB = H_QB + N_HEADS_B
H_VB = H_KB + N_HEADS_B
H_ZB = H_VB + N_HEADS_B
N_BACK_HEADS = H_ZB + N_HEADS_B

TQ = 256
TK = 256
SEL_PER_TK = TK // SEL_LEN
F32_SUBLANES = 8
BF16_SUBLANES = 16
V_ROWS = HEAD_DIM + BF16_SUBLANES
PV_GROUP = 2
MIX_SPLIT = 2
VMEM_LIMIT = 56 * 1024 * 1024
MIXERS_VMEM_LIMIT = 60 * 1024 * 1024
OUT_PROJ_VMEM_LIMIT = 60 * 1024 * 1024


def _cparams(sem):
    return pltpu.CompilerParams(dimension_semantics=sem, vmem_limit_bytes=VMEM_LIMIT)


GATE_ROW0 = 2 * N_HEADS_A * HEAD_DIM
BACK_ROW0 = GATE_ROW0 + 3 * N_HEADS_A


def _nt_dot(x, w):
    return lax.dot_general(x, w, (((1,), (1,)), ((), ())), preferred_element_type=jnp.float32)


def _rmsnorm_kernel(x_ref, w_ref, wg_ref, o_ref, g_ref):
    x = x_ref[...]
    ms = jnp.mean(x * x, axis=-1, keepdims=True)
    h = (x * lax.rsqrt(ms + EPS) * w_ref[...]).astype(o_ref.dtype)
    o_ref[...] = h
    g_ref[...] = _nt_dot(h, wg_ref[...].astype(jnp.bfloat16))


def _rmsnorm(x2d, w, w_t, tm=256):
    m, d = x2d.shape
    return pl.pallas_call(
        _rmsnorm_kernel,
        grid=(m // tm,),
        in_specs=[pl.BlockSpec((tm, d), lambda i: (i, 0)),
                  pl.BlockSpec((1, d), lambda i: (0, 0)),
                  pl.BlockSpec((HEAD_DIM, d), lambda i: (GATE_ROW0 // HEAD_DIM, 0))],
        out_specs=[pl.BlockSpec((tm, d), lambda i: (i, 0)),
                   pl.BlockSpec((tm, HEAD_DIM), lambda i: (i, 0))],
        out_shape=[jax.ShapeDtypeStruct((m, d), jnp.bfloat16),
                   jax.ShapeDtypeStruct((m, HEAD_DIM), jnp.float32)],
        compiler_params=_cparams(("parallel",)),
        name="pre_rmsnorm_gates",
    )(x2d, w.reshape(1, d), w_t)


PROJ_TN = GROUP_A * HEAD_DIM
PROJ_FRONT = GATE_ROW0 // PROJ_TN


def _in_proj_kernel(x_ref, w_ref, front_ref, back_ref):
    j = pl.program_id(1)

    @pl.when(j < PROJ_FRONT)
    def _():
        front_ref[...] = _nt_dot(x_ref[...], w_ref[...].astype(jnp.bfloat16)).astype(front_ref.dtype)

    @pl.when(j >= PROJ_FRONT)
    def _():
        back_ref[...] = _nt_dot(x_ref[...], w_ref[...].astype(jnp.bfloat16)).astype(back_ref.dtype)


def _in_proj(h, w_t, tm=2048):
    m, k = h.shape
    n_back = N_BACK_HEADS * HEAD_DIM
    n_steps = PROJ_FRONT + n_back // PROJ_TN

    def w_row(i, j):
        row = PROJ_TN * j + jnp.where(j >= PROJ_FRONT, BACK_ROW0 - GATE_ROW0, 0)
        return pl.multiple_of(row, F32_SUBLANES), 0

    return pl.pallas_call(
        _in_proj_kernel,
        grid=(m // tm, n_steps),
        in_specs=[
            pl.BlockSpec((tm, k), lambda i, j: (i, 0), pipeline_mode=pl.Buffered(1)),
            pl.BlockSpec((pl.Element(PROJ_TN), pl.Element(k)), w_row)],
        out_specs=[
            pl.BlockSpec((tm, PROJ_TN), lambda i, j: (i, jnp.minimum(j, PROJ_FRONT - 1))),
            pl.BlockSpec((tm, PROJ_TN), lambda i, j: (i, jnp.maximum(j - PROJ_FRONT, 0)))],
        out_shape=[jax.ShapeDtypeStruct((m, N_FRONT_HEADS * HEAD_DIM), jnp.bfloat16),
                   jax.ShapeDtypeStruct((m, n_back), jnp.bfloat16)],
        compiler_params=_cparams(("parallel", "arbitrary")),
        name="in_proj",
    )(h, w_t)


def _rope_table_kernel(pos_ref, inv_ref, cos_ref, sin_ref):
    ang = pos_ref[...].astype(jnp.float32) * inv_ref[...]
    lane = lax.broadcasted_iota(jnp.int32, ang.shape, 1)
    cos_ref[...] = jnp.cos(ang)
    sin_ref[...] = jnp.where(lane < HALF, -jnp.sin(ang), jnp.sin(ang))


def _rope_tables(positions):
    b, s = positions.shape
    inv = ROPE_THETA ** (-jnp.arange(HALF, dtype=jnp.float32) / HALF)
    inv_full = jnp.concatenate([inv, inv]).reshape(1, HEAD_DIM)
    out = jax.ShapeDtypeStruct((b, s, HEAD_DIM), jnp.float32)
    return pl.pallas_call(
        _rope_table_kernel,
        grid=(b,),
        in_specs=[pl.BlockSpec((None, s, 1), lambda i: (i, 0, 0)),
                  pl.BlockSpec((1, HEAD_DIM), lambda i: (0, 0))],
        out_specs=[pl.BlockSpec((None, s, HEAD_DIM), lambda i: (i, 0, 0)),
                   pl.BlockSpec((None, s, HEAD_DIM), lambda i: (i, 0, 0))],
        out_shape=[out, out],
        compiler_params=_cparams(("parallel",)),
        name="rope_tables",
    )(positions.reshape(b, s, 1), inv_full)


def _rope(xf, cos, sin_signed):
    return xf * cos + pltpu.roll(xf, HALF, 1) * sin_signed


def _compress_kernel(x_ref, pe_ref, w1_ref, w2_ref, o_ref, xs_ref):
    s = x_ref.shape[0]
    n_rows = s // CMP_STRIDE
    xs_ref[pl.ds(0, s), :] = x_ref[...].astype(jnp.float32)
    xs_ref[pl.ds(s, CMP_STRIDE), :] = jnp.zeros((CMP_STRIDE, HEAD_DIM), jnp.float32)
    flat = jnp.concatenate(
        [(xs_ref[pl.ds(r, n_rows, stride=CMP_STRIDE), :] + pe_ref[pl.ds(r, 1), :]).astype(jnp.bfloat16)
         for r in range(CMP_LEN)], axis=1)
    hid = jax.nn.gelu(jnp.dot(flat, w1_ref[...], preferred_element_type=jnp.float32))
    o_ref[...] = jnp.dot(hid.astype(jnp.bfloat16), w2_ref[...],
                         preferred_element_type=jnp.float32).astype(o_ref.dtype)


def _compress(proj, batch, s, pe, w1, w2):
    n_rows = s // CMP_STRIDE
    return pl.pallas_call(
        _compress_kernel,
        grid=(batch, 2 * N_KV_A),
        in_specs=[
            pl.BlockSpec((s, HEAD_DIM), lambda b, h: (b, H_KC + h)),
            pl.BlockSpec((None, CMP_LEN, HEAD_DIM), lambda b, h: (h // N_KV_A, 0, 0)),
            pl.BlockSpec((None, CMP_LEN * HEAD_DIM, CMP_HIDDEN), lambda b, h: (h // N_KV_A, 0, 0)),
            pl.BlockSpec((None, CMP_HIDDEN, HEAD_DIM), lambda b, h: (h // N_KV_A, 0, 0)),
        ],
        out_specs=pl.BlockSpec((None, None, n_rows, HEAD_DIM), lambda b, h: (b, h, 0, 0)),
        out_shape=jax.ShapeDtypeStruct((batch, 2 * N_KV_A, n_rows, HEAD_DIM), jnp.bfloat16),
        scratch_shapes=[pltpu.VMEM((s + CMP_STRIDE, HEAD_DIM), jnp.float32)],
        compiler_params=_cparams(("parallel", "parallel")),
        name="compress",
    )(proj, pe, w1, w2)


def _kq(k, q):
    return lax.dot_general(k, q, (((1,), (1,)), ((), ())),
                           preferred_element_type=jnp.float32)


def _memo(fn):
    cache = []

    def get():
        if not cache:
            cache.append(fn())
        return cache[0]
    return get


class _AttnJob:
    def __init__(self, q_fn, chunks):
        self.q_fn, self.chunks = q_fn, chunks
        self.s, self.m, self.acc = [], None, None

    def _score(self, c):
        k_fn, _, mask_fn = self.chunks[c]
        s = mask_fn(_kq(k_fn(), self.q_fn()))
        self.s.append(s)
        cm = jnp.max(s, axis=0, keepdims=True)
        self.m = cm if self.m is None else jnp.maximum(self.m, cm)

    def _value(self, cs):
        p = jnp.concatenate([jnp.exp2(self.s[c] - self.m).astype(jnp.bfloat16) for c in cs], axis=0)
        vt = jnp.concatenate([self.chunks[c][1]() for c in cs], axis=1)
        part = jnp.dot(vt, p, preferred_element_type=jnp.float32)
        self.acc = part if self.acc is None else self.acc + part

    def score_tasks(self):
        return [functools.partial(self._score, c) for c in range(len(self.chunks))]

    def value_tasks(self):
        n = len(self.chunks)
        return [functools.partial(self._value, range(c, min(c + PV_GROUP, n)))
                for c in range(0, n, PV_GROUP)]

    def result(self):
        return self.acc[:HEAD_DIM, :] * (1.0 / self.acc[HEAD_DIM:HEAD_DIM + 1, :])


def _round_robin(task_lists):
    for i in range(max(len(t) for t in task_lists)):
        for tasks in task_lists:
            if i < len(tasks):
                tasks[i]()


def _run_pipelined(streams, on_done):
    n = len(streams[0])
    _round_robin([jobs[0].score_tasks() for jobs in streams])
    for k in range(n):
        lists = []
        for jobs in streams:
            lists.append(jobs[k].value_tasks())
            if k + 1 < n:
                lists.append(jobs[k + 1].score_tasks())
        _round_robin(lists)
        for si, jobs in enumerate(streams):
            on_done(si, k, jobs[k])


def _store_values_transposed(src_ref, dst_ref):
    s_len = src_ref.shape[0]
    for c in range(s_len // TK):
        blk = src_ref[c * TK:(c + 1) * TK, :].astype(jnp.float32)
        dst_ref[0:HEAD_DIM, c * TK:(c + 1) * TK] = blk.T.astype(jnp.bfloat16)
    row = lax.broadcasted_iota(jnp.int32, (BF16_SUBLANES, s_len), 0)
    dst_ref[HEAD_DIM:, :] = jnp.where(row == 0, 1.0, 0.0).astype(jnp.bfloat16)


def _silu(z):
    return z * jax.nn.sigmoid(z)


def _topk_active(q0):
    return q0 + TQ > TOP_N * SEL_LEN


def _nsa_selection_bias(p_sum, q0, s_len):
    n_sel = s_len // SEL_LEN
    n_cmp = s_len // CMP_STRIDE - CMP_LEN // CMP_STRIDE + 1
    oj = lax.broadcasted_iota(jnp.int32, (n_sel, HEAD_DIM), 0)
    oc = lax.broadcasted_iota(jnp.int32, (n_sel, HEAD_DIM), 1)
    ovl = jnp.where(oc * CMP_STRIDE < oj * SEL_LEN + SEL_LEN,
                    jnp.where(oc * CMP_STRIDE + CMP_LEN > oj * SEL_LEN, 1.0, 0.0), 0.0)
    ovl = jnp.where(oc < n_cmp, ovl, 0.0).astype(jnp.bfloat16)
    p_hi = p_sum.astype(jnp.bfloat16)
    p_lo = (p_sum - p_hi.astype(jnp.float32)).astype(jnp.bfloat16)
    imp = (jnp.dot(ovl, p_hi, preferred_element_type=jnp.float32)
           + jnp.dot(ovl, p_lo, preferred_element_type=jnp.float32))
    j_row = lax.broadcasted_iota(jnp.int32, (n_sel, TQ), 0)
    t_sel = q0 + lax.broadcasted_iota(jnp.int32, (n_sel, TQ), 1)
    cur = lax.shift_right_arithmetic(t_sel, int(math.log2(SEL_LEN)))
    forced = (j_row == 0) | (j_row == cur) | (j_row == cur - 1)
    imp = jnp.where(forced, imp + FORCE_BONUS, imp)
    imp = jnp.where(j_row * SEL_LEN <= t_sel, imp, -jnp.inf)
    rank = jnp.zeros((n_sel, TQ), jnp.float32)
    for i in range(n_sel):
        vi = imp[i:i + 1, :]
        rank = rank + jnp.where(j_row > i, jnp.where(vi >= imp, 1.0, 0.0),
                                jnp.where(vi > imp, 1.0, 0.0))
    return jnp.where(rank < TOP_N, 0.0, NEG)


def _mixers_kernel(q_ref, z_ref, g_ref, bg_ref, kc_ref, vc_ref, ksl_ref, vsl_ref, kw_ref, vw_ref,
                   cos_ref, sin_ref, qd_ref, zd_ref, kd_ref, vd_ref, dbias_ref, o_ref, od_ref,
                   ksl_rot, kw_rot, vsl_t, vw_t, ocmp_s, selb_s, gate_s):
    s_len = ksl_ref.shape[0]
    n_q = s_len // TQ

    ksl_rot[...] = _rope(ksl_ref[...].astype(jnp.float32), cos_ref[...], sin_ref[...]).astype(jnp.bfloat16)
    kw_rot[...] = _rope(kw_ref[...].astype(jnp.float32), cos_ref[...], sin_ref[...]).astype(jnp.bfloat16)
    _store_values_transposed(vsl_ref, vsl_t)
    _store_values_transposed(vw_ref, vw_t)
    kc = kc_ref[...]
    vc_t = vc_ref[...].astype(jnp.float32).T.astype(jnp.bfloat16)
    bg = bg_ref[...]
    kv_head = pl.program_id(1)

    c_row = lax.broadcasted_iota(jnp.int32, (HEAD_DIM, TQ), 0)
    lane_q = lax.broadcasted_iota(jnp.int32, (HEAD_DIM, TQ), 1)
    lane_one = lax.broadcasted_iota(jnp.int32, (1, TQ), 1)
    for qi in range(n_q):
        q0 = qi * TQ
        rows = slice(q0, q0 + TQ)
        q_raw = [(q_ref[rows, g * HEAD_DIM:(g + 1) * HEAD_DIM].astype(jnp.float32) * QSCALE
                  ).astype(jnp.bfloat16) for g in range(GROUP_A)]
        c_valid = c_row * CMP_STRIDE + (CMP_LEN - 1) <= q0 + lane_q
        any_valid = jnp.where(q0 + lane_one >= CMP_LEN - 1, 1.0, 0.0)
        sc4 = _kq(kc, jnp.concatenate(q_raw, axis=0))
        p_heads = []
        p_sum = jnp.zeros((HEAD_DIM, TQ), jnp.float32)
        for g in range(GROUP_A):
            sc = jnp.where(c_valid, sc4[:, g * TQ:(g + 1) * TQ], NEG)
            e = jnp.exp2(sc - jnp.max(sc, axis=0, keepdims=True))
            p = e * (any_valid / jnp.sum(e, axis=0, keepdims=True))
            p_heads.append(p.astype(jnp.bfloat16))
            p_sum = p_sum + p
        o_cmp4 = jnp.dot(vc_t, jnp.concatenate(p_heads, axis=1),
                         preferred_element_type=jnp.float32)
        for g in range(GROUP_A):
            ocmp_s[g, :, rows] = o_cmp4[:, g * TQ:(g + 1) * TQ]
        if _topk_active(q0):
            selb_s[qi] = _nsa_selection_bias(p_sum, q0, s_len)
        gate_s[:, rows] = jax.nn.sigmoid(g_ref[rows, :] + bg).T

    diag_rc = (lax.broadcasted_iota(jnp.int32, (TK, TQ), 0)
               - lax.broadcasted_iota(jnp.int32, (TK, TQ), 1))

    def slc_mask(qi, c, s):
        if _topk_active(qi * TQ):
            s = s + jnp.concatenate(
                [jnp.broadcast_to(selb_s[qi, c * SEL_PER_TK + a:c * SEL_PER_TK + a + 1, :],
                                  (SEL_LEN, TQ)) for a in range(SEL_PER_TK)], axis=0)
        if c == qi:
            s = jnp.where(diag_rc <= 0, s, NEG)
        return s

    def win_mask(qi, c, s):
        if c == qi:
            return jnp.where(diag_rc <= 0, s, NEG)
        if (qi - c) * TK + TQ - 1 > WIN - 1:
            return jnp.where(diag_rc >= (qi - c) * TK - (WIN - 1), s, NEG)
        return s

    ones_rows = jnp.where(lax.broadcasted_iota(jnp.int32, (BF16_SUBLANES, TK), 0) == 0,
                          1.0, 0.0).astype(jnp.bfloat16)

    def head_body(g, carry):
        col = pl.ds(pl.multiple_of(g * HEAD_DIM, HEAD_DIM), HEAD_DIM)
        gate_row = GATE_PER_KV * kv_head + 3 * g
        blocks = [slice(c * TK, (c + 1) * TK) for c in range(n_q)]

        def roped(ref, rows, scale):
            xf = ref[rows, col].astype(jnp.float32)
            if scale != 1.0:
                xf = xf * scale
            return _rope(xf, cos_ref[rows, :], sin_ref[rows, :]).astype(jnp.bfloat16)

        def dil_value_t(rows):
            vt = vd_ref[rows, col].astype(jnp.float32).T.astype(jnp.bfloat16)
            return jnp.concatenate([vt, ones_rows], axis=0)

        def dil_bias(delta, s):
            r0 = (n_q - 1 - delta) * TK
            return s + dbias_ref[r0:r0 + TK, :]

        dil_k = [_memo(functools.partial(roped, kd_ref, rows, 1.0)) for rows in blocks]
        dil_v = [_memo(functools.partial(dil_value_t, rows)) for rows in blocks]
        slc_jobs, win_jobs, dil_jobs = [], [], []
        for qi in range(n_q):
            q = _memo(functools.partial(roped, q_ref, blocks[qi], QSCALE))
            first_win = max(0, (qi * TQ - (WIN - 1)) // TK)
            for jobs, k_ref_, v_ref_, mask, c_lo in ((slc_jobs, ksl_rot, vsl_t, slc_mask, 0),
                                                     (win_jobs, kw_rot, vw_t, win_mask, first_win)):
                jobs.append(_AttnJob(q, [
                    (functools.partial(lambda r, c: r[c * TK:(c + 1) * TK, :], k_ref_, c),
                     functools.partial(lambda r, c: r[:, c * TK:(c + 1) * TK], v_ref_, c),
                     functools.partial(mask, qi, c))
                    for c in range(c_lo, qi + 1)]))
            dil_jobs.append(_AttnJob(
                _memo(functools.partial(roped, qd_ref, blocks[qi], QSCALE)),
                [(dil_k[c], dil_v[c], functools.partial(dil_bias, qi - c)) for c in range(qi + 1)]))

        def on_done(si, k, job):
            kind, qi = si % 3, MIX_SPLIT * k + si // 3
            rows = blocks[qi]
            if kind == 1:
                o_t = (gate_s[pl.ds(gate_row, 1), rows] * ocmp_s[g, :, rows]
                       + gate_s[pl.ds(gate_row + 1, 1), rows] * slc_jobs[qi].result()
                       + gate_s[pl.ds(gate_row + 2, 1), rows] * job.result())
                z = z_ref[rows, col].astype(jnp.float32)
                o_ref[rows, col] = (o_t.T * _silu(z)).astype(o_ref.dtype)
            elif kind == 2:
                z = zd_ref[rows, col].astype(jnp.float32)
                od_ref[rows, col] = (job.result().T * _silu(z)).astype(od_ref.dtype)

        streams = []
        for r in range(MIX_SPLIT):
            streams += [slc_jobs[r::MIX_SPLIT], win_jobs[r::MIX_SPLIT], dil_jobs[r::MIX_SPLIT]]
        _run_pipelined(streams, on_done)
        return carry

    lax.fori_loop(0, GROUP_A, head_body, 0)


def _mixers(front, back, gates, cmp_kv, b_gate_row, cos, sin, dil_bias, batch, s):
    gw = GROUP_A * HEAD_DIM
    assert N_HEADS_B == N_HEADS_A
    full = lambda col0: pl.BlockSpec((s, HEAD_DIM), lambda b, h: (b, col0 + h))
    group = lambda head0: pl.BlockSpec((s, gw), lambda b, h: (b, head0 // GROUP_A + h))
    out = jax.ShapeDtypeStruct((batch * s, N_HEADS_A * HEAD_DIM), jnp.bfloat16)
    rarely_changes = dict(pipeline_mode=pl.Buffered(1))
    return pl.pallas_call(
        _mixers_kernel,
        grid=(batch, N_KV_A),
        in_specs=[
            group(H_QA), group(H_ZA),
            pl.BlockSpec((s, HEAD_DIM), lambda b, h: (b, 0)),
            pl.BlockSpec((1, HEAD_DIM), lambda b, h: (0, 0)),
            pl.BlockSpec((None, None, s // CMP_STRIDE, HEAD_DIM), lambda b, h: (b, h, 0, 0)),
            pl.BlockSpec((None, None, s // CMP_STRIDE, HEAD_DIM), lambda b, h: (b, N_KV_A + h, 0, 0)),
            full(H_KSL), full(H_VSL), full(H_KW), full(H_VW),
            pl.BlockSpec((None, s, HEAD_DIM), lambda b, h: (b, 0, 0), **rarely_changes),
            pl.BlockSpec((None, s, HEAD_DIM), lambda b, h: (b, 0, 0), **rarely_changes),
            group(H_QB), group(H_ZB), group(H_KB), group(H_VB),
            pl.BlockSpec((s, TQ), lambda b, h: (0, 0), **rarely_changes),
        ],
        out_specs=[pl.BlockSpec((s, gw), lambda b, h: (b, h)),
                   pl.BlockSpec((s, gw), lambda b, h: (b, h))],
        out_shape=[out, out],
        scratch_shapes=[pltpu.VMEM((s, HEAD_DIM), jnp.bfloat16),
                        pltpu.VMEM((s, HEAD_DIM), jnp.bfloat16),
                        pltpu.VMEM((V_ROWS, s), jnp.bfloat16),
                        pltpu.VMEM((V_ROWS, s), jnp.bfloat16),
                        pltpu.VMEM((GROUP_A, HEAD_DIM, s), jnp.float32),
                        pltpu.VMEM((s // TQ, s // SEL_LEN, TQ), jnp.float32),
                        pltpu.VMEM((HEAD_DIM, s), jnp.float32)],
        compiler_params=pltpu.CompilerParams(dimension_semantics=("parallel", "parallel"),
                                             vmem_limit_bytes=MIXERS_VMEM_LIMIT),
        name="token_mixers",
    )(front, front, gates, b_gate_row, cmp_kv, cmp_kv, back, back, back, back, cos, sin,
      back, back, back, back, dil_bias)


def _dilated_multiplicity(dist):
    cnt = jnp.zeros(dist.shape, jnp.float32)
    for window, dil in DILATIONS:
        cnt = cnt + jnp.where((dist & (dil - 1)) == 0,
                              jnp.where(dist <= (window // dil) * dil, 1.0, 0.0), 0.0)
    return jnp.where(dist >= 0, cnt, 0.0)


def _dilated_table_kernel(bias_ref):
    s_len = bias_ref.shape[0]
    dist = (s_len - TQ + lax.broadcasted_iota(jnp.int32, (s_len, TQ), 1)
            - lax.broadcasted_iota(jnp.int32, (s_len, TQ), 0))
    cnt = _dilated_multiplicity(dist)
    bias_ref[...] = jnp.where(cnt > 0.5, jnp.log2(jnp.maximum(cnt, 1.0)), NEG)


def _dilated_tables(s):
    return pl.pallas_call(
        _dilated_table_kernel,
        out_shape=jax.ShapeDtypeStruct((s, TQ), jnp.float32),
        compiler_params=pltpu.CompilerParams(vmem_limit_bytes=VMEM_LIMIT),
        name="dilated_tables",
    )()


OUT_PANELS = 4


def _out_proj_kernel(ma_ref, mb_ref, wa_ref, wb_ref, x_ref, nw_ref, o_ref):
    d = o_ref.shape[1]
    pn = d // OUT_PANELS
    ma, mb = ma_ref[...], mb_ref[...]
    ys, ssq = [], None
    for p in range(OUT_PANELS):
        cols = slice(p * pn, (p + 1) * pn)
        y = (jnp.dot(ma, wa_ref[:, cols], preferred_element_type=jnp.float32)
             + jnp.dot(mb, wb_ref[:, cols], preferred_element_type=jnp.float32))
        part = jnp.sum(y * y, axis=-1, keepdims=True)
        ssq = part if ssq is None else ssq + part
        ys.append(y)
    scale = lax.rsqrt(ssq * (1.0 / d) + EPS)
    for p in range(OUT_PANELS):
        cols = slice(p * pn, (p + 1) * pn)
        o_ref[:, cols] = x_ref[:, cols] + ys[p] * scale * nw_ref[:, cols]


def _out_proj(mix_a, mix_b, w, x2d, post_w, tm=256):
    m, d = x2d.shape
    ka, kb = mix_a.shape[1], mix_b.shape[1]
    assert ka == kb and w.shape[0] == ka + kb
    resident = dict(pipeline_mode=pl.Buffered(1))
    return pl.pallas_call(
        _out_proj_kernel,
        grid=(m // tm,),
        in_specs=[
            pl.BlockSpec((tm, ka), lambda i: (i, 0)),
            pl.BlockSpec((tm, kb), lambda i: (i, 0)),
            pl.BlockSpec((ka, d), lambda i: (0, 0), **resident),
            pl.BlockSpec((kb, d), lambda i: (1, 0), **resident),
            pl.BlockSpec((tm, d), lambda i: (i, 0)),
            pl.BlockSpec((1, d), lambda i: (0, 0)),
        ],
        out_specs=pl.BlockSpec((tm, d), lambda i: (i, 0)),
        out_shape=jax.ShapeDtypeStruct((m, d), jnp.float32),
        compiler_params=pltpu.CompilerParams(dimension_semantics=("parallel",),
                                             vmem_limit_bytes=OUT_PROJ_VMEM_LIMIT),
        name="out_proj_norm_residual",
    )(mix_a, mix_b, w, w, x2d, post_w.reshape(1, d))


def _layer(x, cos, sin, pre_w, post_w, w_in, b_gate, pe_k, pe_v, wk1, wk2, wv1, wv2, w_out):
    b, s, d = x.shape
    x2d = x.reshape(b * s, d)
    w_t = w_in.T
    h, gates = _rmsnorm(x2d, pre_w, w_t)
    front, back = _in_proj(h, w_t)

    pe = jnp.stack([pe_k, pe_v])
    w1 = jnp.stack([wk1, wv1]).astype(jnp.bfloat16)
    w2 = jnp.stack([wk2, wv2]).astype(jnp.bfloat16)
    cmp_kv = _compress(back, b, s, pe, w1, w2)

    bg = jnp.pad(b_gate.reshape(1, -1), ((0, 0), (0, HEAD_DIM - b_gate.size)))
    mix_a, mix_b = _mixers(front, back, gates, cmp_kv, bg, cos, sin, _dilated_tables(s), b, s)

    out = _out_proj(mix_a, mix_b, w_out.astype(jnp.bfloat16), x2d, post_w)
    return out.reshape(b, s, d)


def kernel(x, positions, pre_norm_w, post_norm_w, w_in, b_gate, cmp_pe_k, cmp_pe_v,
           cmp_wk1, cmp_wk2, cmp_wv1, cmp_wv2, w_out):
    cos, sin = _rope_tables(positions)
    for l in range(pre_norm_w.shape[0]):
        x = _layer(x, cos, sin, pre_norm_w[l], post_norm_w[l], w_in[l], b_gate[l],
                   cmp_pe_k[l], cmp_pe_v[l], cmp_wk1[l], cmp_wk2[l], cmp_wv1[l], cmp_wv2[l],
                   w_out[l])
    return x
```

```python
import functools
import math

import jax
import jax.numpy as jnp
from jax import lax
from jax.experimental import pallas as pl
from jax.experimental.pallas import tpu as pltpu

HEAD_DIM = 128
HALF = HEAD_DIM // 2
N_HEADS_A = 16
N_KV_A = 4
GROUP_A = 4
N_HEADS_B = 16
CMP_LEN = 32
CMP_STRIDE = 16
CMP_HIDDEN = 256
SEL_LEN = 64
TOP_N = 16
WIN = 512
DILATIONS = ((128, 1), (512, 4), (2048, 16))
ROPE_THETA = 10000.0
EPS = 1e-6
FORCE_BONUS = 1e4
NEG = -1e30
SCALE = HEAD_DIM ** -0.5
QSCALE = SCALE * math.log2(math.e)

H_QA = 0
H_ZA = H_QA + N_HEADS_A
N_FRONT_HEADS = H_ZA + N_HEADS_A
GATE_PER_KV = 3 * GROUP_A
H_KC = 0
H_VC = H_KC + N_KV_A
H_KSL = H_VC + N_KV_A
H_VSL = H_KSL + N_KV_A
H_KW = H_VSL + N_KV_A
H_VW = H_KW + N_KV_A
H_QB = H_VW + N_KV_A
H_KB = H_QB + N_HEADS_B
H_VB = H_KB + N_HEADS_B
H_ZB = H_VB + N_HEADS_B
N_BACK_HEADS = H_ZB + N_HEADS_B

TQ = 256
TK = 256
SEL_PER_TK = TK // SEL_LEN
F32_SUBLANES = 8
BF16_SUBLANES = 16
V_ROWS = HEAD_DIM + BF16_SUBLANES
PV_GROUP = 2
MIX_SPLIT = 2
VMEM_LIMIT = 56 * 1024 * 1024
MIXERS_VMEM_LIMIT = 60 * 1024 * 1024
OUT_PROJ_VMEM_LIMIT = 60 * 1024 * 1024


def _cparams(sem):
    return pltpu.CompilerParams(dimension_semantics=sem, vmem_limit_bytes=VMEM_LIMIT)


GATE_ROW0 = 2 * N_HEADS_A * HEAD_DIM
BACK_ROW0 = GATE_ROW0 + 3 * N_HEADS_A


def _nt_dot(x, w):
    return lax.dot_general(x, w, (((1,), (1,)), ((), ())), preferred_element_type=jnp.float32)


def _rmsnorm_kernel(x_ref, w_ref, wg_ref, o_ref, g_ref):
    x = x_ref[...]
    ms = jnp.mean(x * x, axis=-1, keepdims=True)
    h = (x * lax.rsqrt(ms + EPS) * w_ref[...]).astype(o_ref.dtype)
    o_ref[...] = h
    g_ref[...] = _nt_dot(h, wg_ref[...].astype(jnp.bfloat16))


def _rmsnorm(x2d, w, w_t, tm=256):
    m, d = x2d.shape
    return pl.pallas_call(
        _rmsnorm_kernel,
        grid=(m // tm,),
        in_specs=[pl.BlockSpec((tm, d), lambda i: (i, 0)),
                  pl.BlockSpec((1, d), lambda i: (0, 0)),
                  pl.BlockSpec((HEAD_DIM, d), lambda i: (GATE_ROW0 // HEAD_DIM, 0))],
        out_specs=[pl.BlockSpec((tm, d), lambda i: (i, 0)),
                   pl.BlockSpec((tm, HEAD_DIM), lambda i: (i, 0))],
        out_shape=[jax.ShapeDtypeStruct((m, d), jnp.bfloat16),
                   jax.ShapeDtypeStruct((m, HEAD_DIM), jnp.float32)],
        compiler_params=_cparams(("parallel",)),
        name="pre_rmsnorm_gates",
    )(x2d, w.reshape(1, d), w_t)


PROJ_TN = GROUP_A * HEAD_DIM
PROJ_FRONT = GATE_ROW0 // PROJ_TN


def _in_proj_kernel(x_ref, w_ref, front_ref, back_ref):
    j = pl.program_id(1)

    @pl.when(j < PROJ_FRONT)
    def _():
        front_ref[...] = _nt_dot(x_ref[...], w_ref[...].astype(jnp.bfloat16)).astype(front_ref.dtype)

    @pl.when(j >= PROJ_FRONT)
    def _():
        back_ref[...] = _nt_dot(x_ref[...], w_ref[...].astype(jnp.bfloat16)).astype(back_ref.dtype)


def _in_proj(h, w_t, tm=2048):
    m, k = h.shape
    n_back = N_BACK_HEADS * HEAD_DIM
    n_steps = PROJ_FRONT + n_back // PROJ_TN

    def w_row(i, j):
        row = PROJ_TN * j + jnp.where(j >= PROJ_FRONT, BACK_ROW0 - GATE_ROW0, 0)
        return pl.multiple_of(row, F32_SUBLANES), 0

    return pl.pallas_call(
        _in_proj_kernel,
        grid=(m // tm, n_steps),
        in_specs=[
            pl.BlockSpec((tm, k), lambda i, j: (i, 0), pipeline_mode=pl.Buffered(1)),
            pl.BlockSpec((pl.Element(PROJ_TN), pl.Element(k)), w_row)],
        out_specs=[
            pl.BlockSpec((tm, PROJ_TN), lambda i, j: (i, jnp.minimum(j, PROJ_FRONT - 1))),
            pl.BlockSpec((tm, PROJ_TN), lambda i, j: (i, jnp.maximum(j - PROJ_FRONT, 0)))],
        out_shape=[jax.ShapeDtypeStruct((m, N_FRONT_HEADS * HEAD_DIM), jnp.bfloat16),
                   jax.ShapeDtypeStruct((m, n_back), jnp.bfloat16)],
        compiler_params=_cparams(("parallel", "arbitrary")),
        name="in_proj",
    )(h, w_t)


def _rope_table_kernel(pos_ref, inv_ref, cos_ref, sin_ref):
    ang = pos_ref[...].astype(jnp.float32) * inv_ref[...]
    lane = lax.broadcasted_iota(jnp.int32, ang.shape, 1)
    cos_ref[...] = jnp.cos(ang)
    sin_ref[...] = jnp.where(lane < HALF, -jnp.sin(ang), jnp.sin(ang))


def _rope_tables(positions):
    b, s = positions.shape
    inv = ROPE_THETA ** (-jnp.arange(HALF, dtype=jnp.float32) / HALF)
    inv_full = jnp.concatenate([inv, inv]).reshape(1, HEAD_DIM)
    out = jax.ShapeDtypeStruct((b, s, HEAD_DIM), jnp.float32)
    return pl.pallas_call(
        _rope_table_kernel,
        grid=(b,),
        in_specs=[pl.BlockSpec((None, s, 1), lambda i: (i, 0, 0)),
                  pl.BlockSpec((1, HEAD_DIM), lambda i: (0, 0))],
        out_specs=[pl.BlockSpec((None, s, HEAD_DIM), lambda i: (i, 0, 0)),
                   pl.BlockSpec((None, s, HEAD_DIM), lambda i: (i, 0, 0))],
        out_shape=[out, out],
        compiler_params=_cparams(("parallel",)),
        name="rope_tables",
    )(positions.reshape(b, s, 1), inv_full)


def _rope(xf, cos, sin_signed):
    return xf * cos + pltpu.roll(xf, HALF, 1) * sin_signed


def _compress_kernel(x_ref, pe_ref, w1_ref, w2_ref, o_ref, xs_ref):
    s = x_ref.shape[0]
    n_rows = s // CMP_STRIDE
    xs_ref[pl.ds(0, s), :] = x_ref[...].astype(jnp.float32)
    xs_ref[pl.ds(s, CMP_STRIDE), :] = jnp.zeros((CMP_STRIDE, HEAD_DIM), jnp.float32)
    flat = jnp.concatenate(
        [(xs_ref[pl.ds(r, n_rows, stride=CMP_STRIDE), :] + pe_ref[pl.ds(r, 1), :]).astype(jnp.bfloat16)
         for r in range(CMP_LEN)], axis=1)
    hid = jax.nn.gelu(jnp.dot(flat, w1_ref[...], preferred_element_type=jnp.float32))
    o_ref[...] = jnp.dot(hid.astype(jnp.bfloat16), w2_ref[...],
                         preferred_element_type=jnp.float32).astype(o_ref.dtype)


def _compress(proj, batch, s, pe, w1, w2):
    n_rows = s // CMP_STRIDE
    return pl.pallas_call(
        _compress_kernel,
        grid=(batch, 2 * N_KV_A),
        in_specs=[
            pl.BlockSpec((s, HEAD_DIM), lambda b, h: (b, H_KC + h)),
            pl.BlockSpec((None, CMP_LEN, HEAD_DIM), lambda b, h: (h // N_KV_A, 0, 0)),
            pl.BlockSpec((None, CMP_LEN * HEAD_DIM, CMP_HIDDEN), lambda b, h: (h // N_KV_A, 0, 0)),
            pl.BlockSpec((None, CMP_HIDDEN, HEAD_DIM), lambda b, h: (h // N_KV_A, 0, 0)),
        ],
        out_specs=pl.BlockSpec((None, None, n_rows, HEAD_DIM), lambda b, h: (b, h, 0, 0)),
        out_shape=jax.ShapeDtypeStruct((batch, 2 * N_KV_A, n_rows, HEAD_DIM), jnp.bfloat16),
        scratch_shapes=[pltpu.VMEM((s + CMP_STRIDE, HEAD_DIM), jnp.float32)],
        compiler_params=_cparams(("parallel", "parallel")),
        name="compress",
    )(proj, pe, w1, w2)


def _kq(k, q):
    return lax.dot_general(k, q, (((1,), (1,)), ((), ())),
                           preferred_element_type=jnp.float32)


def _memo(fn):
    cache = []

    def get():
        if not cache:
            cache.append(fn())
        return cache[0]
    return get


class _AttnJob:
    def __init__(self, q_fn, chunks):
        self.q_fn, self.chunks = q_fn, chunks
        self.s, self.m, self.acc = [], None, None

    def _score(self, c):
        k_fn, _, mask_fn = self.chunks[c]
        s = mask_fn(_kq(k_fn(), self.q_fn()))
        self.s.append(s)
        cm = jnp.max(s, axis=0, keepdims=True)
        self.m = cm if self.m is None else jnp.maximum(self.m, cm)

    def _value(self, cs):
        p = jnp.concatenate([jnp.exp2(self.s[c] - self.m).astype(jnp.bfloat16) for c in cs], axis=0)
        vt = jnp.concatenate([self.chunks[c][1]() for c in cs], axis=1)
        part = jnp.dot(vt, p, preferred_element_type=jnp.float32)
        self.acc = part if self.acc is None else self.acc + part

    def score_tasks(self):
        return [functools.partial(self._score, c) for c in range(len(self.chunks))]

    def value_tasks(self):
        n = len(self.chunks)
        return [functools.partial(self._value, range(c, min(c + PV_GROUP, n)))
                for c in range(0, n, PV_GROUP)]

    def result(self):
        return self.acc[:HEAD_DIM, :] * (1.0 / self.acc[HEAD_DIM:HEAD_DIM + 1, :])


def _round_robin(task_lists):
    for i in range(max(len(t) for t in task_lists)):
        for tasks in task_lists:
            if i < len(tasks):
                tasks[i]()


def _run_pipelined(streams, on_done):
    n = len(streams[0])
    _round_robin([jobs[0].score_tasks() for jobs in streams])
    for k in range(n):
        lists = []
        for jobs in streams:
            lists.append(jobs[k].value_tasks())
            if k + 1 < n:
                lists.append(jobs[k + 1].score_tasks())
        _round_robin(lists)
        for si, jobs in enumerate(streams):
            on_done(si, k, jobs[k])


def _store_values_transposed(src_ref, dst_ref):
    s_len = src_ref.shape[0]
    for c in range(s_len // TK):
        blk = src_ref[c * TK:(c + 1) * TK, :].astype(jnp.float32)
        dst_ref[0:HEAD_DIM, c * TK:(c + 1) * TK] = blk.T.astype(jnp.bfloat16)
    row = lax.broadcasted_iota(jnp.int32, (BF16_SUBLANES, s_len), 0)
    dst_ref[HEAD_DIM:, :] = jnp.where(row == 0, 1.0, 0.0).astype(jnp.bfloat16)


def _silu(z):
    return z * jax.nn.sigmoid(z)


def _topk_active(q0):
    return q0 + TQ > TOP_N * SEL_LEN


def _nsa_selection_bias(p_sum, q0, s_len):
    n_sel = s_len // SEL_LEN
    n_cmp = s_len // CMP_STRIDE - CMP_LEN // CMP_STRIDE + 1
    oj = lax.broadcasted_iota(jnp.int32, (n_sel, HEAD_DIM), 0)
    oc = lax.broadcasted_iota(jnp.int32, (n_sel, HEAD_DIM), 1)
    ovl = jnp.where(oc * CMP_STRIDE < oj * SEL_LEN + SEL_LEN,
                    jnp.where(oc * CMP_STRIDE + CMP_LEN > oj * SEL_LEN, 1.0, 0.0), 0.0)
    ovl = jnp.where(oc < n_cmp, ovl, 0.0).astype(jnp.bfloat16)
    p_hi = p_sum.astype(jnp.bfloat16)
    p_lo = (p_sum - p_hi.astype(jnp.float32)).astype(jnp.bfloat16)
    imp = (jnp.dot(ovl, p_hi, preferred_element_type=jnp.float32)
           + jnp.dot(ovl, p_lo, preferred_element_type=jnp.float32))
    j_row = lax.broadcasted_iota(jnp.int32, (n_sel, TQ), 0)
    t_sel = q0 + lax.broadcasted_iota(jnp.int32, (n_sel, TQ), 1)
    cur = lax.shift_right_arithmetic(t_sel, int(math.log2(SEL_LEN)))
    forced = (j_row == 0) | (j_row == cur) | (j_row == cur - 1)
    imp = jnp.where(forced, imp + FORCE_BONUS, imp)
    imp = jnp.where(j_row * SEL_LEN <= t_sel, imp, -jnp.inf)
    rank = jnp.zeros((n_sel, TQ), jnp.float32)
    for i in range(n_sel):
        vi = imp[i:i + 1, :]
        rank = rank + jnp.where(j_row > i, jnp.where(vi >= imp, 1.0, 0.0),
                                jnp.where(vi > imp, 1.0, 0.0))
    return jnp.where(rank < TOP_N, 0.0, NEG)


def _mixers_kernel(q_ref, z_ref, g_ref, bg_ref, kc_ref, vc_ref, ksl_ref, vsl_ref, kw_ref, vw_ref,
                   cos_ref, sin_ref, qd_ref, zd_ref, kd_ref, vd_ref, dbias_ref, o_ref, od_ref,
                   ksl_rot, kw_rot, vsl_t, vw_t, ocmp_s, selb_s, gate_s):
    s_len = ksl_ref.shape[0]
    n_q = s_len // TQ

    ksl_rot[...] = _rope(ksl_ref[...].astype(jnp.float32), cos_ref[...], sin_ref[...]).astype(jnp.bfloat16)
    kw_rot[...] = _rope(kw_ref[...].astype(jnp.float32), cos_ref[...], sin_ref[...]).astype(jnp.bfloat16)
    _store_values_transposed(vsl_ref, vsl_t)
    _store_values_transposed(vw_ref, vw_t)
    kc = kc_ref[...]
    vc_t = vc_ref[...].astype(jnp.float32).T.astype(jnp.bfloat16)
    bg = bg_ref[...]
    kv_head = pl.program_id(1)

    c_row = lax.broadcasted_iota(jnp.int32, (HEAD_DIM, TQ), 0)
    lane_q = lax.broadcasted_iota(jnp.int32, (HEAD_DIM, TQ), 1)
    lane_one = lax.broadcasted_iota(jnp.int32, (1, TQ), 1)
    for qi in range(n_q):
        q0 = qi * TQ
        rows = slice(q0, q0 + TQ)
        q_raw = [(q_ref[rows, g * HEAD_DIM:(g + 1) * HEAD_DIM].astype(jnp.float32) * QSCALE
                  ).astype(jnp.bfloat16) for g in range(GROUP_A)]
        c_valid = c_row * CMP_STRIDE + (CMP_LEN - 1) <= q0 + lane_q
        any_valid = jnp.where(q0 + lane_one >= CMP_LEN - 1, 1.0, 0.0)
        sc4 = _kq(kc, jnp.concatenate(q_raw, axis=0))
        p_heads = []
        p_sum = jnp.zeros((HEAD_DIM, TQ), jnp.float32)
        for g in range(GROUP_A):
            sc = jnp.where(c_valid, sc4[:, g * TQ:(g + 1) * TQ], NEG)
            e = jnp.exp2(sc - jnp.max(sc, axis=0, keepdims=True))
            p = e * (any_valid / jnp.sum(e, axis=0, keepdims=True))
            p_heads.append(p.astype(jnp.bfloat16))
            p_sum = p_sum + p
        o_cmp4 = jnp.dot(vc_t, jnp.concatenate(p_heads, axis=1),
                         preferred_element_type=jnp.float32)
        for g in range(GROUP_A):
            ocmp_s[g, :, rows] = o_cmp4[:, g * TQ:(g + 1) * TQ]
        if _topk_active(q0):
            selb_s[qi] = _nsa_selection_bias(p_sum, q0, s_len)
        gate_s[:, rows] = jax.nn.sigmoid(g_ref[rows, :] + bg).T

    diag_rc = (lax.broadcasted_iota(jnp.int32, (TK, TQ), 0)
               - lax.broadcasted_iota(jnp.int32, (TK, TQ), 1))

    def slc_mask(qi, c, s):
        if _topk_active(qi * TQ):
            s = s + jnp.concatenate(
                [jnp.broadcast_to(selb_s[qi, c * SEL_PER_TK + a:c * SEL_PER_TK + a + 1, :],
                                  (SEL_LEN, TQ)) for a in range(SEL_PER_TK)], axis=0)
        if c == qi:
            s = jnp.where(diag_rc <= 0, s, NEG)
        return s

    def win_mask(qi, c, s):
        if c == qi:
            return jnp.where(diag_rc <= 0, s, NEG)
        if (qi - c) * TK + TQ - 1 > WIN - 1:
            return jnp.where(diag_rc >= (qi - c) * TK - (WIN - 1), s, NEG)
        return s

    ones_rows = jnp.where(lax.broadcasted_iota(jnp.int32, (BF16_SUBLANES, TK), 0) == 0,
                          1.0, 0.0).astype(jnp.bfloat16)

    def head_body(g, carry):
        col = pl.ds(pl.multiple_of(g * HEAD_DIM, HEAD_DIM), HEAD_DIM)
        gate_row = GATE_PER_KV * kv_head + 3 * g
        blocks = [slice(c * TK, (c + 1) * TK) for c in range(n_q)]

        def roped(ref, rows, scale):
            xf = ref[rows, col].astype(jnp.float32)
            if scale != 1.0:
                xf = xf * scale
            return _rope(xf, cos_ref[rows, :], sin_ref[rows, :]).astype(jnp.bfloat16)

        def dil_value_t(rows):
            vt = vd_ref[rows, col].astype(jnp.float32).T.astype(jnp.bfloat16)
            return jnp.concatenate([vt, ones_rows], axis=0)

        def dil_bias(delta, s):
            r0 = (n_q - 1 - delta) * TK
            return s + dbias_ref[r0:r0 + TK, :]

        dil_k = [_memo(functools.partial(roped, kd_ref, rows, 1.0)) for rows in blocks]
        dil_v = [_memo(functools.partial(dil_value_t, rows)) for rows in blocks]
        slc_jobs, win_jobs, dil_jobs = [], [], []
        for qi in range(n_q):
            q = _memo(functools.partial(roped, q_ref, blocks[qi], QSCALE))
            first_win = max(0, (qi * TQ - (WIN - 1)) // TK)
            for jobs, k_ref_, v_ref_, mask, c_lo in ((slc_jobs, ksl_rot, vsl_t, slc_mask, 0),
                                                     (win_jobs, kw_rot, vw_t, win_mask, first_win)):
                jobs.append(_AttnJob(q, [
                    (functools.partial(lambda r, c: r[c * TK:(c + 1) * TK, :], k_ref_, c),
                     functools.partial(lambda r, c: r[:, c * TK:(c + 1) * TK], v_ref_, c),
                     functools.partial(mask, qi, c))
                    for c in range(c_lo, qi + 1)]))
            dil_jobs.append(_AttnJob(
                _memo(functools.partial(roped, qd_ref, blocks[qi], QSCALE)),
                [(dil_k[c], dil_v[c], functools.partial(dil_bias, qi - c)) for c in range(qi + 1)]))

        def on_done(si, k, job):
            kind, qi = si % 3, MIX_SPLIT * k + si // 3
            rows = blocks[qi]
            if kind == 1:
                o_t = (gate_s[pl.ds(gate_row, 1), rows] * ocmp_s[g, :, rows]
                       + gate_s[pl.ds(gate_row + 1, 1), rows] * slc_jobs[qi].result()
                       + gate_s[pl.ds(gate_row + 2, 1), rows] * job.result())
                z = z_ref[rows, col].astype(jnp.float32)
                o_ref[rows, col] = (o_t.T * _silu(z)).astype(o_ref.dtype)
            elif kind == 2:
                z = zd_ref[rows, col].astype(jnp.float32)
                od_ref[rows, col] = (job.result().T * _silu(z)).astype(od_ref.dtype)

        streams = []
        for r in range(MIX_SPLIT):
            streams += [slc_jobs[r::MIX_SPLIT], win_jobs[r::MIX_SPLIT], dil_jobs[r::MIX_SPLIT]]
        _run_pipelined(streams, on_done)
        return carry

    lax.fori_loop(0, GROUP_A, head_body, 0)


def _mixers(front, back, gates, cmp_kv, b_gate_row, cos, sin, dil_bias, batch, s):
    gw = GROUP_A * HEAD_DIM
    assert N_HEADS_B == N_HEADS_A
    full = lambda col0: pl.BlockSpec((s, HEAD_DIM), lambda b, h: (b, col0 + h))
    group = lambda head0: pl.BlockSpec((s, gw), lambda b, h: (b, head0 // GROUP_A + h))
    out = jax.ShapeDtypeStruct((batch * s, N_HEADS_A * HEAD_DIM), jnp.bfloat16)
    rarely_changes = dict(pipeline_mode=pl.Buffered(1))
    return pl.pallas_call(
        _mixers_kernel,
        grid=(batch, N_KV_A),
        in_specs=[
            group(H_QA), group(H_ZA),
            pl.BlockSpec((s, HEAD_DIM), lambda b, h: (b, 0)),
            pl.BlockSpec((1, HEAD_DIM), lambda b, h: (0, 0)),
            pl.BlockSpec((None, None, s // CMP_STRIDE, HEAD_DIM), lambda b, h: (b, h, 0, 0)),
            pl.BlockSpec((None, None, s // CMP_STRIDE, HEAD_DIM), lambda b, h: (b, N_KV_A + h, 0, 0)),
            full(H_KSL), full(H_VSL), full(H_KW), full(H_VW),
            pl.BlockSpec((None, s, HEAD_DIM), lambda b, h: (b, 0, 0), **rarely_changes),
            pl.BlockSpec((None, s, HEAD_DIM), lambda b, h: (b, 0, 0), **rarely_changes),
            group(H_QB), group(H_ZB), group(H_KB), group(H_VB),
            pl.BlockSpec((s, TQ), lambda b, h: (0, 0), **rarely_changes),
        ],
        out_specs=[pl.BlockSpec((s, gw), lambda b, h: (b, h)),
                   pl.BlockSpec((s, gw), lambda b, h: (b, h))],
        out_shape=[out, out],
        scratch_shapes=[pltpu.VMEM((s, HEAD_DIM), jnp.bfloat16),
                        pltpu.VMEM((s, HEAD_DIM), jnp.bfloat16),
                        pltpu.VMEM((V_ROWS, s), jnp.bfloat16),
                        pltpu.VMEM((V_ROWS, s), jnp.bfloat16),
                        pltpu.VMEM((GROUP_A, HEAD_DIM, s), jnp.float32),
                        pltpu.VMEM((s // TQ, s // SEL_LEN, TQ), jnp.float32),
                        pltpu.VMEM((HEAD_DIM, s), jnp.float32)],
        compiler_params=pltpu.CompilerParams(dimension_semantics=("parallel", "parallel"),
                                             vmem_limit_bytes=MIXERS_VMEM_LIMIT),
        name="token_mixers",
    )(front, front, gates, b_gate_row, cmp_kv, cmp_kv, back, back, back, back, cos, sin,
      back, back, back, back, dil_bias)


def _dilated_multiplicity(dist):
    cnt = jnp.zeros(dist.shape, jnp.float32)
    for window, dil in DILATIONS:
        cnt = cnt + jnp.where((dist & (dil - 1)) == 0,
                              jnp.where(dist <= (window // dil) * dil, 1.0, 0.0), 0.0)
    return jnp.where(dist >= 0, cnt, 0.0)


def _dilated_table_kernel(bias_ref):
    s_len = bias_ref.shape[0]
    dist = (s_len - TQ + lax.broadcasted_iota(jnp.int32, (s_len, TQ), 1)
            - lax.broadcasted_iota(jnp.int32, (s_len, TQ), 0))
    cnt = _dilated_multiplicity(dist)
    bias_ref[...] = jnp.where(cnt > 0.5, jnp.log2(jnp.maximum(cnt, 1.0)), NEG)


def _dilated_tables(s):
    return pl.pallas_call(
        _dilated_table_kernel,
        out_shape=jax.ShapeDtypeStruct((s, TQ), jnp.float32),
        compiler_params=pltpu.CompilerParams(vmem_limit_bytes=VMEM_LIMIT),
        name="dilated_tables",
    )()


OUT_PANELS = 4


def _out_proj_kernel(ma_ref, mb_ref, wa_ref, wb_ref, x_ref, nw_ref, o_ref):
    d = o_ref.shape[1]
    pn = d // OUT_PANELS
    ma, mb = ma_ref[...], mb_ref[...]
    ys, ssq = [], None
    for p in range(OUT_PANELS):
        cols = slice(p * pn, (p + 1) * pn)
        y = (jnp.dot(ma, wa_ref[:, cols], preferred_element_type=jnp.float32)
             + jnp.dot(mb, wb_ref[:, cols], preferred_element_type=jnp.float32))
        part = jnp.sum(y * y, axis=-1, keepdims=True)
        ssq = part if ssq is None else ssq + part
        ys.append(y)
    scale = lax.rsqrt(ssq * (1.0 / d) + EPS)
    for p in range(OUT_PANELS):
        cols = slice(p * pn, (p + 1) * pn)
        o_ref[:, cols] = x_ref[:, cols] + ys[p] * scale * nw_ref[:, cols]


def _out_proj(mix_a, mix_b, w, x2d, post_w, tm=256):
    m, d = x2d.shape
    ka, kb = mix_a.shape[1], mix_b.shape[1]
    assert ka == kb and w.shape[0] == ka + kb
    resident = dict(pipeline_mode=pl.Buffered(1))
    return pl.pallas_call(
        _out_proj_kernel,
        grid=(m // tm,),
        in_specs=[
            pl.BlockSpec((tm, ka), lambda i: (i, 0)),
            pl.BlockSpec((tm, kb), lambda i: (i, 0)),
            pl.BlockSpec((ka, d), lambda i: (0, 0), **resident),
            pl.BlockSpec((kb, d), lambda i: (1, 0), **resident),
            pl.BlockSpec((tm, d), lambda i: (i, 0)),
            pl.BlockSpec((1, d), lambda i: (0, 0)),
        ],
        out_specs=pl.BlockSpec((tm, d), lambda i: (i, 0)),
        out_shape=jax.ShapeDtypeStruct((m, d), jnp.float32),
        compiler_params=pltpu.CompilerParams(dimension_semantics=("parallel",),
                                             vmem_limit_bytes=OUT_PROJ_VMEM_LIMIT),
        name="out_proj_norm_residual",
    )(mix_a, mix_b, w, w, x2d, post_w.reshape(1, d))


def _layer(x, cos, sin, pre_w, post_w, w_in, b_gate, pe_k, pe_v, wk1, wk2, wv1, wv2, w_out):
    b, s, d = x.shape
    x2d = x.reshape(b * s, d)
    w_t = w_in.T
    h, gates = _rmsnorm(x2d, pre_w, w_t)
    front, back = _in_proj(h, w_t)

    pe = jnp.stack([pe_k, pe_v])
    w1 = jnp.stack([wk1, wv1]).astype(jnp.bfloat16)
    w2 = jnp.stack([wk2, wv2]).astype(jnp.bfloat16)
    cmp_kv = _compress(back, b, s, pe, w1, w2)

    bg = jnp.pad(b_gate.reshape(1, -1), ((0, 0), (0, HEAD_DIM - b_gate.size)))
    mix_a, mix_b = _mixers(front, back, gates, cmp_kv, bg, cos, sin, _dilated_tables(s), b, s)

    out = _out_proj(mix_a, mix_b, w_out.astype(jnp.bfloat16), x2d, post_w)
    return out.reshape(b, s, d)


def kernel(x, positions, pre_norm_w, post_norm_w, w_in, b_gate, cmp_pe_k, cmp_pe_v,
           cmp_wk1, cmp_wk2, cmp_wv1, cmp_wv2, w_out):
    cos, sin = _rope_tables(positions)
    for l in range(pre_norm_w.shape[0]):
        x = _layer(x, cos, sin, pre_norm_w[l], post_norm_w[l], w_in[l], b_gate[l],
                   cmp_pe_k[l], cmp_pe_v[l], cmp_wk1[l], cmp_wk2[l], cmp_wv1[l], cmp_wv2[l],
                   w_out[l])
    return x
```

```python
import functools
import math

import jax
import jax.numpy as jnp
from jax import lax
from jax.experimental import pallas as pl
from jax.experimental.pallas import tpu as pltpu

HEAD_DIM = 128
HALF = HEAD_DIM // 2
N_HEADS_A = 16
N_KV_A = 4
GROUP_A = 4
N_HEADS_B = 16
CMP_LEN = 32
CMP_STRIDE = 16
CMP_HIDDEN = 256
SEL_LEN = 64
TOP_N = 16
WIN = 512
DILATIONS = ((128, 1), (512, 4), (2048, 16))
ROPE_THETA = 10000.0
EPS = 1e-6
FORCE_BONUS = 1e4
NEG = -1e30
SCALE = HEAD_DIM ** -0.5
QSCALE = SCALE * math.log2(math.e)

H_QA = 0
H_ZA = H_QA + N_HEADS_A
N_FRONT_HEADS = H_ZA + N_HEADS_A
GATE_PER_KV = 3 * GROUP_A
H_KC = 0
H_VC = H_KC + N_KV_A
H_KSL = H_VC + N_KV_A
H_VSL = H_KSL + N_KV_A
H_KW = H_VSL + N_KV_A
H_VW = H_KW + N_KV_A
H_QB = H_VW + N_KV_A
H_KB = H_QB + N_HEADS_B
H_VB = H_KB + N_HEADS_B
H_ZB = H_VB + N_HEADS_B
N_BACK_HEADS = H_ZB + N_HEADS_B

TQ = 256
TK = 256
SEL_PER_TK = TK // SEL_LEN
F32_SUBLANES = 8
BF16_SUBLANES = 16
V_ROWS = HEAD_DIM + BF16_SUBLANES
PV_GROUP = 2
MIX_SPLIT = 2
VMEM_LIMIT = 56 * 1024 * 1024
MIXERS_VMEM_LIMIT = 60 * 1024 * 1024
OUT_PROJ_VMEM_LIMIT = 60 * 1024 * 1024


def _cparams(sem):
    return pltpu.CompilerParams(dimension_semantics=sem, vmem_limit_bytes=VMEM_LIMIT)


GATE_ROW0 = 2 * N_HEADS_A * HEAD_DIM
BACK_ROW0 = GATE_ROW0 + 3 * N_HEADS_A


def _nt_dot(x, w):
    return lax.dot_general(x, w, (((1,), (1,)), ((), ())), preferred_element_type=jnp.float32)


def _rmsnorm_kernel(x_ref, w_ref, wg_ref, o_ref, g_ref):
    x = x_ref[...]
    ms = jnp.mean(x * x, axis=-1, keepdims=True)
    h = (x * lax.rsqrt(ms + EPS) * w_ref[...]).astype(o_ref.dtype)
    o_ref[...] = h
    g_ref[...] = _nt_dot(h, wg_ref[...].astype(jnp.bfloat16))


def _rmsnorm(x2d, w, w_t, tm=256):
    m, d = x2d.shape
    return pl.pallas_call(
        _rmsnorm_kernel,
        grid=(m // tm,),
        in_specs=[pl.BlockSpec((tm, d), lambda i: (i, 0)),
                  pl.BlockSpec((1, d), lambda i: (0, 0)),
                  pl.BlockSpec((HEAD_DIM, d), lambda i: (GATE_ROW0 // HEAD_DIM, 0))],
        out_specs=[pl.BlockSpec((tm, d), lambda i: (i, 0)),
                   pl.BlockSpec((tm, HEAD_DIM), lambda i: (i, 0))],
        out_shape=[jax.ShapeDtypeStruct((m, d), jnp.bfloat16),
                   jax.ShapeDtypeStruct((m, HEAD_DIM), jnp.float32)],
        compiler_params=_cparams(("parallel",)),
        name="pre_rmsnorm_gates",
    )(x2d, w.reshape(1, d), w_t)


PROJ_TN = GROUP_A * HEAD_DIM
PROJ_FRONT = GATE_ROW0 // PROJ_TN


def _in_proj_kernel(x_ref, w_ref, front_ref, back_ref):
    j = pl.program_id(1)

    @pl.when(j < PROJ_FRONT)
    def _():
        front_ref[...] = _nt_dot(x_ref[...], w_ref[...].astype(jnp.bfloat16)).astype(front_ref.dtype)

    @pl.when(j >= PROJ_FRONT)
    def _():
        back_ref[...] = _nt_dot(x_ref[...], w_ref[...].astype(jnp.bfloat16)).astype(back_ref.dtype)


def _in_proj(h, w_t, tm=2048):
    m, k = h.shape
    n_back = N_BACK_HEADS * HEAD_DIM
    n_steps = PROJ_FRONT + n_back // PROJ_TN

    def w_row(i, j):
        row = PROJ_TN * j + jnp.where(j >= PROJ_FRONT, BACK_ROW0 - GATE_ROW0, 0)
        return pl.multiple_of(row, F32_SUBLANES), 0

    return pl.pallas_call(
        _in_proj_kernel,
        grid=(m // tm, n_steps),
        in_specs=[
            pl.BlockSpec((tm, k), lambda i, j: (i, 0), pipeline_mode=pl.Buffered(1)),
            pl.BlockSpec((pl.Element(PROJ_TN), pl.Element(k)), w_row)],
        out_specs=[
            pl.BlockSpec((tm, PROJ_TN), lambda i, j: (i, jnp.minimum(j, PROJ_FRONT - 1))),
            pl.BlockSpec((tm, PROJ_TN), lambda i, j: (i, jnp.maximum(j - PROJ_FRONT, 0)))],
        out_shape=[jax.ShapeDtypeStruct((m, N_FRONT_HEADS * HEAD_DIM), jnp.bfloat16),
                   jax.ShapeDtypeStruct((m, n_back), jnp.bfloat16)],
        compiler_params=_cparams(("parallel", "arbitrary")),
        name="in_proj",
    )(h, w_t)


def _rope_table_kernel(pos_ref, inv_ref, cos_ref, sin_ref):
    ang = pos_ref[...].astype(jnp.float32) * inv_ref[...]
    lane = lax.broadcasted_iota(jnp.int32, ang.shape, 1)
    cos_ref[...] = jnp.cos(ang)
    sin_ref[...] = jnp.where(lane < HALF, -jnp.sin(ang), jnp.sin(ang))


def _rope_tables(positions):
    b, s = positions.shape
    inv = ROPE_THETA ** (-jnp.arange(HALF, dtype=jnp.float32) / HALF)
    inv_full = jnp.concatenate([inv, inv]).reshape(1, HEAD_DIM)
    out = jax.ShapeDtypeStruct((b, s, HEAD_DIM), jnp.float32)
    return pl.pallas_call(
        _rope_table_kernel,
        grid=(b,),
        in_specs=[pl.BlockSpec((None, s, 1), lambda i: (i, 0, 0)),
                  pl.BlockSpec((1, HEAD_DIM), lambda i: (0, 0))],
        out_specs=[pl.BlockSpec((None, s, HEAD_DIM), lambda i: (i, 0, 0)),
                   pl.BlockSpec((None, s, HEAD_DIM), lambda i: (i, 0, 0))],
        out_shape=[out, out],
        compiler_params=_cparams(("parallel",)),
        name="rope_tables",
    )(positions.reshape(b, s, 1), inv_full)


def _rope(xf, cos, sin_signed):
    return xf * cos + pltpu.roll(xf, HALF, 1) * sin_signed


def _compress_kernel(x_ref, pe_ref, w1_ref, w2_ref, o_ref, xs_ref):
    s = x_ref.shape[0]
    n_rows = s // CMP_STRIDE
    xs_ref[pl.ds(0, s), :] = x_ref[...].astype(jnp.float32)
    xs_ref[pl.ds(s, CMP_STRIDE), :] = jnp.zeros((CMP_STRIDE, HEAD_DIM), jnp.float32)
    flat = jnp.concatenate(
        [(xs_ref[pl.ds(r, n_rows, stride=CMP_STRIDE), :] + pe_ref[pl.ds(r, 1), :]).astype(jnp.bfloat16)
         for r in range(CMP_LEN)], axis=1)
    hid = jax.nn.gelu(jnp.dot(flat, w1_ref[...], preferred_element_type=jnp.float32))
    o_ref[...] = jnp.dot(hid.astype(jnp.bfloat16), w2_ref[...],
                         preferred_element_type=jnp.float32).astype(o_ref.dtype)


def _compress(proj, batch, s, pe, w1, w2):
    n_rows = s // CMP_STRIDE
    return pl.pallas_call(
        _compress_kernel,
        grid=(batch, 2 * N_KV_A),
        in_specs=[
            pl.BlockSpec((s, HEAD_DIM), lambda b, h: (b, H_KC + h)),
            pl.BlockSpec((None, CMP_LEN, HEAD_DIM), lambda b, h: (h // N_KV_A, 0, 0)),
            pl.BlockSpec((None, CMP_LEN * HEAD_DIM, CMP_HIDDEN), lambda b, h: (h // N_KV_A, 0, 0)),
            pl.BlockSpec((None, CMP_HIDDEN, HEAD_DIM), lambda b, h: (h // N_KV_A, 0, 0)),
        ],
        out_specs=pl.BlockSpec((None, None, n_rows, HEAD_DIM), lambda b, h: (b, h, 0, 0)),
        out_shape=jax.ShapeDtypeStruct((batch, 2 * N_KV_A, n_rows, HEAD_DIM), jnp.bfloat16),
        scratch_shapes=[pltpu.VMEM((s + CMP_STRIDE, HEAD_DIM), jnp.float32)],
        compiler_params=_cparams(("parallel", "parallel")),
        name="compress",
    )(proj, pe, w1, w2)


def _kq(k, q):
    return lax.dot_general(k, q, (((1,), (1,)), ((), ())),
                           preferred_element_type=jnp.float32)


def _memo(fn):
    cache = []

    def get():
        if not cache:
            cache.append(fn())
        return cache[0]
    return get


class _AttnJob:
    def __init__(self, q_fn, chunks):
        self.q_fn, self.chunks = q_fn, chunks
        self.s, self.m, self.acc = [], None, None

    def _score(self, c):
        k_fn, _, mask_fn = self.chunks[c]
        s = mask_fn(_kq(k_fn(), self.q_fn()))
        self.s.append(s)
        cm = jnp.max(s, axis=0, keepdims=True)
        self.m = cm if self.m is None else jnp.maximum(self.m, cm)

    def _value(self, cs):
        p = jnp.concatenate([jnp.exp2(self.s[c] - self.m).astype(jnp.bfloat16) for c in cs], axis=0)
        vt = jnp.concatenate([self.chunks[c][1]() for c in cs], axis=1)
        part = jnp.dot(vt, p, preferred_element_type=jnp.float32)
        self.acc = part if self.acc is None else self.acc + part

    def score_tasks(self):
        return [functools.partial(self._score, c) for c in range(len(self.chunks))]

    def value_tasks(self):
        n = len(self.chunks)
        return [functools.partial(self._value, range(c, min(c + PV_GROUP, n)))
                for c in range(0, n, PV_GROUP)]

    def result(self):
        return self.acc[:HEAD_DIM, :] * (1.0 / self.acc[HEAD_DIM:HEAD_DIM + 1, :])


def _round_robin(task_lists):
    for i in range(max(len(t) for t in task_lists)):
        for tasks in task_lists:
            if i < len(tasks):
                tasks[i]()


def _run_pipelined(streams, on_done):
    n = len(streams[0])
    _round_robin([jobs[0].score_tasks() for jobs in streams])
    for k in range(n):
        lists = []
        for jobs in streams:
            lists.append(jobs[k].value_tasks())
            if k + 1 < n:
                lists.append(jobs[k + 1].score_tasks())
        _round_robin(lists)
        for si, jobs in enumerate(streams):
            on_done(si, k, jobs[k])


def _store_values_transposed(src_ref, dst_ref):
    s_len = src_ref.shape[0]
    for c in range(s_len // TK):
        blk = src_ref[c * TK:(c + 1) * TK, :].astype(jnp.float32)
        dst_ref[0:HEAD_DIM, c * TK:(c + 1) * TK] = blk.T.astype(jnp.bfloat16)
    row = lax.broadcasted_iota(jnp.int32, (BF16_SUBLANES, s_len), 0)
    dst_ref[HEAD_DIM:, :] = jnp.where(row == 0, 1.0, 0.0).astype(jnp.bfloat16)


def _silu(z):
    return z * jax.nn.sigmoid(z)


def _topk_active(q0):
    return q0 + TQ > TOP_N * SEL_LEN


def _nsa_selection_bias(p_sum, q0, s_len):
    n_sel = s_len // SEL_LEN
    n_cmp = s_len // CMP_STRIDE - CMP_LEN // CMP_STRIDE + 1
    oj = lax.broadcasted_iota(jnp.int32, (n_sel, HEAD_DIM), 0)
    oc = lax.broadcasted_iota(jnp.int32, (n_sel, HEAD_DIM), 1)
    ovl = jnp.where(oc * CMP_STRIDE < oj * SEL_LEN + SEL_LEN,
                    jnp.where(oc * CMP_STRIDE + CMP_LEN > oj * SEL_LEN, 1.0, 0.0), 0.0)
    ovl = jnp.where(oc < n_cmp, ovl, 0.0).astype(jnp.bfloat16)
    p_hi = p_sum.astype(jnp.bfloat16)
    p_lo = (p_sum - p_hi.astype(jnp.float32)).astype(jnp.bfloat16)
    imp = (jnp.dot(ovl, p_hi, preferred_element_type=jnp.float32)
           + jnp.dot(ovl, p_lo, preferred_element_type=jnp.float32))
    j_row = lax.broadcasted_iota(jnp.int32, (n_sel, TQ), 0)
    t_sel = q0 + lax.broadcasted_iota(jnp.int32, (n_sel, TQ), 1)
    cur = lax.shift_right_arithmetic(t_sel, int(math.log2(SEL_LEN)))
    forced = (j_row == 0) | (j_row == cur) | (j_row == cur - 1)
    imp = jnp.where(forced, imp + FORCE_BONUS, imp)
    imp = jnp.where(j_row * SEL_LEN <= t_sel, imp, -jnp.inf)
    rank = jnp.zeros((n_sel, TQ), jnp.float32)
    for i in range(n_sel):
        vi = imp[i:i + 1, :]
        rank = rank + jnp.where(j_row > i, jnp.where(vi >= imp, 1.0, 0.0),
                                jnp.where(vi > imp, 1.0, 0.0))
    return jnp.where(rank < TOP_N, 0.0, NEG)


def _mixers_kernel(q_ref, z_ref, g_ref, bg_ref, kc_ref, vc_ref, ksl_ref, vsl_ref, kw_ref, vw_ref,
                   cos_ref, sin_ref, qd_ref, zd_ref, kd_ref, vd_ref, dbias_ref, o_ref, od_ref,
                   ksl_rot, kw_rot, vsl_t, vw_t, ocmp_s, selb_s, gate_s):
    s_len = ksl_ref.shape[0]
    n_q = s_len // TQ

    ksl_rot[...] = _rope(ksl_ref[...].astype(jnp.float32), cos_ref[...], sin_ref[...]).astype(jnp.bfloat16)
    kw_rot[...] = _rope(kw_ref[...].astype(jnp.float32), cos_ref[...], sin_ref[...]).astype(jnp.bfloat16)
    _store_values_transposed(vsl_ref, vsl_t)
    _store_values_transposed(vw_ref, vw_t)
    kc = kc_ref[...]
    vc_t = vc_ref[...].astype(jnp.float32).T.astype(jnp.bfloat16)
    bg = bg_ref[...]
    kv_head = pl.program_id(1)

    c_row = lax.broadcasted_iota(jnp.int32, (HEAD_DIM, TQ), 0)
    lane_q = lax.broadcasted_iota(jnp.int32, (HEAD_DIM, TQ), 1)
    lane_one = lax.broadcasted_iota(jnp.int32, (1, TQ), 1)
    for qi in range(n_q):
        q0 = qi * TQ
        rows = slice(q0, q0 + TQ)
        q_raw = [(q_ref[rows, g * HEAD_DIM:(g + 1) * HEAD_DIM].astype(jnp.float32) * QSCALE
                  ).astype(jnp.bfloat16) for g in range(GROUP_A)]
        c_valid = c_row * CMP_STRIDE + (CMP_LEN - 1) <= q0 + lane_q
        any_valid = jnp.where(q0 + lane_one >= CMP_LEN - 1, 1.0, 0.0)
        sc4 = _kq(kc, jnp.concatenate(q_raw, axis=0))
        p_heads = []
        p_sum = jnp.zeros((HEAD_DIM, TQ), jnp.float32)
        for g in range(GROUP_A):
            sc = jnp.where(c_valid, sc4[:, g * TQ:(g + 1) * TQ], NEG)
            e = jnp.exp2(sc - jnp.max(sc, axis=0, keepdims=True))
            p = e * (any_valid / jnp.sum(e, axis=0, keepdims=True))
            p_heads.append(p.astype(jnp.bfloat16))
            p_sum = p_sum + p
        o_cmp4 = jnp.dot(vc_t, jnp.concatenate(p_heads, axis=1),
                         preferred_element_type=jnp.float32)
        for g in range(GROUP_A):
            ocmp_s[g, :, rows] = o_cmp4[:, g * TQ:(g + 1) * TQ]
        if _topk_active(q0):
            selb_s[qi] = _nsa_selection_bias(p_sum, q0, s_len)
        gate_s[:, rows] = jax.nn.sigmoid(g_ref[rows, :] + bg).T

    diag_rc = (lax.broadcasted_iota(jnp.int32, (TK, TQ), 0)
               - lax.broadcasted_iota(jnp.int32, (TK, TQ), 1))

    def slc_mask(qi, c, s):
        if _topk_active(qi * TQ):
            s = s + jnp.concatenate(
                [jnp.broadcast_to(selb_s[qi, c * SEL_PER_TK + a:c * SEL_PER_TK + a + 1, :],
                                  (SEL_LEN, TQ)) for a in range(SEL_PER_TK)], axis=0)
        if c == qi:
            s = jnp.where(diag_rc <= 0, s, NEG)
        return s

    def win_mask(qi, c, s):
        if c == qi:
            return jnp.where(diag_rc <= 0, s, NEG)
        if (qi - c) * TK + TQ - 1 > WIN - 1:
            return jnp.where(diag_rc >= (qi - c) * TK - (WIN - 1), s, NEG)
        return s

    ones_rows = jnp.where(lax.broadcasted_iota(jnp.int32, (BF16_SUBLANES, TK), 0) == 0,
                          1.0, 0.0).astype(jnp.bfloat16)

    def head_body(g, carry):
        col = pl.ds(pl.multiple_of(g * HEAD_DIM, HEAD_DIM), HEAD_DIM)
        gate_row = GATE_PER_KV * kv_head + 3 * g
        blocks = [slice(c * TK, (c + 1) * TK) for c in range(n_q)]

        def roped(ref, rows, scale):
            xf = ref[rows, col].astype(jnp.float32)
            if scale != 1.0:
                xf = xf * scale
            return _rope(xf, cos_ref[rows, :], sin_ref[rows, :]).astype(jnp.bfloat16)

        def dil_value_t(rows):
            vt = vd_ref[rows, col].astype(jnp.float32).T.astype(jnp.bfloat16)
            return jnp.concatenate([vt, ones_rows], axis=0)

        def dil_bias(delta, s):
            r0 = (n_q - 1 - delta) * TK
            return s + dbias_ref[r0:r0 + TK, :]

        dil_k = [_memo(functools.partial(roped, kd_ref, rows, 1.0)) for rows in blocks]
        dil_v = [_memo(functools.partial(dil_value_t, rows)) for rows in blocks]
        slc_jobs, win_jobs, dil_jobs = [], [], []
        for qi in range(n_q):
            q = _memo(functools.partial(roped, q_ref, blocks[qi], QSCALE))
            first_win = max(0, (qi * TQ - (WIN - 1)) // TK)
            for jobs, k_ref_, v_ref_, mask, c_lo in ((slc_jobs, ksl_rot, vsl_t, slc_mask, 0),
                                                     (win_jobs, kw_rot, vw_t, win_mask, first_win)):
                jobs.append(_AttnJob(q, [
                    (functools.partial(lambda r, c: r[c * TK:(c + 1) * TK, :], k_ref_, c),
                     functools.partial(lambda r, c: r[:, c * TK:(c + 1) * TK], v_ref_, c),
                     functools.partial(mask, qi, c))
                    for c in range(c_lo, qi + 1)]))
            dil_jobs.append(_AttnJob(
                _memo(functools.partial(roped, qd_ref, blocks[qi], QSCALE)),
                [(dil_k[c], dil_v[c], functools.partial(dil_bias, qi - c)) for c in range(qi + 1)]))

        def on_done(si, k, job):
            kind, qi = si % 3, MIX_SPLIT * k + si // 3
            rows = blocks[qi]
            if kind == 1:
                o_t = (gate_s[pl.ds(gate_row, 1), rows] * ocmp_s[g, :, rows]
                       + gate_s[pl.ds(gate_row + 1, 1), rows] * slc_jobs[qi].result()
                       + gate_s[pl.ds(gate_row + 2, 1), rows] * job.result())
                z = z_ref[rows, col].astype(jnp.float32)
                o_ref[rows, col] = (o_t.T * _silu(z)).astype(o_ref.dtype)
            elif kind == 2:
                z = zd_ref[rows, col].astype(jnp.float32)
                od_ref[rows, col] = (job.result().T * _silu(z)).astype(od_ref.dtype)

        streams = []
        for r in range(MIX_SPLIT):
            streams += [slc_jobs[r::MIX_SPLIT], win_jobs[r::MIX_SPLIT], dil_jobs[r::MIX_SPLIT]]
        _run_pipelined(streams, on_done)
        return carry

    lax.fori_loop(0, GROUP_A, head_body, 0)


def _mixers(front, back, gates, cmp_kv, b_gate_row, cos, sin, dil_bias, batch, s):
    gw = GROUP_A * HEAD_DIM
    assert N_HEADS_B == N_HEADS_A
    full = lambda col0: pl.BlockSpec((s, HEAD_DIM), lambda b, h: (b, col0 + h))
    group = lambda head0: pl.BlockSpec((s, gw), lambda b, h: (b, head0 // GROUP_A + h))
    out = jax.ShapeDtypeStruct((batch * s, N_HEADS_A * HEAD_DIM), jnp.bfloat16)
    rarely_changes = dict(pipeline_mode=pl.Buffered(1))
    return pl.pallas_call(
        _mixers_kernel,
        grid=(batch, N_KV_A),
        in_specs=[
            group(H_QA), group(H_ZA),
            pl.BlockSpec((s, HEAD_DIM), lambda b, h: (b, 0)),
            pl.BlockSpec((1, HEAD_DIM), lambda b, h: (0, 0)),
            pl.BlockSpec((None, None, s // CMP_STRIDE, HEAD_DIM), lambda b, h: (b, h, 0, 0)),
            pl.BlockSpec((None, None, s // CMP_STRIDE, HEAD_DIM), lambda b, h: (b, N_KV_A + h, 0, 0)),
            full(H_KSL), full(H_VSL), full(H_KW), full(H_VW),
            pl.BlockSpec((None, s, HEAD_DIM), lambda b, h: (b, 0, 0), **rarely_changes),
            pl.BlockSpec((None, s, HEAD_DIM), lambda b, h: (b, 0, 0), **rarely_changes),
            group(H_QB), group(H_ZB), group(H_KB), group(H_VB),
            pl.BlockSpec((s, TQ), lambda b, h: (0, 0), **rarely_changes),
        ],
        out_specs=[pl.BlockSpec((s, gw), lambda b, h: (b, h)),
                   pl.BlockSpec((s, gw), lambda b, h: (b, h))],
        out_shape=[out, out],
        scratch_shapes=[pltpu.VMEM((s, HEAD_DIM), jnp.bfloat16),
                        pltpu.VMEM((s, HEAD_DIM), jnp.bfloat16),
                        pltpu.VMEM((V_ROWS, s), jnp.bfloat16),
                        pltpu.VMEM((V_ROWS, s), jnp.bfloat16),
                        pltpu.VMEM((GROUP_A, HEAD_DIM, s), jnp.float32),
                        pltpu.VMEM((s // TQ, s // SEL_LEN, TQ), jnp.float32),
                        pltpu.VMEM((HEAD_DIM, s), jnp.float32)],
        compiler_params=pltpu.CompilerParams(dimension_semantics=("parallel", "parallel"),
                                             vmem_limit_bytes=MIXERS_VMEM_LIMIT),
        name="token_mixers",
    )(front, front, gates, b_gate_row, cmp_kv, cmp_kv, back, back, back, back, cos, sin,
      back, back, back, back, dil_bias)


def _dilated_multiplicity(dist):
    cnt = jnp.zeros(dist.shape, jnp.float32)
    for window, dil in DILATIONS:
        cnt = cnt + jnp.where((dist & (dil - 1)) == 0,
                              jnp.where(dist <= (window // dil) * dil, 1.0, 0.0), 0.0)
    return jnp.where(dist >= 0, cnt, 0.0)


def _dilated_table_kernel(bias_ref):
    s_len = bias_ref.shape[0]
    dist = (s_len - TQ + lax.broadcasted_iota(jnp.int32, (s_len, TQ), 1)
            - lax.broadcasted_iota(jnp.int32, (s_len, TQ), 0))
    cnt = _dilated_multiplicity(dist)
    bias_ref[...] = jnp.where(cnt > 0.5, jnp.log2(jnp.maximum(cnt, 1.0)), NEG)


def _dilated_tables(s):
    return pl.pallas_call(
        _dilated_table_kernel,
        out_shape=jax.ShapeDtypeStruct((s, TQ), jnp.float32),
        compiler_params=pltpu.CompilerParams(vmem_limit_bytes=VMEM_LIMIT),
        name="dilated_tables",
    )()


OUT_PANELS = 4


def _out_proj_kernel(ma_ref, mb_ref, wa_ref, wb_ref, x_ref, nw_ref, o_ref, y_s, ssq_s):
    i = pl.program_id(0)
    n_tiles = pl.num_programs(0) - 1
    d = o_ref.shape[1]
    pn = d // OUT_PANELS

    def step(multiply, finish):
        if finish:
            scale = lax.rsqrt(ssq_s[...] * (1.0 / d) + EPS)
        ssq = None
        for p in range(OUT_PANELS):
            cols = slice(p * pn, (p + 1) * pn)
            if finish:
                o_ref[:, cols] = x_ref[:, cols] + y_s[:, cols] * scale * nw_ref[:, cols]
            if multiply:
                y = (jnp.dot(ma_ref[...], wa_ref[:, cols], preferred_element_type=jnp.float32)
                     + jnp.dot(mb_ref[...], wb_ref[:, cols], preferred_element_type=jnp.float32))
                y_s[:, cols] = y
                part = jnp.sum(y * y, axis=-1, keepdims=True)
                ssq = part if ssq is None else ssq + part
        if multiply:
            ssq_s[...] = ssq

    pl.when(i == 0)(functools.partial(step, True, False))
    pl.when((i > 0) & (i < n_tiles))(functools.partial(step, True, True))
    pl.when(i == n_tiles)(functools.partial(step, False, True))


def _out_proj(mix_a, mix_b, w, x2d, post_w, tm=256):
    m, d = x2d.shape
    ka, kb = mix_a.shape[1], mix_b.shape[1]
    assert ka == kb and w.shape[0] == ka + kb
    n_tiles = m // tm
    resident = dict(pipeline_mode=pl.Buffered(1))
    this_tile = lambda i: (jnp.minimum(i, n_tiles - 1), 0)
    prev_tile = lambda i: (jnp.maximum(i - 1, 0), 0)
    return pl.pallas_call(
        _out_proj_kernel,
        grid=(n_tiles + 1,),
        in_specs=[
            pl.BlockSpec((tm, ka), this_tile),
            pl.BlockSpec((tm, kb), this_tile),
            pl.BlockSpec((ka, d), lambda i: (0, 0), **resident),
            pl.BlockSpec((kb, d), lambda i: (1, 0), **resident),
            pl.BlockSpec((tm, d), prev_tile),
            pl.BlockSpec((1, d), lambda i: (0, 0)),
        ],
        out_specs=pl.BlockSpec((tm, d), prev_tile),
        out_shape=jax.ShapeDtypeStruct((m, d), jnp.float32),
        scratch_shapes=[pltpu.VMEM((tm, d), jnp.float32),
                        pltpu.VMEM((tm, 1), jnp.float32)],
        compiler_params=pltpu.CompilerParams(dimension_semantics=("arbitrary",),
                                             vmem_limit_bytes=OUT_PROJ_VMEM_LIMIT),
        name="out_proj_norm_residual",
    )(mix_a, mix_b, w, w, x2d, post_w.reshape(1, d))


def _layer(x, cos, sin, pre_w, post_w, w_in, b_gate, pe_k, pe_v, wk1, wk2, wv1, wv2, w_out):
    b, s, d = x.shape
    x2d = x.reshape(b * s, d)
    w_t = w_in.T
    h, gates = _rmsnorm(x2d, pre_w, w_t)
    front, back = _in_proj(h, w_t)

    pe = jnp.stack([pe_k, pe_v])
    w1 = jnp.stack([wk1, wv1]).astype(jnp.bfloat16)
    w2 = jnp.stack([wk2, wv2]).astype(jnp.bfloat16)
    cmp_kv = _compress(back, b, s, pe, w1, w2)

    bg = jnp.pad(b_gate.reshape(1, -1), ((0, 0), (0, HEAD_DIM - b_gate.size)))
    mix_a, mix_b = _mixers(front, back, gates, cmp_kv, bg, cos, sin, _dilated_tables(s), b, s)

    out = _out_proj(mix_a, mix_b, w_out.astype(jnp.bfloat16), x2d, post_w)
    return out.reshape(b, s, d)


def kernel(x, positions, pre_norm_w, post_norm_w, w_in, b_gate, cmp_pe_k, cmp_pe_v,
           cmp_wk1, cmp_wk2, cmp_wv1, cmp_wv2, w_out):
    cos, sin = _rope_tables(positions)
    for l in range(pre_norm_w.shape[0]):
        x = _layer(x, cos, sin, pre_norm_w[l], post_norm_w[l], w_in[l], b_gate[l],
                   cmp_pe_k[l], cmp_pe_v[l], cmp_wk1[l], cmp_wk2[l], cmp_wv1[l], cmp_wv2[l],
                   w_out[l])
    return x
```

```python
import functools
import math

import jax
import jax.numpy as jnp
from jax import lax
from jax.experimental import pallas as pl
from jax.experimental.pallas import tpu as pltpu

HEAD_DIM = 128
HALF = HEAD_DIM // 2
N_HEADS_A = 16
N_KV_A = 4
GROUP_A = 4
N_HEADS_B = 16
CMP_LEN = 32
CMP_STRIDE = 16
CMP_HIDDEN = 256
SEL_LEN = 64
TOP_N = 16
WIN = 512
DILATIONS = ((128, 1), (512, 4), (2048, 16))
ROPE_THETA = 10000.0
EPS = 1e-6
FORCE_BONUS = 1e4
NEG = -1e30
SCALE = HEAD_DIM ** -0.5
QSCALE = SCALE * math.log2(math.e)

H_QA = 0
H_ZA = H_QA + N_HEADS_A
N_FRONT_HEADS = H_ZA + N_HEADS_A
GATE_PER_KV = 3 * GROUP_A
H_KC = 0
H_VC = H_KC + N_KV_A
H_KSL = H_VC + N_KV_A
H_VSL = H_KSL + N_KV_A
H_KW = H_VSL + N_KV_A
H_VW = H_KW + N_KV_A
H_QB = H_VW + N_KV_A
H_KB = H_QB + N_HEADS_B
H_VB = H_KB + N_HEADS_B
H_ZB = H_VB + N_HEADS_B
N_BACK_HEADS = H_ZB + N_HEADS_B

TQ = 256
TK = 256
SEL_PER_TK = TK // SEL_LEN
F32_SUBLANES = 8
BF16_SUBLANES = 16
V_ROWS = HEAD_DIM + BF16_SUBLANES
PV_GROUP = 2
MIX_SPLIT = 2
VMEM_LIMIT = 56 * 1024 * 1024
MIXERS_VMEM_LIMIT = 60 * 1024 * 1024
OUT_PROJ_VMEM_LIMIT = 60 * 1024 * 1024


def _cparams(sem):
    return pltpu.CompilerParams(dimension_semantics=sem, vmem_limit_bytes=VMEM_LIMIT)


GATE_ROW0 = 2 * N_HEADS_A * HEAD_DIM
BACK_ROW0 = GATE_ROW0 + 3 * N_HEADS_A


def _nt_dot(x, w):
    return lax.dot_general(x, w, (((1,), (1,)), ((), ())), preferred_element_type=jnp.float32)


def _rmsnorm_kernel(*refs, with_rope):
    if with_rope:
        x_ref, w_ref, wg_ref, pos_ref, inv_ref, o_ref, g_ref, cos_ref, sin_ref = refs
    else:
        x_ref, w_ref, wg_ref, o_ref, g_ref = refs
    x = x_ref[...]
    ms = jnp.mean(x * x, axis=-1, keepdims=True)
    h = (x * lax.rsqrt(ms + EPS) * w_ref[...]).astype(o_ref.dtype)
    o_ref[...] = h
    g_ref[...] = _nt_dot(h, wg_ref[...].astype(jnp.bfloat16))
    if with_rope:
        ang = pos_ref[...].astype(jnp.float32) * inv_ref[...]
        lane = lax.broadcasted_iota(jnp.int32, ang.shape, 1)
        sin = jnp.sin(ang)
        cos_ref[...] = jnp.cos(ang)
        sin_ref[...] = jnp.where(lane < HALF, -sin, sin)


def _rmsnorm(x2d, w, w_t, positions=None, tm=256):
    m, d = x2d.shape
    with_rope = positions is not None
    rows = lambda width: pl.BlockSpec((tm, width), lambda i: (i, 0))
    in_specs = [rows(d), pl.BlockSpec((1, d), lambda i: (0, 0)),
                pl.BlockSpec((HEAD_DIM, d), lambda i: (GATE_ROW0 // HEAD_DIM, 0))]
    out_specs = [rows(d), rows(HEAD_DIM)]
    out_shape = [jax.ShapeDtypeStruct((m, d), jnp.bfloat16),
                 jax.ShapeDtypeStruct((m, HEAD_DIM), jnp.float32)]
    args = [x2d, w.reshape(1, d), w_t]
    if with_rope:
        inv = ROPE_THETA ** (-jnp.arange(HALF, dtype=jnp.float32) / HALF)
        in_specs += [rows(1), pl.BlockSpec((1, HEAD_DIM), lambda i: (0, 0))]
        out_specs += [rows(HEAD_DIM), rows(HEAD_DIM)]
        out_shape += [jax.ShapeDtypeStruct((m, HEAD_DIM), jnp.float32)] * 2
        args += [positions.reshape(m, 1), jnp.concatenate([inv, inv]).reshape(1, HEAD_DIM)]
    return pl.pallas_call(
        functools.partial(_rmsnorm_kernel, with_rope=with_rope),
        grid=(m // tm,),
        in_specs=in_specs,
        out_specs=out_specs,
        out_shape=out_shape,
        compiler_params=_cparams(("parallel",)),
        name="pre_rmsnorm_gates",
    )(*args)


PROJ_TN = GROUP_A * HEAD_DIM
PROJ_FRONT = GATE_ROW0 // PROJ_TN


def _in_proj_kernel(x_ref, w_ref, front_ref, back_ref):
    j = pl.program_id(1)

    @pl.when(j < PROJ_FRONT)
    def _():
        front_ref[...] = _nt_dot(x_ref[...], w_ref[...].astype(jnp.bfloat16)).astype(front_ref.dtype)

    @pl.when(j >= PROJ_FRONT)
    def _():
        back_ref[...] = _nt_dot(x_ref[...], w_ref[...].astype(jnp.bfloat16)).astype(back_ref.dtype)


def _in_proj(h, w_t, tm=2048):
    m, k = h.shape
    n_back = N_BACK_HEADS * HEAD_DIM
    n_steps = PROJ_FRONT + n_back // PROJ_TN

    def w_row(i, j):
        row = PROJ_TN * j + jnp.where(j >= PROJ_FRONT, BACK_ROW0 - GATE_ROW0, 0)
        return pl.multiple_of(row, F32_SUBLANES), 0

    return pl.pallas_call(
        _in_proj_kernel,
        grid=(m // tm, n_steps),
        in_specs=[
            pl.BlockSpec((tm, k), lambda i, j: (i, 0), pipeline_mode=pl.Buffered(1)),
            pl.BlockSpec((pl.Element(PROJ_TN), pl.Element(k)), w_row)],
        out_specs=[
            pl.BlockSpec((tm, PROJ_TN), lambda i, j: (i, jnp.minimum(j, PROJ_FRONT - 1))),
            pl.BlockSpec((tm, PROJ_TN), lambda i, j: (i, jnp.maximum(j - PROJ_FRONT, 0)))],
        out_shape=[jax.ShapeDtypeStruct((m, N_FRONT_HEADS * HEAD_DIM), jnp.bfloat16),
                   jax.ShapeDtypeStruct((m, n_back), jnp.bfloat16)],
        compiler_params=_cparams(("parallel", "arbitrary")),
        name="in_proj",
    )(h, w_t)


def _rope(xf, cos, sin_signed):
    return xf * cos + pltpu.roll(xf, HALF, 1) * sin_signed


def _compress_kernel(x_ref, pe_ref, w1_ref, w2_ref, o_ref, xs_ref):
    s = x_ref.shape[0]
    n_rows = s // CMP_STRIDE
    xs_ref[pl.ds(0, s), :] = x_ref[...].astype(jnp.float32)
    xs_ref[pl.ds(s, CMP_STRIDE), :] = jnp.zeros((CMP_STRIDE, HEAD_DIM), jnp.float32)
    flat = jnp.concatenate(
        [(xs_ref[pl.ds(r, n_rows, stride=CMP_STRIDE), :] + pe_ref[pl.ds(r, 1), :]).astype(jnp.bfloat16)
         for r in range(CMP_LEN)], axis=1)
    hid = jax.nn.gelu(jnp.dot(flat, w1_ref[...], preferred_element_type=jnp.float32))
    o_ref[...] = jnp.dot(hid.astype(jnp.bfloat16), w2_ref[...],
                         preferred_element_type=jnp.float32).astype(o_ref.dtype)


def _compress(proj, batch, s, pe, w1, w2):
    n_rows = s // CMP_STRIDE
    return pl.pallas_call(
        _compress_kernel,
        grid=(batch, 2 * N_KV_A),
        in_specs=[
            pl.BlockSpec((s, HEAD_DIM), lambda b, h: (b, H_KC + h)),
            pl.BlockSpec((None, CMP_LEN, HEAD_DIM), lambda b, h: (h // N_KV_A, 0, 0)),
            pl.BlockSpec((None, CMP_LEN * HEAD_DIM, CMP_HIDDEN), lambda b, h: (h // N_KV_A, 0, 0)),
            pl.BlockSpec((None, CMP_HIDDEN, HEAD_DIM), lambda b, h: (h // N_KV_A, 0, 0)),
        ],
        out_specs=pl.BlockSpec((None, None, n_rows, HEAD_DIM), lambda b, h: (b, h, 0, 0)),
        out_shape=jax.ShapeDtypeStruct((batch, 2 * N_KV_A, n_rows, HEAD_DIM), jnp.bfloat16),
        scratch_shapes=[pltpu.VMEM((s + CMP_STRIDE, HEAD_DIM), jnp.float32)],
        compiler_params=_cparams(("parallel", "parallel")),
        name="compress",
    )(proj, pe, w1, w2)


def _kq(k, q):
    return lax.dot_general(k, q, (((1,), (1,)), ((), ())),
                           preferred_element_type=jnp.float32)


def _memo(fn):
    cache = []

    def get():
        if not cache:
            cache.append(fn())
        return cache[0]
    return get


class _AttnJob:
    def __init__(self, q_fn, chunks):
        self.q_fn, self.chunks = q_fn, chunks
        self.s, self.m, self.acc = [], None, None

    def _score(self, c):
        k_fn, _, mask_fn = self.chunks[c]
        s = mask_fn(_kq(k_fn(), self.q_fn()))
        self.s.append(s)
        cm = jnp.max(s, axis=0, keepdims=True)
        self.m = cm if self.m is None else jnp.maximum(self.m, cm)

    def _value(self, cs):
        p = jnp.concatenate([jnp.exp2(self.s[c] - self.m).astype(jnp.bfloat16) for c in cs], axis=0)
        vt = jnp.concatenate([self.chunks[c][1]() for c in cs], axis=1)
        part = jnp.dot(vt, p, preferred_element_type=jnp.float32)
        self.acc = part if self.acc is None else self.acc + part

    def score_tasks(self):
        return [functools.partial(self._score, c) for c in range(len(self.chunks))]

    def value_tasks(self):
        n = len(self.chunks)
        return [functools.partial(self._value, range(c, min(c + PV_GROUP, n)))
                for c in range(0, n, PV_GROUP)]

    def result(self):
        return self.acc[:HEAD_DIM, :] * (1.0 / self.acc[HEAD_DIM:HEAD_DIM + 1, :])


def _round_robin(task_lists):
    for i in range(max(len(t) for t in task_lists)):
        for tasks in task_lists:
            if i < len(tasks):
                tasks[i]()


def _run_pipelined(streams, on_done):
    n = len(streams[0])
    _round_robin([jobs[0].score_tasks() for jobs in streams])
    for k in range(n):
        lists = []
        for jobs in streams:
            lists.append(jobs[k].value_tasks())
            if k + 1 < n:
                lists.append(jobs[k + 1].score_tasks())
        _round_robin(lists)
        for si, jobs in enumerate(streams):
            on_done(si, k, jobs[k])


def _store_values_transposed(src_ref, dst_ref):
    s_len = src_ref.shape[0]
    for c in range(s_len // TK):
        blk = src_ref[c * TK:(c + 1) * TK, :].astype(jnp.float32)
        dst_ref[0:HEAD_DIM, c * TK:(c + 1) * TK] = blk.T.astype(jnp.bfloat16)
    row = lax.broadcasted_iota(jnp.int32, (BF16_SUBLANES, s_len), 0)
    dst_ref[HEAD_DIM:, :] = jnp.where(row == 0, 1.0, 0.0).astype(jnp.bfloat16)


def _silu(z):
    return z * jax.nn.sigmoid(z)


def _topk_active(q0):
    return q0 + TQ > TOP_N * SEL_LEN


def _nsa_selection_bias(p_sum, q0, s_len):
    n_sel = s_len // SEL_LEN
    n_cmp = s_len // CMP_STRIDE - CMP_LEN // CMP_STRIDE + 1
    oj = lax.broadcasted_iota(jnp.int32, (n_sel, HEAD_DIM), 0)
    oc = lax.broadcasted_iota(jnp.int32, (n_sel, HEAD_DIM), 1)
    ovl = jnp.where(oc * CMP_STRIDE < oj * SEL_LEN + SEL_LEN,
                    jnp.where(oc * CMP_STRIDE + CMP_LEN > oj * SEL_LEN, 1.0, 0.0), 0.0)
    ovl = jnp.where(oc < n_cmp, ovl, 0.0).astype(jnp.bfloat16)
    p_hi = p_sum.astype(jnp.bfloat16)
    p_lo = (p_sum - p_hi.astype(jnp.float32)).astype(jnp.bfloat16)
    imp = (jnp.dot(ovl, p_hi, preferred_element_type=jnp.float32)
           + jnp.dot(ovl, p_lo, preferred_element_type=jnp.float32))
    j_row = lax.broadcasted_iota(jnp.int32, (n_sel, TQ), 0)
    t_sel = q0 + lax.broadcasted_iota(jnp.int32, (n_sel, TQ), 1)
    cur = lax.shift_right_arithmetic(t_sel, int(math.log2(SEL_LEN)))
    forced = (j_row == 0) | (j_row == cur) | (j_row == cur - 1)
    imp = jnp.where(forced, imp + FORCE_BONUS, imp)
    imp = jnp.where(j_row * SEL_LEN <= t_sel, imp, -jnp.inf)
    rank = jnp.zeros((n_sel, TQ), jnp.float32)
    for i in range(n_sel):
        vi = imp[i:i + 1, :]
        rank = rank + jnp.where(j_row > i, jnp.where(vi >= imp, 1.0, 0.0),
                                jnp.where(vi > imp, 1.0, 0.0))
    return jnp.where(rank < TOP_N, 0.0, NEG)


def _mixers_kernel(q_ref, z_ref, g_ref, bg_ref, kc_ref, vc_ref, ksl_ref, vsl_ref, kw_ref, vw_ref,
                   cos_ref, sin_ref, qd_ref, zd_ref, kd_ref, vd_ref, dbias_ref, o_ref, od_ref,
                   ksl_rot, kw_rot, vsl_t, vw_t, ocmp_s, selb_s, gate_s):
    s_len = ksl_ref.shape[0]
    n_q = s_len // TQ

    ksl_rot[...] = _rope(ksl_ref[...].astype(jnp.float32), cos_ref[...], sin_ref[...]).astype(jnp.bfloat16)
    kw_rot[...] = _rope(kw_ref[...].astype(jnp.float32), cos_ref[...], sin_ref[...]).astype(jnp.bfloat16)
    _store_values_transposed(vsl_ref, vsl_t)
    _store_values_transposed(vw_ref, vw_t)
    kc = kc_ref[...]
    vc_t = vc_ref[...].astype(jnp.float32).T.astype(jnp.bfloat16)
    bg = bg_ref[...]
    kv_head = pl.program_id(1)

    c_row = lax.broadcasted_iota(jnp.int32, (HEAD_DIM, TQ), 0)
    lane_q = lax.broadcasted_iota(jnp.int32, (HEAD_DIM, TQ), 1)
    lane_one = lax.broadcasted_iota(jnp.int32, (1, TQ), 1)
    for qi in range(n_q):
        q0 = qi * TQ
        rows = slice(q0, q0 + TQ)
        q_raw = [(q_ref[rows, g * HEAD_DIM:(g + 1) * HEAD_DIM].astype(jnp.float32) * QSCALE
                  ).astype(jnp.bfloat16) for g in range(GROUP_A)]
        c_valid = c_row * CMP_STRIDE + (CMP_LEN - 1) <= q0 + lane_q
        any_valid = jnp.where(q0 + lane_one >= CMP_LEN - 1, 1.0, 0.0)
        sc4 = _kq(kc, jnp.concatenate(q_raw, axis=0))
        p_heads = []
        p_sum = jnp.zeros((HEAD_DIM, TQ), jnp.float32)
        for g in range(GROUP_A):
            sc = jnp.where(c_valid, sc4[:, g * TQ:(g + 1) * TQ], NEG)
            e = jnp.exp2(sc - jnp.max(sc, axis=0, keepdims=True))
            p = e * (any_valid / jnp.sum(e, axis=0, keepdims=True))
            p_heads.append(p.astype(jnp.bfloat16))
            p_sum = p_sum + p
        o_cmp4 = jnp.dot(vc_t, jnp.concatenate(p_heads, axis=1),
                         preferred_element_type=jnp.float32)
        for g in range(GROUP_A):
            ocmp_s[g, :, rows] = o_cmp4[:, g * TQ:(g + 1) * TQ]
        if _topk_active(q0):
            selb_s[qi] = _nsa_selection_bias(p_sum, q0, s_len)
        gate_s[:, rows] = jax.nn.sigmoid(g_ref[rows, :] + bg).T

    diag_rc = (lax.broadcasted_iota(jnp.int32, (TK, TQ), 0)
               - lax.broadcasted_iota(jnp.int32, (TK, TQ), 1))

    def slc_mask(qi, c, s):
        if _topk_active(qi * TQ):
            s = s + jnp.concatenate(
                [jnp.broadcast_to(selb_s[qi, c * SEL_PER_TK + a:c * SEL_PER_TK + a + 1, :],
                                  (SEL_LEN, TQ)) for a in range(SEL_PER_TK)], axis=0)
        if c == qi:
            s = jnp.where(diag_rc <= 0, s, NEG)
        return s

    def win_mask(qi, c, s):
        if c == qi:
            return jnp.where(diag_rc <= 0, s, NEG)
        if (qi - c) * TK + TQ - 1 > WIN - 1:
            return jnp.where(diag_rc >= (qi - c) * TK - (WIN - 1), s, NEG)
        return s

    ones_rows = jnp.where(lax.broadcasted_iota(jnp.int32, (BF16_SUBLANES, TK), 0) == 0,
                          1.0, 0.0).astype(jnp.bfloat16)

    def head_body(g, carry):
        col = pl.ds(pl.multiple_of(g * HEAD_DIM, HEAD_DIM), HEAD_DIM)
        gate_row = GATE_PER_KV * kv_head + 3 * g
        blocks = [slice(c * TK, (c + 1) * TK) for c in range(n_q)]

        def roped(ref, rows, scale):
            xf = ref[rows, col].astype(jnp.float32)
            if scale != 1.0:
                xf = xf * scale
            return _rope(xf, cos_ref[rows, :], sin_ref[rows, :]).astype(jnp.bfloat16)

        def dil_value_t(rows):
            vt = vd_ref[rows, col].astype(jnp.float32).T.astype(jnp.bfloat16)
            return jnp.concatenate([vt, ones_rows], axis=0)

        def dil_bias(delta, s):
            r0 = (n_q - 1 - delta) * TK
            return s + dbias_ref[r0:r0 + TK, :]

        dil_k = [_memo(functools.partial(roped, kd_ref, rows, 1.0)) for rows in blocks]
        dil_v = [_memo(functools.partial(dil_value_t, rows)) for rows in blocks]
        slc_jobs, win_jobs, dil_jobs = [], [], []
        for qi in range(n_q):
            q = _memo(functools.partial(roped, q_ref, blocks[qi], QSCALE))
            first_win = max(0, (qi * TQ - (WIN - 1)) // TK)
            for jobs, k_ref_, v_ref_, mask, c_lo in ((slc_jobs, ksl_rot, vsl_t, slc_mask, 0),
                                                     (win_jobs, kw_rot, vw_t, win_mask, first_win)):
                jobs.append(_AttnJob(q, [
                    (functools.partial(lambda r, c: r[c * TK:(c + 1) * TK, :], k_ref_, c),
                     functools.partial(lambda r, c: r[:, c * TK:(c + 1) * TK], v_ref_, c),
                     functools.partial(mask, qi, c))
                    for c in range(c_lo, qi + 1)]))
            dil_jobs.append(_AttnJob(
                _memo(functools.partial(roped, qd_ref, blocks[qi], QSCALE)),
                [(dil_k[c], dil_v[c], functools.partial(dil_bias, qi - c)) for c in range(qi + 1)]))

        def on_done(si, k, job):
            kind, qi = si % 3, MIX_SPLIT * k + si // 3
            rows = blocks[qi]
            if kind == 1:
                o_t = (gate_s[pl.ds(gate_row, 1), rows] * ocmp_s[g, :, rows]
                       + gate_s[pl.ds(gate_row + 1, 1), rows] * slc_jobs[qi].result()
                       + gate_s[pl.ds(gate_row + 2, 1), rows] * job.result())
                z = z_ref[rows, col].astype(jnp.float32)
                o_ref[rows, col] = (o_t.T * _silu(z)).astype(o_ref.dtype)
            elif kind == 2:
                z = zd_ref[rows, col].astype(jnp.float32)
                od_ref[rows, col] = (job.result().T * _silu(z)).astype(od_ref.dtype)

        streams = []
        for r in range(MIX_SPLIT):
            streams += [slc_jobs[r::MIX_SPLIT], win_jobs[r::MIX_SPLIT], dil_jobs[r::MIX_SPLIT]]
        _run_pipelined(streams, on_done)
        return carry

    lax.fori_loop(0, GROUP_A, head_body, 0)


def _mixers(front, back, gates, cmp_kv, b_gate_row, cos, sin, dil_bias, batch, s):
    gw = GROUP_A * HEAD_DIM
    assert N_HEADS_B == N_HEADS_A
    full = lambda col0: pl.BlockSpec((s, HEAD_DIM), lambda b, h: (b, col0 + h))
    group = lambda head0: pl.BlockSpec((s, gw), lambda b, h: (b, head0 // GROUP_A + h))
    out = jax.ShapeDtypeStruct((batch * s, N_HEADS_A * HEAD_DIM), jnp.bfloat16)
    rarely_changes = dict(pipeline_mode=pl.Buffered(1))
    return pl.pallas_call(
        _mixers_kernel,
        grid=(batch, N_KV_A),
        in_specs=[
            group(H_QA), group(H_ZA),
            pl.BlockSpec((s, HEAD_DIM), lambda b, h: (b, 0)),
            pl.BlockSpec((1, HEAD_DIM), lambda b, h: (0, 0)),
            pl.BlockSpec((None, None, s // CMP_STRIDE, HEAD_DIM), lambda b, h: (b, h, 0, 0)),
            pl.BlockSpec((None, None, s // CMP_STRIDE, HEAD_DIM), lambda b, h: (b, N_KV_A + h, 0, 0)),
            full(H_KSL), full(H_VSL), full(H_KW), full(H_VW),
            pl.BlockSpec((None, s, HEAD_DIM), lambda b, h: (b, 0, 0), **rarely_changes),
            pl.BlockSpec((None, s, HEAD_DIM), lambda b, h: (b, 0, 0), **rarely_changes),
            group(H_QB), group(H_ZB), group(H_KB), group(H_VB),
            pl.BlockSpec((s, TQ), lambda b, h: (0, 0), **rarely_changes),
        ],
        out_specs=[pl.BlockSpec((s, gw), lambda b, h: (b, h)),
                   pl.BlockSpec((s, gw), lambda b, h: (b, h))],
        out_shape=[out, out],
        scratch_shapes=[pltpu.VMEM((s, HEAD_DIM), jnp.bfloat16),
                        pltpu.VMEM((s, HEAD_DIM), jnp.bfloat16),
                        pltpu.VMEM((V_ROWS, s), jnp.bfloat16),
                        pltpu.VMEM((V_ROWS, s), jnp.bfloat16),
                        pltpu.VMEM((GROUP_A, HEAD_DIM, s), jnp.float32),
                        pltpu.VMEM((s // TQ, s // SEL_LEN, TQ), jnp.float32),
                        pltpu.VMEM((HEAD_DIM, s), jnp.float32)],
        compiler_params=pltpu.CompilerParams(dimension_semantics=("parallel", "parallel"),
                                             vmem_limit_bytes=MIXERS_VMEM_LIMIT),
        name="token_mixers",
    )(front, front, gates, b_gate_row, cmp_kv, cmp_kv, back, back, back, back, cos, sin,
      back, back, back, back, dil_bias)


def _dilated_multiplicity(dist):
    cnt = jnp.zeros(dist.shape, jnp.float32)
    for window, dil in DILATIONS:
        cnt = cnt + jnp.where((dist & (dil - 1)) == 0,
                              jnp.where(dist <= (window // dil) * dil, 1.0, 0.0), 0.0)
    return jnp.where(dist >= 0, cnt, 0.0)


def _dilated_table_kernel(bias_ref):
    s_len = bias_ref.shape[0]
    dist = (s_len - TQ + lax.broadcasted_iota(jnp.int32, (s_len, TQ), 1)
            - lax.broadcasted_iota(jnp.int32, (s_len, TQ), 0))
    cnt = _dilated_multiplicity(dist)
    bias_ref[...] = jnp.where(cnt > 0.5, jnp.log2(jnp.maximum(cnt, 1.0)), NEG)


def _dilated_tables(s):
    return pl.pallas_call(
        _dilated_table_kernel,
        out_shape=jax.ShapeDtypeStruct((s, TQ), jnp.float32),
        compiler_params=pltpu.CompilerParams(vmem_limit_bytes=VMEM_LIMIT),
        name="dilated_tables",
    )()


OUT_PANELS = 4


def _out_proj_kernel(ma_ref, mb_ref, wa_ref, wb_ref, x_ref, nw_ref, o_ref, y_s, ssq_s):
    i = pl.program_id(0)
    n_tiles = pl.num_programs(0) - 1
    d = o_ref.shape[1]
    pn = d // OUT_PANELS

    def step(multiply, finish):
        if finish:
            scale = lax.rsqrt(ssq_s[...] * (1.0 / d) + EPS)
        ssq = None
        for p in range(OUT_PANELS):
            cols = slice(p * pn, (p + 1) * pn)
            if finish:
                o_ref[:, cols] = x_ref[:, cols] + y_s[:, cols] * scale * nw_ref[:, cols]
            if multiply:
                y = (jnp.dot(ma_ref[...], wa_ref[:, cols], preferred_element_type=jnp.float32)
                     + jnp.dot(mb_ref[...], wb_ref[:, cols], preferred_element_type=jnp.float32))
                y_s[:, cols] = y
                part = jnp.sum(y * y, axis=-1, keepdims=True)
                ssq = part if ssq is None else ssq + part
        if multiply:
            ssq_s[...] = ssq

    pl.when(i == 0)(functools.partial(step, True, False))
    pl.when((i > 0) & (i < n_tiles))(functools.partial(step, True, True))
    pl.when(i == n_tiles)(functools.partial(step, False, True))


def _out_proj(mix_a, mix_b, w, x2d, post_w, tm=256):
    m, d = x2d.shape
    ka, kb = mix_a.shape[1], mix_b.shape[1]
    assert ka == kb and w.shape[0] == ka + kb
    n_tiles = m // tm
    resident = dict(pipeline_mode=pl.Buffered(1))
    this_tile = lambda i: (jnp.minimum(i, n_tiles - 1), 0)
    prev_tile = lambda i: (jnp.maximum(i - 1, 0), 0)
    return pl.pallas_call(
        _out_proj_kernel,
        grid=(n_tiles + 1,),
        in_specs=[
            pl.BlockSpec((tm, ka), this_tile),
            pl.BlockSpec((tm, kb), this_tile),
            pl.BlockSpec((ka, d), lambda i: (0, 0), **resident),
            pl.BlockSpec((kb, d), lambda i: (1, 0), **resident),
            pl.BlockSpec((tm, d), prev_tile),
            pl.BlockSpec((1, d), lambda i: (0, 0)),
        ],
        out_specs=pl.BlockSpec((tm, d), prev_tile),
        out_shape=jax.ShapeDtypeStruct((m, d), jnp.float32),
        scratch_shapes=[pltpu.VMEM((tm, d), jnp.float32),
                        pltpu.VMEM((tm, 1), jnp.float32)],
        compiler_params=pltpu.CompilerParams(dimension_semantics=("arbitrary",),
                                             vmem_limit_bytes=OUT_PROJ_VMEM_LIMIT),
        name="out_proj_norm_residual",
    )(mix_a, mix_b, w, w, x2d, post_w.reshape(1, d))


def _layer(x, positions, rope, pre_w, post_w, w_in, b_gate, pe_k, pe_v, wk1, wk2, wv1, wv2, w_out):
    b, s, d = x.shape
    x2d = x.reshape(b * s, d)
    w_t = w_in.T
    if rope is None:
        h, gates, cos, sin = _rmsnorm(x2d, pre_w, w_t, positions.reshape(b * s))
        rope = (cos.reshape(b, s, HEAD_DIM), sin.reshape(b, s, HEAD_DIM))
    else:
        h, gates = _rmsnorm(x2d, pre_w, w_t)
    cos, sin = rope
    front, back = _in_proj(h, w_t)

    pe = jnp.stack([pe_k, pe_v])
    w1 = jnp.stack([wk1, wv1]).astype(jnp.bfloat16)
    w2 = jnp.stack([wk2, wv2]).astype(jnp.bfloat16)
    cmp_kv = _compress(back, b, s, pe, w1, w2)

    bg = jnp.pad(b_gate.reshape(1, -1), ((0, 0), (0, HEAD_DIM - b_gate.size)))
    mix_a, mix_b = _mixers(front, back, gates, cmp_kv, bg, cos, sin, _dilated_tables(s), b, s)

    out = _out_proj(mix_a, mix_b, w_out.astype(jnp.bfloat16), x2d, post_w)
    return out.reshape(b, s, d), rope


def kernel(x, positions, pre_norm_w, post_norm_w, w_in, b_gate, cmp_pe_k, cmp_pe_v,
           cmp_wk1, cmp_wk2, cmp_wv1, cmp_wv2, w_out):
    rope = None
    for l in range(pre_norm_w.shape[0]):
        x, rope = _layer(x, positions, rope, pre_norm_w[l], post_norm_w[l], w_in[l], b_gate[l],
                         cmp_pe_k[l], cmp_pe_v[l], cmp_wk1[l], cmp_wk2[l], cmp_wv1[l],
                         cmp_wv2[l], w_out[l])
    return x
```

```python
import functools
import math

import jax
import jax.numpy as jnp
from jax import lax
from jax.experimental import pallas as pl
from jax.experimental.pallas import tpu as pltpu

HEAD_DIM = 128
HALF = HEAD_DIM // 2
N_HEADS_A = 16
N_KV_A = 4
GROUP_A = 4
N_HEADS_B = 16
CMP_LEN = 32
CMP_STRIDE = 16
CMP_HIDDEN = 256
SEL_LEN = 64
TOP_N = 16
WIN = 512
DILATIONS = ((128, 1), (512, 4), (2048, 16))
ROPE_THETA = 10000.0
EPS = 1e-6
FORCE_BONUS = 1e4
NEG = -1e30
SCALE = HEAD_DIM ** -0.5
QSCALE = SCALE * math.log2(math.e)

H_QA = 0
H_ZA = H_QA + N_HEADS_A
N_FRONT_HEADS = H_ZA + N_HEADS_A
GATE_PER_KV = 3 * GROUP_A
H_KC = 0
H_VC = H_KC + N_KV_A
H_KSL = H_VC + N_KV_A
H_VSL = H_KSL + N_KV_A
H_KW = H_VSL + N_KV_A
H_VW = H_KW + N_KV_A
H_QB = H_VW + N_KV_A
H_KB = H_QB + N_HEADS_B
H_VB = H_KB + N_HEADS_B
H_ZB = H_VB + N_HEADS_B
N_BACK_HEADS = H_ZB + N_HEADS_B

TQ = 256
TK = 256
SEL_PER_TK = TK // SEL_LEN
F32_SUBLANES = 8
BF16_SUBLANES = 16
V_ROWS = HEAD_DIM + BF16_SUBLANES
PV_GROUP = 2
MIX_SPLIT = 2
VMEM_LIMIT = 56 * 1024 * 1024
MIXERS_VMEM_LIMIT = 60 * 1024 * 1024
OUT_PROJ_VMEM_LIMIT = 60 * 1024 * 1024


def _cparams(sem):
    return pltpu.CompilerParams(dimension_semantics=sem, vmem_limit_bytes=VMEM_LIMIT)


GATE_ROW0 = 2 * N_HEADS_A * HEAD_DIM
BACK_ROW0 = GATE_ROW0 + 3 * N_HEADS_A


def _nt_dot(x, w):
    return lax.dot_general(x, w, (((1,), (1,)), ((), ())), preferred_element_type=jnp.float32)


def _rmsnorm_kernel(*refs, with_rope):
    if with_rope:
        x_ref, w_ref, wg_ref, pos_ref, inv_ref, o_ref, g_ref, cos_ref, sin_ref = refs
    else:
        x_ref, w_ref, wg_ref, o_ref, g_ref = refs
    x = x_ref[...]
    ms = jnp.mean(x * x, axis=-1, keepdims=True)
    h = (x * lax.rsqrt(ms + EPS) * w_ref[...]).astype(o_ref.dtype)
    o_ref[...] = h
    g_ref[...] = _nt_dot(h, wg_ref[...].astype(jnp.bfloat16))
    if with_rope:
        ang = pos_ref[...].astype(jnp.float32) * inv_ref[...]
        lane = lax.broadcasted_iota(jnp.int32, ang.shape, 1)
        sin = jnp.sin(ang)
        cos_ref[...] = jnp.cos(ang)
        sin_ref[...] = jnp.where(lane < HALF, -sin, sin)


def _rmsnorm(x2d, w, w_t, positions=None, tm=256):
    m, d = x2d.shape
    with_rope = positions is not None
    rows = lambda width: pl.BlockSpec((tm, width), lambda i: (i, 0))
    in_specs = [rows(d), pl.BlockSpec((1, d), lambda i: (0, 0)),
                pl.BlockSpec((HEAD_DIM, d), lambda i: (GATE_ROW0 // HEAD_DIM, 0))]
    out_specs = [rows(d), rows(HEAD_DIM)]
    out_shape = [jax.ShapeDtypeStruct((m, d), jnp.bfloat16),
                 jax.ShapeDtypeStruct((m, HEAD_DIM), jnp.float32)]
    args = [x2d, w.reshape(1, d), w_t]
    if with_rope:
        inv = ROPE_THETA ** (-jnp.arange(HALF, dtype=jnp.float32) / HALF)
        in_specs += [rows(1), pl.BlockSpec((1, HEAD_DIM), lambda i: (0, 0))]
        out_specs += [rows(HEAD_DIM), rows(HEAD_DIM)]
        out_shape += [jax.ShapeDtypeStruct((m, HEAD_DIM), jnp.float32)] * 2
        args += [positions.reshape(m, 1), jnp.concatenate([inv, inv]).reshape(1, HEAD_DIM)]
    return pl.pallas_call(
        functools.partial(_rmsnorm_kernel, with_rope=with_rope),
        grid=(m // tm,),
        in_specs=in_specs,
        out_specs=out_specs,
        out_shape=out_shape,
        compiler_params=_cparams(("parallel",)),
        name="pre_rmsnorm_gates",
    )(*args)


PROJ_TN = GROUP_A * HEAD_DIM
PROJ_FRONT = GATE_ROW0 // PROJ_TN


WOUT_CAST_ROWS = 64


def _in_proj_kernel(x_ref, w_ref, wo_ref, front_ref, back_ref, wo_bf16_ref, *, wo_blocks):
    j = pl.program_id(1)
    t = pl.program_id(0) * pl.num_programs(1) + j

    @pl.when(t < wo_blocks)
    def _():
        wo_bf16_ref[...] = wo_ref[...].astype(wo_bf16_ref.dtype)

    @pl.when(j < PROJ_FRONT)
    def _():
        front_ref[...] = _nt_dot(x_ref[...], w_ref[...].astype(jnp.bfloat16)).astype(front_ref.dtype)

    @pl.when(j >= PROJ_FRONT)
    def _():
        back_ref[...] = _nt_dot(x_ref[...], w_ref[...].astype(jnp.bfloat16)).astype(back_ref.dtype)


def _in_proj(h, w_t, w_out, tm=2048):
    m, k = h.shape
    n_back = N_BACK_HEADS * HEAD_DIM
    n_steps = PROJ_FRONT + n_back // PROJ_TN
    wo_blocks = w_out.shape[0] // WOUT_CAST_ROWS
    assert wo_blocks <= (m // tm) * n_steps

    def wo_block(i, j):
        return jnp.minimum(i * n_steps + j, wo_blocks - 1), 0

    def w_row(i, j):
        row = PROJ_TN * j + jnp.where(j >= PROJ_FRONT, BACK_ROW0 - GATE_ROW0, 0)
        return pl.multiple_of(row, F32_SUBLANES), 0

    wo_spec = pl.BlockSpec((WOUT_CAST_ROWS, w_out.shape[1]), wo_block)
    return pl.pallas_call(
        functools.partial(_in_proj_kernel, wo_blocks=wo_blocks),
        grid=(m // tm, n_steps),
        in_specs=[
            pl.BlockSpec((tm, k), lambda i, j: (i, 0), pipeline_mode=pl.Buffered(1)),
            pl.BlockSpec((pl.Element(PROJ_TN), pl.Element(k)), w_row),
            wo_spec],
        out_specs=[
            pl.BlockSpec((tm, PROJ_TN), lambda i, j: (i, jnp.minimum(j, PROJ_FRONT - 1))),
            pl.BlockSpec((tm, PROJ_TN), lambda i, j: (i, jnp.maximum(j - PROJ_FRONT, 0))),
            wo_spec],
        out_shape=[jax.ShapeDtypeStruct((m, N_FRONT_HEADS * HEAD_DIM), jnp.bfloat16),
                   jax.ShapeDtypeStruct((m, n_back), jnp.bfloat16),
                   jax.ShapeDtypeStruct(w_out.shape, jnp.bfloat16)],
        compiler_params=_cparams(("arbitrary", "arbitrary")),
        name="in_proj",
    )(h, w_t, w_out)


def _rope(xf, cos, sin_signed):
    return xf * cos + pltpu.roll(xf, HALF, 1) * sin_signed


def _compress_kernel(x_ref, pe_ref, w1_ref, w2_ref, o_ref, xs_ref):
    s = x_ref.shape[0]
    n_rows = s // CMP_STRIDE
    xs_ref[pl.ds(0, s), :] = x_ref[...].astype(jnp.float32)
    xs_ref[pl.ds(s, CMP_STRIDE), :] = jnp.zeros((CMP_STRIDE, HEAD_DIM), jnp.float32)
    flat = jnp.concatenate(
        [(xs_ref[pl.ds(r, n_rows, stride=CMP_STRIDE), :] + pe_ref[pl.ds(r, 1), :]).astype(jnp.bfloat16)
         for r in range(CMP_LEN)], axis=1)
    hid = jax.nn.gelu(jnp.dot(flat, w1_ref[...], preferred_element_type=jnp.float32))
    o_ref[...] = jnp.dot(hid.astype(jnp.bfloat16), w2_ref[...],
                         preferred_element_type=jnp.float32).astype(o_ref.dtype)


def _compress(proj, batch, s, pe, w1, w2):
    n_rows = s // CMP_STRIDE
    return pl.pallas_call(
        _compress_kernel,
        grid=(batch, 2 * N_KV_A),
        in_specs=[
            pl.BlockSpec((s, HEAD_DIM), lambda b, h: (b, H_KC + h)),
            pl.BlockSpec((None, CMP_LEN, HEAD_DIM), lambda b, h: (h // N_KV_A, 0, 0)),
            pl.BlockSpec((None, CMP_LEN * HEAD_DIM, CMP_HIDDEN), lambda b, h: (h // N_KV_A, 0, 0)),
            pl.BlockSpec((None, CMP_HIDDEN, HEAD_DIM), lambda b, h: (h // N_KV_A, 0, 0)),
        ],
        out_specs=pl.BlockSpec((None, None, n_rows, HEAD_DIM), lambda b, h: (b, h, 0, 0)),
        out_shape=jax.ShapeDtypeStruct((batch, 2 * N_KV_A, n_rows, HEAD_DIM), jnp.bfloat16),
        scratch_shapes=[pltpu.VMEM((s + CMP_STRIDE, HEAD_DIM), jnp.float32)],
        compiler_params=_cparams(("parallel", "parallel")),
        name="compress",
    )(proj, pe, w1, w2)


def _kq(k, q):
    return lax.dot_general(k, q, (((1,), (1,)), ((), ())),
                           preferred_element_type=jnp.float32)


def _memo(fn):
    cache = []

    def get():
        if not cache:
            cache.append(fn())
        return cache[0]
    return get


class _AttnJob:
    def __init__(self, q_fn, chunks):
        self.q_fn, self.chunks = q_fn, chunks
        self.s, self.m, self.acc = [], None, None

    def _score(self, c):
        k_fn, _, mask_fn = self.chunks[c]
        s = mask_fn(_kq(k_fn(), self.q_fn()))
        self.s.append(s)
        cm = jnp.max(s, axis=0, keepdims=True)
        self.m = cm if self.m is None else jnp.maximum(self.m, cm)

    def _value(self, cs):
        p = jnp.concatenate([jnp.exp2(self.s[c] - self.m).astype(jnp.bfloat16) for c in cs], axis=0)
        vt = jnp.concatenate([self.chunks[c][1]() for c in cs], axis=1)
        part = jnp.dot(vt, p, preferred_element_type=jnp.float32)
        self.acc = part if self.acc is None else self.acc + part

    def score_tasks(self):
        return [functools.partial(self._score, c) for c in range(len(self.chunks))]

    def value_tasks(self):
        n = len(self.chunks)
        return [functools.partial(self._value, range(c, min(c + PV_GROUP, n)))
                for c in range(0, n, PV_GROUP)]

    def result(self):
        return self.acc[:HEAD_DIM, :] * (1.0 / self.acc[HEAD_DIM:HEAD_DIM + 1, :])


def _round_robin(task_lists):
    for i in range(max(len(t) for t in task_lists)):
        for tasks in task_lists:
            if i < len(tasks):
                tasks[i]()


def _run_pipelined(streams, on_done):
    n = len(streams[0])
    _round_robin([jobs[0].score_tasks() for jobs in streams])
    for k in range(n):
        lists = []
        for jobs in streams:
            lists.append(jobs[k].value_tasks())
            if k + 1 < n:
                lists.append(jobs[k + 1].score_tasks())
        _round_robin(lists)
        for si, jobs in enumerate(streams):
            on_done(si, k, jobs[k])


def _store_values_transposed(src_ref, dst_ref):
    s_len = src_ref.shape[0]
    for c in range(s_len // TK):
        blk = src_ref[c * TK:(c + 1) * TK, :].astype(jnp.float32)
        dst_ref[0:HEAD_DIM, c * TK:(c + 1) * TK] = blk.T.astype(jnp.bfloat16)
    row = lax.broadcasted_iota(jnp.int32, (BF16_SUBLANES, s_len), 0)
    dst_ref[HEAD_DIM:, :] = jnp.where(row == 0, 1.0, 0.0).astype(jnp.bfloat16)


def _silu(z):
    return z * jax.nn.sigmoid(z)


def _topk_active(q0):
    return q0 + TQ > TOP_N * SEL_LEN


def _nsa_selection_bias(p_sum, q0, s_len):
    n_sel = s_len // SEL_LEN
    n_cmp = s_len // CMP_STRIDE - CMP_LEN // CMP_STRIDE + 1
    oj = lax.broadcasted_iota(jnp.int32, (n_sel, HEAD_DIM), 0)
    oc = lax.broadcasted_iota(jnp.int32, (n_sel, HEAD_DIM), 1)
    ovl = jnp.where(oc * CMP_STRIDE < oj * SEL_LEN + SEL_LEN,
                    jnp.where(oc * CMP_STRIDE + CMP_LEN > oj * SEL_LEN, 1.0, 0.0), 0.0)
    ovl = jnp.where(oc < n_cmp, ovl, 0.0).astype(jnp.bfloat16)
    p_hi = p_sum.astype(jnp.bfloat16)
    p_lo = (p_sum - p_hi.astype(jnp.float32)).astype(jnp.bfloat16)
    imp = (jnp.dot(ovl, p_hi, preferred_element_type=jnp.float32)
           + jnp.dot(ovl, p_lo, preferred_element_type=jnp.float32))
    j_row = lax.broadcasted_iota(jnp.int32, (n_sel, TQ), 0)
    t_sel = q0 + lax.broadcasted_iota(jnp.int32, (n_sel, TQ), 1)
    cur = lax.shift_right_arithmetic(t_sel, int(math.log2(SEL_LEN)))
    forced = (j_row == 0) | (j_row == cur) | (j_row == cur - 1)
    imp = jnp.where(forced, imp + FORCE_BONUS, imp)
    imp = jnp.where(j_row * SEL_LEN <= t_sel, imp, -jnp.inf)
    rank = jnp.zeros((n_sel, TQ), jnp.float32)
    for i in range(n_sel):
        vi = imp[i:i + 1, :]
        rank = rank + jnp.where(j_row > i, jnp.where(vi >= imp, 1.0, 0.0),
                                jnp.where(vi > imp, 1.0, 0.0))
    return jnp.where(rank < TOP_N, 0.0, NEG)


def _mixers_kernel(q_ref, z_ref, g_ref, bg_ref, kc_ref, vc_ref, ksl_ref, vsl_ref, kw_ref, vw_ref,
                   cos_ref, sin_ref, qd_ref, zd_ref, kd_ref, vd_ref, dbias_ref, o_ref, od_ref,
                   ksl_rot, kw_rot, vsl_t, vw_t, ocmp_s, selb_s, gate_s):
    s_len = ksl_ref.shape[0]
    n_q = s_len // TQ

    ksl_rot[...] = _rope(ksl_ref[...].astype(jnp.float32), cos_ref[...], sin_ref[...]).astype(jnp.bfloat16)
    kw_rot[...] = _rope(kw_ref[...].astype(jnp.float32), cos_ref[...], sin_ref[...]).astype(jnp.bfloat16)
    _store_values_transposed(vsl_ref, vsl_t)
    _store_values_transposed(vw_ref, vw_t)
    kc = kc_ref[...]
    vc_t = vc_ref[...].astype(jnp.float32).T.astype(jnp.bfloat16)
    bg = bg_ref[...]
    kv_head = pl.program_id(1)

    c_row = lax.broadcasted_iota(jnp.int32, (HEAD_DIM, TQ), 0)
    lane_q = lax.broadcasted_iota(jnp.int32, (HEAD_DIM, TQ), 1)
    lane_one = lax.broadcasted_iota(jnp.int32, (1, TQ), 1)
    for qi in range(n_q):
        q0 = qi * TQ
        rows = slice(q0, q0 + TQ)
        q_raw = [(q_ref[rows, g * HEAD_DIM:(g + 1) * HEAD_DIM].astype(jnp.float32) * QSCALE
                  ).astype(jnp.bfloat16) for g in range(GROUP_A)]
        c_valid = c_row * CMP_STRIDE + (CMP_LEN - 1) <= q0 + lane_q
        any_valid = jnp.where(q0 + lane_one >= CMP_LEN - 1, 1.0, 0.0)
        sc4 = _kq(kc, jnp.concatenate(q_raw, axis=0))
        p_heads = []
        p_sum = jnp.zeros((HEAD_DIM, TQ), jnp.float32)
        for g in range(GROUP_A):
            sc = jnp.where(c_valid, sc4[:, g * TQ:(g + 1) * TQ], NEG)
            e = jnp.exp2(sc - jnp.max(sc, axis=0, keepdims=True))
            p = e * (any_valid / jnp.sum(e, axis=0, keepdims=True))
            p_heads.append(p.astype(jnp.bfloat16))
            p_sum = p_sum + p
        o_cmp4 = jnp.dot(vc_t, jnp.concatenate(p_heads, axis=1),
                         preferred_element_type=jnp.float32)
        for g in range(GROUP_A):
            ocmp_s[g, :, rows] = o_cmp4[:, g * TQ:(g + 1) * TQ]
        if _topk_active(q0):
            selb_s[qi] = _nsa_selection_bias(p_sum, q0, s_len)
        gate_s[:, rows] = jax.nn.sigmoid(g_ref[rows, :] + bg).T

    diag_rc = (lax.broadcasted_iota(jnp.int32, (TK, TQ), 0)
               - lax.broadcasted_iota(jnp.int32, (TK, TQ), 1))

    def slc_mask(qi, c, s):
        if _topk_active(qi * TQ):
            s = s + jnp.concatenate(
                [jnp.broadcast_to(selb_s[qi, c * SEL_PER_TK + a:c * SEL_PER_TK + a + 1, :],
                                  (SEL_LEN, TQ)) for a in range(SEL_PER_TK)], axis=0)
        if c == qi:
            s = jnp.where(diag_rc <= 0, s, NEG)
        return s

    def win_mask(qi, c, s):
        if c == qi:
            return jnp.where(diag_rc <= 0, s, NEG)
        if (qi - c) * TK + TQ - 1 > WIN - 1:
            return jnp.where(diag_rc >= (qi - c) * TK - (WIN - 1), s, NEG)
        return s

    ones_rows = jnp.where(lax.broadcasted_iota(jnp.int32, (BF16_SUBLANES, TK), 0) == 0,
                          1.0, 0.0).astype(jnp.bfloat16)

    def head_body(g, carry):
        col = pl.ds(pl.multiple_of(g * HEAD_DIM, HEAD_DIM), HEAD_DIM)
        gate_row = GATE_PER_KV * kv_head + 3 * g
        blocks = [slice(c * TK, (c + 1) * TK) for c in range(n_q)]

        def roped(ref, rows, scale):
            xf = ref[rows, col].astype(jnp.float32)
            if scale != 1.0:
                xf = xf * scale
            return _rope(xf, cos_ref[rows, :], sin_ref[rows, :]).astype(jnp.bfloat16)

        def dil_value_t(rows):
            vt = vd_ref[rows, col].astype(jnp.float32).T.astype(jnp.bfloat16)
            return jnp.concatenate([vt, ones_rows], axis=0)

        def dil_bias(delta, s):
            r0 = (n_q - 1 - delta) * TK
            return s + dbias_ref[r0:r0 + TK, :]

        dil_k = [_memo(functools.partial(roped, kd_ref, rows, 1.0)) for rows in blocks]
        dil_v = [_memo(functools.partial(dil_value_t, rows)) for rows in blocks]
        slc_jobs, win_jobs, dil_jobs = [], [], []
        for qi in range(n_q):
            q = _memo(functools.partial(roped, q_ref, blocks[qi], QSCALE))
            first_win = max(0, (qi * TQ - (WIN - 1)) // TK)
            for jobs, k_ref_, v_ref_, mask, c_lo in ((slc_jobs, ksl_rot, vsl_t, slc_mask, 0),
                                                     (win_jobs, kw_rot, vw_t, win_mask, first_win)):
                jobs.append(_AttnJob(q, [
                    (functools.partial(lambda r, c: r[c * TK:(c + 1) * TK, :], k_ref_, c),
                     functools.partial(lambda r, c: r[:, c * TK:(c + 1) * TK], v_ref_, c),
                     functools.partial(mask, qi, c))
                    for c in range(c_lo, qi + 1)]))
            dil_jobs.append(_AttnJob(
                _memo(functools.partial(roped, qd_ref, blocks[qi], QSCALE)),
                [(dil_k[c], dil_v[c], functools.partial(dil_bias, qi - c)) for c in range(qi + 1)]))

        def on_done(si, k, job):
            kind, qi = si % 3, MIX_SPLIT * k + si // 3
            rows = blocks[qi]
            if kind == 1:
                o_t = (gate_s[pl.ds(gate_row, 1), rows] * ocmp_s[g, :, rows]
                       + gate_s[pl.ds(gate_row + 1, 1), rows] * slc_jobs[qi].result()
                       + gate_s[pl.ds(gate_row + 2, 1), rows] * job.result())
                z = z_ref[rows, col].astype(jnp.float32)
                o_ref[rows, col] = (o_t.T * _silu(z)).astype(o_ref.dtype)
            elif kind == 2:
                z = zd_ref[rows, col].astype(jnp.float32)
                od_ref[rows, col] = (job.result().T * _silu(z)).astype(od_ref.dtype)

        streams = []
        for r in range(MIX_SPLIT):
            streams += [slc_jobs[r::MIX_SPLIT], win_jobs[r::MIX_SPLIT], dil_jobs[r::MIX_SPLIT]]
        _run_pipelined(streams, on_done)
        return carry

    lax.fori_loop(0, GROUP_A, head_body, 0)


def _mixers(front, back, gates, cmp_kv, b_gate_row, cos, sin, dil_bias, batch, s):
    gw = GROUP_A * HEAD_DIM
    assert N_HEADS_B == N_HEADS_A
    full = lambda col0: pl.BlockSpec((s, HEAD_DIM), lambda b, h: (b, col0 + h))
    group = lambda head0: pl.BlockSpec((s, gw), lambda b, h: (b, head0 // GROUP_A + h))
    out = jax.ShapeDtypeStruct((batch * s, N_HEADS_A * HEAD_DIM), jnp.bfloat16)
    rarely_changes = dict(pipeline_mode=pl.Buffered(1))
    return pl.pallas_call(
        _mixers_kernel,
        grid=(batch, N_KV_A),
        in_specs=[
            group(H_QA), group(H_ZA),
            pl.BlockSpec((s, HEAD_DIM), lambda b, h: (b, 0)),
            pl.BlockSpec((1, HEAD_DIM), lambda b, h: (0, 0)),
            pl.BlockSpec((None, None, s // CMP_STRIDE, HEAD_DIM), lambda b, h: (b, h, 0, 0)),
            pl.BlockSpec((None, None, s // CMP_STRIDE, HEAD_DIM), lambda b, h: (b, N_KV_A + h, 0, 0)),
            full(H_KSL), full(H_VSL), full(H_KW), full(H_VW),
            pl.BlockSpec((None, s, HEAD_DIM), lambda b, h: (b, 0, 0), **rarely_changes),
            pl.BlockSpec((None, s, HEAD_DIM), lambda b, h: (b, 0, 0), **rarely_changes),
            group(H_QB), group(H_ZB), group(H_KB), group(H_VB),
            pl.BlockSpec((s, TQ), lambda b, h: (0, 0), **rarely_changes),
        ],
        out_specs=[pl.BlockSpec((s, gw), lambda b, h: (b, h)),
                   pl.BlockSpec((s, gw), lambda b, h: (b, h))],
        out_shape=[out, out],
        scratch_shapes=[pltpu.VMEM((s, HEAD_DIM), jnp.bfloat16),
                        pltpu.VMEM((s, HEAD_DIM), jnp.bfloat16),
                        pltpu.VMEM((V_ROWS, s), jnp.bfloat16),
                        pltpu.VMEM((V_ROWS, s), jnp.bfloat16),
                        pltpu.VMEM((GROUP_A, HEAD_DIM, s), jnp.float32),
                        pltpu.VMEM((s // TQ, s // SEL_LEN, TQ), jnp.float32),
                        pltpu.VMEM((HEAD_DIM, s), jnp.float32)],
        compiler_params=pltpu.CompilerParams(dimension_semantics=("parallel", "parallel"),
                                             vmem_limit_bytes=MIXERS_VMEM_LIMIT),
        name="token_mixers",
    )(front, front, gates, b_gate_row, cmp_kv, cmp_kv, back, back, back, back, cos, sin,
      back, back, back, back, dil_bias)


def _dilated_multiplicity(dist):
    cnt = jnp.zeros(dist.shape, jnp.float32)
    for window, dil in DILATIONS:
        cnt = cnt + jnp.where((dist & (dil - 1)) == 0,
                              jnp.where(dist <= (window // dil) * dil, 1.0, 0.0), 0.0)
    return jnp.where(dist >= 0, cnt, 0.0)


def _dilated_table_kernel(bias_ref):
    s_len = bias_ref.shape[0]
    dist = (s_len - TQ + lax.broadcasted_iota(jnp.int32, (s_len, TQ), 1)
            - lax.broadcasted_iota(jnp.int32, (s_len, TQ), 0))
    cnt = _dilated_multiplicity(dist)
    bias_ref[...] = jnp.where(cnt > 0.5, jnp.log2(jnp.maximum(cnt, 1.0)), NEG)


def _dilated_tables(s):
    return pl.pallas_call(
        _dilated_table_kernel,
        out_shape=jax.ShapeDtypeStruct((s, TQ), jnp.float32),
        compiler_params=pltpu.CompilerParams(vmem_limit_bytes=VMEM_LIMIT),
        name="dilated_tables",
    )()


OUT_PANELS = 4


def _out_proj_kernel(ma_ref, mb_ref, wa_ref, wb_ref, x_ref, nw_ref, o_ref, y_s, ssq_s):
    i = pl.program_id(0)
    n_tiles = pl.num_programs(0) - 1
    d = o_ref.shape[1]
    pn = d // OUT_PANELS

    def step(multiply, finish):
        if finish:
            scale = lax.rsqrt(ssq_s[...] * (1.0 / d) + EPS)
        ssq = None
        for p in range(OUT_PANELS):
            cols = slice(p * pn, (p + 1) * pn)
            if finish:
                o_ref[:, cols] = x_ref[:, cols] + y_s[:, cols] * scale * nw_ref[:, cols]
            if multiply:
                y = (jnp.dot(ma_ref[...], wa_ref[:, cols], preferred_element_type=jnp.float32)
                     + jnp.dot(mb_ref[...], wb_ref[:, cols], preferred_element_type=jnp.float32))
                y_s[:, cols] = y
                part = jnp.sum(y * y, axis=-1, keepdims=True)
                ssq = part if ssq is None else ssq + part
        if multiply:
            ssq_s[...] = ssq

    pl.when(i == 0)(functools.partial(step, True, False))
    pl.when((i > 0) & (i < n_tiles))(functools.partial(step, True, True))
    pl.when(i == n_tiles)(functools.partial(step, False, True))


def _out_proj(mix_a, mix_b, w, x2d, post_w, tm=256):
    m, d = x2d.shape
    ka, kb = mix_a.shape[1], mix_b.shape[1]
    assert ka == kb and w.shape[0] == ka + kb
    n_tiles = m // tm
    resident = dict(pipeline_mode=pl.Buffered(1))
    this_tile = lambda i: (jnp.minimum(i, n_tiles - 1), 0)
    prev_tile = lambda i: (jnp.maximum(i - 1, 0), 0)
    return pl.pallas_call(
        _out_proj_kernel,
        grid=(n_tiles + 1,),
        in_specs=[
            pl.BlockSpec((tm, ka), this_tile),
            pl.BlockSpec((tm, kb), this_tile),
            pl.BlockSpec((ka, d), lambda i: (0, 0), **resident),
            pl.BlockSpec((kb, d), lambda i: (1, 0), **resident),
            pl.BlockSpec((tm, d), prev_tile),
            pl.BlockSpec((1, d), lambda i: (0, 0)),
        ],
        out_specs=pl.BlockSpec((tm, d), prev_tile),
        out_shape=jax.ShapeDtypeStruct((m, d), jnp.float32),
        scratch_shapes=[pltpu.VMEM((tm, d), jnp.float32),
                        pltpu.VMEM((tm, 1), jnp.float32)],
        compiler_params=pltpu.CompilerParams(dimension_semantics=("arbitrary",),
                                             vmem_limit_bytes=OUT_PROJ_VMEM_LIMIT),
        name="out_proj_norm_residual",
    )(mix_a, mix_b, w, w, x2d, post_w.reshape(1, d))


def _layer(x, positions, rope, pre_w, post_w, w_in, b_gate, pe_k, pe_v, wk1, wk2, wv1, wv2, w_out):
    b, s, d = x.shape
    x2d = x.reshape(b * s, d)
    w_t = w_in.T
    if rope is None:
        h, gates, cos, sin = _rmsnorm(x2d, pre_w, w_t, positions.reshape(b * s))
        rope = (cos.reshape(b, s, HEAD_DIM), sin.reshape(b, s, HEAD_DIM))
    else:
        h, gates = _rmsnorm(x2d, pre_w, w_t)
    cos, sin = rope
    front, back, w_out_bf16 = _in_proj(h, w_t, w_out)

    pe = jnp.stack([pe_k, pe_v])
    w1 = jnp.stack([wk1, wv1]).astype(jnp.bfloat16)
    w2 = jnp.stack([wk2, wv2]).astype(jnp.bfloat16)
    cmp_kv = _compress(back, b, s, pe, w1, w2)

    bg = jnp.pad(b_gate.reshape(1, -1), ((0, 0), (0, HEAD_DIM - b_gate.size)))
    mix_a, mix_b = _mixers(front, back, gates, cmp_kv, bg, cos, sin, _dilated_tables(s), b, s)

    out = _out_proj(mix_a, mix_b, w_out_bf16, x2d, post_w)
    return out.reshape(b, s, d), rope


def kernel(x, positions, pre_norm_w, post_norm_w, w_in, b_gate, cmp_pe_k, cmp_pe_v,
           cmp_wk1, cmp_wk2, cmp_wv1, cmp_wv2, w_out):
    rope = None
    for l in range(pre_norm_w.shape[0]):
        x, rope = _layer(x, positions, rope, pre_norm_w[l], post_norm_w[l], w_in[l], b_gate[l],
                         cmp_pe_k[l], cmp_pe_v[l], cmp_wk1[l], cmp_wk2[l], cmp_wv1[l],
                         cmp_wv2[l], w_out[l])
    return x
```

```python
import functools
import math

import jax
import jax.numpy as jnp
from jax import lax
from jax.experimental import pallas as pl
from jax.experimental.pallas import tpu as pltpu

HEAD_DIM = 128
HALF = HEAD_DIM // 2
N_HEADS_A = 16
N_KV_A = 4
GROUP_A = 4
N_HEADS_B = 16
CMP_LEN = 32
CMP_STRIDE = 16
CMP_HIDDEN = 256
SEL_LEN = 64
TOP_N = 16
WIN = 512
DILATIONS = ((128, 1), (512, 4), (2048, 16))
ROPE_THETA = 10000.0
EPS = 1e-6
FORCE_BONUS = 1e4
NEG = -1e30
SCALE = HEAD_DIM ** -0.5
QSCALE = SCALE * math.log2(math.e)

H_QA = 0
H_ZA = H_QA + N_HEADS_A
N_FRONT_HEADS = H_ZA + N_HEADS_A
GATE_PER_KV = 3 * GROUP_A
H_KC = 0
H_VC = H_KC + N_KV_A
H_KSL = H_VC + N_KV_A
H_VSL = H_KSL + N_KV_A
H_KW = H_VSL + N_KV_A
H_VW = H_KW + N_KV_A
H_QB = H_VW + N_KV_A
H_KB = H_QB + N_HEADS_B
H_VB = H_KB + N_HEADS_B
H_ZB = H_VB + N_HEADS_B
N_BACK_HEADS = H_ZB + N_HEADS_B

TQ = 256
TK = 256
SEL_PER_TK = TK // SEL_LEN
F32_SUBLANES = 8
BF16_SUBLANES = 16
V_ROWS = HEAD_DIM + BF16_SUBLANES
PV_GROUP = 2
MIX_SPLIT = 2
VMEM_LIMIT = 56 * 1024 * 1024
MIXERS_VMEM_LIMIT = 60 * 1024 * 1024
OUT_PROJ_VMEM_LIMIT = 60 * 1024 * 1024


def _cparams(sem):
    return pltpu.CompilerParams(dimension_semantics=sem, vmem_limit_bytes=VMEM_LIMIT)


GATE_ROW0 = 2 * N_HEADS_A * HEAD_DIM
BACK_ROW0 = GATE_ROW0 + 3 * N_HEADS_A


def _nt_dot(x, w):
    return lax.dot_general(x, w, (((1,), (1,)), ((), ())), preferred_element_type=jnp.float32)


def _rmsnorm_kernel(*refs, with_rope):
    if with_rope:
        x_ref, w_ref, wg_ref, pos_ref, inv_ref, o_ref, g_ref, cos_ref, sin_ref = refs
    else:
        x_ref, w_ref, wg_ref, o_ref, g_ref = refs
    x = x_ref[...]
    ms = jnp.mean(x * x, axis=-1, keepdims=True)
    h = (x * lax.rsqrt(ms + EPS) * w_ref[...]).astype(o_ref.dtype)
    o_ref[...] = h
    g_ref[...] = _nt_dot(h, wg_ref[...].astype(jnp.bfloat16))
    if with_rope:
        ang = pos_ref[...].astype(jnp.float32) * inv_ref[...]
        lane = lax.broadcasted_iota(jnp.int32, ang.shape, 1)
        sin = jnp.sin(ang)
        cos_ref[...] = jnp.cos(ang)
        sin_ref[...] = jnp.where(lane < HALF, -sin, sin)


def _rmsnorm(x2d, w, w_t, positions=None, tm=512):
    m, d = x2d.shape
    with_rope = positions is not None
    rows = lambda width: pl.BlockSpec((tm, width), lambda i: (i, 0))
    in_specs = [rows(d), pl.BlockSpec((1, d), lambda i: (0, 0)),
                pl.BlockSpec((HEAD_DIM, d), lambda i: (GATE_ROW0 // HEAD_DIM, 0))]
    out_specs = [rows(d), rows(HEAD_DIM)]
    out_shape = [jax.ShapeDtypeStruct((m, d), jnp.bfloat16),
                 jax.ShapeDtypeStruct((m, HEAD_DIM), jnp.float32)]
    args = [x2d, w.reshape(1, d), w_t]
    if with_rope:
        inv = ROPE_THETA ** (-jnp.arange(HALF, dtype=jnp.float32) / HALF)
        in_specs += [rows(1), pl.BlockSpec((1, HEAD_DIM), lambda i: (0, 0))]
        out_specs += [rows(HEAD_DIM), rows(HEAD_DIM)]
        out_shape += [jax.ShapeDtypeStruct((m, HEAD_DIM), jnp.float32)] * 2
        args += [positions.reshape(m, 1), jnp.concatenate([inv, inv]).reshape(1, HEAD_DIM)]
    return pl.pallas_call(
        functools.partial(_rmsnorm_kernel, with_rope=with_rope),
        grid=(m // tm,),
        in_specs=in_specs,
        out_specs=out_specs,
        out_shape=out_shape,
        compiler_params=_cparams(("parallel",)),
        name="pre_rmsnorm_gates",
    )(*args)


PROJ_TN = GROUP_A * HEAD_DIM
PROJ_FRONT = GATE_ROW0 // PROJ_TN


WOUT_CAST_ROWS = 64


def _in_proj_kernel(x_ref, w_ref, wo_ref, front_ref, back_ref, wo_bf16_ref, *, wo_blocks):
    j = pl.program_id(1)
    t = pl.program_id(0) * pl.num_programs(1) + j

    @pl.when(t < wo_blocks)
    def _():
        wo_bf16_ref[...] = wo_ref[...].astype(wo_bf16_ref.dtype)

    @pl.when(j < PROJ_FRONT)
    def _():
        front_ref[...] = _nt_dot(x_ref[...], w_ref[...].astype(jnp.bfloat16)).astype(front_ref.dtype)

    @pl.when(j >= PROJ_FRONT)
    def _():
        back_ref[...] = _nt_dot(x_ref[...], w_ref[...].astype(jnp.bfloat16)).astype(back_ref.dtype)


def _in_proj(h, w_t, w_out, tm=2048):
    m, k = h.shape
    n_back = N_BACK_HEADS * HEAD_DIM
    n_steps = PROJ_FRONT + n_back // PROJ_TN
    wo_blocks = w_out.shape[0] // WOUT_CAST_ROWS
    assert wo_blocks <= (m // tm) * n_steps

    def wo_block(i, j):
        return jnp.minimum(i * n_steps + j, wo_blocks - 1), 0

    def w_row(i, j):
        row = PROJ_TN * j + jnp.where(j >= PROJ_FRONT, BACK_ROW0 - GATE_ROW0, 0)
        return pl.multiple_of(row, F32_SUBLANES), 0

    wo_spec = pl.BlockSpec((WOUT_CAST_ROWS, w_out.shape[1]), wo_block)
    return pl.pallas_call(
        functools.partial(_in_proj_kernel, wo_blocks=wo_blocks),
        grid=(m // tm, n_steps),
        in_specs=[
            pl.BlockSpec((tm, k), lambda i, j: (i, 0), pipeline_mode=pl.Buffered(1)),
            pl.BlockSpec((pl.Element(PROJ_TN), pl.Element(k)), w_row),
            wo_spec],
        out_specs=[
            pl.BlockSpec((tm, PROJ_TN), lambda i, j: (i, jnp.minimum(j, PROJ_FRONT - 1))),
            pl.BlockSpec((tm, PROJ_TN), lambda i, j: (i, jnp.maximum(j - PROJ_FRONT, 0))),
            wo_spec],
        out_shape=[jax.ShapeDtypeStruct((m, N_FRONT_HEADS * HEAD_DIM), jnp.bfloat16),
                   jax.ShapeDtypeStruct((m, n_back), jnp.bfloat16),
                   jax.ShapeDtypeStruct(w_out.shape, jnp.bfloat16)],
        compiler_params=_cparams(("arbitrary", "arbitrary")),
        name="in_proj",
    )(h, w_t, w_out)


def _rope(xf, cos, sin_signed):
    return xf * cos + pltpu.roll(xf, HALF, 1) * sin_signed


CMP_PITCH = CMP_STRIDE + 4


def _compress_kernel(x_ref, pe_ref, w1_ref, w2_ref, o_ref, xs_ref):
    s = x_ref.shape[0]
    n_rows = s // CMP_STRIDE
    for g in range(n_rows):
        xs_ref[pl.ds(g * CMP_PITCH, CMP_STRIDE), :] = (
            x_ref[pl.ds(g * CMP_STRIDE, CMP_STRIDE), :].astype(jnp.float32))
    xs_ref[pl.ds(n_rows * CMP_PITCH, CMP_STRIDE), :] = jnp.zeros((CMP_STRIDE, HEAD_DIM), jnp.float32)

    def block_rows(r):
        start = (r // CMP_STRIDE) * CMP_PITCH + r % CMP_STRIDE
        return xs_ref[pl.ds(start, n_rows, stride=CMP_PITCH), :]

    flat = jnp.concatenate(
        [(block_rows(r) + pe_ref[pl.ds(r, 1), :]).astype(jnp.bfloat16)
         for r in range(CMP_LEN)], axis=1)
    hid = jax.nn.gelu(jnp.dot(flat, w1_ref[...], preferred_element_type=jnp.float32))
    o_ref[...] = jnp.dot(hid.astype(jnp.bfloat16), w2_ref[...],
                         preferred_element_type=jnp.float32).astype(o_ref.dtype)


def _compress(proj, batch, s, pe, w1, w2):
    n_rows = s // CMP_STRIDE
    return pl.pallas_call(
        _compress_kernel,
        grid=(batch, 2 * N_KV_A),
        in_specs=[
            pl.BlockSpec((s, HEAD_DIM), lambda b, h: (b, H_KC + h)),
            pl.BlockSpec((None, CMP_LEN, HEAD_DIM), lambda b, h: (h // N_KV_A, 0, 0)),
            pl.BlockSpec((None, CMP_LEN * HEAD_DIM, CMP_HIDDEN), lambda b, h: (h // N_KV_A, 0, 0)),
            pl.BlockSpec((None, CMP_HIDDEN, HEAD_DIM), lambda b, h: (h // N_KV_A, 0, 0)),
        ],
        out_specs=pl.BlockSpec((None, None, n_rows, HEAD_DIM), lambda b, h: (b, h, 0, 0)),
        out_shape=jax.ShapeDtypeStruct((batch, 2 * N_KV_A, n_rows, HEAD_DIM), jnp.bfloat16),
        scratch_shapes=[pltpu.VMEM(((n_rows + 1) * CMP_PITCH, HEAD_DIM), jnp.float32)],
        compiler_params=_cparams(("parallel", "parallel")),
        name="compress",
    )(proj, pe, w1, w2)


def _kq(k, q):
    return lax.dot_general(k, q, (((1,), (1,)), ((), ())),
                           preferred_element_type=jnp.float32)


def _memo(fn):
    cache = []

    def get():
        if not cache:
            cache.append(fn())
        return cache[0]
    return get


class _AttnJob:
    def __init__(self, q_fn, chunks):
        self.q_fn, self.chunks = q_fn, chunks
        self.s, self.m, self.acc = [], None, None

    def _score(self, c):
        k_fn, _, mask_fn = self.chunks[c]
        s = mask_fn(_kq(k_fn(), self.q_fn()))
        self.s.append(s)
        cm = jnp.max(s, axis=0, keepdims=True)
        self.m = cm if self.m is None else jnp.maximum(self.m, cm)

    def _value(self, cs):
        p = jnp.concatenate([jnp.exp2(self.s[c] - self.m).astype(jnp.bfloat16) for c in cs], axis=0)
        vt = jnp.concatenate([self.chunks[c][1]() for c in cs], axis=1)
        part = jnp.dot(vt, p, preferred_element_type=jnp.float32)
        self.acc = part if self.acc is None else self.acc + part

    def score_tasks(self):
        return [functools.partial(self._score, c) for c in range(len(self.chunks))]

    def value_tasks(self):
        n = len(self.chunks)
        return [functools.partial(self._value, range(c, min(c + PV_GROUP, n)))
                for c in range(0, n, PV_GROUP)]

    def result(self):
        return self.acc[:HEAD_DIM, :] * (1.0 / self.acc[HEAD_DIM:HEAD_DIM + 1, :])


def _round_robin(task_lists):
    for i in range(max(len(t) for t in task_lists)):
        for tasks in task_lists:
            if i < len(tasks):
                tasks[i]()


def _run_pipelined(streams, on_done):
    n = len(streams[0])
    _round_robin([jobs[0].score_tasks() for jobs in streams])
    for k in range(n):
        lists = []
        for jobs in streams:
            lists.append(jobs[k].value_tasks())
            if k + 1 < n:
                lists.append(jobs[k + 1].score_tasks())
        _round_robin(lists)
        for si, jobs in enumerate(streams):
            on_done(si, k, jobs[k])


def _store_values_transposed(src_ref, dst_ref):
    s_len = src_ref.shape[0]
    for c in range(s_len // TK):
        blk = src_ref[c * TK:(c + 1) * TK, :].astype(jnp.float32)
        dst_ref[0:HEAD_DIM, c * TK:(c + 1) * TK] = blk.T.astype(jnp.bfloat16)
    row = lax.broadcasted_iota(jnp.int32, (BF16_SUBLANES, s_len), 0)
    dst_ref[HEAD_DIM:, :] = jnp.where(row == 0, 1.0, 0.0).astype(jnp.bfloat16)


def _silu(z):
    return z * jax.nn.sigmoid(z)


def _topk_active(q0):
    return q0 + TQ > TOP_N * SEL_LEN


def _nsa_selection_bias(p_sum, q0, s_len):
    n_sel = s_len // SEL_LEN
    n_cmp = s_len // CMP_STRIDE - CMP_LEN // CMP_STRIDE + 1
    oj = lax.broadcasted_iota(jnp.int32, (n_sel, HEAD_DIM), 0)
    oc = lax.broadcasted_iota(jnp.int32, (n_sel, HEAD_DIM), 1)
    ovl = jnp.where(oc * CMP_STRIDE < oj * SEL_LEN + SEL_LEN,
                    jnp.where(oc * CMP_STRIDE + CMP_LEN > oj * SEL_LEN, 1.0, 0.0), 0.0)
    ovl = jnp.where(oc < n_cmp, ovl, 0.0).astype(jnp.bfloat16)
    p_hi = p_sum.astype(jnp.bfloat16)
    p_lo = (p_sum - p_hi.astype(jnp.float32)).astype(jnp.bfloat16)
    imp = (jnp.dot(ovl, p_hi, preferred_element_type=jnp.float32)
           + jnp.dot(ovl, p_lo, preferred_element_type=jnp.float32))
    j_row = lax.broadcasted_iota(jnp.int32, (n_sel, TQ), 0)
    t_sel = q0 + lax.broadcasted_iota(jnp.int32, (n_sel, TQ), 1)
    cur = lax.shift_right_arithmetic(t_sel, int(math.log2(SEL_LEN)))
    forced = (j_row == 0) | (j_row == cur) | (j_row == cur - 1)
    imp = jnp.where(forced, imp + FORCE_BONUS, imp)
    imp = jnp.where(j_row * SEL_LEN <= t_sel, imp, -jnp.inf)
    rank = jnp.zeros((n_sel, TQ), jnp.float32)
    for i in range(n_sel):
        vi = imp[i:i + 1, :]
        rank = rank + jnp.where(j_row > i, jnp.where(vi >= imp, 1.0, 0.0),
                                jnp.where(vi > imp, 1.0, 0.0))
    return jnp.where(rank < TOP_N, 0.0, NEG)


def _mixers_kernel(q_ref, z_ref, g_ref, bg_ref, kc_ref, vc_ref, ksl_ref, vsl_ref, kw_ref, vw_ref,
                   cos_ref, sin_ref, qd_ref, zd_ref, kd_ref, vd_ref, dbias_ref, o_ref, od_ref,
                   ksl_rot, kw_rot, vsl_t, vw_t, ocmp_s, selb_s, gate_s):
    s_len = ksl_ref.shape[0]
    n_q = s_len // TQ

    ksl_rot[...] = _rope(ksl_ref[...].astype(jnp.float32), cos_ref[...], sin_ref[...]).astype(jnp.bfloat16)
    kw_rot[...] = _rope(kw_ref[...].astype(jnp.float32), cos_ref[...], sin_ref[...]).astype(jnp.bfloat16)
    _store_values_transposed(vsl_ref, vsl_t)
    _store_values_transposed(vw_ref, vw_t)
    kc = kc_ref[...]
    vc_t = vc_ref[...].astype(jnp.float32).T.astype(jnp.bfloat16)
    bg = bg_ref[...]
    kv_head = pl.program_id(1)

    c_row = lax.broadcasted_iota(jnp.int32, (HEAD_DIM, TQ), 0)
    lane_q = lax.broadcasted_iota(jnp.int32, (HEAD_DIM, TQ), 1)
    lane_one = lax.broadcasted_iota(jnp.int32, (1, TQ), 1)
    for qi in range(n_q):
        q0 = qi * TQ
        rows = slice(q0, q0 + TQ)
        q_raw = [(q_ref[rows, g * HEAD_DIM:(g + 1) * HEAD_DIM].astype(jnp.float32) * QSCALE
                  ).astype(jnp.bfloat16) for g in range(GROUP_A)]
        c_valid = c_row * CMP_STRIDE + (CMP_LEN - 1) <= q0 + lane_q
        any_valid = jnp.where(q0 + lane_one >= CMP_LEN - 1, 1.0, 0.0)
        sc4 = _kq(kc, jnp.concatenate(q_raw, axis=0))
        p_heads = []
        p_sum = jnp.zeros((HEAD_DIM, TQ), jnp.float32)
        for g in range(GROUP_A):
            sc = jnp.where(c_valid, sc4[:, g * TQ:(g + 1) * TQ], NEG)
            e = jnp.exp2(sc - jnp.max(sc, axis=0, keepdims=True))
            p = e * (any_valid / jnp.sum(e, axis=0, keepdims=True))
            p_heads.append(p.astype(jnp.bfloat16))
            p_sum = p_sum + p
        o_cmp4 = jnp.dot(vc_t, jnp.concatenate(p_heads, axis=1),
                         preferred_element_type=jnp.float32)
        for g in range(GROUP_A):
            ocmp_s[g, :, rows] = o_cmp4[:, g * TQ:(g + 1) * TQ]
        if _topk_active(q0):
            selb_s[qi] = _nsa_selection_bias(p_sum, q0, s_len)
        gate_s[:, rows] = jax.nn.sigmoid(g_ref[rows, :] + bg).T

    diag_rc = (lax.broadcasted_iota(jnp.int32, (TK, TQ), 0)
               - lax.broadcasted_iota(jnp.int32, (TK, TQ), 1))

    def slc_mask(qi, c, s):
        if _topk_active(qi * TQ):
            s = s + jnp.concatenate(
                [jnp.broadcast_to(selb_s[qi, c * SEL_PER_TK + a:c * SEL_PER_TK + a + 1, :],
                                  (SEL_LEN, TQ)) for a in range(SEL_PER_TK)], axis=0)
        if c == qi:
            s = jnp.where(diag_rc <= 0, s, NEG)
        return s

    def win_mask(qi, c, s):
        if c == qi:
            return jnp.where(diag_rc <= 0, s, NEG)
        if (qi - c) * TK + TQ - 1 > WIN - 1:
            return jnp.where(diag_rc >= (qi - c) * TK - (WIN - 1), s, NEG)
        return s

    ones_rows = jnp.where(lax.broadcasted_iota(jnp.int32, (BF16_SUBLANES, TK), 0) == 0,
                          1.0, 0.0).astype(jnp.bfloat16)

    def head_body(g, carry):
        col = pl.ds(pl.multiple_of(g * HEAD_DIM, HEAD_DIM), HEAD_DIM)
        gate_row = GATE_PER_KV * kv_head + 3 * g
        blocks = [slice(c * TK, (c + 1) * TK) for c in range(n_q)]

        def roped(ref, rows, scale):
            xf = ref[rows, col].astype(jnp.float32)
            if scale != 1.0:
                xf = xf * scale
            return _rope(xf, cos_ref[rows, :], sin_ref[rows, :]).astype(jnp.bfloat16)

        def dil_value_t(rows):
            vt = vd_ref[rows, col].astype(jnp.float32).T.astype(jnp.bfloat16)
            return jnp.concatenate([vt, ones_rows], axis=0)

        def dil_bias(delta, s):
            r0 = (n_q - 1 - delta) * TK
            return s + dbias_ref[r0:r0 + TK, :]

        dil_k = [_memo(functools.partial(roped, kd_ref, rows, 1.0)) for rows in blocks]
        dil_v = [_memo(functools.partial(dil_value_t, rows)) for rows in blocks]
        slc_jobs, win_jobs, dil_jobs = [], [], []
        for qi in range(n_q):
            q = _memo(functools.partial(roped, q_ref, blocks[qi], QSCALE))
            first_win = max(0, (qi * TQ - (WIN - 1)) // TK)
            for jobs, k_ref_, v_ref_, mask, c_lo in ((slc_jobs, ksl_rot, vsl_t, slc_mask, 0),
                                                     (win_jobs, kw_rot, vw_t, win_mask, first_win)):
                jobs.append(_AttnJob(q, [
                    (functools.partial(lambda r, c: r[c * TK:(c + 1) * TK, :], k_ref_, c),
                     functools.partial(lambda r, c: r[:, c * TK:(c + 1) * TK], v_ref_, c),
                     functools.partial(mask, qi, c))
                    for c in range(c_lo, qi + 1)]))
            dil_jobs.append(_AttnJob(
                _memo(functools.partial(roped, qd_ref, blocks[qi], QSCALE)),
                [(dil_k[c], dil_v[c], functools.partial(dil_bias, qi - c)) for c in range(qi + 1)]))

        def on_done(si, k, job):
            kind, qi = si % 3, MIX_SPLIT * k + si // 3
            rows = blocks[qi]
            if kind == 1:
                o_t = (gate_s[pl.ds(gate_row, 1), rows] * ocmp_s[g, :, rows]
                       + gate_s[pl.ds(gate_row + 1, 1), rows] * slc_jobs[qi].result()
                       + gate_s[pl.ds(gate_row + 2, 1), rows] * job.result())
                z = z_ref[rows, col].astype(jnp.float32)
                o_ref[rows, col] = (o_t.T * _silu(z)).astype(o_ref.dtype)
            elif kind == 2:
                z = zd_ref[rows, col].astype(jnp.float32)
                od_ref[rows, col] = (job.result().T * _silu(z)).astype(od_ref.dtype)

        streams = []
        for r in range(MIX_SPLIT):
            streams += [slc_jobs[r::MIX_SPLIT], win_jobs[r::MIX_SPLIT], dil_jobs[r::MIX_SPLIT]]
        _run_pipelined(streams, on_done)
        return carry

    lax.fori_loop(0, GROUP_A, head_body, 0)


def _mixers(front, back, gates, cmp_kv, b_gate_row, cos, sin, dil_bias, batch, s):
    gw = GROUP_A * HEAD_DIM
    assert N_HEADS_B == N_HEADS_A
    full = lambda col0: pl.BlockSpec((s, HEAD_DIM), lambda b, h: (b, col0 + h))
    group = lambda head0: pl.BlockSpec((s, gw), lambda b, h: (b, head0 // GROUP_A + h))
    out = jax.ShapeDtypeStruct((batch * s, N_HEADS_A * HEAD_DIM), jnp.bfloat16)
    rarely_changes = dict(pipeline_mode=pl.Buffered(1))
    return pl.pallas_call(
        _mixers_kernel,
        grid=(batch, N_KV_A),
        in_specs=[
            group(H_QA), group(H_ZA),
            pl.BlockSpec((s, HEAD_DIM), lambda b, h: (b, 0)),
            pl.BlockSpec((1, HEAD_DIM), lambda b, h: (0, 0)),
            pl.BlockSpec((None, None, s // CMP_STRIDE, HEAD_DIM), lambda b, h: (b, h, 0, 0)),
            pl.BlockSpec((None, None, s // CMP_STRIDE, HEAD_DIM), lambda b, h: (b, N_KV_A + h, 0, 0)),
            full(H_KSL), full(H_VSL), full(H_KW), full(H_VW),
            pl.BlockSpec((None, s, HEAD_DIM), lambda b, h: (b, 0, 0), **rarely_changes),
            pl.BlockSpec((None, s, HEAD_DIM), lambda b, h: (b, 0, 0), **rarely_changes),
            group(H_QB), group(H_ZB), group(H_KB), group(H_VB),
            pl.BlockSpec((s, TQ), lambda b, h: (0, 0), **rarely_changes),
        ],
        out_specs=[pl.BlockSpec((s, gw), lambda b, h: (b, h)),
                   pl.BlockSpec((s, gw), lambda b, h: (b, h))],
        out_shape=[out, out],
        scratch_shapes=[pltpu.VMEM((s, HEAD_DIM), jnp.bfloat16),
                        pltpu.VMEM((s, HEAD_DIM), jnp.bfloat16),
                        pltpu.VMEM((V_ROWS, s), jnp.bfloat16),
                        pltpu.VMEM((V_ROWS, s), jnp.bfloat16),
                        pltpu.VMEM((GROUP_A, HEAD_DIM, s), jnp.float32),
                        pltpu.VMEM((s // TQ, s // SEL_LEN, TQ), jnp.float32),
                        pltpu.VMEM((HEAD_DIM, s), jnp.float32)],
        compiler_params=pltpu.CompilerParams(dimension_semantics=("parallel", "parallel"),
                                             vmem_limit_bytes=MIXERS_VMEM_LIMIT),
        name="token_mixers",
    )(front, front, gates, b_gate_row, cmp_kv, cmp_kv, back, back, back, back, cos, sin,
      back, back, back, back, dil_bias)


def _dilated_multiplicity(dist):
    cnt = jnp.zeros(dist.shape, jnp.float32)
    for window, dil in DILATIONS:
        cnt = cnt + jnp.where((dist & (dil - 1)) == 0,
                              jnp.where(dist <= (window // dil) * dil, 1.0, 0.0), 0.0)
    return jnp.where(dist >= 0, cnt, 0.0)


def _dilated_table_kernel(bias_ref):
    s_len = bias_ref.shape[0]
    dist = (s_len - TQ + lax.broadcasted_iota(jnp.int32, (s_len, TQ), 1)
            - lax.broadcasted_iota(jnp.int32, (s_len, TQ), 0))
    cnt = _dilated_multiplicity(dist)
    bias_ref[...] = jnp.where(cnt > 0.5, jnp.log2(jnp.maximum(cnt, 1.0)), NEG)


def _dilated_tables(s):
    return pl.pallas_call(
        _dilated_table_kernel,
        out_shape=jax.ShapeDtypeStruct((s, TQ), jnp.float32),
        compiler_params=pltpu.CompilerParams(vmem_limit_bytes=VMEM_LIMIT),
        name="dilated_tables",
    )()


OUT_PANELS = 4


def _out_proj_kernel(ma_ref, mb_ref, wa_ref, wb_ref, x_ref, nw_ref, o_ref, y_s, ssq_s):
    i = pl.program_id(0)
    n_tiles = pl.num_programs(0) - 1
    d = o_ref.shape[1]
    pn = d // OUT_PANELS

    def step(multiply, finish):
        if finish:
            scale = lax.rsqrt(ssq_s[...] * (1.0 / d) + EPS)
        ssq = None
        for p in range(OUT_PANELS):
            cols = slice(p * pn, (p + 1) * pn)
            if finish:
                o_ref[:, cols] = x_ref[:, cols] + y_s[:, cols] * scale * nw_ref[:, cols]
            if multiply:
                y = (jnp.dot(ma_ref[...], wa_ref[:, cols], preferred_element_type=jnp.float32)
                     + jnp.dot(mb_ref[...], wb_ref[:, cols], preferred_element_type=jnp.float32))
                y_s[:, cols] = y
                part = jnp.sum(y * y, axis=-1, keepdims=True)
                ssq = part if ssq is None else ssq + part
        if multiply:
            ssq_s[...] = ssq

    pl.when(i == 0)(functools.partial(step, True, False))
    pl.when((i > 0) & (i < n_tiles))(functools.partial(step, True, True))
    pl.when(i == n_tiles)(functools.partial(step, False, True))


def _out_proj(mix_a, mix_b, w, x2d, post_w, tm=256):
    m, d = x2d.shape
    ka, kb = mix_a.shape[1], mix_b.shape[1]
    assert ka == kb and w.shape[0] == ka + kb
    n_tiles = m // tm
    resident = dict(pipeline_mode=pl.Buffered(1))
    this_tile = lambda i: (jnp.minimum(i, n_tiles - 1), 0)
    prev_tile = lambda i: (jnp.maximum(i - 1, 0), 0)
    return pl.pallas_call(
        _out_proj_kernel,
        grid=(n_tiles + 1,),
        in_specs=[
            pl.BlockSpec((tm, ka), this_tile),
            pl.BlockSpec((tm, kb), this_tile),
            pl.BlockSpec((ka, d), lambda i: (0, 0), **resident),
            pl.BlockSpec((kb, d), lambda i: (1, 0), **resident),
            pl.BlockSpec((tm, d), prev_tile),
            pl.BlockSpec((1, d), lambda i: (0, 0)),
        ],
        out_specs=pl.BlockSpec((tm, d), prev_tile),
        out_shape=jax.ShapeDtypeStruct((m, d), jnp.float32),
        scratch_shapes=[pltpu.VMEM((tm, d), jnp.float32),
                        pltpu.VMEM((tm, 1), jnp.float32)],
        compiler_params=pltpu.CompilerParams(dimension_semantics=("arbitrary",),
                                             vmem_limit_bytes=OUT_PROJ_VMEM_LIMIT),
        name="out_proj_norm_residual",
    )(mix_a, mix_b, w, w, x2d, post_w.reshape(1, d))


def _layer(x, positions, rope, pre_w, post_w, w_in, b_gate, pe_k, pe_v, wk1, wk2, wv1, wv2, w_out):
    b, s, d = x.shape
    x2d = x.reshape(b * s, d)
    w_t = w_in.T
    if rope is None:
        h, gates, cos, sin = _rmsnorm(x2d, pre_w, w_t, positions.reshape(b * s))
        rope = (cos.reshape(b, s, HEAD_DIM), sin.reshape(b, s, HEAD_DIM))
    else:
        h, gates = _rmsnorm(x2d, pre_w, w_t)
    cos, sin = rope
    front, back, w_out_bf16 = _in_proj(h, w_t, w_out)

    pe = jnp.stack([pe_k, pe_v])
    w1 = jnp.stack([wk1, wv1]).astype(jnp.bfloat16)
    w2 = jnp.stack([wk2, wv2]).astype(jnp.bfloat16)
    cmp_kv = _compress(back, b, s, pe, w1, w2)

    bg = jnp.pad(b_gate.reshape(1, -1), ((0, 0), (0, HEAD_DIM - b_gate.size)))
    mix_a, mix_b = _mixers(front, back, gates, cmp_kv, bg, cos, sin, _dilated_tables(s), b, s)

    out = _out_proj(mix_a, mix_b, w_out_bf16, x2d, post_w)
    return out.reshape(b, s, d), rope


def kernel(x, positions, pre_norm_w, post_norm_w, w_in, b_gate, cmp_pe_k, cmp_pe_v,
           cmp_wk1, cmp_wk2, cmp_wv1, cmp_wv2, w_out):
    rope = None
    for l in range(pre_norm_w.shape[0]):
        x, rope = _layer(x, positions, rope, pre_norm_w[l], post_norm_w[l], w_in[l], b_gate[l],
                         cmp_pe_k[l], cmp_pe_v[l], cmp_wk1[l], cmp_wk2[l], cmp_wv1[l],
                         cmp_wv2[l], w_out[l])
    return x
```

```python
import functools
import math

import jax
import jax.numpy as jnp
from jax import lax
from jax.experimental import pallas as pl
from jax.experimental.pallas import tpu as pltpu

HEAD_DIM = 128
HALF = HEAD_DIM // 2
N_HEADS_A = 16
N_KV_A = 4
GROUP_A = 4
N_HEADS_B = 16
CMP_LEN = 32
CMP_STRIDE = 16
CMP_HIDDEN = 256
SEL_LEN = 64
TOP_N = 16
WIN = 512
DILATIONS = ((128, 1), (512, 4), (2048, 16))
ROPE_THETA = 10000.0
EPS = 1e-6
FORCE_BONUS = 1e4
NEG = -1e30
SCALE = HEAD_DIM ** -0.5
QSCALE = SCALE * math.log2(math.e)

H_QA = 0
H_ZA = H_QA + N_HEADS_A
N_FRONT_HEADS = H_ZA + N_HEADS_A
GATE_PER_KV = 3 * GROUP_A
H_KC = 0
H_VC = H_KC + N_KV_A
H_KSL = H_VC + N_KV_A
H_VSL = H_KSL + N_KV_A
H_KW = H_VSL + N_KV_A
H_VW = H_KW + N_KV_A
H_QB = H_VW + N_KV_A
H_KB = H_QB + N_HEADS_B
H_VB = H_KB + N_HEADS_B
H_ZB = H_VB + N_HEADS_B
N_BACK_HEADS = H_ZB + N_HEADS_B

TQ = 256
TK = 256
SEL_PER_TK = TK // SEL_LEN
F32_SUBLANES = 8
BF16_SUBLANES = 16
V_ROWS = HEAD_DIM + BF16_SUBLANES
PV_GROUP = 2
MIX_SPLIT = 2
VMEM_LIMIT = 56 * 1024 * 1024
MIXERS_VMEM_LIMIT = 60 * 1024 * 1024
OUT_PROJ_VMEM_LIMIT = 60 * 1024 * 1024


def _cparams(sem):
    return pltpu.CompilerParams(dimension_semantics=sem, vmem_limit_bytes=VMEM_LIMIT)


GATE_ROW0 = 2 * N_HEADS_A * HEAD_DIM
BACK_ROW0 = GATE_ROW0 + 3 * N_HEADS_A


def _nt_dot(x, w):
    return lax.dot_general(x, w, (((1,), (1,)), ((), ())), preferred_element_type=jnp.float32)


def _rmsnorm_kernel(*refs, with_rope):
    if with_rope:
        x_ref, w_ref, wg_ref, pos_ref, inv_ref, o_ref, g_ref, cos_ref, sin_ref = refs
    else:
        x_ref, w_ref, wg_ref, o_ref, g_ref = refs
    x = x_ref[...]
    ms = jnp.mean(x * x, axis=-1, keepdims=True)
    h = (x * lax.rsqrt(ms + EPS) * w_ref[...]).astype(o_ref.dtype)
    o_ref[...] = h
    g_ref[...] = _nt_dot(h, wg_ref[...].astype(jnp.bfloat16))
    if with_rope:
        ang = pos_ref[...].astype(jnp.float32) * inv_ref[...]
        lane = lax.broadcasted_iota(jnp.int32, ang.shape, 1)
        sin = jnp.sin(ang)
        cos_ref[...] = jnp.cos(ang)
        sin_ref[...] = jnp.where(lane < HALF, -sin, sin)


def _rmsnorm(x2d, w, w_t, positions=None, tm=512):
    m, d = x2d.shape
    with_rope = positions is not None
    rows = lambda width: pl.BlockSpec((tm, width), lambda i: (i, 0))
    in_specs = [rows(d), pl.BlockSpec((1, d), lambda i: (0, 0)),
                pl.BlockSpec((HEAD_DIM, d), lambda i: (GATE_ROW0 // HEAD_DIM, 0))]
    out_specs = [rows(d), rows(HEAD_DIM)]
    out_shape = [jax.ShapeDtypeStruct((m, d), jnp.bfloat16),
                 jax.ShapeDtypeStruct((m, HEAD_DIM), jnp.float32)]
    args = [x2d, w.reshape(1, d), w_t]
    if with_rope:
        inv = ROPE_THETA ** (-jnp.arange(HALF, dtype=jnp.float32) / HALF)
        in_specs += [rows(1), pl.BlockSpec((1, HEAD_DIM), lambda i: (0, 0))]
        out_specs += [rows(HEAD_DIM), rows(HEAD_DIM)]
        out_shape += [jax.ShapeDtypeStruct((m, HEAD_DIM), jnp.float32)] * 2
        args += [positions.reshape(m, 1), jnp.concatenate([inv, inv]).reshape(1, HEAD_DIM)]
    return pl.pallas_call(
        functools.partial(_rmsnorm_kernel, with_rope=with_rope),
        grid=(m // tm,),
        in_specs=in_specs,
        out_specs=out_specs,
        out_shape=out_shape,
        compiler_params=_cparams(("parallel",)),
        name="pre_rmsnorm_gates",
    )(*args)


PROJ_TN = GROUP_A * HEAD_DIM
PROJ_FRONT = GATE_ROW0 // PROJ_TN


WOUT_CAST_ROWS = 64


def _in_proj_kernel(x_ref, w_ref, wo_ref, front_ref, back_ref, wo_bf16_ref, *, wo_blocks):
    j = pl.program_id(1)
    t = pl.program_id(0) * pl.num_programs(1) + j

    @pl.when(t < wo_blocks)
    def _():
        wo_bf16_ref[...] = wo_ref[...].astype(wo_bf16_ref.dtype)

    @pl.when(j < PROJ_FRONT)
    def _():
        front_ref[...] = _nt_dot(x_ref[...], w_ref[...].astype(jnp.bfloat16)).astype(front_ref.dtype)

    @pl.when(j >= PROJ_FRONT)
    def _():
        back_ref[...] = _nt_dot(x_ref[...], w_ref[...].astype(jnp.bfloat16)).astype(back_ref.dtype)


def _in_proj(h, w_t, w_out, tm=2048):
    m, k = h.shape
    n_back = N_BACK_HEADS * HEAD_DIM
    n_steps = PROJ_FRONT + n_back // PROJ_TN
    wo_blocks = w_out.shape[0] // WOUT_CAST_ROWS
    assert wo_blocks <= (m // tm) * n_steps

    def wo_block(i, j):
        return jnp.minimum(i * n_steps + j, wo_blocks - 1), 0

    def w_row(i, j):
        row = PROJ_TN * j + jnp.where(j >= PROJ_FRONT, BACK_ROW0 - GATE_ROW0, 0)
        return pl.multiple_of(row, F32_SUBLANES), 0

    wo_spec = pl.BlockSpec((WOUT_CAST_ROWS, w_out.shape[1]), wo_block)
    return pl.pallas_call(
        functools.partial(_in_proj_kernel, wo_blocks=wo_blocks),
        grid=(m // tm, n_steps),
        in_specs=[
            pl.BlockSpec((tm, k), lambda i, j: (i, 0), pipeline_mode=pl.Buffered(1)),
            pl.BlockSpec((pl.Element(PROJ_TN), pl.Element(k)), w_row),
            wo_spec],
        out_specs=[
            pl.BlockSpec((tm, PROJ_TN), lambda i, j: (i, jnp.minimum(j, PROJ_FRONT - 1))),
            pl.BlockSpec((tm, PROJ_TN), lambda i, j: (i, jnp.maximum(j - PROJ_FRONT, 0))),
            wo_spec],
        out_shape=[jax.ShapeDtypeStruct((m, N_FRONT_HEADS * HEAD_DIM), jnp.bfloat16),
                   jax.ShapeDtypeStruct((m, n_back), jnp.bfloat16),
                   jax.ShapeDtypeStruct(w_out.shape, jnp.bfloat16)],
        compiler_params=_cparams(("arbitrary", "arbitrary")),
        name="in_proj",
    )(h, w_t, w_out)


def _rope(xf, cos, sin_signed):
    return xf * cos + pltpu.roll(xf, HALF, 1) * sin_signed


CMP_PITCH = CMP_STRIDE + 4


def _compress_kernel(x_ref, pe_ref, w1_ref, w2_ref, o_ref, xs_ref):
    s = x_ref.shape[0]
    n_rows = s // CMP_STRIDE
    for hh in range(N_KV_A):
        cols = slice(hh * HEAD_DIM, (hh + 1) * HEAD_DIM)
        for g in range(n_rows):
            xs_ref[hh, pl.ds(g * CMP_PITCH, CMP_STRIDE), :] = (
                x_ref[pl.ds(g * CMP_STRIDE, CMP_STRIDE), cols].astype(jnp.float32))
        xs_ref[hh, pl.ds(n_rows * CMP_PITCH, CMP_STRIDE), :] = jnp.zeros(
            (CMP_STRIDE, HEAD_DIM), jnp.float32)

    def block_rows(hh, r):
        start = (r // CMP_STRIDE) * CMP_PITCH + r % CMP_STRIDE
        return xs_ref[hh, pl.ds(start, n_rows, stride=CMP_PITCH), :]

    flat = jnp.concatenate(
        [jnp.concatenate([(block_rows(hh, r) + pe_ref[pl.ds(r, 1), :]).astype(jnp.bfloat16)
                          for r in range(CMP_LEN)], axis=1)
         for hh in range(N_KV_A)], axis=0)
    hid = jax.nn.gelu(jnp.dot(flat, w1_ref[...], preferred_element_type=jnp.float32))
    out = jnp.dot(hid.astype(jnp.bfloat16), w2_ref[...], preferred_element_type=jnp.float32)
    for hh in range(N_KV_A):
        o_ref[hh] = out[hh * n_rows:(hh + 1) * n_rows, :].astype(o_ref.dtype)


def _compress(proj, batch, s, pe, w1, w2):
    n_rows = s // CMP_STRIDE
    assert H_KC % N_KV_A == 0 and H_VC == H_KC + N_KV_A
    return pl.pallas_call(
        _compress_kernel,
        grid=(batch, 2),
        in_specs=[
            pl.BlockSpec((s, N_KV_A * HEAD_DIM), lambda b, kv: (b, H_KC // N_KV_A + kv)),
            pl.BlockSpec((None, CMP_LEN, HEAD_DIM), lambda b, kv: (kv, 0, 0)),
            pl.BlockSpec((None, CMP_LEN * HEAD_DIM, CMP_HIDDEN), lambda b, kv: (kv, 0, 0)),
            pl.BlockSpec((None, CMP_HIDDEN, HEAD_DIM), lambda b, kv: (kv, 0, 0)),
        ],
        out_specs=pl.BlockSpec((None, N_KV_A, n_rows, HEAD_DIM), lambda b, kv: (b, kv, 0, 0)),
        out_shape=jax.ShapeDtypeStruct((batch, 2 * N_KV_A, n_rows, HEAD_DIM), jnp.bfloat16),
        scratch_shapes=[pltpu.VMEM((N_KV_A, (n_rows + 1) * CMP_PITCH, HEAD_DIM), jnp.float32)],
        compiler_params=_cparams(("parallel", "parallel")),
        name="compress",
    )(proj, pe, w1, w2)


def _kq(k, q):
    return lax.dot_general(k, q, (((1,), (1,)), ((), ())),
                           preferred_element_type=jnp.float32)


def _memo(fn):
    cache = []

    def get():
        if not cache:
            cache.append(fn())
        return cache[0]
    return get


class _AttnJob:
    def __init__(self, q_fn, chunks):
        self.q_fn, self.chunks = q_fn, chunks
        self.s, self.bias, self.m, self.acc = [], [], None, None

    def _score(self, c):
        k_fn, _, mask_fn = self.chunks[c][:3]
        s = mask_fn(_kq(k_fn(), self.q_fn()))
        bias = self.chunks[c][3]() if len(self.chunks[c]) > 3 else None
        self.s.append(s)
        self.bias.append(bias)
        if bias is None:
            cm = jnp.max(s, axis=0, keepdims=True)
        else:
            sub = s.shape[0] // len(bias)
            cm = functools.reduce(jnp.maximum, [
                jnp.max(s[a * sub:(a + 1) * sub, :], axis=0, keepdims=True) + b
                for a, b in enumerate(bias)])
        self.m = cm if self.m is None else jnp.maximum(self.m, cm)

    def _probabilities(self, c):
        s, bias = self.s[c], self.bias[c]
        if bias is None:
            return jnp.exp2(s - self.m).astype(jnp.bfloat16)
        sub = s.shape[0] // len(bias)
        return jnp.concatenate([jnp.exp2(s[a * sub:(a + 1) * sub, :] + (b - self.m))
                                for a, b in enumerate(bias)], axis=0).astype(jnp.bfloat16)

    def _value(self, cs):
        p = jnp.concatenate([self._probabilities(c) for c in cs], axis=0)
        vt = jnp.concatenate([self.chunks[c][1]() for c in cs], axis=1)
        part = jnp.dot(vt, p, preferred_element_type=jnp.float32)
        self.acc = part if self.acc is None else self.acc + part

    def score_tasks(self):
        return [functools.partial(self._score, c) for c in range(len(self.chunks))]

    def value_tasks(self):
        n = len(self.chunks)
        return [functools.partial(self._value, range(c, min(c + PV_GROUP, n)))
                for c in range(0, n, PV_GROUP)]

    def result(self):
        return self.acc[:HEAD_DIM, :] * (1.0 / self.acc[HEAD_DIM:HEAD_DIM + 1, :])


def _round_robin(task_lists):
    for i in range(max(len(t) for t in task_lists)):
        for tasks in task_lists:
            if i < len(tasks):
                tasks[i]()


def _run_pipelined(streams, on_done):
    n = len(streams[0])
    _round_robin([jobs[0].score_tasks() for jobs in streams])
    for k in range(n):
        lists = []
        for jobs in streams:
            lists.append(jobs[k].value_tasks())
            if k + 1 < n:
                lists.append(jobs[k + 1].score_tasks())
        _round_robin(lists)
        for si, jobs in enumerate(streams):
            on_done(si, k, jobs[k])


def _store_values_transposed(src_ref, dst_ref):
    s_len = src_ref.shape[0]
    for c in range(s_len // TK):
        blk = src_ref[c * TK:(c + 1) * TK, :].astype(jnp.float32)
        dst_ref[0:HEAD_DIM, c * TK:(c + 1) * TK] = blk.T.astype(jnp.bfloat16)
    row = lax.broadcasted_iota(jnp.int32, (BF16_SUBLANES, s_len), 0)
    dst_ref[HEAD_DIM:, :] = jnp.where(row == 0, 1.0, 0.0).astype(jnp.bfloat16)


def _silu(z):
    return z * jax.nn.sigmoid(z)


def _topk_active(q0):
    return q0 + TQ > TOP_N * SEL_LEN


def _nsa_selection_bias(p_sum, q0, s_len):
    n_sel = s_len // SEL_LEN
    n_cmp = s_len // CMP_STRIDE - CMP_LEN // CMP_STRIDE + 1
    oj = lax.broadcasted_iota(jnp.int32, (n_sel, HEAD_DIM), 0)
    oc = lax.broadcasted_iota(jnp.int32, (n_sel, HEAD_DIM), 1)
    ovl = jnp.where(oc * CMP_STRIDE < oj * SEL_LEN + SEL_LEN,
                    jnp.where(oc * CMP_STRIDE + CMP_LEN > oj * SEL_LEN, 1.0, 0.0), 0.0)
    ovl = jnp.where(oc < n_cmp, ovl, 0.0).astype(jnp.bfloat16)
    p_hi = p_sum.astype(jnp.bfloat16)
    p_lo = (p_sum - p_hi.astype(jnp.float32)).astype(jnp.bfloat16)
    imp = (jnp.dot(ovl, p_hi, preferred_element_type=jnp.float32)
           + jnp.dot(ovl, p_lo, preferred_element_type=jnp.float32))
    j_row = lax.broadcasted_iota(jnp.int32, (n_sel, TQ), 0)
    t_sel = q0 + lax.broadcasted_iota(jnp.int32, (n_sel, TQ), 1)
    cur = lax.shift_right_arithmetic(t_sel, int(math.log2(SEL_LEN)))
    forced = (j_row == 0) | (j_row == cur) | (j_row == cur - 1)
    imp = jnp.where(forced, imp + FORCE_BONUS, imp)
    imp = jnp.where(j_row * SEL_LEN <= t_sel, imp, -jnp.inf)
    rank = jnp.zeros((n_sel, TQ), jnp.float32)
    for i in range(n_sel):
        vi = imp[i:i + 1, :]
        rank = rank + jnp.where(j_row > i, jnp.where(vi >= imp, 1.0, 0.0),
                                jnp.where(vi > imp, 1.0, 0.0))
    return jnp.where(rank < TOP_N, 0.0, NEG)


def _mixers_kernel(q_ref, z_ref, g_ref, bg_ref, kc_ref, vc_ref, ksl_ref, vsl_ref, kw_ref, vw_ref,
                   cos_ref, sin_ref, qd_ref, zd_ref, kd_ref, vd_ref, dbias_ref, o_ref, od_ref,
                   ksl_rot, kw_rot, vsl_t, vw_t, ocmp_s, selb_s, gate_s):
    s_len = ksl_ref.shape[0]
    n_q = s_len // TQ

    ksl_rot[...] = _rope(ksl_ref[...].astype(jnp.float32), cos_ref[...], sin_ref[...]).astype(jnp.bfloat16)
    kw_rot[...] = _rope(kw_ref[...].astype(jnp.float32), cos_ref[...], sin_ref[...]).astype(jnp.bfloat16)
    _store_values_transposed(vsl_ref, vsl_t)
    _store_values_transposed(vw_ref, vw_t)
    kc = kc_ref[...]
    vc_t = vc_ref[...].astype(jnp.float32).T.astype(jnp.bfloat16)
    bg = bg_ref[...]
    kv_head = pl.program_id(1)

    c_row = lax.broadcasted_iota(jnp.int32, (HEAD_DIM, TQ), 0)
    lane_q = lax.broadcasted_iota(jnp.int32, (HEAD_DIM, TQ), 1)
    lane_one = lax.broadcasted_iota(jnp.int32, (1, TQ), 1)
    for qi in range(n_q):
        q0 = qi * TQ
        rows = slice(q0, q0 + TQ)
        q_raw = [(q_ref[rows, g * HEAD_DIM:(g + 1) * HEAD_DIM].astype(jnp.float32) * QSCALE
                  ).astype(jnp.bfloat16) for g in range(GROUP_A)]
        c_valid = c_row * CMP_STRIDE + (CMP_LEN - 1) <= q0 + lane_q
        any_valid = jnp.where(q0 + lane_one >= CMP_LEN - 1, 1.0, 0.0)
        sc4 = _kq(kc, jnp.concatenate(q_raw, axis=0))
        p_heads = []
        p_sum = jnp.zeros((HEAD_DIM, TQ), jnp.float32)
        for g in range(GROUP_A):
            sc = jnp.where(c_valid, sc4[:, g * TQ:(g + 1) * TQ], NEG)
            e = jnp.exp2(sc - jnp.max(sc, axis=0, keepdims=True))
            p = e * (any_valid / jnp.sum(e, axis=0, keepdims=True))
            p_heads.append(p.astype(jnp.bfloat16))
            p_sum = p_sum + p
        o_cmp4 = jnp.dot(vc_t, jnp.concatenate(p_heads, axis=1),
                         preferred_element_type=jnp.float32)
        for g in range(GROUP_A):
            ocmp_s[g, :, rows] = o_cmp4[:, g * TQ:(g + 1) * TQ]
        if _topk_active(q0):
            selb_s[qi] = _nsa_selection_bias(p_sum, q0, s_len)
        gate_s[:, rows] = jax.nn.sigmoid(g_ref[rows, :] + bg).T

    diag_rc = (lax.broadcasted_iota(jnp.int32, (TK, TQ), 0)
               - lax.broadcasted_iota(jnp.int32, (TK, TQ), 1))

    def slc_mask(qi, c, s):
        if c == qi:
            s = jnp.where(diag_rc <= 0, s, NEG)
        return s

    def slc_row_bias(qi, c):
        return [selb_s[qi, c * SEL_PER_TK + a:c * SEL_PER_TK + a + 1, :] for a in range(SEL_PER_TK)]

    def win_mask(qi, c, s):
        if c == qi:
            return jnp.where(diag_rc <= 0, s, NEG)
        if (qi - c) * TK + TQ - 1 > WIN - 1:
            return jnp.where(diag_rc >= (qi - c) * TK - (WIN - 1), s, NEG)
        return s

    ones_rows = jnp.where(lax.broadcasted_iota(jnp.int32, (BF16_SUBLANES, TK), 0) == 0,
                          1.0, 0.0).astype(jnp.bfloat16)

    def head_body(g, carry):
        col = pl.ds(pl.multiple_of(g * HEAD_DIM, HEAD_DIM), HEAD_DIM)
        gate_row = GATE_PER_KV * kv_head + 3 * g
        blocks = [slice(c * TK, (c + 1) * TK) for c in range(n_q)]

        def roped(ref, rows, scale):
            xf = ref[rows, col].astype(jnp.float32)
            if scale != 1.0:
                xf = xf * scale
            return _rope(xf, cos_ref[rows, :], sin_ref[rows, :]).astype(jnp.bfloat16)

        def dil_value_t(rows):
            vt = vd_ref[rows, col].astype(jnp.float32).T.astype(jnp.bfloat16)
            return jnp.concatenate([vt, ones_rows], axis=0)

        def dil_bias(delta, s):
            r0 = (n_q - 1 - delta) * TK
            return s + dbias_ref[r0:r0 + TK, :]

        dil_k = [_memo(functools.partial(roped, kd_ref, rows, 1.0)) for rows in blocks]
        dil_v = [_memo(functools.partial(dil_value_t, rows)) for rows in blocks]
        slc_jobs, win_jobs, dil_jobs = [], [], []
        for qi in range(n_q):
            q = _memo(functools.partial(roped, q_ref, blocks[qi], QSCALE))
            first_win = max(0, (qi * TQ - (WIN - 1)) // TK)
            ranked = _topk_active(qi * TQ)
            for jobs, k_ref_, v_ref_, mask, row_bias, c_lo in (
                    (slc_jobs, ksl_rot, vsl_t, slc_mask, slc_row_bias if ranked else None, 0),
                    (win_jobs, kw_rot, vw_t, win_mask, None, first_win)):
                jobs.append(_AttnJob(q, [
                    (functools.partial(lambda r, c: r[c * TK:(c + 1) * TK, :], k_ref_, c),
                     functools.partial(lambda r, c: r[:, c * TK:(c + 1) * TK], v_ref_, c),
                     functools.partial(mask, qi, c))
                    + ((functools.partial(row_bias, qi, c),) if row_bias else ())
                    for c in range(c_lo, qi + 1)]))
            dil_jobs.append(_AttnJob(
                _memo(functools.partial(roped, qd_ref, blocks[qi], QSCALE)),
                [(dil_k[c], dil_v[c], functools.partial(dil_bias, qi - c)) for c in range(qi + 1)]))

        def on_done(si, k, job):
            kind, qi = si % 3, MIX_SPLIT * k + si // 3
            rows = blocks[qi]
            if kind == 1:
                o_t = (gate_s[pl.ds(gate_row, 1), rows] * ocmp_s[g, :, rows]
                       + gate_s[pl.ds(gate_row + 1, 1), rows] * slc_jobs[qi].result()
                       + gate_s[pl.ds(gate_row + 2, 1), rows] * job.result())
                z = z_ref[rows, col].astype(jnp.float32)
                o_ref[rows, col] = (o_t.T * _silu(z)).astype(o_ref.dtype)
            elif kind == 2:
                z = zd_ref[rows, col].astype(jnp.float32)
                od_ref[rows, col] = (job.result().T * _silu(z)).astype(od_ref.dtype)

        streams = []
        for r in range(MIX_SPLIT):
            streams += [slc_jobs[r::MIX_SPLIT], win_jobs[r::MIX_SPLIT], dil_jobs[r::MIX_SPLIT]]
        _run_pipelined(streams, on_done)
        return carry

    lax.fori_loop(0, GROUP_A, head_body, 0)


def _mixers(front, back, gates, cmp_kv, b_gate_row, cos, sin, dil_bias, batch, s):
    gw = GROUP_A * HEAD_DIM
    assert N_HEADS_B == N_HEADS_A
    full = lambda col0: pl.BlockSpec((s, HEAD_DIM), lambda b, h: (b, col0 + h))
    group = lambda head0: pl.BlockSpec((s, gw), lambda b, h: (b, head0 // GROUP_A + h))
    out = jax.ShapeDtypeStruct((batch * s, N_HEADS_A * HEAD_DIM), jnp.bfloat16)
    rarely_changes = dict(pipeline_mode=pl.Buffered(1))
    return pl.pallas_call(
        _mixers_kernel,
        grid=(batch, N_KV_A),
        in_specs=[
            group(H_QA), group(H_ZA),
            pl.BlockSpec((s, HEAD_DIM), lambda b, h: (b, 0)),
            pl.BlockSpec((1, HEAD_DIM), lambda b, h: (0, 0)),
            pl.BlockSpec((None, None, s // CMP_STRIDE, HEAD_DIM), lambda b, h: (b, h, 0, 0)),
            pl.BlockSpec((None, None, s // CMP_STRIDE, HEAD_DIM), lambda b, h: (b, N_KV_A + h, 0, 0)),
            full(H_KSL), full(H_VSL), full(H_KW), full(H_VW),
            pl.BlockSpec((None, s, HEAD_DIM), lambda b, h: (b, 0, 0), **rarely_changes),
            pl.BlockSpec((None, s, HEAD_DIM), lambda b, h: (b, 0, 0), **rarely_changes),
            group(H_QB), group(H_ZB), group(H_KB), group(H_VB),
            pl.BlockSpec((s, TQ), lambda b, h: (0, 0), **rarely_changes),
        ],
        out_specs=[pl.BlockSpec((s, gw), lambda b, h: (b, h)),
                   pl.BlockSpec((s, gw), lambda b, h: (b, h))],
        out_shape=[out, out],
        scratch_shapes=[pltpu.VMEM((s, HEAD_DIM), jnp.bfloat16),
                        pltpu.VMEM((s, HEAD_DIM), jnp.bfloat16),
                        pltpu.VMEM((V_ROWS, s), jnp.bfloat16),
                        pltpu.VMEM((V_ROWS, s), jnp.bfloat16),
                        pltpu.VMEM((GROUP_A, HEAD_DIM, s), jnp.float32),
                        pltpu.VMEM((s // TQ, s // SEL_LEN, TQ), jnp.float32),
                        pltpu.VMEM((HEAD_DIM, s), jnp.float32)],
        compiler_params=pltpu.CompilerParams(dimension_semantics=("parallel", "parallel"),
                                             vmem_limit_bytes=MIXERS_VMEM_LIMIT),
        name="token_mixers",
    )(front, front, gates, b_gate_row, cmp_kv, cmp_kv, back, back, back, back, cos, sin,
      back, back, back, back, dil_bias)


def _dilated_multiplicity(dist):
    cnt = jnp.zeros(dist.shape, jnp.float32)
    for window, dil in DILATIONS:
        cnt = cnt + jnp.where((dist & (dil - 1)) == 0,
                              jnp.where(dist <= (window // dil) * dil, 1.0, 0.0), 0.0)
    return jnp.where(dist >= 0, cnt, 0.0)


def _dilated_table_kernel(bias_ref):
    s_len = bias_ref.shape[0]
    dist = (s_len - TQ + lax.broadcasted_iota(jnp.int32, (s_len, TQ), 1)
            - lax.broadcasted_iota(jnp.int32, (s_len, TQ), 0))
    cnt = _dilated_multiplicity(dist)
    bias_ref[...] = jnp.where(cnt > 0.5, jnp.log2(jnp.maximum(cnt, 1.0)), NEG)


def _dilated_tables(s):
    return pl.pallas_call(
        _dilated_table_kernel,
        out_shape=jax.ShapeDtypeStruct((s, TQ), jnp.float32),
        compiler_params=pltpu.CompilerParams(vmem_limit_bytes=VMEM_LIMIT),
        name="dilated_tables",
    )()


OUT_PANELS = 4


def _out_proj_kernel(ma_ref, mb_ref, wa_ref, wb_ref, x_ref, nw_ref, o_ref, y_s, ssq_s):
    i = pl.program_id(0)
    n_tiles = pl.num_programs(0) - 1
    d = o_ref.shape[1]
    pn = d // OUT_PANELS

    def step(multiply, finish):
        if finish:
            scale = lax.rsqrt(ssq_s[...] * (1.0 / d) + EPS)
        ssq = None
        for p in range(OUT_PANELS):
            cols = slice(p * pn, (p + 1) * pn)
            if finish:
                o_ref[:, cols] = x_ref[:, cols] + y_s[:, cols] * scale * nw_ref[:, cols]
            if multiply:
                y = (jnp.dot(ma_ref[...], wa_ref[:, cols], preferred_element_type=jnp.float32)
                     + jnp.dot(mb_ref[...], wb_ref[:, cols], preferred_element_type=jnp.float32))
                y_s[:, cols] = y
                part = jnp.sum(y * y, axis=-1, keepdims=True)
                ssq = part if ssq is None else ssq + part
        if multiply:
            ssq_s[...] = ssq

    pl.when(i == 0)(functools.partial(step, True, False))
    pl.when((i > 0) & (i < n_tiles))(functools.partial(step, True, True))
    pl.when(i == n_tiles)(functools.partial(step, False, True))


def _out_proj(mix_a, mix_b, w, x2d, post_w, tm=256):
    m, d = x2d.shape
    ka, kb = mix_a.shape[1], mix_b.shape[1]
    assert ka == kb and w.shape[0] == ka + kb
    n_tiles = m // tm
    resident = dict(pipeline_mode=pl.Buffered(1))
    this_tile = lambda i: (jnp.minimum(i, n_tiles - 1), 0)
    prev_tile = lambda i: (jnp.maximum(i - 1, 0), 0)
    return pl.pallas_call(
        _out_proj_kernel,
        grid=(n_tiles + 1,),
        in_specs=[
            pl.BlockSpec((tm, ka), this_tile),
            pl.BlockSpec((tm, kb), this_tile),
            pl.BlockSpec((ka, d), lambda i: (0, 0), **resident),
            pl.BlockSpec((kb, d), lambda i: (1, 0), **resident),
            pl.BlockSpec((tm, d), prev_tile),
            pl.BlockSpec((1, d), lambda i: (0, 0)),
        ],
        out_specs=pl.BlockSpec((tm, d), prev_tile),
        out_shape=jax.ShapeDtypeStruct((m, d), jnp.float32),
        scratch_shapes=[pltpu.VMEM((tm, d), jnp.float32),
                        pltpu.VMEM((tm, 1), jnp.float32)],
        compiler_params=pltpu.CompilerParams(dimension_semantics=("arbitrary",),
                                             vmem_limit_bytes=OUT_PROJ_VMEM_LIMIT),
        name="out_proj_norm_residual",
    )(mix_a, mix_b, w, w, x2d, post_w.reshape(1, d))


def _layer(x, positions, rope, pre_w, post_w, w_in, b_gate, pe_k, pe_v, wk1, wk2, wv1, wv2, w_out):
    b, s, d = x.shape
    x2d = x.reshape(b * s, d)
    w_t = w_in.T
    if rope is None:
        h, gates, cos, sin = _rmsnorm(x2d, pre_w, w_t, positions.reshape(b * s))
        rope = (cos.reshape(b, s, HEAD_DIM), sin.reshape(b, s, HEAD_DIM))
    else:
        h, gates = _rmsnorm(x2d, pre_w, w_t)
    cos, sin = rope
    front, back, w_out_bf16 = _in_proj(h, w_t, w_out)

    pe = jnp.stack([pe_k, pe_v])
    w1 = jnp.stack([wk1, wv1]).astype(jnp.bfloat16)
    w2 = jnp.stack([wk2, wv2]).astype(jnp.bfloat16)
    cmp_kv = _compress(back, b, s, pe, w1, w2)

    bg = jnp.pad(b_gate.reshape(1, -1), ((0, 0), (0, HEAD_DIM - b_gate.size)))
    mix_a, mix_b = _mixers(front, back, gates, cmp_kv, bg, cos, sin, _dilated_tables(s), b, s)

    out = _out_proj(mix_a, mix_b, w_out_bf16, x2d, post_w)
    return out.reshape(b, s, d), rope


def kernel(x, positions, pre_norm_w, post_norm_w, w_in, b_gate, cmp_pe_k, cmp_pe_v,
           cmp_wk1, cmp_wk2, cmp_wv1, cmp_wv2, w_out):
    rope = None
    for l in range(pre_norm_w.shape[0]):
        x, rope = _layer(x, positions, rope, pre_norm_w[l], post_norm_w[l], w_in[l], b_gate[l],
                         cmp_pe_k[l], cmp_pe_v[l], cmp_wk1[l], cmp_wk2[l], cmp_wv1[l],
                         cmp_wv2[l], w_out[l])
    return x
```

```python
import functools
import math

import jax
import jax.numpy as jnp
from jax import lax
from jax.experimental import pallas as pl
from jax.experimental.pallas import tpu as pltpu

HEAD_DIM = 128
HALF = HEAD_DIM // 2
N_HEADS_A = 16
N_KV_A = 4
GROUP_A = 4
N_HEADS_B = 16
CMP_LEN = 32
CMP_STRIDE = 16
CMP_HIDDEN = 256
SEL_LEN = 64
TOP_N = 16
WIN = 512
DILATIONS = ((128, 1), (512, 4), (2048, 16))
ROPE_THETA = 10000.0
EPS = 1e-6
FORCE_BONUS = 1e4
NEG = -1e30
SCALE = HEAD_DIM ** -0.5
QSCALE = SCALE * math.log2(math.e)

H_QA = 0
H_ZA = H_QA + N_HEADS_A
N_FRONT_HEADS = H_ZA + N_HEADS_A
GATE_PER_KV = 3 * GROUP_A
H_KC = 0
H_VC = H_KC + N_KV_A
H_KSL = H_VC + N_KV_A
H_VSL = H_KSL + N_KV_A
H_KW = H_VSL + N_KV_A
H_VW = H_KW + N_KV_A
H_QB = H_VW + N_KV_A
H_KB = H_QB + N_HEADS_B
H_VB = H_KB + N_HEADS_B
H_ZB = H_VB + N_HEADS_B
N_BACK_HEADS = H_ZB + N_HEADS_B

TQ = 256
TK = 256
SEL_PER_TK = TK // SEL_LEN
F32_SUBLANES = 8
BF16_SUBLANES = 16
V_ROWS = HEAD_DIM + BF16_SUBLANES
PV_GROUP = 2
MIX_SPLIT = 2
VMEM_LIMIT = 56 * 1024 * 1024
MIXERS_VMEM_LIMIT = 60 * 1024 * 1024
OUT_PROJ_VMEM_LIMIT = 60 * 1024 * 1024


def _cparams(sem):
    return pltpu.CompilerParams(dimension_semantics=sem, vmem_limit_bytes=VMEM_LIMIT)


GATE_ROW0 = 2 * N_HEADS_A * HEAD_DIM
BACK_ROW0 = GATE_ROW0 + 3 * N_HEADS_A


def _nt_dot(x, w):
    return lax.dot_general(x, w, (((1,), (1,)), ((), ())), preferred_element_type=jnp.float32)


def _rmsnorm_kernel(*refs, with_rope):
    if with_rope:
        x_ref, w_ref, wg_ref, pos_ref, inv_ref, o_ref, g_ref, cos_ref, sin_ref = refs
    else:
        x_ref, w_ref, wg_ref, o_ref, g_ref = refs
    x = x_ref[...]
    ms = jnp.mean(x * x, axis=-1, keepdims=True)
    h = (x * lax.rsqrt(ms + EPS) * w_ref[...]).astype(o_ref.dtype)
    o_ref[...] = h
    g_ref[...] = _nt_dot(h, wg_ref[...].astype(jnp.bfloat16))
    if with_rope:
        ang = pos_ref[...].astype(jnp.float32) * inv_ref[...]
        lane = lax.broadcasted_iota(jnp.int32, ang.shape, 1)
        sin = jnp.sin(ang)
        cos_ref[...] = jnp.cos(ang)
        sin_ref[...] = jnp.where(lane < HALF, -sin, sin)


def _rmsnorm(x2d, w, w_t, positions=None, tm=512):
    m, d = x2d.shape
    with_rope = positions is not None
    rows = lambda width: pl.BlockSpec((tm, width), lambda i: (i, 0))
    in_specs = [rows(d), pl.BlockSpec((1, d), lambda i: (0, 0)),
                pl.BlockSpec((HEAD_DIM, d), lambda i: (GATE_ROW0 // HEAD_DIM, 0))]
    out_specs = [rows(d), rows(HEAD_DIM)]
    out_shape = [jax.ShapeDtypeStruct((m, d), jnp.bfloat16),
                 jax.ShapeDtypeStruct((m, HEAD_DIM), jnp.float32)]
    args = [x2d, w.reshape(1, d), w_t]
    if with_rope:
        inv = ROPE_THETA ** (-jnp.arange(HALF, dtype=jnp.float32) / HALF)
        in_specs += [rows(1), pl.BlockSpec((1, HEAD_DIM), lambda i: (0, 0))]
        out_specs += [rows(HEAD_DIM), rows(HEAD_DIM)]
        out_shape += [jax.ShapeDtypeStruct((m, HEAD_DIM), jnp.float32)] * 2
        args += [positions.reshape(m, 1), jnp.concatenate([inv, inv]).reshape(1, HEAD_DIM)]
    return pl.pallas_call(
        functools.partial(_rmsnorm_kernel, with_rope=with_rope),
        grid=(m // tm,),
        in_specs=in_specs,
        out_specs=out_specs,
        out_shape=out_shape,
        compiler_params=_cparams(("parallel",)),
        name="pre_rmsnorm_gates",
    )(*args)


PROJ_TN = GROUP_A * HEAD_DIM
PROJ_FRONT = GATE_ROW0 // PROJ_TN


WOUT_CAST_ROWS = 64


def _in_proj_kernel(x_ref, w_ref, wo_ref, front_ref, back_ref, wo_bf16_ref, *, wo_blocks):
    j = pl.program_id(1)
    t = pl.program_id(0) * pl.num_programs(1) + j

    @pl.when(t < wo_blocks)
    def _():
        wo_bf16_ref[...] = wo_ref[...].astype(wo_bf16_ref.dtype)

    @pl.when(j < PROJ_FRONT)
    def _():
        front_ref[...] = _nt_dot(x_ref[...], w_ref[...].astype(jnp.bfloat16)).astype(front_ref.dtype)

    @pl.when(j >= PROJ_FRONT)
    def _():
        back_ref[...] = _nt_dot(x_ref[...], w_ref[...].astype(jnp.bfloat16)).astype(back_ref.dtype)


def _in_proj(h, w_t, w_out, tm=2048):
    m, k = h.shape
    n_back = N_BACK_HEADS * HEAD_DIM
    n_steps = PROJ_FRONT + n_back // PROJ_TN
    wo_blocks = w_out.shape[0] // WOUT_CAST_ROWS
    assert wo_blocks <= (m // tm) * n_steps

    def wo_block(i, j):
        return jnp.minimum(i * n_steps + j, wo_blocks - 1), 0

    def w_row(i, j):
        row = PROJ_TN * j + jnp.where(j >= PROJ_FRONT, BACK_ROW0 - GATE_ROW0, 0)
        return pl.multiple_of(row, F32_SUBLANES), 0

    wo_spec = pl.BlockSpec((WOUT_CAST_ROWS, w_out.shape[1]), wo_block)
    return pl.pallas_call(
        functools.partial(_in_proj_kernel, wo_blocks=wo_blocks),
        grid=(m // tm, n_steps),
        in_specs=[
            pl.BlockSpec((tm, k), lambda i, j: (i, 0), pipeline_mode=pl.Buffered(1)),
            pl.BlockSpec((pl.Element(PROJ_TN), pl.Element(k)), w_row),
            wo_spec],
        out_specs=[
            pl.BlockSpec((tm, PROJ_TN), lambda i, j: (i, jnp.minimum(j, PROJ_FRONT - 1))),
            pl.BlockSpec((tm, PROJ_TN), lambda i, j: (i, jnp.maximum(j - PROJ_FRONT, 0))),
            wo_spec],
        out_shape=[jax.ShapeDtypeStruct((m, N_FRONT_HEADS * HEAD_DIM), jnp.bfloat16),
                   jax.ShapeDtypeStruct((m, n_back), jnp.bfloat16),
                   jax.ShapeDtypeStruct(w_out.shape, jnp.bfloat16)],
        compiler_params=_cparams(("arbitrary", "arbitrary")),
        name="in_proj",
    )(h, w_t, w_out)


def _rope(xf, cos, sin_signed):
    return xf * cos + pltpu.roll(xf, HALF, 1) * sin_signed


CMP_PITCH = CMP_STRIDE + 4


def _compress_kernel(x_ref, pe_ref, w1_ref, w2_ref, o_ref, xs_ref):
    s = x_ref.shape[0]
    n_rows = s // CMP_STRIDE
    for hh in range(N_KV_A):
        cols = slice(hh * HEAD_DIM, (hh + 1) * HEAD_DIM)
        for g in range(n_rows):
            xs_ref[hh, pl.ds(g * CMP_PITCH, CMP_STRIDE), :] = (
                x_ref[pl.ds(g * CMP_STRIDE, CMP_STRIDE), cols].astype(jnp.float32))
        xs_ref[hh, pl.ds(n_rows * CMP_PITCH, CMP_STRIDE), :] = jnp.zeros(
            (CMP_STRIDE, HEAD_DIM), jnp.float32)

    def block_rows(hh, r):
        start = (r // CMP_STRIDE) * CMP_PITCH + r % CMP_STRIDE
        return xs_ref[hh, pl.ds(start, n_rows, stride=CMP_PITCH), :]

    flat = jnp.concatenate(
        [jnp.concatenate([(block_rows(hh, r) + pe_ref[pl.ds(r, 1), :]).astype(jnp.bfloat16)
                          for r in range(CMP_LEN)], axis=1)
         for hh in range(N_KV_A)], axis=0)
    hid = jax.nn.gelu(jnp.dot(flat, w1_ref[...], preferred_element_type=jnp.float32))
    out = jnp.dot(hid.astype(jnp.bfloat16), w2_ref[...], preferred_element_type=jnp.float32)
    for hh in range(N_KV_A):
        o_ref[hh] = out[hh * n_rows:(hh + 1) * n_rows, :].astype(o_ref.dtype)


def _compress(proj, batch, s, pe, w1, w2):
    n_rows = s // CMP_STRIDE
    assert H_KC % N_KV_A == 0 and H_VC == H_KC + N_KV_A
    return pl.pallas_call(
        _compress_kernel,
        grid=(batch, 2),
        in_specs=[
            pl.BlockSpec((s, N_KV_A * HEAD_DIM), lambda b, kv: (b, H_KC // N_KV_A + kv)),
            pl.BlockSpec((None, CMP_LEN, HEAD_DIM), lambda b, kv: (kv, 0, 0)),
            pl.BlockSpec((None, CMP_LEN * HEAD_DIM, CMP_HIDDEN), lambda b, kv: (kv, 0, 0)),
            pl.BlockSpec((None, CMP_HIDDEN, HEAD_DIM), lambda b, kv: (kv, 0, 0)),
        ],
        out_specs=pl.BlockSpec((None, N_KV_A, n_rows, HEAD_DIM), lambda b, kv: (b, kv, 0, 0)),
        out_shape=jax.ShapeDtypeStruct((batch, 2 * N_KV_A, n_rows, HEAD_DIM), jnp.bfloat16),
        scratch_shapes=[pltpu.VMEM((N_KV_A, (n_rows + 1) * CMP_PITCH, HEAD_DIM), jnp.float32)],
        compiler_params=_cparams(("parallel", "parallel")),
        name="compress",
    )(proj, pe, w1, w2)


def _kq(k, q):
    return lax.dot_general(k, q, (((1,), (1,)), ((), ())),
                           preferred_element_type=jnp.float32)


def _memo(fn):
    cache = []

    def get():
        if not cache:
            cache.append(fn())
        return cache[0]
    return get


class _AttnJob:
    def __init__(self, q_fn, chunks):
        self.q_fn, self.chunks = q_fn, chunks
        self.s, self.bias, self.m, self.acc = [], [], None, None

    def _score(self, c):
        k_fn, _, mask_fn = self.chunks[c][:3]
        s = mask_fn(_kq(k_fn(), self.q_fn()))
        bias = self.chunks[c][3]() if len(self.chunks[c]) > 3 else None
        self.s.append(s)
        self.bias.append(bias)
        if bias is None:
            cm = jnp.max(s, axis=0, keepdims=True)
        else:
            sub = s.shape[0] // len(bias)
            cm = functools.reduce(jnp.maximum, [
                jnp.max(s[a * sub:(a + 1) * sub, :], axis=0, keepdims=True) + b
                for a, b in enumerate(bias)])
        self.m = cm if self.m is None else jnp.maximum(self.m, cm)

    def _probabilities(self, c):
        s, bias = self.s[c], self.bias[c]
        if bias is None:
            return jnp.exp2(s - self.m).astype(jnp.bfloat16)
        sub = s.shape[0] // len(bias)
        return jnp.concatenate([jnp.exp2(s[a * sub:(a + 1) * sub, :] + (b - self.m))
                                for a, b in enumerate(bias)], axis=0).astype(jnp.bfloat16)

    def _value(self, cs):
        p = jnp.concatenate([self._probabilities(c) for c in cs], axis=0)
        vt = jnp.concatenate([self.chunks[c][1]() for c in cs], axis=1)
        part = jnp.dot(vt, p, preferred_element_type=jnp.float32)
        self.acc = part if self.acc is None else self.acc + part

    def score_tasks(self):
        return [functools.partial(self._score, c) for c in range(len(self.chunks))]

    def value_tasks(self):
        n = len(self.chunks)
        return [functools.partial(self._value, range(c, min(c + PV_GROUP, n)))
                for c in range(0, n, PV_GROUP)]

    def result(self):
        return self.acc[:HEAD_DIM, :] * (1.0 / self.acc[HEAD_DIM:HEAD_DIM + 1, :])


def _round_robin(task_lists):
    for i in range(max(len(t) for t in task_lists)):
        for tasks in task_lists:
            if i < len(tasks):
                tasks[i]()


def _run_pipelined(streams, on_done):
    n = len(streams[0])
    _round_robin([jobs[0].score_tasks() for jobs in streams])
    for k in range(n):
        lists = []
        for jobs in streams:
            lists.append(jobs[k].value_tasks())
            if k + 1 < n:
                lists.append(jobs[k + 1].score_tasks())
        _round_robin(lists)
        for si, jobs in enumerate(streams):
            on_done(si, k, jobs[k])


def _store_values_transposed(src_ref, dst_ref):
    s_len = src_ref.shape[0]
    for c in range(s_len // TK):
        blk = src_ref[c * TK:(c + 1) * TK, :].astype(jnp.float32)
        dst_ref[0:HEAD_DIM, c * TK:(c + 1) * TK] = blk.T.astype(jnp.bfloat16)
    row = lax.broadcasted_iota(jnp.int32, (BF16_SUBLANES, s_len), 0)
    dst_ref[HEAD_DIM:, :] = jnp.where(row == 0, 1.0, 0.0).astype(jnp.bfloat16)


def _silu(z):
    return z * jax.nn.sigmoid(z)


def _topk_active(q0):
    return q0 + TQ > TOP_N * SEL_LEN


def _nsa_selection_bias(p_sum, q0, s_len):
    n_sel = s_len // SEL_LEN
    n_cmp = s_len // CMP_STRIDE - CMP_LEN // CMP_STRIDE + 1
    oj = lax.broadcasted_iota(jnp.int32, (n_sel, HEAD_DIM), 0)
    oc = lax.broadcasted_iota(jnp.int32, (n_sel, HEAD_DIM), 1)
    ovl = jnp.where(oc * CMP_STRIDE < oj * SEL_LEN + SEL_LEN,
                    jnp.where(oc * CMP_STRIDE + CMP_LEN > oj * SEL_LEN, 1.0, 0.0), 0.0)
    ovl = jnp.where(oc < n_cmp, ovl, 0.0).astype(jnp.bfloat16)
    p_hi = p_sum.astype(jnp.bfloat16)
    p_lo = (p_sum - p_hi.astype(jnp.float32)).astype(jnp.bfloat16)
    imp = (jnp.dot(ovl, p_hi, preferred_element_type=jnp.float32)
           + jnp.dot(ovl, p_lo, preferred_element_type=jnp.float32))
    j_row = lax.broadcasted_iota(jnp.int32, (n_sel, TQ), 0)
    t_sel = q0 + lax.broadcasted_iota(jnp.int32, (n_sel, TQ), 1)
    cur = lax.shift_right_arithmetic(t_sel, int(math.log2(SEL_LEN)))
    forced = (j_row == 0) | (j_row == cur) | (j_row == cur - 1)
    imp = jnp.where(forced, imp + FORCE_BONUS, imp)
    imp = jnp.where(j_row * SEL_LEN <= t_sel, imp, -jnp.inf)
    rank = jnp.zeros((n_sel, TQ), jnp.float32)
    for i in range(n_sel):
        vi = imp[i:i + 1, :]
        rank = rank + jnp.where(j_row > i, jnp.where(vi >= imp, 1.0, 0.0),
                                jnp.where(vi > imp, 1.0, 0.0))
    return jnp.where(rank < TOP_N, 0.0, NEG)


def _mixers_kernel(q_ref, z_ref, g_ref, bg_ref, kc_ref, vc_ref, ksl_ref, vsl_ref, kw_ref, vw_ref,
                   cos_ref, sin_ref, qd_ref, zd_ref, kd_ref, vd_ref, dbias_ref, o_ref, od_ref,
                   ksl_rot, kw_rot, vsl_t, vw_t, ocmp_s, selb_s, gate_s):
    s_len = ksl_ref.shape[0]
    n_q = s_len // TQ

    ksl_rot[...] = _rope(ksl_ref[...].astype(jnp.float32), cos_ref[...], sin_ref[...]).astype(jnp.bfloat16)
    kw_rot[...] = _rope(kw_ref[...].astype(jnp.float32), cos_ref[...], sin_ref[...]).astype(jnp.bfloat16)
    _store_values_transposed(vsl_ref, vsl_t)
    _store_values_transposed(vw_ref, vw_t)
    kc = (kc_ref[...].astype(jnp.float32) * QSCALE).astype(jnp.bfloat16)
    vc_t = vc_ref[...].astype(jnp.float32).T.astype(jnp.bfloat16)
    bg = bg_ref[...]
    kv_head = pl.program_id(1)

    c_row = lax.broadcasted_iota(jnp.int32, (HEAD_DIM, TQ), 0)
    lane_q = lax.broadcasted_iota(jnp.int32, (HEAD_DIM, TQ), 1)
    lane_one = lax.broadcasted_iota(jnp.int32, (1, TQ), 1)
    for qi in range(n_q):
        q0 = qi * TQ
        rows = slice(q0, q0 + TQ)
        n_vis = min(HEAD_DIM, (q0 + TQ) // CMP_STRIDE)
        hidden = jnp.zeros((HEAD_DIM - n_vis, TQ), jnp.float32)
        c_valid = (c_row * CMP_STRIDE + (CMP_LEN - 1) <= q0 + lane_q)[:n_vis, :]
        any_valid = jnp.where(q0 + lane_one >= CMP_LEN - 1, 1.0, 0.0)
        q4 = jnp.concatenate([q_ref[rows, g * HEAD_DIM:(g + 1) * HEAD_DIM]
                              for g in range(GROUP_A)], axis=0)
        sc4 = _kq(kc[:n_vis, :], q4)
        p_heads = []
        p_sum = jnp.zeros((HEAD_DIM, TQ), jnp.float32)
        for g in range(GROUP_A):
            sc = jnp.where(c_valid, sc4[:, g * TQ:(g + 1) * TQ], NEG)
            e = jnp.exp2(sc - jnp.max(sc, axis=0, keepdims=True))
            p = e * (any_valid / jnp.sum(e, axis=0, keepdims=True))
            if n_vis < HEAD_DIM:
                p = jnp.concatenate([p, hidden], axis=0)
            p_heads.append(p.astype(jnp.bfloat16))
            p_sum = p_sum + p
        o_cmp4 = jnp.dot(vc_t, jnp.concatenate(p_heads, axis=1),
                         preferred_element_type=jnp.float32)
        for g in range(GROUP_A):
            ocmp_s[g, :, rows] = o_cmp4[:, g * TQ:(g + 1) * TQ]
        if _topk_active(q0):
            selb_s[qi] = _nsa_selection_bias(p_sum, q0, s_len)
        gate_s[:, rows] = jax.nn.sigmoid(g_ref[rows, :] + bg).T

    diag_rc = (lax.broadcasted_iota(jnp.int32, (TK, TQ), 0)
               - lax.broadcasted_iota(jnp.int32, (TK, TQ), 1))

    def slc_mask(qi, c, s):
        if c == qi:
            s = jnp.where(diag_rc <= 0, s, NEG)
        return s

    def slc_row_bias(qi, c):
        return [selb_s[qi, c * SEL_PER_TK + a:c * SEL_PER_TK + a + 1, :] for a in range(SEL_PER_TK)]

    def win_mask(qi, c, s):
        if c == qi:
            return jnp.where(diag_rc <= 0, s, NEG)
        if (qi - c) * TK + TQ - 1 > WIN - 1:
            return jnp.where(diag_rc >= (qi - c) * TK - (WIN - 1), s, NEG)
        return s

    ones_rows = jnp.where(lax.broadcasted_iota(jnp.int32, (BF16_SUBLANES, TK), 0) == 0,
                          1.0, 0.0).astype(jnp.bfloat16)

    def head_body(g, carry):
        col = pl.ds(pl.multiple_of(g * HEAD_DIM, HEAD_DIM), HEAD_DIM)
        gate_row = GATE_PER_KV * kv_head + 3 * g
        blocks = [slice(c * TK, (c + 1) * TK) for c in range(n_q)]

        def roped(ref, rows, scale):
            xf = ref[rows, col].astype(jnp.float32)
            if scale != 1.0:
                xf = xf * scale
            return _rope(xf, cos_ref[rows, :], sin_ref[rows, :]).astype(jnp.bfloat16)

        def dil_value_t(rows):
            vt = vd_ref[rows, col].astype(jnp.float32).T.astype(jnp.bfloat16)
            return jnp.concatenate([vt, ones_rows], axis=0)

        def dil_bias(delta, s):
            r0 = (n_q - 1 - delta) * TK
            return s + dbias_ref[r0:r0 + TK, :]

        dil_k = [_memo(functools.partial(roped, kd_ref, rows, 1.0)) for rows in blocks]
        dil_v = [_memo(functools.partial(dil_value_t, rows)) for rows in blocks]
        slc_jobs, win_jobs, dil_jobs = [], [], []
        for qi in range(n_q):
            q = _memo(functools.partial(roped, q_ref, blocks[qi], QSCALE))
            first_win = max(0, (qi * TQ - (WIN - 1)) // TK)
            ranked = _topk_active(qi * TQ)
            for jobs, k_ref_, v_ref_, mask, row_bias, c_lo in (
                    (slc_jobs, ksl_rot, vsl_t, slc_mask, slc_row_bias if ranked else None, 0),
                    (win_jobs, kw_rot, vw_t, win_mask, None, first_win)):
                jobs.append(_AttnJob(q, [
                    (functools.partial(lambda r, c: r[c * TK:(c + 1) * TK, :], k_ref_, c),
                     functools.partial(lambda r, c: r[:, c * TK:(c + 1) * TK], v_ref_, c),
                     functools.partial(mask, qi, c))
                    + ((functools.partial(row_bias, qi, c),) if row_bias else ())
                    for c in range(c_lo, qi + 1)]))
            dil_jobs.append(_AttnJob(
                _memo(functools.partial(roped, qd_ref, blocks[qi], QSCALE)),
                [(dil_k[c], dil_v[c], functools.partial(dil_bias, qi - c)) for c in range(qi + 1)]))

        def on_done(si, k, job):
            kind, qi = si % 3, MIX_SPLIT * k + si // 3
            rows = blocks[qi]
            if kind == 1:
                o_t = (gate_s[pl.ds(gate_row, 1), rows] * ocmp_s[g, :, rows]
                       + gate_s[pl.ds(gate_row + 1, 1), rows] * slc_jobs[qi].result()
                       + gate_s[pl.ds(gate_row + 2, 1), rows] * job.result())
                z = z_ref[rows, col].astype(jnp.float32)
                o_ref[rows, col] = (o_t.T * _silu(z)).astype(o_ref.dtype)
            elif kind == 2:
                z = zd_ref[rows, col].astype(jnp.float32)
                od_ref[rows, col] = (job.result().T * _silu(z)).astype(od_ref.dtype)

        streams = []
        for r in range(MIX_SPLIT):
            streams += [slc_jobs[r::MIX_SPLIT], win_jobs[r::MIX_SPLIT], dil_jobs[r::MIX_SPLIT]]
        _run_pipelined(streams, on_done)
        return carry

    lax.fori_loop(0, GROUP_A, head_body, 0)


def _mixers(front, back, gates, cmp_kv, b_gate_row, cos, sin, dil_bias, batch, s):
    gw = GROUP_A * HEAD_DIM
    assert N_HEADS_B == N_HEADS_A
    full = lambda col0: pl.BlockSpec((s, HEAD_DIM), lambda b, h: (b, col0 + h))
    group = lambda head0: pl.BlockSpec((s, gw), lambda b, h: (b, head0 // GROUP_A + h))
    out = jax.ShapeDtypeStruct((batch * s, N_HEADS_A * HEAD_DIM), jnp.bfloat16)
    rarely_changes = dict(pipeline_mode=pl.Buffered(1))
    return pl.pallas_call(
        _mixers_kernel,
        grid=(batch, N_KV_A),
        in_specs=[
            group(H_QA), group(H_ZA),
            pl.BlockSpec((s, HEAD_DIM), lambda b, h: (b, 0)),
            pl.BlockSpec((1, HEAD_DIM), lambda b, h: (0, 0)),
            pl.BlockSpec((None, None, s // CMP_STRIDE, HEAD_DIM), lambda b, h: (b, h, 0, 0)),
            pl.BlockSpec((None, None, s // CMP_STRIDE, HEAD_DIM), lambda b, h: (b, N_KV_A + h, 0, 0)),
            full(H_KSL), full(H_VSL), full(H_KW), full(H_VW),
            pl.BlockSpec((None, s, HEAD_DIM), lambda b, h: (b, 0, 0), **rarely_changes),
            pl.BlockSpec((None, s, HEAD_DIM), lambda b, h: (b, 0, 0), **rarely_changes),
            group(H_QB), group(H_ZB), group(H_KB), group(H_VB),
            pl.BlockSpec((s, TQ), lambda b, h: (0, 0), **rarely_changes),
        ],
        out_specs=[pl.BlockSpec((s, gw), lambda b, h: (b, h)),
                   pl.BlockSpec((s, gw), lambda b, h: (b, h))],
        out_shape=[out, out],
        scratch_shapes=[pltpu.VMEM((s, HEAD_DIM), jnp.bfloat16),
                        pltpu.VMEM((s, HEAD_DIM), jnp.bfloat16),
                        pltpu.VMEM((V_ROWS, s), jnp.bfloat16),
                        pltpu.VMEM((V_ROWS, s), jnp.bfloat16),
                        pltpu.VMEM((GROUP_A, HEAD_DIM, s), jnp.float32),
                        pltpu.VMEM((s // TQ, s // SEL_LEN, TQ), jnp.float32),
                        pltpu.VMEM((HEAD_DIM, s), jnp.float32)],
        compiler_params=pltpu.CompilerParams(dimension_semantics=("parallel", "parallel"),
                                             vmem_limit_bytes=MIXERS_VMEM_LIMIT),
        name="token_mixers",
    )(front, front, gates, b_gate_row, cmp_kv, cmp_kv, back, back, back, back, cos, sin,
      back, back, back, back, dil_bias)


def _dilated_multiplicity(dist):
    cnt = jnp.zeros(dist.shape, jnp.float32)
    for window, dil in DILATIONS:
        cnt = cnt + jnp.where((dist & (dil - 1)) == 0,
                              jnp.where(dist <= (window // dil) * dil, 1.0, 0.0), 0.0)
    return jnp.where(dist >= 0, cnt, 0.0)


def _dilated_table_kernel(bias_ref):
    s_len = bias_ref.shape[0]
    dist = (s_len - TQ + lax.broadcasted_iota(jnp.int32, (s_len, TQ), 1)
            - lax.broadcasted_iota(jnp.int32, (s_len, TQ), 0))
    cnt = _dilated_multiplicity(dist)
    bias_ref[...] = jnp.where(cnt > 0.5, jnp.log2(jnp.maximum(cnt, 1.0)), NEG)


def _dilated_tables(s):
    return pl.pallas_call(
        _dilated_table_kernel,
        out_shape=jax.ShapeDtypeStruct((s, TQ), jnp.float32),
        compiler_params=pltpu.CompilerParams(vmem_limit_bytes=VMEM_LIMIT),
        name="dilated_tables",
    )()


OUT_PANELS = 4


def _out_proj_kernel(ma_ref, mb_ref, wa_ref, wb_ref, x_ref, nw_ref, o_ref, y_s, ssq_s):
    i = pl.program_id(0)
    n_tiles = pl.num_programs(0) - 1
    d = o_ref.shape[1]
    pn = d // OUT_PANELS

    def step(multiply, finish):
        if finish:
            scale = lax.rsqrt(ssq_s[...] * (1.0 / d) + EPS)
        ssq = None
        for p in range(OUT_PANELS):
            cols = slice(p * pn, (p + 1) * pn)
            if finish:
                o_ref[:, cols] = x_ref[:, cols] + y_s[:, cols] * scale * nw_ref[:, cols]
            if multiply:
                y = (jnp.dot(ma_ref[...], wa_ref[:, cols], preferred_element_type=jnp.float32)
                     + jnp.dot(mb_ref[...], wb_ref[:, cols], preferred_element_type=jnp.float32))
                y_s[:, cols] = y
                part = jnp.sum(y * y, axis=-1, keepdims=True)
                ssq = part if ssq is None else ssq + part
        if multiply:
            ssq_s[...] = ssq

    pl.when(i == 0)(functools.partial(step, True, False))
    pl.when((i > 0) & (i < n_tiles))(functools.partial(step, True, True))
    pl.when(i == n_tiles)(functools.partial(step, False, True))


def _out_proj(mix_a, mix_b, w, x2d, post_w, tm=256):
    m, d = x2d.shape
    ka, kb = mix_a.shape[1], mix_b.shape[1]
    assert ka == kb and w.shape[0] == ka + kb
    n_tiles = m // tm
    resident = dict(pipeline_mode=pl.Buffered(1))
    this_tile = lambda i: (jnp.minimum(i, n_tiles - 1), 0)
    prev_tile = lambda i: (jnp.maximum(i - 1, 0), 0)
    return pl.pallas_call(
        _out_proj_kernel,
        grid=(n_tiles + 1,),
        in_specs=[
            pl.BlockSpec((tm, ka), this_tile),
            pl.BlockSpec((tm, kb), this_tile),
            pl.BlockSpec((ka, d), lambda i: (0, 0), **resident),
            pl.BlockSpec((kb, d), lambda i: (1, 0), **resident),
            pl.BlockSpec((tm, d), prev_tile),
            pl.BlockSpec((1, d), lambda i: (0, 0)),
        ],
        out_specs=pl.BlockSpec((tm, d), prev_tile),
        out_shape=jax.ShapeDtypeStruct((m, d), jnp.float32),
        scratch_shapes=[pltpu.VMEM((tm, d), jnp.float32),
                        pltpu.VMEM((tm, 1), jnp.float32)],
        compiler_params=pltpu.CompilerParams(dimension_semantics=("arbitrary",),
                                             vmem_limit_bytes=OUT_PROJ_VMEM_LIMIT),
        name="out_proj_norm_residual",
    )(mix_a, mix_b, w, w, x2d, post_w.reshape(1, d))


def _layer(x, positions, rope, pre_w, post_w, w_in, b_gate, pe_k, pe_v, wk1, wk2, wv1, wv2, w_out):
    b, s, d = x.shape
    x2d = x.reshape(b * s, d)
    w_t = w_in.T
    if rope is None:
        h, gates, cos, sin = _rmsnorm(x2d, pre_w, w_t, positions.reshape(b * s))
        rope = (cos.reshape(b, s, HEAD_DIM), sin.reshape(b, s, HEAD_DIM))
    else:
        h, gates = _rmsnorm(x2d, pre_w, w_t)
    cos, sin = rope
    front, back, w_out_bf16 = _in_proj(h, w_t, w_out)

    pe = jnp.stack([pe_k, pe_v])
    w1 = jnp.stack([wk1, wv1]).astype(jnp.bfloat16)
    w2 = jnp.stack([wk2, wv2]).astype(jnp.bfloat16)
    cmp_kv = _compress(back, b, s, pe, w1, w2)

    bg = jnp.pad(b_gate.reshape(1, -1), ((0, 0), (0, HEAD_DIM - b_gate.size)))
    mix_a, mix_b = _mixers(front, back, gates, cmp_kv, bg, cos, sin, _dilated_tables(s), b, s)

    out = _out_proj(mix_a, mix_b, w_out_bf16, x2d, post_w)
    return out.reshape(b, s, d), rope


def kernel(x, positions, pre_norm_w, post_norm_w, w_in, b_gate, cmp_pe_k, cmp_pe_v,
           cmp_wk1, cmp_wk2, cmp_wv1, cmp_wv2, w_out):
    rope = None
    for l in range(pre_norm_w.shape[0]):
        x, rope = _layer(x, positions, rope, pre_norm_w[l], post_norm_w[l], w_in[l], b_gate[l],
                         cmp_pe_k[l], cmp_pe_v[l], cmp_wk1[l], cmp_wk2[l], cmp_wv1[l],
                         cmp_wv2[l], w_out[l])
    return x
```

```python
import functools
import math

import jax
import jax.numpy as jnp
from jax import lax
from jax.experimental import pallas as pl
from jax.experimental.pallas import tpu as pltpu

HEAD_DIM = 128
HALF = HEAD_DIM // 2
N_HEADS_A = 16
N_KV_A = 4
GROUP_A = 4
N_HEADS_B = 16
CMP_LEN = 32
CMP_STRIDE = 16
CMP_HIDDEN = 256
SEL_LEN = 64
TOP_N = 16
WIN = 512
DILATIONS = ((128, 1), (512, 4), (2048, 16))
ROPE_THETA = 10000.0
EPS = 1e-6
FORCE_BONUS = 1e4
NEG = -1e30
SCALE = HEAD_DIM ** -0.5
QSCALE = SCALE * math.log2(math.e)

H_QA = 0
H_ZA = H_QA + N_HEADS_A
N_FRONT_HEADS = H_ZA + N_HEADS_A
GATE_PER_KV = 3 * GROUP_A
H_KC = 0
H_VC = H_KC + N_KV_A
H_KSL = H_VC + N_KV_A
H_VSL = H_KSL + N_KV_A
H_KW = H_VSL + N_KV_A
H_VW = H_KW + N_KV_A
H_QB = H_VW + N_KV_A
H_KB = H_QB + N_HEADS_B
H_VB = H_KB + N_HEADS_B
H_ZB = H_VB + N_HEADS_B
N_BACK_HEADS = H_ZB + N_HEADS_B

TQ = 256
TK = 256
SEL_PER_TK = TK // SEL_LEN
F32_SUBLANES = 8
BF16_SUBLANES = 16
V_ROWS = HEAD_DIM + BF16_SUBLANES
PV_GROUP = 2
MIX_SPLIT = 2
VMEM_LIMIT = 56 * 1024 * 1024
MIXERS_VMEM_LIMIT = 60 * 1024 * 1024
OUT_PROJ_VMEM_LIMIT = 60 * 1024 * 1024


def _cparams(sem):
    return pltpu.CompilerParams(dimension_semantics=sem, vmem_limit_bytes=VMEM_LIMIT)


GATE_ROW0 = 2 * N_HEADS_A * HEAD_DIM
BACK_ROW0 = GATE_ROW0 + 3 * N_HEADS_A


def _nt_dot(x, w):
    return lax.dot_general(x, w, (((1,), (1,)), ((), ())), preferred_element_type=jnp.float32)


def _rmsnorm_kernel(*refs, with_rope):
    if with_rope:
        x_ref, w_ref, wg_ref, pos_ref, inv_ref, o_ref, g_ref, cos_ref, sin_ref = refs
    else:
        x_ref, w_ref, wg_ref, o_ref, g_ref = refs
    x = x_ref[...]
    ms = jnp.mean(x * x, axis=-1, keepdims=True)
    h = (x * lax.rsqrt(ms + EPS) * w_ref[...]).astype(o_ref.dtype)
    o_ref[...] = h
    g_ref[...] = _nt_dot(h, wg_ref[...].astype(jnp.bfloat16))
    if with_rope:
        half_rows = pos_ref.shape[0] // 2
        lane = lax.broadcasted_iota(jnp.int32, (half_rows, HEAD_DIM), 1)
        first = lane < HALF
        pos = jnp.where(first, pos_ref[0:half_rows, :], pos_ref[half_rows:, :]).astype(jnp.float32)
        ang = pos * inv_ref[...]
        cos, sin = jnp.cos(ang), jnp.sin(ang)
        cos_sw, sin_sw = pltpu.roll(cos, HALF, 1), pltpu.roll(sin, HALF, 1)
        cos_ref[0:half_rows, :] = jnp.where(first, cos, cos_sw)
        cos_ref[half_rows:, :] = jnp.where(first, cos_sw, cos)
        sin_ref[0:half_rows, :] = jnp.where(first, -sin, sin_sw)
        sin_ref[half_rows:, :] = jnp.where(first, -sin_sw, sin)


def _rmsnorm(x2d, w, w_t, positions=None, tm=512):
    m, d = x2d.shape
    with_rope = positions is not None
    rows = lambda width: pl.BlockSpec((tm, width), lambda i: (i, 0))
    in_specs = [rows(d), pl.BlockSpec((1, d), lambda i: (0, 0)),
                pl.BlockSpec((HEAD_DIM, d), lambda i: (GATE_ROW0 // HEAD_DIM, 0))]
    out_specs = [rows(d), rows(HEAD_DIM)]
    out_shape = [jax.ShapeDtypeStruct((m, d), jnp.bfloat16),
                 jax.ShapeDtypeStruct((m, HEAD_DIM), jnp.float32)]
    args = [x2d, w.reshape(1, d), w_t]
    if with_rope:
        inv = ROPE_THETA ** (-jnp.arange(HALF, dtype=jnp.float32) / HALF)
        in_specs += [rows(1), pl.BlockSpec((1, HEAD_DIM), lambda i: (0, 0))]
        out_specs += [rows(HEAD_DIM), rows(HEAD_DIM)]
        out_shape += [jax.ShapeDtypeStruct((m, HEAD_DIM), jnp.float32)] * 2
        args += [positions.reshape(m, 1), jnp.concatenate([inv, inv]).reshape(1, HEAD_DIM)]
    return pl.pallas_call(
        functools.partial(_rmsnorm_kernel, with_rope=with_rope),
        grid=(m // tm,),
        in_specs=in_specs,
        out_specs=out_specs,
        out_shape=out_shape,
        compiler_params=_cparams(("parallel",)),
        name="pre_rmsnorm_gates",
    )(*args)


PROJ_TN = GROUP_A * HEAD_DIM
PROJ_FRONT = GATE_ROW0 // PROJ_TN


WOUT_CAST_ROWS = 64


def _in_proj_kernel(x_ref, w_ref, wo_ref, front_ref, back_ref, wo_bf16_ref, *, wo_blocks):
    j = pl.program_id(1)
    t = pl.program_id(0) * pl.num_programs(1) + j

    @pl.when(t < wo_blocks)
    def _():
        wo_bf16_ref[...] = wo_ref[...].astype(wo_bf16_ref.dtype)

    @pl.when(j < PROJ_FRONT)
    def _():
        front_ref[...] = _nt_dot(x_ref[...], w_ref[...].astype(jnp.bfloat16)).astype(front_ref.dtype)

    @pl.when(j >= PROJ_FRONT)
    def _():
        back_ref[...] = _nt_dot(x_ref[...], w_ref[...].astype(jnp.bfloat16)).astype(back_ref.dtype)


def _in_proj(h, w_t, w_out, tm=2048):
    m, k = h.shape
    n_back = N_BACK_HEADS * HEAD_DIM
    n_steps = PROJ_FRONT + n_back // PROJ_TN
    wo_blocks = w_out.shape[0] // WOUT_CAST_ROWS
    assert wo_blocks <= (m // tm) * n_steps

    def wo_block(i, j):
        return jnp.minimum(i * n_steps + j, wo_blocks - 1), 0

    def w_row(i, j):
        row = PROJ_TN * j + jnp.where(j >= PROJ_FRONT, BACK_ROW0 - GATE_ROW0, 0)
        return pl.multiple_of(row, F32_SUBLANES), 0

    wo_spec = pl.BlockSpec((WOUT_CAST_ROWS, w_out.shape[1]), wo_block)
    return pl.pallas_call(
        functools.partial(_in_proj_kernel, wo_blocks=wo_blocks),
        grid=(m // tm, n_steps),
        in_specs=[
            pl.BlockSpec((tm, k), lambda i, j: (i, 0), pipeline_mode=pl.Buffered(1)),
            pl.BlockSpec((pl.Element(PROJ_TN), pl.Element(k)), w_row),
            wo_spec],
        out_specs=[
            pl.BlockSpec((tm, PROJ_TN), lambda i, j: (i, jnp.minimum(j, PROJ_FRONT - 1))),
            pl.BlockSpec((tm, PROJ_TN), lambda i, j: (i, jnp.maximum(j - PROJ_FRONT, 0))),
            wo_spec],
        out_shape=[jax.ShapeDtypeStruct((m, N_FRONT_HEADS * HEAD_DIM), jnp.bfloat16),
                   jax.ShapeDtypeStruct((m, n_back), jnp.bfloat16),
                   jax.ShapeDtypeStruct(w_out.shape, jnp.bfloat16)],
        compiler_params=_cparams(("arbitrary", "arbitrary")),
        name="in_proj",
    )(h, w_t, w_out)


def _rope(xf, cos, sin_signed):
    return xf * cos + pltpu.roll(xf, HALF, 1) * sin_signed


CMP_PITCH = CMP_STRIDE + 4


def _compress_kernel(x_ref, pe_ref, w1_ref, w2_ref, o_ref, xs_ref):
    s = x_ref.shape[0]
    n_rows = s // CMP_STRIDE
    for hh in range(N_KV_A):
        cols = slice(hh * HEAD_DIM, (hh + 1) * HEAD_DIM)
        for g in range(n_rows):
            xs_ref[hh, pl.ds(g * CMP_PITCH, CMP_STRIDE), :] = (
                x_ref[pl.ds(g * CMP_STRIDE, CMP_STRIDE), cols].astype(jnp.float32))
        xs_ref[hh, pl.ds(n_rows * CMP_PITCH, CMP_STRIDE), :] = jnp.zeros(
            (CMP_STRIDE, HEAD_DIM), jnp.float32)

    def block_rows(hh, r):
        start = (r // CMP_STRIDE) * CMP_PITCH + r % CMP_STRIDE
        return xs_ref[hh, pl.ds(start, n_rows, stride=CMP_PITCH), :]

    flat = jnp.concatenate(
        [jnp.concatenate([(block_rows(hh, r) + pe_ref[pl.ds(r, 1), :]).astype(jnp.bfloat16)
                          for r in range(CMP_LEN)], axis=1)
         for hh in range(N_KV_A)], axis=0)
    hid = jax.nn.gelu(jnp.dot(flat, w1_ref[...], preferred_element_type=jnp.float32))
    out = jnp.dot(hid.astype(jnp.bfloat16), w2_ref[...], preferred_element_type=jnp.float32)
    for hh in range(N_KV_A):
        o_ref[hh] = out[hh * n_rows:(hh + 1) * n_rows, :].astype(o_ref.dtype)


def _compress(proj, batch, s, pe, w1, w2):
    n_rows = s // CMP_STRIDE
    assert H_KC % N_KV_A == 0 and H_VC == H_KC + N_KV_A
    return pl.pallas_call(
        _compress_kernel,
        grid=(batch, 2),
        in_specs=[
            pl.BlockSpec((s, N_KV_A * HEAD_DIM), lambda b, kv: (b, H_KC // N_KV_A + kv)),
            pl.BlockSpec((None, CMP_LEN, HEAD_DIM), lambda b, kv: (kv, 0, 0)),
            pl.BlockSpec((None, CMP_LEN * HEAD_DIM, CMP_HIDDEN), lambda b, kv: (kv, 0, 0)),
            pl.BlockSpec((None, CMP_HIDDEN, HEAD_DIM), lambda b, kv: (kv, 0, 0)),
        ],
        out_specs=pl.BlockSpec((None, N_KV_A, n_rows, HEAD_DIM), lambda b, kv: (b, kv, 0, 0)),
        out_shape=jax.ShapeDtypeStruct((batch, 2 * N_KV_A, n_rows, HEAD_DIM), jnp.bfloat16),
        scratch_shapes=[pltpu.VMEM((N_KV_A, (n_rows + 1) * CMP_PITCH, HEAD_DIM), jnp.float32)],
        compiler_params=_cparams(("parallel", "parallel")),
        name="compress",
    )(proj, pe, w1, w2)


def _kq(k, q):
    return lax.dot_general(k, q, (((1,), (1,)), ((), ())),
                           preferred_element_type=jnp.float32)


def _memo(fn):
    cache = []

    def get():
        if not cache:
            cache.append(fn())
        return cache[0]
    return get


class _AttnJob:
    def __init__(self, q_fn, chunks):
        self.q_fn, self.chunks = q_fn, chunks
        self.s, self.bias, self.m, self.acc = [], [], None, None

    def _score(self, c):
        k_fn, _, mask_fn = self.chunks[c][:3]
        s = mask_fn(_kq(k_fn(), self.q_fn()))
        bias = self.chunks[c][3]() if len(self.chunks[c]) > 3 else None
        self.s.append(s)
        self.bias.append(bias)
        if bias is None:
            cm = jnp.max(s, axis=0, keepdims=True)
        else:
            sub = s.shape[0] // len(bias)
            cm = functools.reduce(jnp.maximum, [
                jnp.max(s[a * sub:(a + 1) * sub, :], axis=0, keepdims=True) + b
                for a, b in enumerate(bias)])
        self.m = cm if self.m is None else jnp.maximum(self.m, cm)

    def _probabilities(self, c):
        s, bias = self.s[c], self.bias[c]
        if bias is None:
            return jnp.exp2(s - self.m).astype(jnp.bfloat16)
        sub = s.shape[0] // len(bias)
        return jnp.concatenate([jnp.exp2(s[a * sub:(a + 1) * sub, :] + (b - self.m))
                                for a, b in enumerate(bias)], axis=0).astype(jnp.bfloat16)

    def _value(self, cs):
        p = jnp.concatenate([self._probabilities(c) for c in cs], axis=0)
        vt = jnp.concatenate([self.chunks[c][1]() for c in cs], axis=1)
        part = jnp.dot(vt, p, preferred_element_type=jnp.float32)
        self.acc = part if self.acc is None else self.acc + part

    def score_tasks(self):
        return [functools.partial(self._score, c) for c in range(len(self.chunks))]

    def value_tasks(self):
        n = len(self.chunks)
        return [functools.partial(self._value, range(c, min(c + PV_GROUP, n)))
                for c in range(0, n, PV_GROUP)]

    def result(self, weight=None):
        inv_l = 1.0 / self.acc[HEAD_DIM:HEAD_DIM + 1, :]
        return self.acc[:HEAD_DIM, :] * (inv_l if weight is None else weight * inv_l)


def _round_robin(task_lists):
    for i in range(max(len(t) for t in task_lists)):
        for tasks in task_lists:
            if i < len(tasks):
                tasks[i]()


def _run_pipelined(streams, on_done):
    n = len(streams[0])
    _round_robin([jobs[0].score_tasks() for jobs in streams])
    for k in range(n):
        lists = []
        for jobs in streams:
            lists.append(jobs[k].value_tasks())
            if k + 1 < n:
                lists.append(jobs[k + 1].score_tasks())
        _round_robin(lists)
        for si, jobs in enumerate(streams):
            on_done(si, k, jobs[k])


def _store_values_transposed(src_ref, dst_ref):
    s_len = src_ref.shape[0]
    for c in range(s_len // TK):
        blk = src_ref[c * TK:(c + 1) * TK, :].astype(jnp.float32)
        dst_ref[0:HEAD_DIM, c * TK:(c + 1) * TK] = blk.T.astype(jnp.bfloat16)
    row = lax.broadcasted_iota(jnp.int32, (BF16_SUBLANES, s_len), 0)
    dst_ref[HEAD_DIM:, :] = jnp.where(row == 0, 1.0, 0.0).astype(jnp.bfloat16)


def _silu(z):
    return z * jax.nn.sigmoid(z)


def _topk_active(q0):
    return q0 + TQ > TOP_N * SEL_LEN


def _nsa_selection_bias(p_sum, q0, s_len):
    n_sel = s_len // SEL_LEN
    n_cmp = s_len // CMP_STRIDE - CMP_LEN // CMP_STRIDE + 1
    oj = lax.broadcasted_iota(jnp.int32, (n_sel, HEAD_DIM), 0)
    oc = lax.broadcasted_iota(jnp.int32, (n_sel, HEAD_DIM), 1)
    ovl = jnp.where(oc * CMP_STRIDE < oj * SEL_LEN + SEL_LEN,
                    jnp.where(oc * CMP_STRIDE + CMP_LEN > oj * SEL_LEN, 1.0, 0.0), 0.0)
    ovl = jnp.where(oc < n_cmp, ovl, 0.0).astype(jnp.bfloat16)
    p_hi = p_sum.astype(jnp.bfloat16)
    p_lo = (p_sum - p_hi.astype(jnp.float32)).astype(jnp.bfloat16)
    imp = (jnp.dot(ovl, p_hi, preferred_element_type=jnp.float32)
           + jnp.dot(ovl, p_lo, preferred_element_type=jnp.float32))
    j_row = lax.broadcasted_iota(jnp.int32, (n_sel, TQ), 0)
    t_sel = q0 + lax.broadcasted_iota(jnp.int32, (n_sel, TQ), 1)
    cur = lax.shift_right_arithmetic(t_sel, int(math.log2(SEL_LEN)))
    forced = (j_row == 0) | (j_row == cur) | (j_row == cur - 1)
    imp = jnp.where(forced, imp + FORCE_BONUS, imp)
    imp = jnp.where(j_row * SEL_LEN <= t_sel, imp, -jnp.inf)
    rank = jnp.zeros((n_sel, TQ), jnp.float32)
    for i in range(n_sel):
        vi = imp[i:i + 1, :]
        rank = rank + jnp.where(j_row > i, jnp.where(vi >= imp, 1.0, 0.0),
                                jnp.where(vi > imp, 1.0, 0.0))
    return jnp.where(rank < TOP_N, 0.0, NEG)


def _mixers_kernel(q_ref, z_ref, g_ref, bg_ref, kc_ref, vc_ref, ksl_ref, vsl_ref, kw_ref, vw_ref,
                   cos_ref, sin_ref, qd_ref, zd_ref, kd_ref, vd_ref, dbias_ref, o_ref, od_ref,
                   ksl_rot, kw_rot, vsl_t, vw_t, ocmp_s, selb_s, gate_s):
    s_len = ksl_ref.shape[0]
    n_q = s_len // TQ

    ksl_rot[...] = _rope(ksl_ref[...].astype(jnp.float32), cos_ref[...], sin_ref[...]).astype(jnp.bfloat16)
    kw_rot[...] = _rope(kw_ref[...].astype(jnp.float32), cos_ref[...], sin_ref[...]).astype(jnp.bfloat16)
    _store_values_transposed(vsl_ref, vsl_t)
    _store_values_transposed(vw_ref, vw_t)
    kc = (kc_ref[...].astype(jnp.float32) * QSCALE).astype(jnp.bfloat16)
    vc_t = vc_ref[...].astype(jnp.float32).T.astype(jnp.bfloat16)
    bg = bg_ref[...]
    kv_head = pl.program_id(1)

    c_row = lax.broadcasted_iota(jnp.int32, (HEAD_DIM, TQ), 0)
    lane_q = lax.broadcasted_iota(jnp.int32, (HEAD_DIM, TQ), 1)
    lane_one = lax.broadcasted_iota(jnp.int32, (1, TQ), 1)
    for qi in range(n_q):
        q0 = qi * TQ
        rows = slice(q0, q0 + TQ)
        n_vis = min(HEAD_DIM, (q0 + TQ) // CMP_STRIDE)
        hidden = jnp.zeros((HEAD_DIM - n_vis, TQ), jnp.float32)
        c_valid = (c_row * CMP_STRIDE + (CMP_LEN - 1) <= q0 + lane_q)[:n_vis, :]
        any_valid = jnp.where(q0 + lane_one >= CMP_LEN - 1, 1.0, 0.0)
        q4 = jnp.concatenate([q_ref[rows, g * HEAD_DIM:(g + 1) * HEAD_DIM]
                              for g in range(GROUP_A)], axis=0)
        sc4 = _kq(kc[:n_vis, :], q4)
        p_heads = []
        p_sum = jnp.zeros((HEAD_DIM, TQ), jnp.float32)
        for g in range(GROUP_A):
            sc = jnp.where(c_valid, sc4[:, g * TQ:(g + 1) * TQ], NEG)
            e = jnp.exp2(sc - jnp.max(sc, axis=0, keepdims=True))
            p = e * (any_valid / jnp.sum(e, axis=0, keepdims=True))
            if n_vis < HEAD_DIM:
                p = jnp.concatenate([p, hidden], axis=0)
            p_heads.append(p.astype(jnp.bfloat16))
            p_sum = p_sum + p
        o_cmp4 = jnp.dot(vc_t, jnp.concatenate(p_heads, axis=1),
                         preferred_element_type=jnp.float32)
        for g in range(GROUP_A):
            ocmp_s[g, :, rows] = o_cmp4[:, g * TQ:(g + 1) * TQ]
        if _topk_active(q0):
            selb_s[qi] = _nsa_selection_bias(p_sum, q0, s_len)
        gate_s[:, rows] = jax.nn.sigmoid(g_ref[rows, :] + bg).T

    diag_rc = (lax.broadcasted_iota(jnp.int32, (TK, TQ), 0)
               - lax.broadcasted_iota(jnp.int32, (TK, TQ), 1))

    def slc_mask(qi, c, s):
        if c == qi:
            s = jnp.where(diag_rc <= 0, s, NEG)
        return s

    def slc_row_bias(qi, c):
        return [selb_s[qi, c * SEL_PER_TK + a:c * SEL_PER_TK + a + 1, :] for a in range(SEL_PER_TK)]

    def win_mask(qi, c, s):
        if c == qi:
            return jnp.where(diag_rc <= 0, s, NEG)
        if (qi - c) * TK + TQ - 1 > WIN - 1:
            return jnp.where(diag_rc >= (qi - c) * TK - (WIN - 1), s, NEG)
        return s

    ones_rows = jnp.where(lax.broadcasted_iota(jnp.int32, (BF16_SUBLANES, TK), 0) == 0,
                          1.0, 0.0).astype(jnp.bfloat16)

    def head_body(g, carry):
        col = pl.ds(pl.multiple_of(g * HEAD_DIM, HEAD_DIM), HEAD_DIM)
        gate_row = GATE_PER_KV * kv_head + 3 * g
        blocks = [slice(c * TK, (c + 1) * TK) for c in range(n_q)]

        def roped(ref, rows, scale):
            xf = ref[rows, col].astype(jnp.float32)
            if scale != 1.0:
                xf = xf * scale
            return _rope(xf, cos_ref[rows, :], sin_ref[rows, :]).astype(jnp.bfloat16)

        def dil_value_t(rows):
            vt = vd_ref[rows, col].astype(jnp.float32).T.astype(jnp.bfloat16)
            return jnp.concatenate([vt, ones_rows], axis=0)

        def dil_bias(delta, s):
            r0 = (n_q - 1 - delta) * TK
            return s + dbias_ref[r0:r0 + TK, :]

        dil_k = [_memo(functools.partial(roped, kd_ref, rows, 1.0)) for rows in blocks]
        dil_v = [_memo(functools.partial(dil_value_t, rows)) for rows in blocks]
        slc_jobs, win_jobs, dil_jobs = [], [], []
        for qi in range(n_q):
            q = _memo(functools.partial(roped, q_ref, blocks[qi], QSCALE))
            first_win = max(0, (qi * TQ - (WIN - 1)) // TK)
            ranked = _topk_active(qi * TQ)
            for jobs, k_ref_, v_ref_, mask, row_bias, c_lo in (
                    (slc_jobs, ksl_rot, vsl_t, slc_mask, slc_row_bias if ranked else None, 0),
                    (win_jobs, kw_rot, vw_t, win_mask, None, first_win)):
                jobs.append(_AttnJob(q, [
                    (functools.partial(lambda r, c: r[c * TK:(c + 1) * TK, :], k_ref_, c),
                     functools.partial(lambda r, c: r[:, c * TK:(c + 1) * TK], v_ref_, c),
                     functools.partial(mask, qi, c))
                    + ((functools.partial(row_bias, qi, c),) if row_bias else ())
                    for c in range(c_lo, qi + 1)]))
            dil_jobs.append(_AttnJob(
                _memo(functools.partial(roped, qd_ref, blocks[qi], QSCALE)),
                [(dil_k[c], dil_v[c], functools.partial(dil_bias, qi - c)) for c in range(qi + 1)]))

        def on_done(si, k, job):
            kind, qi = si % 3, MIX_SPLIT * k + si // 3
            rows = blocks[qi]
            if kind == 1:
                o_t = (gate_s[pl.ds(gate_row, 1), rows] * ocmp_s[g, :, rows]
                       + slc_jobs[qi].result(gate_s[pl.ds(gate_row + 1, 1), rows])
                       + job.result(gate_s[pl.ds(gate_row + 2, 1), rows]))
                z = z_ref[rows, col].astype(jnp.float32)
                o_ref[rows, col] = (o_t.T * _silu(z)).astype(o_ref.dtype)
            elif kind == 2:
                z = zd_ref[rows, col].astype(jnp.float32)
                od_ref[rows, col] = (job.result().T * _silu(z)).astype(od_ref.dtype)

        streams = []
        for r in range(MIX_SPLIT):
            streams += [slc_jobs[r::MIX_SPLIT], win_jobs[r::MIX_SPLIT], dil_jobs[r::MIX_SPLIT]]
        _run_pipelined(streams, on_done)
        return carry

    lax.fori_loop(0, GROUP_A, head_body, 0)


def _mixers(front, back, gates, cmp_kv, b_gate_row, cos, sin, dil_bias, batch, s):
    gw = GROUP_A * HEAD_DIM
    assert N_HEADS_B == N_HEADS_A
    full = lambda col0: pl.BlockSpec((s, HEAD_DIM), lambda b, h: (b, col0 + h))
    group = lambda head0: pl.BlockSpec((s, gw), lambda b, h: (b, head0 // GROUP_A + h))
    out = jax.ShapeDtypeStruct((batch * s, N_HEADS_A * HEAD_DIM), jnp.bfloat16)
    rarely_changes = dict(pipeline_mode=pl.Buffered(1))
    return pl.pallas_call(
        _mixers_kernel,
        grid=(batch, N_KV_A),
        in_specs=[
            group(H_QA), group(H_ZA),
            pl.BlockSpec((s, HEAD_DIM), lambda b, h: (b, 0)),
            pl.BlockSpec((1, HEAD_DIM), lambda b, h: (0, 0)),
            pl.BlockSpec((None, None, s // CMP_STRIDE, HEAD_DIM), lambda b, h: (b, h, 0, 0)),
            pl.BlockSpec((None, None, s // CMP_STRIDE, HEAD_DIM), lambda b, h: (b, N_KV_A + h, 0, 0)),
            full(H_KSL), full(H_VSL), full(H_KW), full(H_VW),
            pl.BlockSpec((None, s, HEAD_DIM), lambda b, h: (b, 0, 0), **rarely_changes),
            pl.BlockSpec((None, s, HEAD_DIM), lambda b, h: (b, 0, 0), **rarely_changes),
            group(H_QB), group(H_ZB), group(H_KB), group(H_VB),
            pl.BlockSpec((s, TQ), lambda b, h: (0, 0), **rarely_changes),
        ],
        out_specs=[pl.BlockSpec((s, gw), lambda b, h: (b, h)),
                   pl.BlockSpec((s, gw), lambda b, h: (b, h))],
        out_shape=[out, out],
        scratch_shapes=[pltpu.VMEM((s, HEAD_DIM), jnp.bfloat16),
                        pltpu.VMEM((s, HEAD_DIM), jnp.bfloat16),
                        pltpu.VMEM((V_ROWS, s), jnp.bfloat16),
                        pltpu.VMEM((V_ROWS, s), jnp.bfloat16),
                        pltpu.VMEM((GROUP_A, HEAD_DIM, s), jnp.float32),
                        pltpu.VMEM((s // TQ, s // SEL_LEN, TQ), jnp.float32),
                        pltpu.VMEM((HEAD_DIM, s), jnp.float32)],
        compiler_params=pltpu.CompilerParams(dimension_semantics=("parallel", "parallel"),
                                             vmem_limit_bytes=MIXERS_VMEM_LIMIT),
        name="token_mixers",
    )(front, front, gates, b_gate_row, cmp_kv, cmp_kv, back, back, back, back, cos, sin,
      back, back, back, back, dil_bias)


def _dilated_multiplicity(dist):
    cnt = jnp.zeros(dist.shape, jnp.float32)
    for window, dil in DILATIONS:
        cnt = cnt + jnp.where((dist & (dil - 1)) == 0,
                              jnp.where(dist <= (window // dil) * dil, 1.0, 0.0), 0.0)
    return jnp.where(dist >= 0, cnt, 0.0)


def _dilated_table_kernel(bias_ref):
    s_len = bias_ref.shape[0]
    dist = (s_len - TQ + lax.broadcasted_iota(jnp.int32, (s_len, TQ), 1)
            - lax.broadcasted_iota(jnp.int32, (s_len, TQ), 0))
    cnt = _dilated_multiplicity(dist)
    bias_ref[...] = jnp.where(cnt > 0.5, jnp.log2(jnp.maximum(cnt, 1.0)), NEG)


def _dilated_tables(s):
    return pl.pallas_call(
        _dilated_table_kernel,
        out_shape=jax.ShapeDtypeStruct((s, TQ), jnp.float32),
        compiler_params=pltpu.CompilerParams(vmem_limit_bytes=VMEM_LIMIT),
        name="dilated_tables",
    )()


OUT_PANELS = 4


def _out_proj_kernel(ma_ref, mb_ref, wa_ref, wb_ref, x_ref, nw_ref, o_ref, y_s, ssq_s):
    i = pl.program_id(0)
    n_tiles = pl.num_programs(0) - 1
    d = o_ref.shape[1]
    pn = d // OUT_PANELS

    def step(multiply, finish):
        if finish:
            scale = lax.rsqrt(ssq_s[...] * (1.0 / d) + EPS)
        ssq = None
        for p in range(OUT_PANELS):
            cols = slice(p * pn, (p + 1) * pn)
            if finish:
                o_ref[:, cols] = x_ref[:, cols] + y_s[:, cols] * scale * nw_ref[:, cols]
            if multiply:
                y = (jnp.dot(ma_ref[...], wa_ref[:, cols], preferred_element_type=jnp.float32)
                     + jnp.dot(mb_ref[...], wb_ref[:, cols], preferred_element_type=jnp.float32))
                y_s[:, cols] = y
                part = jnp.sum(y * y, axis=-1, keepdims=True)
                ssq = part if ssq is None else ssq + part
        if multiply:
            ssq_s[...] = ssq

    pl.when(i == 0)(functools.partial(step, True, False))
    pl.when((i > 0) & (i < n_tiles))(functools.partial(step, True, True))
    pl.when(i == n_tiles)(functools.partial(step, False, True))


def _out_proj(mix_a, mix_b, w, x2d, post_w, tm=256):
    m, d = x2d.shape
    ka, kb = mix_a.shape[1], mix_b.shape[1]
    assert ka == kb and w.shape[0] == ka + kb
    n_tiles = m // tm
    resident = dict(pipeline_mode=pl.Buffered(1))
    this_tile = lambda i: (jnp.minimum(i, n_tiles - 1), 0)
    prev_tile = lambda i: (jnp.maximum(i - 1, 0), 0)
    return pl.pallas_call(
        _out_proj_kernel,
        grid=(n_tiles + 1,),
        in_specs=[
            pl.BlockSpec((tm, ka), this_tile),
            pl.BlockSpec((tm, kb), this_tile),
            pl.BlockSpec((ka, d), lambda i: (0, 0), **resident),
            pl.BlockSpec((kb, d), lambda i: (1, 0), **resident),
            pl.BlockSpec((tm, d), prev_tile),
            pl.BlockSpec((1, d), lambda i: (0, 0)),
        ],
        out_specs=pl.BlockSpec((tm, d), prev_tile),
        out_shape=jax.ShapeDtypeStruct((m, d), jnp.float32),
        scratch_shapes=[pltpu.VMEM((tm, d), jnp.float32),
                        pltpu.VMEM((tm, 1), jnp.float32)],
        compiler_params=pltpu.CompilerParams(dimension_semantics=("arbitrary",),
                                             vmem_limit_bytes=OUT_PROJ_VMEM_LIMIT),
        name="out_proj_norm_residual",
    )(mix_a, mix_b, w, w, x2d, post_w.reshape(1, d))


def _layer(x, positions, rope, pre_w, post_w, w_in, b_gate, pe_k, pe_v, wk1, wk2, wv1, wv2, w_out):
    b, s, d = x.shape
    x2d = x.reshape(b * s, d)
    w_t = w_in.T
    if rope is None:
        h, gates, cos, sin = _rmsnorm(x2d, pre_w, w_t, positions.reshape(b * s))
        rope = (cos.reshape(b, s, HEAD_DIM), sin.reshape(b, s, HEAD_DIM))
    else:
        h, gates = _rmsnorm(x2d, pre_w, w_t)
    cos, sin = rope
    front, back, w_out_bf16 = _in_proj(h, w_t, w_out)

    pe = jnp.stack([pe_k, pe_v])
    w1 = jnp.stack([wk1, wv1]).astype(jnp.bfloat16)
    w2 = jnp.stack([wk2, wv2]).astype(jnp.bfloat16)
    cmp_kv = _compress(back, b, s, pe, w1, w2)

    bg = jnp.pad(b_gate.reshape(1, -1), ((0, 0), (0, HEAD_DIM - b_gate.size)))
    mix_a, mix_b = _mixers(front, back, gates, cmp_kv, bg, cos, sin, _dilated_tables(s), b, s)

    out = _out_proj(mix_a, mix_b, w_out_bf16, x2d, post_w)
    return out.reshape(b, s, d), rope


def kernel(x, positions, pre_norm_w, post_norm_w, w_in, b_gate, cmp_pe_k, cmp_pe_v,
           cmp_wk1, cmp_wk2, cmp_wv1, cmp_wv2, w_out):
    rope = None
    for l in range(pre_norm_w.shape[0]):
        x, rope = _layer(x, positions, rope, pre_norm_w[l], post_norm_w[l], w_in[l], b_gate[l],
                         cmp_pe_k[l], cmp_pe_v[l], cmp_wk1[l], cmp_wk2[l], cmp_wv1[l],
                         cmp_wv2[l], w_out[l])
    return x
```

```python
import functools
import math

import jax
import jax.numpy as jnp
from jax import lax
from jax.experimental import pallas as pl
from jax.experimental.pallas import tpu as pltpu

HEAD_DIM = 128
HALF = HEAD_DIM // 2
N_HEADS_A = 16
N_KV_A = 4
GROUP_A = 4
N_HEADS_B = 16
CMP_LEN = 32
CMP_STRIDE = 16
CMP_HIDDEN = 256
SEL_LEN = 64
TOP_N = 16
WIN = 512
DILATIONS = ((128, 1), (512, 4), (2048, 16))
ROPE_THETA = 10000.0
EPS = 1e-6
FORCE_BONUS = 1e4
NEG = -1e30
SCALE = HEAD_DIM ** -0.5
QSCALE = SCALE * math.log2(math.e)

H_QA = 0
H_ZA = H_QA + N_HEADS_A
N_FRONT_HEADS = H_ZA + N_HEADS_A
GATE_PER_KV = 3 * GROUP_A
H_KC = 0
H_VC = H_KC + N_KV_A
H_KSL = H_VC + N_KV_A
H_VSL = H_KSL + N_KV_A
H_KW = H_VSL + N_KV_A
H_VW = H_KW + N_KV_A
H_QB = H_VW + N_KV_A
H_KB = H_QB + N_HEADS_B
H_VB = H_KB + N_HEADS_B
H_ZB = H_VB + N_HEADS_B
N_BACK_HEADS = H_ZB + N_HEADS_B

TQ = 256
TK = 256
SEL_PER_TK = TK // SEL_LEN
F32_SUBLANES = 8
BF16_SUBLANES = 16
V_ROWS = HEAD_DIM + BF16_SUBLANES
PV_GROUP = 2
MIX_SPLIT = 2
VMEM_LIMIT = 56 * 1024 * 1024
MIXERS_VMEM_LIMIT = 60 * 1024 * 1024
OUT_PROJ_VMEM_LIMIT = 60 * 1024 * 1024


def _cparams(sem):
    return pltpu.CompilerParams(dimension_semantics=sem, vmem_limit_bytes=VMEM_LIMIT)


GATE_ROW0 = 2 * N_HEADS_A * HEAD_DIM
BACK_ROW0 = GATE_ROW0 + 3 * N_HEADS_A


def _nt_dot(x, w):
    return lax.dot_general(x, w, (((1,), (1,)), ((), ())), preferred_element_type=jnp.float32)


def _rmsnorm_kernel(*refs, with_rope):
    if with_rope:
        x_ref, w_ref, wg_ref, pos_ref, inv_ref, o_ref, g_ref, cos_ref, sin_ref = refs
    else:
        x_ref, w_ref, wg_ref, o_ref, g_ref = refs
    x = x_ref[...]
    ms = jnp.mean(x * x, axis=-1, keepdims=True)
    h = (x * lax.rsqrt(ms + EPS) * w_ref[...]).astype(o_ref.dtype)
    o_ref[...] = h
    g_ref[...] = _nt_dot(h, wg_ref[...].astype(jnp.bfloat16))
    if with_rope:
        ang = pos_ref[...].astype(jnp.float32) * inv_ref[...]
        lane = lax.broadcasted_iota(jnp.int32, ang.shape, 1)
        sin = jnp.sin(ang)
        cos_ref[...] = jnp.cos(ang)
        sin_ref[...] = jnp.where(lane < HALF, -sin, sin)


def _rmsnorm(x2d, w, w_t, positions=None, tm=512):
    m, d = x2d.shape
    with_rope = positions is not None
    rows = lambda width: pl.BlockSpec((tm, width), lambda i: (i, 0))
    in_specs = [rows(d), pl.BlockSpec((1, d), lambda i: (0, 0)),
                pl.BlockSpec((HEAD_DIM, d), lambda i: (GATE_ROW0 // HEAD_DIM, 0))]
    out_specs = [rows(d), rows(HEAD_DIM)]
    out_shape = [jax.ShapeDtypeStruct((m, d), jnp.bfloat16),
                 jax.ShapeDtypeStruct((m, HEAD_DIM), jnp.float32)]
    args = [x2d, w.reshape(1, d), w_t]
    if with_rope:
        inv = ROPE_THETA ** (-jnp.arange(HALF, dtype=jnp.float32) / HALF)
        in_specs += [rows(1), pl.BlockSpec((1, HEAD_DIM), lambda i: (0, 0))]
        out_specs += [rows(HEAD_DIM), rows(HEAD_DIM)]
        out_shape += [jax.ShapeDtypeStruct((m, HEAD_DIM), jnp.float32)] * 2
        args += [positions.reshape(m, 1), jnp.concatenate([inv, inv]).reshape(1, HEAD_DIM)]
    return pl.pallas_call(
        functools.partial(_rmsnorm_kernel, with_rope=with_rope),
        grid=(m // tm,),
        in_specs=in_specs,
        out_specs=out_specs,
        out_shape=out_shape,
        compiler_params=_cparams(("parallel",)),
        name="pre_rmsnorm_gates",
    )(*args)


PROJ_TN = GROUP_A * HEAD_DIM
PROJ_FRONT = GATE_ROW0 // PROJ_TN


WOUT_CAST_ROWS = 64


def _in_proj_kernel(x_ref, w_ref, wo_ref, front_ref, back_ref, wo_bf16_ref):
    j = pl.program_id(1)

    def project(out_ref):
        wo_bf16_ref[...] = wo_ref[...].astype(wo_bf16_ref.dtype)
        out_ref[...] = _nt_dot(x_ref[...], w_ref[...].astype(jnp.bfloat16)).astype(out_ref.dtype)

    pl.when(j < PROJ_FRONT)(functools.partial(project, front_ref))
    pl.when(j >= PROJ_FRONT)(functools.partial(project, back_ref))


def _in_proj(h, w_t, w_out, tm=2048):
    m, k = h.shape
    n_back = N_BACK_HEADS * HEAD_DIM
    n_steps = PROJ_FRONT + n_back // PROJ_TN
    wo_blocks = w_out.shape[0] // WOUT_CAST_ROWS
    assert wo_blocks <= (m // tm) * n_steps

    def wo_block(i, j):
        return jnp.minimum(i * n_steps + j, wo_blocks - 1), 0

    def w_row(i, j):
        row = PROJ_TN * j + jnp.where(j >= PROJ_FRONT, BACK_ROW0 - GATE_ROW0, 0)
        return pl.multiple_of(row, F32_SUBLANES), 0

    wo_spec = pl.BlockSpec((WOUT_CAST_ROWS, w_out.shape[1]), wo_block)
    return pl.pallas_call(
        _in_proj_kernel,
        grid=(m // tm, n_steps),
        in_specs=[
            pl.BlockSpec((tm, k), lambda i, j: (i, 0), pipeline_mode=pl.Buffered(1)),
            pl.BlockSpec((pl.Element(PROJ_TN), pl.Element(k)), w_row),
            wo_spec],
        out_specs=[
            pl.BlockSpec((tm, PROJ_TN), lambda i, j: (i, jnp.minimum(j, PROJ_FRONT - 1))),
            pl.BlockSpec((tm, PROJ_TN), lambda i, j: (i, jnp.maximum(j - PROJ_FRONT, 0))),
            wo_spec],
        out_shape=[jax.ShapeDtypeStruct((m, N_FRONT_HEADS * HEAD_DIM), jnp.bfloat16),
                   jax.ShapeDtypeStruct((m, n_back), jnp.bfloat16),
                   jax.ShapeDtypeStruct(w_out.shape, jnp.bfloat16)],
        compiler_params=_cparams(("arbitrary", "arbitrary")),
        name="in_proj",
    )(h, w_t, w_out)


def _rope(xf, cos, sin_signed):
    return xf * cos + pltpu.roll(xf, HALF, 1) * sin_signed


CMP_PITCH = CMP_STRIDE + 4


def _compress_kernel(x_ref, pe_ref, w1_ref, w2_ref, o_ref, xs_ref):
    s = x_ref.shape[0]
    n_rows = s // CMP_STRIDE
    for hh in range(N_KV_A):
        cols = slice(hh * HEAD_DIM, (hh + 1) * HEAD_DIM)
        for g in range(n_rows):
            xs_ref[hh, pl.ds(g * CMP_PITCH, CMP_STRIDE), :] = (
                x_ref[pl.ds(g * CMP_STRIDE, CMP_STRIDE), cols].astype(jnp.float32))
        xs_ref[hh, pl.ds(n_rows * CMP_PITCH, CMP_STRIDE), :] = jnp.zeros(
            (CMP_STRIDE, HEAD_DIM), jnp.float32)

    def block_rows(hh, r):
        start = (r // CMP_STRIDE) * CMP_PITCH + r % CMP_STRIDE
        return xs_ref[hh, pl.ds(start, n_rows, stride=CMP_PITCH), :]

    flat = jnp.concatenate(
        [jnp.concatenate([(block_rows(hh, r) + pe_ref[pl.ds(r, 1), :]).astype(jnp.bfloat16)
                          for r in range(CMP_LEN)], axis=1)
         for hh in range(N_KV_A)], axis=0)
    hid = jax.nn.gelu(jnp.dot(flat, w1_ref[...], preferred_element_type=jnp.float32))
    out = jnp.dot(hid.astype(jnp.bfloat16), w2_ref[...], preferred_element_type=jnp.float32)
    for hh in range(N_KV_A):
        o_ref[hh] = out[hh * n_rows:(hh + 1) * n_rows, :].astype(o_ref.dtype)


def _compress(proj, batch, s, pe, w1, w2):
    n_rows = s // CMP_STRIDE
    assert H_KC % N_KV_A == 0 and H_VC == H_KC + N_KV_A
    return pl.pallas_call(
        _compress_kernel,
        grid=(batch, 2),
        in_specs=[
            pl.BlockSpec((s, N_KV_A * HEAD_DIM), lambda b, kv: (b, H_KC // N_KV_A + kv)),
            pl.BlockSpec((None, CMP_LEN, HEAD_DIM), lambda b, kv: (kv, 0, 0)),
            pl.BlockSpec((None, CMP_LEN * HEAD_DIM, CMP_HIDDEN), lambda b, kv: (kv, 0, 0)),
            pl.BlockSpec((None, CMP_HIDDEN, HEAD_DIM), lambda b, kv: (kv, 0, 0)),
        ],
        out_specs=pl.BlockSpec((None, N_KV_A, n_rows, HEAD_DIM), lambda b, kv: (b, kv, 0, 0)),
        out_shape=jax.ShapeDtypeStruct((batch, 2 * N_KV_A, n_rows, HEAD_DIM), jnp.bfloat16),
        scratch_shapes=[pltpu.VMEM((N_KV_A, (n_rows + 1) * CMP_PITCH, HEAD_DIM), jnp.float32)],
        compiler_params=_cparams(("parallel", "parallel")),
        name="compress",
    )(proj, pe, w1, w2)


def _kq(k, q):
    return lax.dot_general(k, q, (((1,), (1,)), ((), ())),
                           preferred_element_type=jnp.float32)


def _memo(fn):
    cache = []

    def get():
        if not cache:
            cache.append(fn())
        return cache[0]
    return get


class _AttnJob:
    def __init__(self, q_fn, chunks):
        self.q_fn, self.chunks = q_fn, chunks
        self.s, self.bias, self.m, self.acc = [], [], None, None

    def _score(self, c):
        k_fn, _, mask_fn = self.chunks[c][:3]
        s = mask_fn(_kq(k_fn(), self.q_fn()))
        bias = self.chunks[c][3]() if len(self.chunks[c]) > 3 else None
        self.s.append(s)
        self.bias.append(bias)
        if bias is None:
            cm = jnp.max(s, axis=0, keepdims=True)
        else:
            sub = s.shape[0] // len(bias)
            cm = functools.reduce(jnp.maximum, [
                jnp.max(s[a * sub:(a + 1) * sub, :], axis=0, keepdims=True) + b
                for a, b in enumerate(bias)])
        self.m = cm if self.m is None else jnp.maximum(self.m, cm)

    def _probabilities(self, c):
        s, bias = self.s[c], self.bias[c]
        if bias is None:
            return jnp.exp2(s - self.m).astype(jnp.bfloat16)
        sub = s.shape[0] // len(bias)
        return jnp.concatenate([jnp.exp2(s[a * sub:(a + 1) * sub, :] + (b - self.m))
                                for a, b in enumerate(bias)], axis=0).astype(jnp.bfloat16)

    def _value(self, cs):
        p = jnp.concatenate([self._probabilities(c) for c in cs], axis=0)
        vt = jnp.concatenate([self.chunks[c][1]() for c in cs], axis=1)
        part = jnp.dot(vt, p, preferred_element_type=jnp.float32)
        self.acc = part if self.acc is None else self.acc + part

    def score_tasks(self):
        return [functools.partial(self._score, c) for c in range(len(self.chunks))]

    def value_tasks(self):
        n = len(self.chunks)
        return [functools.partial(self._value, range(c, min(c + PV_GROUP, n)))
                for c in range(0, n, PV_GROUP)]

    def result(self):
        return self.acc[:HEAD_DIM, :] * (1.0 / self.acc[HEAD_DIM:HEAD_DIM + 1, :])


def _round_robin(task_lists):
    for i in range(max(len(t) for t in task_lists)):
        for tasks in task_lists:
            if i < len(tasks):
                tasks[i]()


def _run_pipelined(streams, on_done):
    n = len(streams[0])
    _round_robin([jobs[0].score_tasks() for jobs in streams])
    for k in range(n):
        lists = []
        for jobs in streams:
            lists.append(jobs[k].value_tasks())
            if k + 1 < n:
                lists.append(jobs[k + 1].score_tasks())
        _round_robin(lists)
        for si, jobs in enumerate(streams):
            on_done(si, k, jobs[k])


def _store_values_transposed(src_ref, dst_ref):
    s_len = src_ref.shape[0]
    for c in range(s_len // TK):
        blk = src_ref[c * TK:(c + 1) * TK, :].astype(jnp.float32)
        dst_ref[0:HEAD_DIM, c * TK:(c + 1) * TK] = blk.T.astype(jnp.bfloat16)
    row = lax.broadcasted_iota(jnp.int32, (BF16_SUBLANES, s_len), 0)
    dst_ref[HEAD_DIM:, :] = jnp.where(row == 0, 1.0, 0.0).astype(jnp.bfloat16)


def _silu(z):
    return z * jax.nn.sigmoid(z)


def _topk_active(q0):
    return q0 + TQ > TOP_N * SEL_LEN


def _nsa_selection_bias(p_sum, q0, s_len):
    n_sel = s_len // SEL_LEN
    n_cmp = s_len // CMP_STRIDE - CMP_LEN // CMP_STRIDE + 1
    oj = lax.broadcasted_iota(jnp.int32, (n_sel, HEAD_DIM), 0)
    oc = lax.broadcasted_iota(jnp.int32, (n_sel, HEAD_DIM), 1)
    ovl = jnp.where(oc * CMP_STRIDE < oj * SEL_LEN + SEL_LEN,
                    jnp.where(oc * CMP_STRIDE + CMP_LEN > oj * SEL_LEN, 1.0, 0.0), 0.0)
    ovl = jnp.where(oc < n_cmp, ovl, 0.0).astype(jnp.bfloat16)
    p_hi = p_sum.astype(jnp.bfloat16)
    p_lo = (p_sum - p_hi.astype(jnp.float32)).astype(jnp.bfloat16)
    imp = (jnp.dot(ovl, p_hi, preferred_element_type=jnp.float32)
           + jnp.dot(ovl, p_lo, preferred_element_type=jnp.float32))
    j_row = lax.broadcasted_iota(jnp.int32, (n_sel, TQ), 0)
    t_sel = q0 + lax.broadcasted_iota(jnp.int32, (n_sel, TQ), 1)
    cur = lax.shift_right_arithmetic(t_sel, int(math.log2(SEL_LEN)))
    forced = (j_row == 0) | (j_row == cur) | (j_row == cur - 1)
    imp = jnp.where(forced, imp + FORCE_BONUS, imp)
    imp = jnp.where(j_row * SEL_LEN <= t_sel, imp, -jnp.inf)
    rank = jnp.zeros((n_sel, TQ), jnp.float32)
    for i in range(n_sel):
        vi = imp[i:i + 1, :]
        rank = rank + jnp.where(j_row > i, jnp.where(vi >= imp, 1.0, 0.0),
                                jnp.where(vi > imp, 1.0, 0.0))
    return jnp.where(rank < TOP_N, 0.0, NEG)


def _mixers_kernel(q_ref, z_ref, g_ref, bg_ref, kc_ref, vc_ref, ksl_ref, vsl_ref, kw_ref, vw_ref,
                   cos_ref, sin_ref, qd_ref, zd_ref, kd_ref, vd_ref, dbias_ref, o_ref, od_ref,
                   ksl_rot, kw_rot, vsl_t, vw_t, ocmp_s, selb_s, gate_s):
    s_len = ksl_ref.shape[0]
    n_q = s_len // TQ

    ksl_rot[...] = _rope(ksl_ref[...].astype(jnp.float32), cos_ref[...], sin_ref[...]).astype(jnp.bfloat16)
    kw_rot[...] = _rope(kw_ref[...].astype(jnp.float32), cos_ref[...], sin_ref[...]).astype(jnp.bfloat16)
    _store_values_transposed(vsl_ref, vsl_t)
    _store_values_transposed(vw_ref, vw_t)
    kc = (kc_ref[...].astype(jnp.float32) * QSCALE).astype(jnp.bfloat16)
    vc_t = vc_ref[...].astype(jnp.float32).T.astype(jnp.bfloat16)
    bg = bg_ref[...]
    kv_head = pl.program_id(1)

    c_row = lax.broadcasted_iota(jnp.int32, (HEAD_DIM, TQ), 0)
    lane_q = lax.broadcasted_iota(jnp.int32, (HEAD_DIM, TQ), 1)
    lane_one = lax.broadcasted_iota(jnp.int32, (1, TQ), 1)
    for qi in range(n_q):
        q0 = qi * TQ
        rows = slice(q0, q0 + TQ)
        n_vis = min(HEAD_DIM, (q0 + TQ) // CMP_STRIDE)
        hidden = jnp.zeros((HEAD_DIM - n_vis, TQ), jnp.float32)
        c_valid = (c_row * CMP_STRIDE + (CMP_LEN - 1) <= q0 + lane_q)[:n_vis, :]
        any_valid = jnp.where(q0 + lane_one >= CMP_LEN - 1, 1.0, 0.0)
        q4 = jnp.concatenate([q_ref[rows, g * HEAD_DIM:(g + 1) * HEAD_DIM]
                              for g in range(GROUP_A)], axis=0)
        sc4 = _kq(kc[:n_vis, :], q4)
        p_heads = []
        p_sum = jnp.zeros((HEAD_DIM, TQ), jnp.float32)
        for g in range(GROUP_A):
            sc = jnp.where(c_valid, sc4[:, g * TQ:(g + 1) * TQ], NEG)
            e = jnp.exp2(sc - jnp.max(sc, axis=0, keepdims=True))
            p = e * (any_valid / jnp.sum(e, axis=0, keepdims=True))
            if n_vis < HEAD_DIM:
                p = jnp.concatenate([p, hidden], axis=0)
            p_heads.append(p.astype(jnp.bfloat16))
            p_sum = p_sum + p
        o_cmp4 = jnp.dot(vc_t, jnp.concatenate(p_heads, axis=1),
                         preferred_element_type=jnp.float32)
        for g in range(GROUP_A):
            ocmp_s[g, :, rows] = o_cmp4[:, g * TQ:(g + 1) * TQ]
        if _topk_active(q0):
            selb_s[qi] = _nsa_selection_bias(p_sum, q0, s_len)
        gate_s[:, rows] = jax.nn.sigmoid(g_ref[rows, :] + bg).T

    diag_rc = (lax.broadcasted_iota(jnp.int32, (TK, TQ), 0)
               - lax.broadcasted_iota(jnp.int32, (TK, TQ), 1))

    def slc_mask(qi, c, s):
        if c == qi:
            s = jnp.where(diag_rc <= 0, s, NEG)
        return s

    def slc_row_bias(qi, c):
        return [selb_s[qi, c * SEL_PER_TK + a:c * SEL_PER_TK + a + 1, :] for a in range(SEL_PER_TK)]

    def win_mask(qi, c, s):
        if c == qi:
            return jnp.where(diag_rc <= 0, s, NEG)
        if (qi - c) * TK + TQ - 1 > WIN - 1:
            return jnp.where(diag_rc >= (qi - c) * TK - (WIN - 1), s, NEG)
        return s

    ones_rows = jnp.where(lax.broadcasted_iota(jnp.int32, (BF16_SUBLANES, TK), 0) == 0,
                          1.0, 0.0).astype(jnp.bfloat16)

    def head_body(g, carry):
        col = pl.ds(pl.multiple_of(g * HEAD_DIM, HEAD_DIM), HEAD_DIM)
        gate_row = GATE_PER_KV * kv_head + 3 * g
        blocks = [slice(c * TK, (c + 1) * TK) for c in range(n_q)]

        def roped(ref, rows, scale):
            xf = ref[rows, col].astype(jnp.float32)
            if scale != 1.0:
                xf = xf * scale
            return _rope(xf, cos_ref[rows, :], sin_ref[rows, :]).astype(jnp.bfloat16)

        def dil_value_t(rows):
            vt = vd_ref[rows, col].astype(jnp.float32).T.astype(jnp.bfloat16)
            return jnp.concatenate([vt, ones_rows], axis=0)

        def dil_bias(delta, s):
            r0 = (n_q - 1 - delta) * TK
            return s + dbias_ref[r0:r0 + TK, :]

        dil_k = [_memo(functools.partial(roped, kd_ref, rows, 1.0)) for rows in blocks]
        dil_v = [_memo(functools.partial(dil_value_t, rows)) for rows in blocks]
        slc_jobs, win_jobs, dil_jobs = [], [], []
        for qi in range(n_q):
            q = _memo(functools.partial(roped, q_ref, blocks[qi], QSCALE))
            first_win = max(0, (qi * TQ - (WIN - 1)) // TK)
            ranked = _topk_active(qi * TQ)
            for jobs, k_ref_, v_ref_, mask, row_bias, c_lo in (
                    (slc_jobs, ksl_rot, vsl_t, slc_mask, slc_row_bias if ranked else None, 0),
                    (win_jobs, kw_rot, vw_t, win_mask, None, first_win)):
                jobs.append(_AttnJob(q, [
                    (functools.partial(lambda r, c: r[c * TK:(c + 1) * TK, :], k_ref_, c),
                     functools.partial(lambda r, c: r[:, c * TK:(c + 1) * TK], v_ref_, c),
                     functools.partial(mask, qi, c))
                    + ((functools.partial(row_bias, qi, c),) if row_bias else ())
                    for c in range(c_lo, qi + 1)]))
            dil_jobs.append(_AttnJob(
                _memo(functools.partial(roped, qd_ref, blocks[qi], QSCALE)),
                [(dil_k[c], dil_v[c], functools.partial(dil_bias, qi - c)) for c in range(qi + 1)]))

        def on_done(si, k, job):
            kind, qi = si % 3, MIX_SPLIT * k + si // 3
            rows = blocks[qi]
            if kind == 1:
                o_t = (gate_s[pl.ds(gate_row, 1), rows] * ocmp_s[g, :, rows]
                       + gate_s[pl.ds(gate_row + 1, 1), rows] * slc_jobs[qi].result()
                       + gate_s[pl.ds(gate_row + 2, 1), rows] * job.result())
                z = z_ref[rows, col].astype(jnp.float32)
                o_ref[rows, col] = (o_t.T * _silu(z)).astype(o_ref.dtype)
            elif kind == 2:
                z = zd_ref[rows, col].astype(jnp.float32)
                od_ref[rows, col] = (job.result().T * _silu(z)).astype(od_ref.dtype)

        streams = []
        for r in range(MIX_SPLIT):
            streams += [slc_jobs[r::MIX_SPLIT], win_jobs[r::MIX_SPLIT], dil_jobs[r::MIX_SPLIT]]
        _run_pipelined(streams, on_done)
        return carry

    lax.fori_loop(0, GROUP_A, head_body, 0)


def _mixers(front, back, gates, cmp_kv, b_gate_row, cos, sin, dil_bias, batch, s):
    gw = GROUP_A * HEAD_DIM
    assert N_HEADS_B == N_HEADS_A
    full = lambda col0: pl.BlockSpec((s, HEAD_DIM), lambda b, h: (b, col0 + h))
    group = lambda head0: pl.BlockSpec((s, gw), lambda b, h: (b, head0 // GROUP_A + h))
    out = jax.ShapeDtypeStruct((batch * s, N_HEADS_A * HEAD_DIM), jnp.bfloat16)
    rarely_changes = dict(pipeline_mode=pl.Buffered(1))
    return pl.pallas_call(
        _mixers_kernel,
        grid=(batch, N_KV_A),
        in_specs=[
            group(H_QA), group(H_ZA),
            pl.BlockSpec((s, HEAD_DIM), lambda b, h: (b, 0)),
            pl.BlockSpec((1, HEAD_DIM), lambda b, h: (0, 0)),
            pl.BlockSpec((None, None, s // CMP_STRIDE, HEAD_DIM), lambda b, h: (b, h, 0, 0)),
            pl.BlockSpec((None, None, s // CMP_STRIDE, HEAD_DIM), lambda b, h: (b, N_KV_A + h, 0, 0)),
            full(H_KSL), full(H_VSL), full(H_KW), full(H_VW),
            pl.BlockSpec((None, s, HEAD_DIM), lambda b, h: (b, 0, 0), **rarely_changes),
            pl.BlockSpec((None, s, HEAD_DIM), lambda b, h: (b, 0, 0), **rarely_changes),
            group(H_QB), group(H_ZB), group(H_KB), group(H_VB),
            pl.BlockSpec((s, TQ), lambda b, h: (0, 0), **rarely_changes),
        ],
        out_specs=[pl.BlockSpec((s, gw), lambda b, h: (b, h)),
                   pl.BlockSpec((s, gw), lambda b, h: (b, h))],
        out_shape=[out, out],
        scratch_shapes=[pltpu.VMEM((s, HEAD_DIM), jnp.bfloat16),
                        pltpu.VMEM((s, HEAD_DIM), jnp.bfloat16),
                        pltpu.VMEM((V_ROWS, s), jnp.bfloat16),
                        pltpu.VMEM((V_ROWS, s), jnp.bfloat16),
                        pltpu.VMEM((GROUP_A, HEAD_DIM, s), jnp.float32),
                        pltpu.VMEM((s // TQ, s // SEL_LEN, TQ), jnp.float32),
                        pltpu.VMEM((HEAD_DIM, s), jnp.float32)],
        compiler_params=pltpu.CompilerParams(dimension_semantics=("parallel", "parallel"),
                                             vmem_limit_bytes=MIXERS_VMEM_LIMIT),
        name="token_mixers",
    )(front, front, gates, b_gate_row, cmp_kv, cmp_kv, back, back, back, back, cos, sin,
      back, back, back, back, dil_bias)


def _dilated_multiplicity(dist):
    cnt = jnp.zeros(dist.shape, jnp.float32)
    for window, dil in DILATIONS:
        cnt = cnt + jnp.where((dist & (dil - 1)) == 0,
                              jnp.where(dist <= (window // dil) * dil, 1.0, 0.0), 0.0)
    return jnp.where(dist >= 0, cnt, 0.0)


def _dilated_table_kernel(bias_ref):
    s_len = bias_ref.shape[0]
    dist = (s_len - TQ + lax.broadcasted_iota(jnp.int32, (s_len, TQ), 1)
            - lax.broadcasted_iota(jnp.int32, (s_len, TQ), 0))
    cnt = _dilated_multiplicity(dist)
    bias_ref[...] = jnp.where(cnt > 0.5, jnp.log2(jnp.maximum(cnt, 1.0)), NEG)


def _dilated_tables(s):
    return pl.pallas_call(
        _dilated_table_kernel,
        out_shape=jax.ShapeDtypeStruct((s, TQ), jnp.float32),
        compiler_params=pltpu.CompilerParams(vmem_limit_bytes=VMEM_LIMIT),
        name="dilated_tables",
    )()


OUT_PANELS = 8


def _out_proj_kernel(ma_ref, mb_ref, wa_ref, wb_ref, x_ref, nw_ref, o_ref, y_s, ssq_s):
    i = pl.program_id(0)
    n_tiles = pl.num_programs(0) - 1
    d = o_ref.shape[1]
    pn = d // OUT_PANELS

    def step(multiply, finish):
        if finish:
            scale = lax.rsqrt(ssq_s[...] * (1.0 / d) + EPS)
        ssq = None
        for p in range(OUT_PANELS):
            cols = slice(p * pn, (p + 1) * pn)
            if finish:
                o_ref[:, cols] = x_ref[:, cols] + y_s[:, cols] * scale * nw_ref[:, cols]
            if multiply:
                y = (jnp.dot(ma_ref[...], wa_ref[:, cols], preferred_element_type=jnp.float32)
                     + jnp.dot(mb_ref[...], wb_ref[:, cols], preferred_element_type=jnp.float32))
                y_s[:, cols] = y
                part = jnp.sum(y * y, axis=-1, keepdims=True)
                ssq = part if ssq is None else ssq + part
        if multiply:
            ssq_s[...] = ssq

    pl.when(i == 0)(functools.partial(step, True, False))
    pl.when((i > 0) & (i < n_tiles))(functools.partial(step, True, True))
    pl.when(i == n_tiles)(functools.partial(step, False, True))


def _out_proj(mix_a, mix_b, w, x2d, post_w, tm=256):
    m, d = x2d.shape
    ka, kb = mix_a.shape[1], mix_b.shape[1]
    assert ka == kb and w.shape[0] == ka + kb
    n_tiles = m // tm
    resident = dict(pipeline_mode=pl.Buffered(1))
    this_tile = lambda i: (jnp.minimum(i, n_tiles - 1), 0)
    prev_tile = lambda i: (jnp.maximum(i - 1, 0), 0)
    return pl.pallas_call(
        _out_proj_kernel,
        grid=(n_tiles + 1,),
        in_specs=[
            pl.BlockSpec((tm, ka), this_tile),
            pl.BlockSpec((tm, kb), this_tile),
            pl.BlockSpec((ka, d), lambda i: (0, 0), **resident),
            pl.BlockSpec((kb, d), lambda i: (1, 0), **resident),
            pl.BlockSpec((tm, d), prev_tile),
            pl.BlockSpec((1, d), lambda i: (0, 0)),
        ],
        out_specs=pl.BlockSpec((tm, d), prev_tile),
        out_shape=jax.ShapeDtypeStruct((m, d), jnp.float32),
        scratch_shapes=[pltpu.VMEM((tm, d), jnp.float32),
                        pltpu.VMEM((tm, 1), jnp.float32)],
        compiler_params=pltpu.CompilerParams(dimension_semantics=("arbitrary",),
                                             vmem_limit_bytes=OUT_PROJ_VMEM_LIMIT),
        name="out_proj_norm_residual",
    )(mix_a, mix_b, w, w, x2d, post_w.reshape(1, d))


def _layer(x, positions, rope, pre_w, post_w, w_in, b_gate, pe_k, pe_v, wk1, wk2, wv1, wv2, w_out):
    b, s, d = x.shape
    x2d = x.reshape(b * s, d)
    w_t = w_in.T
    if rope is None:
        h, gates, cos, sin = _rmsnorm(x2d, pre_w, w_t, positions.reshape(b * s))
        rope = (cos.reshape(b, s, HEAD_DIM), sin.reshape(b, s, HEAD_DIM))
    else:
        h, gates = _rmsnorm(x2d, pre_w, w_t)
    cos, sin = rope
    front, back, w_out_bf16 = _in_proj(h, w_t, w_out)

    pe = jnp.stack([pe_k, pe_v])
    w1 = jnp.stack([wk1, wv1]).astype(jnp.bfloat16)
    w2 = jnp.stack([wk2, wv2]).astype(jnp.bfloat16)
    cmp_kv = _compress(back, b, s, pe, w1, w2)

    bg = jnp.pad(b_gate.reshape(1, -1), ((0, 0), (0, HEAD_DIM - b_gate.size)))
    mix_a, mix_b = _mixers(front, back, gates, cmp_kv, bg, cos, sin, _dilated_tables(s), b, s)

    out = _out_proj(mix_a, mix_b, w_out_bf16, x2d, post_w)
    return out.reshape(b, s, d), rope


def kernel(x, positions, pre_norm_w, post_norm_w, w_in, b_gate, cmp_pe_k, cmp_pe_v,
           cmp_wk1, cmp_wk2, cmp_wv1, cmp_wv2, w_out):
    rope = None
    for l in range(pre_norm_w.shape[0]):
        x, rope = _layer(x, positions, rope, pre_norm_w[l], post_norm_w[l], w_in[l], b_gate[l],
                         cmp_pe_k[l], cmp_pe_v[l], cmp_wk1[l], cmp_wk2[l], cmp_wv1[l],
                         cmp_wv2[l], w_out[l])
    return x
```

```python
import functools
import math

import jax
import jax.numpy as jnp
from jax import lax
from jax.experimental import pallas as pl
from jax.experimental.pallas import tpu as pltpu

HEAD_DIM = 128
HALF = HEAD_DIM // 2
N_HEADS_A = 16
N_KV_A = 4
GROUP_A = 4
N_HEADS_B = 16
CMP_LEN = 32
CMP_STRIDE = 16
CMP_HIDDEN = 256
SEL_LEN = 64
TOP_N = 16
WIN = 512
DILATIONS = ((128, 1), (512, 4), (2048, 16))
ROPE_THETA = 10000.0
EPS = 1e-6
FORCE_BONUS = 1e4
NEG = -1e30
SCALE = HEAD_DIM ** -0.5
QSCALE = SCALE * math.log2(math.e)

H_QA = 0
H_ZA = H_QA + N_HEADS_A
N_FRONT_HEADS = H_ZA + N_HEADS_A
GATE_PER_KV = 3 * GROUP_A
H_KC = 0
H_VC = H_KC + N_KV_A
H_KSL = H_VC + N_KV_A
H_VSL = H_KSL + N_KV_A
H_KW = H_VSL + N_KV_A
H_VW = H_KW + N_KV_A
H_QB = H_VW + N_KV_A
H_KB = H_QB + N_HEADS_B
H_VB = H_KB + N_HEADS_B
H_ZB = H_VB + N_HEADS_B
N_BACK_HEADS = H_ZB + N_HEADS_B

TQ = 256
TK = 256
SEL_PER_TK = TK // SEL_LEN
F32_SUBLANES = 8
BF16_SUBLANES = 16
V_ROWS = HEAD_DIM + BF16_SUBLANES
PV_GROUP = 2
MIX_SPLIT = 2
VMEM_LIMIT = 56 * 1024 * 1024
MIXERS_VMEM_LIMIT = 60 * 1024 * 1024
OUT_PROJ_VMEM_LIMIT = 60 * 1024 * 1024


def _cparams(sem):
    return pltpu.CompilerParams(dimension_semantics=sem, vmem_limit_bytes=VMEM_LIMIT)


GATE_ROW0 = 2 * N_HEADS_A * HEAD_DIM
BACK_ROW0 = GATE_ROW0 + 3 * N_HEADS_A


def _nt_dot(x, w):
    return lax.dot_general(x, w, (((1,), (1,)), ((), ())), preferred_element_type=jnp.float32)


def _rmsnorm_kernel(*refs, with_rope):
    if with_rope:
        x_ref, w_ref, wg_ref, pos_ref, inv_ref, o_ref, g_ref, cos_ref, sin_ref = refs
    else:
        x_ref, w_ref, wg_ref, o_ref, g_ref = refs
    x = x_ref[...]
    ms = jnp.mean(x * x, axis=-1, keepdims=True)
    h = (x * lax.rsqrt(ms + EPS) * w_ref[...]).astype(o_ref.dtype)
    o_ref[...] = h
    g_ref[...] = _nt_dot(h, wg_ref[...].astype(jnp.bfloat16))
    if with_rope:
        half_rows = pos_ref.shape[0] // 2
        lane = lax.broadcasted_iota(jnp.int32, (half_rows, HEAD_DIM), 1)
        first = lane < HALF
        pos = jnp.where(first, pos_ref[0:half_rows, :], pos_ref[half_rows:, :]).astype(jnp.float32)
        ang = pos * inv_ref[...]
        cos, sin = jnp.cos(ang), jnp.sin(ang)
        cos_sw, sin_sw = pltpu.roll(cos, HALF, 1), pltpu.roll(sin, HALF, 1)
        cos_ref[0:half_rows, :] = jnp.where(first, cos, cos_sw)
        cos_ref[half_rows:, :] = jnp.where(first, cos_sw, cos)
        sin_ref[0:half_rows, :] = jnp.where(first, -sin, sin_sw)
        sin_ref[half_rows:, :] = jnp.where(first, -sin_sw, sin)


def _rmsnorm(x2d, w, w_t, positions=None, tm=512):
    m, d = x2d.shape
    with_rope = positions is not None
    rows = lambda width: pl.BlockSpec((tm, width), lambda i: (i, 0))
    in_specs = [rows(d), pl.BlockSpec((1, d), lambda i: (0, 0)),
                pl.BlockSpec((HEAD_DIM, d), lambda i: (GATE_ROW0 // HEAD_DIM, 0))]
    out_specs = [rows(d), rows(HEAD_DIM)]
    out_shape = [jax.ShapeDtypeStruct((m, d), jnp.bfloat16),
                 jax.ShapeDtypeStruct((m, HEAD_DIM), jnp.float32)]
    args = [x2d, w.reshape(1, d), w_t]
    if with_rope:
        inv = ROPE_THETA ** (-jnp.arange(HALF, dtype=jnp.float32) / HALF)
        in_specs += [rows(1), pl.BlockSpec((1, HEAD_DIM), lambda i: (0, 0))]
        out_specs += [rows(HEAD_DIM), rows(HEAD_DIM)]
        out_shape += [jax.ShapeDtypeStruct((m, HEAD_DIM), jnp.float32)] * 2
        args += [positions.reshape(m, 1), jnp.concatenate([inv, inv]).reshape(1, HEAD_DIM)]
    return pl.pallas_call(
        functools.partial(_rmsnorm_kernel, with_rope=with_rope),
        grid=(m // tm,),
        in_specs=in_specs,
        out_specs=out_specs,
        out_shape=out_shape,
        compiler_params=_cparams(("parallel",)),
        name="pre_rmsnorm_gates",
    )(*args)


PROJ_TN = GROUP_A * HEAD_DIM
PROJ_FRONT = GATE_ROW0 // PROJ_TN


WOUT_CAST_ROWS = 64


def _in_proj_kernel(x_ref, w_ref, wo_ref, front_ref, back_ref, wo_bf16_ref):
    j = pl.program_id(1)

    def project(out_ref):
        wo_bf16_ref[...] = wo_ref[...].astype(wo_bf16_ref.dtype)
        out_ref[...] = _nt_dot(x_ref[...], w_ref[...].astype(jnp.bfloat16)).astype(out_ref.dtype)

    pl.when(j < PROJ_FRONT)(functools.partial(project, front_ref))
    pl.when(j >= PROJ_FRONT)(functools.partial(project, back_ref))


def _in_proj(h, w_t, w_out, tm=2048):
    m, k = h.shape
    n_back = N_BACK_HEADS * HEAD_DIM
    n_steps = PROJ_FRONT + n_back // PROJ_TN
    wo_blocks = w_out.shape[0] // WOUT_CAST_ROWS
    assert wo_blocks <= (m // tm) * n_steps

    def wo_block(i, j):
        return jnp.minimum(i * n_steps + j, wo_blocks - 1), 0

    def w_row(i, j):
        row = PROJ_TN * j + jnp.where(j >= PROJ_FRONT, BACK_ROW0 - GATE_ROW0, 0)
        return pl.multiple_of(row, F32_SUBLANES), 0

    wo_spec = pl.BlockSpec((WOUT_CAST_ROWS, w_out.shape[1]), wo_block)
    return pl.pallas_call(
        _in_proj_kernel,
        grid=(m // tm, n_steps),
        in_specs=[
            pl.BlockSpec((tm, k), lambda i, j: (i, 0), pipeline_mode=pl.Buffered(1)),
            pl.BlockSpec((pl.Element(PROJ_TN), pl.Element(k)), w_row),
            wo_spec],
        out_specs=[
            pl.BlockSpec((tm, PROJ_TN), lambda i, j: (i, jnp.minimum(j, PROJ_FRONT - 1))),
            pl.BlockSpec((tm, PROJ_TN), lambda i, j: (i, jnp.maximum(j - PROJ_FRONT, 0))),
            wo_spec],
        out_shape=[jax.ShapeDtypeStruct((m, N_FRONT_HEADS * HEAD_DIM), jnp.bfloat16),
                   jax.ShapeDtypeStruct((m, n_back), jnp.bfloat16),
                   jax.ShapeDtypeStruct(w_out.shape, jnp.bfloat16)],
        compiler_params=_cparams(("arbitrary", "arbitrary")),
        name="in_proj",
    )(h, w_t, w_out)


def _rope(xf, cos, sin_signed):
    return xf * cos + pltpu.roll(xf, HALF, 1) * sin_signed


CMP_PITCH = CMP_STRIDE + 4


def _compress_kernel(x_ref, pe_ref, w1_ref, w2_ref, o_ref, xs_ref):
    s = x_ref.shape[0]
    n_rows = s // CMP_STRIDE
    for hh in range(N_KV_A):
        cols = slice(hh * HEAD_DIM, (hh + 1) * HEAD_DIM)
        for g in range(n_rows):
            xs_ref[hh, pl.ds(g * CMP_PITCH, CMP_STRIDE), :] = (
                x_ref[pl.ds(g * CMP_STRIDE, CMP_STRIDE), cols].astype(jnp.float32))
        xs_ref[hh, pl.ds(n_rows * CMP_PITCH, CMP_STRIDE), :] = jnp.zeros(
            (CMP_STRIDE, HEAD_DIM), jnp.float32)

    def block_rows(hh, r):
        start = (r // CMP_STRIDE) * CMP_PITCH + r % CMP_STRIDE
        return xs_ref[hh, pl.ds(start, n_rows, stride=CMP_PITCH), :]

    flat = jnp.concatenate(
        [jnp.concatenate([(block_rows(hh, r) + pe_ref[pl.ds(r, 1), :]).astype(jnp.bfloat16)
                          for r in range(CMP_LEN)], axis=1)
         for hh in range(N_KV_A)], axis=0)
    hid = jax.nn.gelu(jnp.dot(flat, w1_ref[...], preferred_element_type=jnp.float32))
    out = jnp.dot(hid.astype(jnp.bfloat16), w2_ref[...], preferred_element_type=jnp.float32)
    for hh in range(N_KV_A):
        o_ref[hh] = out[hh * n_rows:(hh + 1) * n_rows, :].astype(o_ref.dtype)


def _compress(proj, batch, s, pe, w1, w2):
    n_rows = s // CMP_STRIDE
    assert H_KC % N_KV_A == 0 and H_VC == H_KC + N_KV_A
    return pl.pallas_call(
        _compress_kernel,
        grid=(batch, 2),
        in_specs=[
            pl.BlockSpec((s, N_KV_A * HEAD_DIM), lambda b, kv: (b, H_KC // N_KV_A + kv)),
            pl.BlockSpec((None, CMP_LEN, HEAD_DIM), lambda b, kv: (kv, 0, 0)),
            pl.BlockSpec((None, CMP_LEN * HEAD_DIM, CMP_HIDDEN), lambda b, kv: (kv, 0, 0)),
            pl.BlockSpec((None, CMP_HIDDEN, HEAD_DIM), lambda b, kv: (kv, 0, 0)),
        ],
        out_specs=pl.BlockSpec((None, N_KV_A, n_rows, HEAD_DIM), lambda b, kv: (b, kv, 0, 0)),
        out_shape=jax.ShapeDtypeStruct((batch, 2 * N_KV_A, n_rows, HEAD_DIM), jnp.bfloat16),
        scratch_shapes=[pltpu.VMEM((N_KV_A, (n_rows + 1) * CMP_PITCH, HEAD_DIM), jnp.float32)],
        compiler_params=_cparams(("parallel", "parallel")),
        name="compress",
    )(proj, pe, w1, w2)


def _kq(k, q):
    return lax.dot_general(k, q, (((1,), (1,)), ((), ())),
                           preferred_element_type=jnp.float32)


def _memo(fn):
    cache = []

    def get():
        if not cache:
            cache.append(fn())
        return cache[0]
    return get


class _AttnJob:
    def __init__(self, q_fn, chunks):
        self.q_fn, self.chunks = q_fn, chunks
        self.s, self.bias, self.m, self.acc = [], [], None, None

    def _score(self, c):
        k_fn, _, mask_fn = self.chunks[c][:3]
        s = mask_fn(_kq(k_fn(), self.q_fn()))
        bias = self.chunks[c][3]() if len(self.chunks[c]) > 3 else None
        self.s.append(s)
        self.bias.append(bias)
        if bias is None:
            cm = jnp.max(s, axis=0, keepdims=True)
        else:
            sub = s.shape[0] // len(bias)
            cm = functools.reduce(jnp.maximum, [
                jnp.max(s[a * sub:(a + 1) * sub, :], axis=0, keepdims=True) + b
                for a, b in enumerate(bias)])
        self.m = cm if self.m is None else jnp.maximum(self.m, cm)

    def _probabilities(self, c):
        s, bias = self.s[c], self.bias[c]
        if bias is None:
            return jnp.exp2(s - self.m).astype(jnp.bfloat16)
        sub = s.shape[0] // len(bias)
        return jnp.concatenate([jnp.exp2(s[a * sub:(a + 1) * sub, :] + (b - self.m))
                                for a, b in enumerate(bias)], axis=0).astype(jnp.bfloat16)

    def _value(self, cs):
        p = jnp.concatenate([self._probabilities(c) for c in cs], axis=0)
        vt = jnp.concatenate([self.chunks[c][1]() for c in cs], axis=1)
        part = jnp.dot(vt, p, preferred_element_type=jnp.float32)
        self.acc = part if self.acc is None else self.acc + part

    def score_tasks(self):
        return [functools.partial(self._score, c) for c in range(len(self.chunks))]

    def value_tasks(self):
        n = len(self.chunks)
        return [functools.partial(self._value, range(c, min(c + PV_GROUP, n)))
                for c in range(0, n, PV_GROUP)]

    def result(self):
        return self.acc[:HEAD_DIM, :] * (1.0 / self.acc[HEAD_DIM:HEAD_DIM + 1, :])


def _round_robin(task_lists):
    for i in range(max(len(t) for t in task_lists)):
        for tasks in task_lists:
            if i < len(tasks):
                tasks[i]()


def _run_pipelined(streams, on_done):
    n = len(streams[0])
    _round_robin([jobs[0].score_tasks() for jobs in streams])
    finish = []
    for k in range(n):
        lists = [finish] if finish else []
        for jobs in streams:
            lists.append(jobs[k].value_tasks())
            if k + 1 < n:
                lists.append(jobs[k + 1].score_tasks())
        _round_robin(lists)
        finish = [functools.partial(on_done, si, k, jobs[k]) for si, jobs in enumerate(streams)]
    _round_robin([finish])


def _store_values_transposed(src_ref, dst_ref):
    s_len = src_ref.shape[0]
    for c in range(s_len // TK):
        blk = src_ref[c * TK:(c + 1) * TK, :].astype(jnp.float32)
        dst_ref[0:HEAD_DIM, c * TK:(c + 1) * TK] = blk.T.astype(jnp.bfloat16)
    row = lax.broadcasted_iota(jnp.int32, (BF16_SUBLANES, s_len), 0)
    dst_ref[HEAD_DIM:, :] = jnp.where(row == 0, 1.0, 0.0).astype(jnp.bfloat16)


def _silu(z):
    return z * jax.nn.sigmoid(z)


def _topk_active(q0):
    return q0 + TQ > TOP_N * SEL_LEN


def _nsa_selection_bias(p_sum, q0, s_len):
    n_sel = s_len // SEL_LEN
    n_cmp = s_len // CMP_STRIDE - CMP_LEN // CMP_STRIDE + 1
    oj = lax.broadcasted_iota(jnp.int32, (n_sel, HEAD_DIM), 0)
    oc = lax.broadcasted_iota(jnp.int32, (n_sel, HEAD_DIM), 1)
    ovl = jnp.where(oc * CMP_STRIDE < oj * SEL_LEN + SEL_LEN,
                    jnp.where(oc * CMP_STRIDE + CMP_LEN > oj * SEL_LEN, 1.0, 0.0), 0.0)
    ovl = jnp.where(oc < n_cmp, ovl, 0.0).astype(jnp.bfloat16)
    p_hi = p_sum.astype(jnp.bfloat16)
    p_lo = (p_sum - p_hi.astype(jnp.float32)).astype(jnp.bfloat16)
    imp = (jnp.dot(ovl, p_hi, preferred_element_type=jnp.float32)
           + jnp.dot(ovl, p_lo, preferred_element_type=jnp.float32))
    j_row = lax.broadcasted_iota(jnp.int32, (n_sel, TQ), 0)
    t_sel = q0 + lax.broadcasted_iota(jnp.int32, (n_sel, TQ), 1)
    cur = lax.shift_right_arithmetic(t_sel, int(math.log2(SEL_LEN)))
    forced = (j_row == 0) | (j_row == cur) | (j_row == cur - 1)
    imp = jnp.where(forced, imp + FORCE_BONUS, imp)
    imp = jnp.where(j_row * SEL_LEN <= t_sel, imp, -jnp.inf)
    rank = jnp.zeros((n_sel, TQ), jnp.float32)
    for i in range(n_sel):
        vi = imp[i:i + 1, :]
        rank = rank + jnp.where(j_row > i, jnp.where(vi >= imp, 1.0, 0.0),
                                jnp.where(vi > imp, 1.0, 0.0))
    return jnp.where(rank < TOP_N, 0.0, NEG)


def _mixers_kernel(q_ref, z_ref, g_ref, bg_ref, kc_ref, vc_ref, ksl_ref, vsl_ref, kw_ref, vw_ref,
                   cos_ref, sin_ref, qd_ref, zd_ref, kd_ref, vd_ref, dbias_ref, o_ref, od_ref,
                   ksl_rot, kw_rot, vsl_t, vw_t, ocmp_s, selb_s, gate_s):
    s_len = ksl_ref.shape[0]
    n_q = s_len // TQ

    ksl_rot[...] = _rope(ksl_ref[...].astype(jnp.float32), cos_ref[...], sin_ref[...]).astype(jnp.bfloat16)
    kw_rot[...] = _rope(kw_ref[...].astype(jnp.float32), cos_ref[...], sin_ref[...]).astype(jnp.bfloat16)
    _store_values_transposed(vsl_ref, vsl_t)
    _store_values_transposed(vw_ref, vw_t)
    kc = (kc_ref[...].astype(jnp.float32) * QSCALE).astype(jnp.bfloat16)
    vc_t = vc_ref[...].astype(jnp.float32).T.astype(jnp.bfloat16)
    bg = bg_ref[...]
    kv_head = pl.program_id(1)

    c_row = lax.broadcasted_iota(jnp.int32, (HEAD_DIM, TQ), 0)
    lane_q = lax.broadcasted_iota(jnp.int32, (HEAD_DIM, TQ), 1)
    lane_one = lax.broadcasted_iota(jnp.int32, (1, TQ), 1)
    for qi in range(n_q):
        q0 = qi * TQ
        rows = slice(q0, q0 + TQ)
        n_vis = min(HEAD_DIM, (q0 + TQ) // CMP_STRIDE)
        hidden = jnp.zeros((HEAD_DIM - n_vis, TQ), jnp.float32)
        c_valid = (c_row * CMP_STRIDE + (CMP_LEN - 1) <= q0 + lane_q)[:n_vis, :]
        any_valid = jnp.where(q0 + lane_one >= CMP_LEN - 1, 1.0, 0.0)
        q4 = jnp.concatenate([q_ref[rows, g * HEAD_DIM:(g + 1) * HEAD_DIM]
                              for g in range(GROUP_A)], axis=0)
        sc4 = _kq(kc[:n_vis, :], q4)
        p_heads = []
        p_sum = jnp.zeros((HEAD_DIM, TQ), jnp.float32)
        for g in range(GROUP_A):
            sc = jnp.where(c_valid, sc4[:, g * TQ:(g + 1) * TQ], NEG)
            e = jnp.exp2(sc - jnp.max(sc, axis=0, keepdims=True))
            p = e * (any_valid / jnp.sum(e, axis=0, keepdims=True))
            if n_vis < HEAD_DIM:
                p = jnp.concatenate([p, hidden], axis=0)
            p_heads.append(p.astype(jnp.bfloat16))
            p_sum = p_sum + p
        o_cmp4 = jnp.dot(vc_t, jnp.concatenate(p_heads, axis=1),
                         preferred_element_type=jnp.float32)
        for g in range(GROUP_A):
            ocmp_s[g, :, rows] = o_cmp4[:, g * TQ:(g + 1) * TQ]
        if _topk_active(q0):
            selb_s[qi] = _nsa_selection_bias(p_sum, q0, s_len)
        gate_s[:, rows] = jax.nn.sigmoid(g_ref[rows, :] + bg).T

    diag_rc = (lax.broadcasted_iota(jnp.int32, (TK, TQ), 0)
               - lax.broadcasted_iota(jnp.int32, (TK, TQ), 1))

    def slc_mask(qi, c, s):
        if c == qi:
            s = jnp.where(diag_rc <= 0, s, NEG)
        return s

    def slc_row_bias(qi, c):
        return [selb_s[qi, c * SEL_PER_TK + a:c * SEL_PER_TK + a + 1, :] for a in range(SEL_PER_TK)]

    def win_mask(qi, c, s):
        if c == qi:
            return jnp.where(diag_rc <= 0, s, NEG)
        if (qi - c) * TK + TQ - 1 > WIN - 1:
            return jnp.where(diag_rc >= (qi - c) * TK - (WIN - 1), s, NEG)
        return s

    ones_rows = jnp.where(lax.broadcasted_iota(jnp.int32, (BF16_SUBLANES, TK), 0) == 0,
                          1.0, 0.0).astype(jnp.bfloat16)

    def head_body(g, carry):
        col = pl.ds(pl.multiple_of(g * HEAD_DIM, HEAD_DIM), HEAD_DIM)
        gate_row = GATE_PER_KV * kv_head + 3 * g
        blocks = [slice(c * TK, (c + 1) * TK) for c in range(n_q)]

        def roped(ref, rows, scale):
            xf = ref[rows, col].astype(jnp.float32)
            if scale != 1.0:
                xf = xf * scale
            return _rope(xf, cos_ref[rows, :], sin_ref[rows, :]).astype(jnp.bfloat16)

        def dil_value_t(rows):
            vt = vd_ref[rows, col].astype(jnp.float32).T.astype(jnp.bfloat16)
            return jnp.concatenate([vt, ones_rows], axis=0)

        def dil_bias(delta, s):
            r0 = (n_q - 1 - delta) * TK
            return s + dbias_ref[r0:r0 + TK, :]

        dil_k = [_memo(functools.partial(roped, kd_ref, rows, 1.0)) for rows in blocks]
        dil_v = [_memo(functools.partial(dil_value_t, rows)) for rows in blocks]
        slc_jobs, win_jobs, dil_jobs = [], [], []
        for qi in range(n_q):
            q = _memo(functools.partial(roped, q_ref, blocks[qi], QSCALE))
            first_win = max(0, (qi * TQ - (WIN - 1)) // TK)
            ranked = _topk_active(qi * TQ)
            for jobs, k_ref_, v_ref_, mask, row_bias, c_lo in (
                    (slc_jobs, ksl_rot, vsl_t, slc_mask, slc_row_bias if ranked else None, 0),
                    (win_jobs, kw_rot, vw_t, win_mask, None, first_win)):
                jobs.append(_AttnJob(q, [
                    (functools.partial(lambda r, c: r[c * TK:(c + 1) * TK, :], k_ref_, c),
                     functools.partial(lambda r, c: r[:, c * TK:(c + 1) * TK], v_ref_, c),
                     functools.partial(mask, qi, c))
                    + ((functools.partial(row_bias, qi, c),) if row_bias else ())
                    for c in range(c_lo, qi + 1)]))
            dil_jobs.append(_AttnJob(
                _memo(functools.partial(roped, qd_ref, blocks[qi], QSCALE)),
                [(dil_k[c], dil_v[c], functools.partial(dil_bias, qi - c)) for c in range(qi + 1)]))

        def on_done(si, k, job):
            kind, qi = si % 3, MIX_SPLIT * k + si // 3
            rows = blocks[qi]
            if kind == 1:
                o_t = (gate_s[pl.ds(gate_row, 1), rows] * ocmp_s[g, :, rows]
                       + gate_s[pl.ds(gate_row + 1, 1), rows] * slc_jobs[qi].result()
                       + gate_s[pl.ds(gate_row + 2, 1), rows] * job.result())
                z = z_ref[rows, col].astype(jnp.float32)
                o_ref[rows, col] = (o_t.T * _silu(z)).astype(o_ref.dtype)
            elif kind == 2:
                z = zd_ref[rows, col].astype(jnp.float32)
                od_ref[rows, col] = (job.result().T * _silu(z)).astype(od_ref.dtype)

        streams = []
        for r in range(MIX_SPLIT):
            streams += [slc_jobs[r::MIX_SPLIT], win_jobs[r::MIX_SPLIT], dil_jobs[r::MIX_SPLIT]]
        _run_pipelined(streams, on_done)
        return carry

    lax.fori_loop(0, GROUP_A, head_body, 0)


def _mixers(front, back, gates, cmp_kv, b_gate_row, cos, sin, dil_bias, batch, s):
    gw = GROUP_A * HEAD_DIM
    assert N_HEADS_B == N_HEADS_A
    full = lambda col0: pl.BlockSpec((s, HEAD_DIM), lambda b, h: (b, col0 + h))
    group = lambda head0: pl.BlockSpec((s, gw), lambda b, h: (b, head0 // GROUP_A + h))
    out = jax.ShapeDtypeStruct((batch * s, N_HEADS_A * HEAD_DIM), jnp.bfloat16)
    rarely_changes = dict(pipeline_mode=pl.Buffered(1))
    return pl.pallas_call(
        _mixers_kernel,
        grid=(batch, N_KV_A),
        in_specs=[
            group(H_QA), group(H_ZA),
            pl.BlockSpec((s, HEAD_DIM), lambda b, h: (b, 0)),
            pl.BlockSpec((1, HEAD_DIM), lambda b, h: (0, 0)),
            pl.BlockSpec((None, None, s // CMP_STRIDE, HEAD_DIM), lambda b, h: (b, h, 0, 0)),
            pl.BlockSpec((None, None, s // CMP_STRIDE, HEAD_DIM), lambda b, h: (b, N_KV_A + h, 0, 0)),
            full(H_KSL), full(H_VSL), full(H_KW), full(H_VW),
            pl.BlockSpec((None, s, HEAD_DIM), lambda b, h: (b, 0, 0), **rarely_changes),
            pl.BlockSpec((None, s, HEAD_DIM), lambda b, h: (b, 0, 0), **rarely_changes),
            group(H_QB), group(H_ZB), group(H_KB), group(H_VB),
            pl.BlockSpec((s, TQ), lambda b, h: (0, 0), **rarely_changes),
        ],
        out_specs=[pl.BlockSpec((s, gw), lambda b, h: (b, h)),
                   pl.BlockSpec((s, gw), lambda b, h: (b, h))],
        out_shape=[out, out],
        scratch_shapes=[pltpu.VMEM((s, HEAD_DIM), jnp.bfloat16),
                        pltpu.VMEM((s, HEAD_DIM), jnp.bfloat16),
                        pltpu.VMEM((V_ROWS, s), jnp.bfloat16),
                        pltpu.VMEM((V_ROWS, s), jnp.bfloat16),
                        pltpu.VMEM((GROUP_A, HEAD_DIM, s), jnp.float32),
                        pltpu.VMEM((s // TQ, s // SEL_LEN, TQ), jnp.float32),
                        pltpu.VMEM((HEAD_DIM, s), jnp.float32)],
        compiler_params=pltpu.CompilerParams(dimension_semantics=("parallel", "parallel"),
                                             vmem_limit_bytes=MIXERS_VMEM_LIMIT),
        name="token_mixers",
    )(front, front, gates, b_gate_row, cmp_kv, cmp_kv, back, back, back, back, cos, sin,
      back, back, back, back, dil_bias)


def _dilated_multiplicity(dist):
    cnt = jnp.zeros(dist.shape, jnp.float32)
    for window, dil in DILATIONS:
        cnt = cnt + jnp.where((dist & (dil - 1)) == 0,
                              jnp.where(dist <= (window // dil) * dil, 1.0, 0.0), 0.0)
    return jnp.where(dist >= 0, cnt, 0.0)


def _dilated_table_kernel(bias_ref):
    s_len = bias_ref.shape[0]
    dist = (s_len - TQ + lax.broadcasted_iota(jnp.int32, (s_len, TQ), 1)
            - lax.broadcasted_iota(jnp.int32, (s_len, TQ), 0))
    cnt = _dilated_multiplicity(dist)
    bias_ref[...] = jnp.where(cnt > 0.5, jnp.log2(jnp.maximum(cnt, 1.0)), NEG)


def _dilated_tables(s):
    return pl.pallas_call(
        _dilated_table_kernel,
        out_shape=jax.ShapeDtypeStruct((s, TQ), jnp.float32),
        compiler_params=pltpu.CompilerParams(vmem_limit_bytes=VMEM_LIMIT),
        name="dilated_tables",
    )()


OUT_PANELS = 8


def _out_proj_kernel(ma_ref, mb_ref, wa_ref, wb_ref, x_ref, nw_ref, o_ref, y_s, ssq_s):
    i = pl.program_id(0)
    n_tiles = pl.num_programs(0) - 1
    d = o_ref.shape[1]
    pn = d // OUT_PANELS

    def step(multiply, finish):
        if finish:
            scale = lax.rsqrt(ssq_s[...] * (1.0 / d) + EPS)
        ssq = None
        for p in range(OUT_PANELS):
            cols = slice(p * pn, (p + 1) * pn)
            if finish:
                o_ref[:, cols] = x_ref[:, cols] + y_s[:, cols] * scale * nw_ref[:, cols]
            if multiply:
                y = (jnp.dot(ma_ref[...], wa_ref[:, cols], preferred_element_type=jnp.float32)
                     + jnp.dot(mb_ref[...], wb_ref[:, cols], preferred_element_type=jnp.float32))
                y_s[:, cols] = y
                part = jnp.sum(y * y, axis=-1, keepdims=True)
                ssq = part if ssq is None else ssq + part
        if multiply:
            ssq_s[...] = ssq

    pl.when(i == 0)(functools.partial(step, True, False))
    pl.when((i > 0) & (i < n_tiles))(functools.partial(step, True, True))
    pl.when(i == n_tiles)(functools.partial(step, False, True))


def _out_proj(mix_a, mix_b, w, x2d, post_w, tm=256):
    m, d = x2d.shape
    ka, kb = mix_a.shape[1], mix_b.shape[1]
    assert ka == kb and w.shape[0] == ka + kb
    n_tiles = m // tm
    resident = dict(pipeline_mode=pl.Buffered(1))
    this_tile = lambda i: (jnp.minimum(i, n_tiles - 1), 0)
    prev_tile = lambda i: (jnp.maximum(i - 1, 0), 0)
    return pl.pallas_call(
        _out_proj_kernel,
        grid=(n_tiles + 1,),
        in_specs=[
            pl.BlockSpec((tm, ka), this_tile),
            pl.BlockSpec((tm, kb), this_tile),
            pl.BlockSpec((ka, d), lambda i: (0, 0), **resident),
            pl.BlockSpec((kb, d), lambda i: (1, 0), **resident),
            pl.BlockSpec((tm, d), prev_tile),
            pl.BlockSpec((1, d), lambda i: (0, 0)),
        ],
        out_specs=pl.BlockSpec((tm, d), prev_tile),
        out_shape=jax.ShapeDtypeStruct((m, d), jnp.float32),
        scratch_shapes=[pltpu.VMEM((tm, d), jnp.float32),
                        pltpu.VMEM((tm, 1), jnp.float32)],
        compiler_params=pltpu.CompilerParams(dimension_semantics=("arbitrary",),
                                             vmem_limit_bytes=OUT_PROJ_VMEM_LIMIT),
        name="out_proj_norm_residual",
    )(mix_a, mix_b, w, w, x2d, post_w.reshape(1, d))


def _layer(x, positions, rope, pre_w, post_w, w_in, b_gate, pe_k, pe_v, wk1, wk2, wv1, wv2, w_out):
    b, s, d = x.shape
    x2d = x.reshape(b * s, d)
    w_t = w_in.T
    if rope is None:
        h, gates, cos, sin = _rmsnorm(x2d, pre_w, w_t, positions.reshape(b * s))
        rope = (cos.reshape(b, s, HEAD_DIM), sin.reshape(b, s, HEAD_DIM))
    else:
        h, gates = _rmsnorm(x2d, pre_w, w_t)
    cos, sin = rope
    front, back, w_out_bf16 = _in_proj(h, w_t, w_out)

    pe = jnp.stack([pe_k, pe_v])
    w1 = jnp.stack([wk1, wv1]).astype(jnp.bfloat16)
    w2 = jnp.stack([wk2, wv2]).astype(jnp.bfloat16)
    cmp_kv = _compress(back, b, s, pe, w1, w2)

    bg = jnp.pad(b_gate.reshape(1, -1), ((0, 0), (0, HEAD_DIM - b_gate.size)))
    mix_a, mix_b = _mixers(front, back, gates, cmp_kv, bg, cos, sin, _dilated_tables(s), b, s)

    out = _out_proj(mix_a, mix_b, w_out_bf16, x2d, post_w)
    return out.reshape(b, s, d), rope


def kernel(x, positions, pre_norm_w, post_norm_w, w_in, b_gate, cmp_pe_k, cmp_pe_v,
           cmp_wk1, cmp_wk2, cmp_wv1, cmp_wv2, w_out):
    rope = None
    for l in range(pre_norm_w.shape[0]):
        x, rope = _layer(x, positions, rope, pre_norm_w[l], post_norm_w[l], w_in[l], b_gate[l],
                         cmp_pe_k[l], cmp_pe_v[l], cmp_wk1[l], cmp_wk2[l], cmp_wv1[l],
                         cmp_wv2[l], w_out[l])
    return x
```

```python
import functools
import math

import jax
import jax.numpy as jnp
from jax import lax
from jax.experimental import pallas as pl
from jax.experimental.pallas import tpu as pltpu

HEAD_DIM = 128
HALF = HEAD_DIM // 2
N_HEADS_A = 16
N_KV_A = 4
GROUP_A = 4
N_HEADS_B = 16
CMP_LEN = 32
CMP_STRIDE = 16
CMP_HIDDEN = 256
SEL_LEN = 64
TOP_N = 16
WIN = 512
DILATIONS = ((128, 1), (512, 4), (2048, 16))
ROPE_THETA = 10000.0
EPS = 1e-6
FORCE_BONUS = 1e4
NEG = -1e30
SCALE = HEAD_DIM ** -0.5
QSCALE = SCALE * math.log2(math.e)

H_QA = 0
H_ZA = H_QA + N_HEADS_A
N_FRONT_HEADS = H_ZA + N_HEADS_A
GATE_PER_KV = 3 * GROUP_A
H_KC = 0
H_VC = H_KC + N_KV_A
H_KSL = H_VC + N_KV_A
H_VSL = H_KSL + N_KV_A
H_KW = H_VSL + N_KV_A
H_VW = H_KW + N_KV_A
H_QB = H_VW + N_KV_A
H_KB = H_QB + N_HEADS_B
H_VB = H_KB + N_HEADS_B
H_ZB = H_VB + N_HEADS_B
N_BACK_HEADS = H_ZB + N_HEADS_B

TQ = 256
TK = 256
SEL_PER_TK = TK // SEL_LEN
F32_SUBLANES = 8
BF16_SUBLANES = 16
V_ROWS = HEAD_DIM + BF16_SUBLANES
PV_GROUP = 2
MIX_SPLIT = 2
VMEM_LIMIT = 56 * 1024 * 1024
MIXERS_VMEM_LIMIT = 60 * 1024 * 1024
OUT_PROJ_VMEM_LIMIT = 60 * 1024 * 1024


def _cparams(sem):
    return pltpu.CompilerParams(dimension_semantics=sem, vmem_limit_bytes=VMEM_LIMIT)


GATE_ROW0 = 2 * N_HEADS_A * HEAD_DIM
BACK_ROW0 = GATE_ROW0 + 3 * N_HEADS_A


def _nt_dot(x, w):
    return lax.dot_general(x, w, (((1,), (1,)), ((), ())), preferred_element_type=jnp.float32)


def _rmsnorm_kernel(*refs, with_rope):
    if with_rope:
        x_ref, w_ref, wg_ref, pos_ref, inv_ref, o_ref, g_ref, cos_ref, sin_ref = refs
    else:
        x_ref, w_ref, wg_ref, o_ref, g_ref = refs
    x = x_ref[...]
    ms = jnp.mean(x * x, axis=-1, keepdims=True)
    h = (x * lax.rsqrt(ms + EPS) * w_ref[...]).astype(o_ref.dtype)
    o_ref[...] = h
    g_ref[...] = _nt_dot(h, wg_ref[...].astype(jnp.bfloat16))
    if with_rope:
        half_rows = pos_ref.shape[0] // 2
        lane = lax.broadcasted_iota(jnp.int32, (half_rows, HEAD_DIM), 1)
        first = lane < HALF
        pos = jnp.where(first, pos_ref[0:half_rows, :], pos_ref[half_rows:, :]).astype(jnp.float32)
        ang = pos * inv_ref[...]
        cos, sin = jnp.cos(ang), jnp.sin(ang)
        cos_sw, sin_sw = pltpu.roll(cos, HALF, 1), pltpu.roll(sin, HALF, 1)
        cos_ref[0:half_rows, :] = jnp.where(first, cos, cos_sw)
        cos_ref[half_rows:, :] = jnp.where(first, cos_sw, cos)
        sin_ref[0:half_rows, :] = jnp.where(first, -sin, sin_sw)
        sin_ref[half_rows:, :] = jnp.where(first, -sin_sw, sin)


def _rmsnorm(x2d, w, w_t, positions=None, tm=512):
    m, d = x2d.shape
    with_rope = positions is not None
    rows = lambda width: pl.BlockSpec((tm, width), lambda i: (i, 0))
    in_specs = [rows(d), pl.BlockSpec((1, d), lambda i: (0, 0)),
                pl.BlockSpec((HEAD_DIM, d), lambda i: (GATE_ROW0 // HEAD_DIM, 0))]
    out_specs = [rows(d), rows(HEAD_DIM)]
    out_shape = [jax.ShapeDtypeStruct((m, d), jnp.bfloat16),
                 jax.ShapeDtypeStruct((m, HEAD_DIM), jnp.float32)]
    args = [x2d, w.reshape(1, d), w_t]
    if with_rope:
        inv = ROPE_THETA ** (-jnp.arange(HALF, dtype=jnp.float32) / HALF)
        in_specs += [rows(1), pl.BlockSpec((1, HEAD_DIM), lambda i: (0, 0))]
        out_specs += [rows(HEAD_DIM), rows(HEAD_DIM)]
        out_shape += [jax.ShapeDtypeStruct((m, HEAD_DIM), jnp.float32)] * 2
        args += [positions.reshape(m, 1), jnp.concatenate([inv, inv]).reshape(1, HEAD_DIM)]
    return pl.pallas_call(
        functools.partial(_rmsnorm_kernel, with_rope=with_rope),
        grid=(m // tm,),
        in_specs=in_specs,
        out_specs=out_specs,
        out_shape=out_shape,
        compiler_params=_cparams(("parallel",)),
        name="pre_rmsnorm_gates",
    )(*args)


PROJ_TN = GROUP_A * HEAD_DIM
PROJ_FRONT = GATE_ROW0 // PROJ_TN


WOUT_CAST_ROWS = 64


def _in_proj_kernel(x_ref, w_ref, wo_ref, front_ref, back_ref, wo_bf16_ref):
    j = pl.program_id(1)

    def project(out_ref):
        wo_bf16_ref[...] = wo_ref[...].astype(wo_bf16_ref.dtype)
        out_ref[...] = _nt_dot(x_ref[...], w_ref[...].astype(jnp.bfloat16)).astype(out_ref.dtype)

    pl.when(j < PROJ_FRONT)(functools.partial(project, front_ref))
    pl.when(j >= PROJ_FRONT)(functools.partial(project, back_ref))


def _in_proj(h, w_t, w_out, tm=2048):
    m, k = h.shape
    n_back = N_BACK_HEADS * HEAD_DIM
    n_steps = PROJ_FRONT + n_back // PROJ_TN
    wo_blocks = w_out.shape[0] // WOUT_CAST_ROWS
    assert wo_blocks <= (m // tm) * n_steps

    def wo_block(i, j):
        return jnp.minimum(i * n_steps + j, wo_blocks - 1), 0

    def w_row(i, j):
        row = PROJ_TN * j + jnp.where(j >= PROJ_FRONT, BACK_ROW0 - GATE_ROW0, 0)
        return pl.multiple_of(row, F32_SUBLANES), 0

    wo_spec = pl.BlockSpec((WOUT_CAST_ROWS, w_out.shape[1]), wo_block)
    return pl.pallas_call(
        _in_proj_kernel,
        grid=(m // tm, n_steps),
        in_specs=[
            pl.BlockSpec((tm, k), lambda i, j: (i, 0), pipeline_mode=pl.Buffered(1)),
            pl.BlockSpec((pl.Element(PROJ_TN), pl.Element(k)), w_row),
            wo_spec],
        out_specs=[
            pl.BlockSpec((tm, PROJ_TN), lambda i, j: (i, jnp.minimum(j, PROJ_FRONT - 1))),
            pl.BlockSpec((tm, PROJ_TN), lambda i, j: (i, jnp.maximum(j - PROJ_FRONT, 0))),
            wo_spec],
        out_shape=[jax.ShapeDtypeStruct((m, N_FRONT_HEADS * HEAD_DIM), jnp.bfloat16),
                   jax.ShapeDtypeStruct((m, n_back), jnp.bfloat16),
                   jax.ShapeDtypeStruct(w_out.shape, jnp.bfloat16)],
        compiler_params=_cparams(("arbitrary", "arbitrary")),
        name="in_proj",
    )(h, w_t, w_out)


def _rope(xf, cos, sin_signed):
    return xf * cos + pltpu.roll(xf, HALF, 1) * sin_signed


CMP_PITCH = CMP_STRIDE + 4


def _compress_kernel(x_ref, pe_ref, w1_ref, w2_ref, o_ref, xs_ref):
    s = x_ref.shape[0]
    n_rows = s // CMP_STRIDE
    for hh in range(N_KV_A):
        cols = slice(hh * HEAD_DIM, (hh + 1) * HEAD_DIM)
        for g in range(n_rows):
            xs_ref[hh, pl.ds(g * CMP_PITCH, CMP_STRIDE), :] = (
                x_ref[pl.ds(g * CMP_STRIDE, CMP_STRIDE), cols].astype(jnp.float32))
        xs_ref[hh, pl.ds(n_rows * CMP_PITCH, CMP_STRIDE), :] = jnp.zeros(
            (CMP_STRIDE, HEAD_DIM), jnp.float32)

    def block_rows(hh, r):
        start = (r // CMP_STRIDE) * CMP_PITCH + r % CMP_STRIDE
        return xs_ref[hh, pl.ds(start, n_rows, stride=CMP_PITCH), :]

    flat = jnp.concatenate(
        [jnp.concatenate([(block_rows(hh, r) + pe_ref[pl.ds(r, 1), :]).astype(jnp.bfloat16)
                          for r in range(CMP_LEN)], axis=1)
         for hh in range(N_KV_A)], axis=0)
    hid = jax.nn.gelu(jnp.dot(flat, w1_ref[...], preferred_element_type=jnp.float32))
    out = jnp.dot(hid.astype(jnp.bfloat16), w2_ref[...], preferred_element_type=jnp.float32)
    for hh in range(N_KV_A):
        o_ref[hh] = out[hh * n_rows:(hh + 1) * n_rows, :].astype(o_ref.dtype)


def _compress(proj, batch, s, pe, w1, w2):
    n_rows = s // CMP_STRIDE
    assert H_KC % N_KV_A == 0 and H_VC == H_KC + N_KV_A
    return pl.pallas_call(
        _compress_kernel,
        grid=(batch, 2),
        in_specs=[
            pl.BlockSpec((s, N_KV_A * HEAD_DIM), lambda b, kv: (b, H_KC // N_KV_A + kv)),
            pl.BlockSpec((None, CMP_LEN, HEAD_DIM), lambda b, kv: (kv, 0, 0)),
            pl.BlockSpec((None, CMP_LEN * HEAD_DIM, CMP_HIDDEN), lambda b, kv: (kv, 0, 0)),
            pl.BlockSpec((None, CMP_HIDDEN, HEAD_DIM), lambda b, kv: (kv, 0, 0)),
        ],
        out_specs=pl.BlockSpec((None, N_KV_A, n_rows, HEAD_DIM), lambda b, kv: (b, kv, 0, 0)),
        out_shape=jax.ShapeDtypeStruct((batch, 2 * N_KV_A, n_rows, HEAD_DIM), jnp.bfloat16),
        scratch_shapes=[pltpu.VMEM((N_KV_A, (n_rows + 1) * CMP_PITCH, HEAD_DIM), jnp.float32)],
        compiler_params=_cparams(("parallel", "parallel")),
        name="compress",
    )(proj, pe, w1, w2)


def _kq(k, q):
    return lax.dot_general(k, q, (((1,), (1,)), ((), ())),
                           preferred_element_type=jnp.float32)


def _memo(fn):
    cache = []

    def get():
        if not cache:
            cache.append(fn())
        return cache[0]
    return get


class _AttnJob:
    def __init__(self, q_fn, chunks):
        self.q_fn, self.chunks = q_fn, chunks
        self.s, self.bias, self.m, self.acc = [], [], None, None

    def _score(self, c):
        k_fn, _, mask_fn = self.chunks[c][:3]
        s = mask_fn(_kq(k_fn(), self.q_fn()))
        bias = self.chunks[c][3]() if len(self.chunks[c]) > 3 else None
        self.s.append(s)
        self.bias.append(bias)
        if bias is None:
            cm = jnp.max(s, axis=0, keepdims=True)
        else:
            sub = s.shape[0] // len(bias)
            cm = functools.reduce(jnp.maximum, [
                jnp.max(s[a * sub:(a + 1) * sub, :], axis=0, keepdims=True) + b
                for a, b in enumerate(bias)])
        self.m = cm if self.m is None else jnp.maximum(self.m, cm)

    def _probabilities(self, c):
        s, bias = self.s[c], self.bias[c]
        if bias is None:
            return jnp.exp2(s - self.m).astype(jnp.bfloat16)
        sub = s.shape[0] // len(bias)
        return jnp.concatenate([jnp.exp2(s[a * sub:(a + 1) * sub, :] + (b - self.m))
                                for a, b in enumerate(bias)], axis=0).astype(jnp.bfloat16)

    def _value(self, cs):
        p = jnp.concatenate([self._probabilities(c) for c in cs], axis=0)
        vt = jnp.concatenate([self.chunks[c][1]() for c in cs], axis=1)
        part = jnp.dot(vt, p, preferred_element_type=jnp.float32)
        self.acc = part if self.acc is None else self.acc + part

    def score_tasks(self):
        return [functools.partial(self._score, c) for c in range(len(self.chunks))]

    def value_tasks(self):
        n = len(self.chunks)
        return [functools.partial(self._value, range(c, min(c + PV_GROUP, n)))
                for c in range(0, n, PV_GROUP)]

    def result(self, weight=None):
        inv_l = 1.0 / self.acc[HEAD_DIM:HEAD_DIM + 1, :]
        return self.acc[:HEAD_DIM, :] * (inv_l if weight is None else weight * inv_l)


def _round_robin(task_lists):
    for i in range(max(len(t) for t in task_lists)):
        for tasks in task_lists:
            if i < len(tasks):
                tasks[i]()


def _run_pipelined(streams, on_done):
    n = len(streams[0])
    _round_robin([jobs[0].score_tasks() for jobs in streams])
    finish = []
    for k in range(n):
        lists = [finish] if finish else []
        for jobs in streams:
            lists.append(jobs[k].value_tasks())
            if k + 1 < n:
                lists.append(jobs[k + 1].score_tasks())
        _round_robin(lists)
        finish = [functools.partial(on_done, si, k, jobs[k]) for si, jobs in enumerate(streams)]
    _round_robin([finish])


def _store_values_transposed(src_ref, dst_ref):
    s_len = src_ref.shape[0]
    for c in range(s_len // TK):
        blk = src_ref[c * TK:(c + 1) * TK, :].astype(jnp.float32)
        dst_ref[0:HEAD_DIM, c * TK:(c + 1) * TK] = blk.T.astype(jnp.bfloat16)
    row = lax.broadcasted_iota(jnp.int32, (BF16_SUBLANES, s_len), 0)
    dst_ref[HEAD_DIM:, :] = jnp.where(row == 0, 1.0, 0.0).astype(jnp.bfloat16)


def _silu(z):
    return z * jax.nn.sigmoid(z)


def _topk_active(q0):
    return q0 + TQ > TOP_N * SEL_LEN


def _nsa_selection_bias(p_sum, q0, s_len):
    n_sel = s_len // SEL_LEN
    n_cmp = s_len // CMP_STRIDE - CMP_LEN // CMP_STRIDE + 1
    oj = lax.broadcasted_iota(jnp.int32, (n_sel, HEAD_DIM), 0)
    oc = lax.broadcasted_iota(jnp.int32, (n_sel, HEAD_DIM), 1)
    ovl = jnp.where(oc * CMP_STRIDE < oj * SEL_LEN + SEL_LEN,
                    jnp.where(oc * CMP_STRIDE + CMP_LEN > oj * SEL_LEN, 1.0, 0.0), 0.0)
    ovl = jnp.where(oc < n_cmp, ovl, 0.0).astype(jnp.bfloat16)
    p_hi = p_sum.astype(jnp.bfloat16)
    p_lo = (p_sum - p_hi.astype(jnp.float32)).astype(jnp.bfloat16)
    imp = (jnp.dot(ovl, p_hi, preferred_element_type=jnp.float32)
           + jnp.dot(ovl, p_lo, preferred_element_type=jnp.float32))
    j_row = lax.broadcasted_iota(jnp.int32, (n_sel, TQ), 0)
    t_sel = q0 + lax.broadcasted_iota(jnp.int32, (n_sel, TQ), 1)
    cur = lax.shift_right_arithmetic(t_sel, int(math.log2(SEL_LEN)))
    forced = (j_row == 0) | (j_row == cur) | (j_row == cur - 1)
    imp = jnp.where(forced, imp + FORCE_BONUS, imp)
    imp = jnp.where(j_row * SEL_LEN <= t_sel, imp, -jnp.inf)
    rank = jnp.zeros((n_sel, TQ), jnp.float32)
    for i in range(n_sel):
        vi = imp[i:i + 1, :]
        rank = rank + jnp.where(j_row > i, jnp.where(vi >= imp, 1.0, 0.0),
                                jnp.where(vi > imp, 1.0, 0.0))
    return jnp.where(rank < TOP_N, 0.0, NEG)


def _mixers_kernel(q_ref, z_ref, g_ref, bg_ref, kc_ref, vc_ref, ksl_ref, vsl_ref, kw_ref, vw_ref,
                   cos_ref, sin_ref, qd_ref, zd_ref, kd_ref, vd_ref, dbias_ref, o_ref, od_ref,
                   ksl_rot, kw_rot, vsl_t, vw_t, ocmp_s, selb_s, gate_s):
    s_len = ksl_ref.shape[0]
    n_q = s_len // TQ

    ksl_rot[...] = _rope(ksl_ref[...].astype(jnp.float32), cos_ref[...], sin_ref[...]).astype(jnp.bfloat16)
    kw_rot[...] = _rope(kw_ref[...].astype(jnp.float32), cos_ref[...], sin_ref[...]).astype(jnp.bfloat16)
    _store_values_transposed(vsl_ref, vsl_t)
    _store_values_transposed(vw_ref, vw_t)
    kc = (kc_ref[...].astype(jnp.float32) * QSCALE).astype(jnp.bfloat16)
    vc_t = vc_ref[...].astype(jnp.float32).T.astype(jnp.bfloat16)
    bg = bg_ref[...]
    kv_head = pl.program_id(1)

    c_row = lax.broadcasted_iota(jnp.int32, (HEAD_DIM, TQ), 0)
    lane_q = lax.broadcasted_iota(jnp.int32, (HEAD_DIM, TQ), 1)
    lane_one = lax.broadcasted_iota(jnp.int32, (1, TQ), 1)
    for qi in range(n_q):
        q0 = qi * TQ
        rows = slice(q0, q0 + TQ)
        n_vis = min(HEAD_DIM, (q0 + TQ) // CMP_STRIDE)
        hidden = jnp.zeros((HEAD_DIM - n_vis, TQ), jnp.float32)
        c_valid = (c_row * CMP_STRIDE + (CMP_LEN - 1) <= q0 + lane_q)[:n_vis, :]
        any_valid = jnp.where(q0 + lane_one >= CMP_LEN - 1, 1.0, 0.0)
        q4 = jnp.concatenate([q_ref[rows, g * HEAD_DIM:(g + 1) * HEAD_DIM]
                              for g in range(GROUP_A)], axis=0)
        sc4 = _kq(kc[:n_vis, :], q4)
        p_heads = []
        p_sum = jnp.zeros((HEAD_DIM, TQ), jnp.float32)
        for g in range(GROUP_A):
            sc = jnp.where(c_valid, sc4[:, g * TQ:(g + 1) * TQ], NEG)
            e = jnp.exp2(sc - jnp.max(sc, axis=0, keepdims=True))
            p = e * (any_valid / jnp.sum(e, axis=0, keepdims=True))
            if n_vis < HEAD_DIM:
                p = jnp.concatenate([p, hidden], axis=0)
            p_heads.append(p.astype(jnp.bfloat16))
            p_sum = p_sum + p
        o_cmp4 = jnp.dot(vc_t, jnp.concatenate(p_heads, axis=1),
                         preferred_element_type=jnp.float32)
        for g in range(GROUP_A):
            ocmp_s[g, :, rows] = o_cmp4[:, g * TQ:(g + 1) * TQ]
        if _topk_active(q0):
            selb_s[qi] = _nsa_selection_bias(p_sum, q0, s_len)
        gate_s[:, rows] = jax.nn.sigmoid(g_ref[rows, :] + bg).T

    diag_rc = (lax.broadcasted_iota(jnp.int32, (TK, TQ), 0)
               - lax.broadcasted_iota(jnp.int32, (TK, TQ), 1))

    def slc_mask(qi, c, s):
        if c == qi:
            s = jnp.where(diag_rc <= 0, s, NEG)
        return s

    def slc_row_bias(qi, c):
        return [selb_s[qi, c * SEL_PER_TK + a:c * SEL_PER_TK + a + 1, :] for a in range(SEL_PER_TK)]

    def win_mask(qi, c, s):
        if c == qi:
            return jnp.where(diag_rc <= 0, s, NEG)
        if (qi - c) * TK + TQ - 1 > WIN - 1:
            return jnp.where(diag_rc >= (qi - c) * TK - (WIN - 1), s, NEG)
        return s

    ones_rows = jnp.where(lax.broadcasted_iota(jnp.int32, (BF16_SUBLANES, TK), 0) == 0,
                          1.0, 0.0).astype(jnp.bfloat16)

    def head_body(g, carry):
        col = pl.ds(pl.multiple_of(g * HEAD_DIM, HEAD_DIM), HEAD_DIM)
        gate_row = GATE_PER_KV * kv_head + 3 * g
        blocks = [slice(c * TK, (c + 1) * TK) for c in range(n_q)]

        def roped(ref, rows, scale):
            xf = ref[rows, col].astype(jnp.float32)
            if scale != 1.0:
                xf = xf * scale
            return _rope(xf, cos_ref[rows, :], sin_ref[rows, :]).astype(jnp.bfloat16)

        def dil_value_t(rows):
            vt = vd_ref[rows, col].astype(jnp.float32).T.astype(jnp.bfloat16)
            return jnp.concatenate([vt, ones_rows], axis=0)

        def dil_bias(delta, s):
            r0 = (n_q - 1 - delta) * TK
            return s + dbias_ref[r0:r0 + TK, :]

        dil_k = [_memo(functools.partial(roped, kd_ref, rows, 1.0)) for rows in blocks]
        dil_v = [_memo(functools.partial(dil_value_t, rows)) for rows in blocks]
        slc_jobs, win_jobs, dil_jobs = [], [], []
        for qi in range(n_q):
            q = _memo(functools.partial(roped, q_ref, blocks[qi], QSCALE))
            first_win = max(0, (qi * TQ - (WIN - 1)) // TK)
            ranked = _topk_active(qi * TQ)
            for jobs, k_ref_, v_ref_, mask, row_bias, c_lo in (
                    (slc_jobs, ksl_rot, vsl_t, slc_mask, slc_row_bias if ranked else None, 0),
                    (win_jobs, kw_rot, vw_t, win_mask, None, first_win)):
                jobs.append(_AttnJob(q, [
                    (functools.partial(lambda r, c: r[c * TK:(c + 1) * TK, :], k_ref_, c),
                     functools.partial(lambda r, c: r[:, c * TK:(c + 1) * TK], v_ref_, c),
                     functools.partial(mask, qi, c))
                    + ((functools.partial(row_bias, qi, c),) if row_bias else ())
                    for c in range(c_lo, qi + 1)]))
            dil_jobs.append(_AttnJob(
                _memo(functools.partial(roped, qd_ref, blocks[qi], QSCALE)),
                [(dil_k[c], dil_v[c], functools.partial(dil_bias, qi - c)) for c in range(qi + 1)]))

        def on_done(si, k, job):
            kind, qi = si % 3, MIX_SPLIT * k + si // 3
            rows = blocks[qi]
            if kind == 1:
                o_t = (gate_s[pl.ds(gate_row, 1), rows] * ocmp_s[g, :, rows]
                       + slc_jobs[qi].result(gate_s[pl.ds(gate_row + 1, 1), rows])
                       + job.result(gate_s[pl.ds(gate_row + 2, 1), rows]))
                z = z_ref[rows, col].astype(jnp.float32)
                o_ref[rows, col] = (o_t.T * _silu(z)).astype(o_ref.dtype)
            elif kind == 2:
                z = zd_ref[rows, col].astype(jnp.float32)
                od_ref[rows, col] = (job.result().T * _silu(z)).astype(od_ref.dtype)

        streams = []
        for r in range(MIX_SPLIT):
            streams += [slc_jobs[r::MIX_SPLIT], win_jobs[r::MIX_SPLIT], dil_jobs[r::MIX_SPLIT]]
        _run_pipelined(streams, on_done)
        return carry

    lax.fori_loop(0, GROUP_A, head_body, 0)


def _mixers(front, back, gates, cmp_kv, b_gate_row, cos, sin, dil_bias, batch, s):
    gw = GROUP_A * HEAD_DIM
    assert N_HEADS_B == N_HEADS_A
    full = lambda col0: pl.BlockSpec((s, HEAD_DIM), lambda b, h: (b, col0 + h))
    group = lambda head0: pl.BlockSpec((s, gw), lambda b, h: (b, head0 // GROUP_A + h))
    out = jax.ShapeDtypeStruct((batch * s, N_HEADS_A * HEAD_DIM), jnp.bfloat16)
    rarely_changes = dict(pipeline_mode=pl.Buffered(1))
    return pl.pallas_call(
        _mixers_kernel,
        grid=(batch, N_KV_A),
        in_specs=[
            group(H_QA), group(H_ZA),
            pl.BlockSpec((s, HEAD_DIM), lambda b, h: (b, 0)),
            pl.BlockSpec((1, HEAD_DIM), lambda b, h: (0, 0)),
            pl.BlockSpec((None, None, s // CMP_STRIDE, HEAD_DIM), lambda b, h: (b, h, 0, 0)),
            pl.BlockSpec((None, None, s // CMP_STRIDE, HEAD_DIM), lambda b, h: (b, N_KV_A + h, 0, 0)),
            full(H_KSL), full(H_VSL), full(H_KW), full(H_VW),
            pl.BlockSpec((None, s, HEAD_DIM), lambda b, h: (b, 0, 0), **rarely_changes),
            pl.BlockSpec((None, s, HEAD_DIM), lambda b, h: (b, 0, 0), **rarely_changes),
            group(H_QB), group(H_ZB), group(H_KB), group(H_VB),
            pl.BlockSpec((s, TQ), lambda b, h: (0, 0), **rarely_changes),
        ],
        out_specs=[pl.BlockSpec((s, gw), lambda b, h: (b, h)),
                   pl.BlockSpec((s, gw), lambda b, h: (b, h))],
        out_shape=[out, out],
        scratch_shapes=[pltpu.VMEM((s, HEAD_DIM), jnp.bfloat16),
                        pltpu.VMEM((s, HEAD_DIM), jnp.bfloat16),
                        pltpu.VMEM((V_ROWS, s), jnp.bfloat16),
                        pltpu.VMEM((V_ROWS, s), jnp.bfloat16),
                        pltpu.VMEM((GROUP_A, HEAD_DIM, s), jnp.float32),
                        pltpu.VMEM((s // TQ, s // SEL_LEN, TQ), jnp.float32),
                        pltpu.VMEM((HEAD_DIM, s), jnp.float32)],
        compiler_params=pltpu.CompilerParams(dimension_semantics=("parallel", "parallel"),
                                             vmem_limit_bytes=MIXERS_VMEM_LIMIT),
        name="token_mixers",
    )(front, front, gates, b_gate_row, cmp_kv, cmp_kv, back, back, back, back, cos, sin,
      back, back, back, back, dil_bias)


def _dilated_multiplicity(dist):
    cnt = jnp.zeros(dist.shape, jnp.float32)
    for window, dil in DILATIONS:
        cnt = cnt + jnp.where((dist & (dil - 1)) == 0,
                              jnp.where(dist <= (window // dil) * dil, 1.0, 0.0), 0.0)
    return jnp.where(dist >= 0, cnt, 0.0)


def _dilated_table_kernel(bias_ref):
    s_len = bias_ref.shape[0]
    dist = (s_len - TQ + lax.broadcasted_iota(jnp.int32, (s_len, TQ), 1)
            - lax.broadcasted_iota(jnp.int32, (s_len, TQ), 0))
    cnt = _dilated_multiplicity(dist)
    bias_ref[...] = jnp.where(cnt > 0.5, jnp.log2(jnp.maximum(cnt, 1.0)), NEG)


def _dilated_tables(s):
    return pl.pallas_call(
        _dilated_table_kernel,
        out_shape=jax.ShapeDtypeStruct((s, TQ), jnp.float32),
        compiler_params=pltpu.CompilerParams(vmem_limit_bytes=VMEM_LIMIT),
        name="dilated_tables",
    )()


OUT_PANELS = 8


def _out_proj_kernel(ma_ref, mb_ref, wa_ref, wb_ref, x_ref, nw_ref, o_ref, y_s, ssq_s):
    i = pl.program_id(0)
    n_tiles = pl.num_programs(0) - 1
    d = o_ref.shape[1]
    pn = d // OUT_PANELS

    def step(multiply, finish):
        if finish:
            scale = lax.rsqrt(ssq_s[...] * (1.0 / d) + EPS)
        ssq = None
        for p in range(OUT_PANELS):
            cols = slice(p * pn, (p + 1) * pn)
            if finish:
                o_ref[:, cols] = x_ref[:, cols] + y_s[:, cols] * scale * nw_ref[:, cols]
            if multiply:
                y = (jnp.dot(ma_ref[...], wa_ref[:, cols], preferred_element_type=jnp.float32)
                     + jnp.dot(mb_ref[...], wb_ref[:, cols], preferred_element_type=jnp.float32))
                y_s[:, cols] = y
                part = jnp.sum(y * y, axis=-1, keepdims=True)
                ssq = part if ssq is None else ssq + part
        if multiply:
            ssq_s[...] = ssq

    pl.when(i == 0)(functools.partial(step, True, False))
    pl.when((i > 0) & (i < n_tiles))(functools.partial(step, True, True))
    pl.when(i == n_tiles)(functools.partial(step, False, True))


def _out_proj(mix_a, mix_b, w, x2d, post_w, tm=256):
    m, d = x2d.shape
    ka, kb = mix_a.shape[1], mix_b.shape[1]
    assert ka == kb and w.shape[0] == ka + kb
    n_tiles = m // tm
    resident = dict(pipeline_mode=pl.Buffered(1))
    this_tile = lambda i: (jnp.minimum(i, n_tiles - 1), 0)
    prev_tile = lambda i: (jnp.maximum(i - 1, 0), 0)
    return pl.pallas_call(
        _out_proj_kernel,
        grid=(n_tiles + 1,),
        in_specs=[
            pl.BlockSpec((tm, ka), this_tile),
            pl.BlockSpec((tm, kb), this_tile),
            pl.BlockSpec((ka, d), lambda i: (0, 0), **resident),
            pl.BlockSpec((kb, d), lambda i: (1, 0), **resident),
            pl.BlockSpec((tm, d), prev_tile),
            pl.BlockSpec((1, d), lambda i: (0, 0)),
        ],
        out_specs=pl.BlockSpec((tm, d), prev_tile),
        out_shape=jax.ShapeDtypeStruct((m, d), jnp.float32),
        scratch_shapes=[pltpu.VMEM((tm, d), jnp.float32),
                        pltpu.VMEM((tm, 1), jnp.float32)],
        compiler_params=pltpu.CompilerParams(dimension_semantics=("arbitrary",),
                                             vmem_limit_bytes=OUT_PROJ_VMEM_LIMIT),
        name="out_proj_norm_residual",
    )(mix_a, mix_b, w, w, x2d, post_w.reshape(1, d))


def _layer(x, positions, rope, pre_w, post_w, w_in, b_gate, pe_k, pe_v, wk1, wk2, wv1, wv2, w_out):
    b, s, d = x.shape
    x2d = x.reshape(b * s, d)
    w_t = w_in.T
    if rope is None:
        h, gates, cos, sin = _rmsnorm(x2d, pre_w, w_t, positions.reshape(b * s))
        rope = (cos.reshape(b, s, HEAD_DIM), sin.reshape(b, s, HEAD_DIM))
    else:
        h, gates = _rmsnorm(x2d, pre_w, w_t)
    cos, sin = rope
    front, back, w_out_bf16 = _in_proj(h, w_t, w_out)

    pe = jnp.stack([pe_k, pe_v])
    w1 = jnp.stack([wk1, wv1]).astype(jnp.bfloat16)
    w2 = jnp.stack([wk2, wv2]).astype(jnp.bfloat16)
    cmp_kv = _compress(back, b, s, pe, w1, w2)

    bg = jnp.pad(b_gate.reshape(1, -1), ((0, 0), (0, HEAD_DIM - b_gate.size)))
    mix_a, mix_b = _mixers(front, back, gates, cmp_kv, bg, cos, sin, _dilated_tables(s), b, s)

    out = _out_proj(mix_a, mix_b, w_out_bf16, x2d, post_w)
    return out.reshape(b, s, d), rope


def kernel(x, positions, pre_norm_w, post_norm_w, w_in, b_gate, cmp_pe_k, cmp_pe_v,
           cmp_wk1, cmp_wk2, cmp_wv1, cmp_wv2, w_out):
    rope = None
    for l in range(pre_norm_w.shape[0]):
        x, rope = _layer(x, positions, rope, pre_norm_w[l], post_norm_w[l], w_in[l], b_gate[l],
                         cmp_pe_k[l], cmp_pe_v[l], cmp_wk1[l], cmp_wk2[l], cmp_wv1[l],
                         cmp_wv2[l], w_out[l])
    return x
```

```python
import functools
import math

import jax
import jax.numpy as jnp
from jax import lax
from jax.experimental import pallas as pl
from jax.experimental.pallas import tpu as pltpu

HEAD_DIM = 128
HALF = HEAD_DIM // 2
N_HEADS_A = 16
N_KV_A = 4
GROUP_A = 4
N_HEADS_B = 16
CMP_LEN = 32
CMP_STRIDE = 16
CMP_HIDDEN = 256
SEL_LEN = 64
TOP_N = 16
WIN = 512
DILATIONS = ((128, 1), (512, 4), (2048, 16))
ROPE_THETA = 10000.0
EPS = 1e-6
FORCE_BONUS = 1e4
NEG = -1e30
SCALE = HEAD_DIM ** -0.5
QSCALE = SCALE * math.log2(math.e)

H_QA = 0
H_ZA = H_QA + N_HEADS_A
N_FRONT_HEADS = H_ZA + N_HEADS_A
GATE_PER_KV = 3 * GROUP_A
H_KC = 0
H_VC = H_KC + N_KV_A
H_KSL = H_VC + N_KV_A
H_VSL = H_KSL + N_KV_A
H_KW = H_VSL + N_KV_A
H_VW = H_KW + N_KV_A
H_QB = H_VW + N_KV_A
H_KB = H_QB + N_HEADS_B
H_VB = H_KB + N_HEADS_B
H_ZB = H_VB + N_HEADS_B
N_BACK_HEADS = H_ZB + N_HEADS_B

TQ = 256
TK = 256
SEL_PER_TK = TK // SEL_LEN
F32_SUBLANES = 8
BF16_SUBLANES = 16
V_ROWS = HEAD_DIM + BF16_SUBLANES
PV_GROUP = 2
MIX_SPLIT = 2
VMEM_LIMIT = 56 * 1024 * 1024
MIXERS_VMEM_LIMIT = 60 * 1024 * 1024
OUT_PROJ_VMEM_LIMIT = 60 * 1024 * 1024


def _cparams(sem):
    return pltpu.CompilerParams(dimension_semantics=sem, vmem_limit_bytes=VMEM_LIMIT)


GATE_ROW0 = 2 * N_HEADS_A * HEAD_DIM
BACK_ROW0 = GATE_ROW0 + 3 * N_HEADS_A


def _nt_dot(x, w):
    return lax.dot_general(x, w, (((1,), (1,)), ((), ())), preferred_element_type=jnp.float32)


N_XBUF = 3


def _rmsnorm_kernel(*refs, with_rope):
    x_hbm, xbuf, xsem = refs[0], refs[-2], refs[-1]
    refs = refs[1:-2]
    if with_rope:
        w_ref, wg_ref, pos_ref, inv_ref, o_ref, g_ref, cos_ref, sin_ref = refs
    else:
        w_ref, wg_ref, o_ref, g_ref = refs
    i, n = pl.program_id(0), pl.num_programs(0)
    tm = xbuf.shape[1]

    def x_copy(tile, slot):
        return pltpu.make_async_copy(x_hbm.at[pl.ds(pl.multiple_of(tile * tm, tm), tm), :],
                                     xbuf.at[slot], xsem.at[slot])

    @pl.when(i == 0)
    def _():
        for t in range(N_XBUF):
            x_copy(t, t).start()

    slot = lax.rem(i, N_XBUF)
    x_copy(i, slot).wait()
    x = xbuf[slot]
    ms = jnp.mean(x * x, axis=-1, keepdims=True)
    h = (x * lax.rsqrt(ms + EPS) * w_ref[...]).astype(o_ref.dtype)
    o_ref[...] = h

    @pl.when(i + N_XBUF < n)
    def _():
        x_copy(i + N_XBUF, slot).start()

    g_ref[...] = _nt_dot(h, wg_ref[...].astype(jnp.bfloat16))
    if with_rope:
        half_rows = pos_ref.shape[0] // 2
        lane = lax.broadcasted_iota(jnp.int32, (half_rows, HEAD_DIM), 1)
        first = lane < HALF
        pos = jnp.where(first, pos_ref[0:half_rows, :], pos_ref[half_rows:, :]).astype(jnp.float32)
        ang = pos * inv_ref[...]
        cos, sin = jnp.cos(ang), jnp.sin(ang)
        cos_sw, sin_sw = pltpu.roll(cos, HALF, 1), pltpu.roll(sin, HALF, 1)
        cos_ref[0:half_rows, :] = jnp.where(first, cos, cos_sw)
        cos_ref[half_rows:, :] = jnp.where(first, cos_sw, cos)
        sin_ref[0:half_rows, :] = jnp.where(first, -sin, sin_sw)
        sin_ref[half_rows:, :] = jnp.where(first, -sin_sw, sin)


def _rmsnorm(x2d, w, w_t, positions=None, tm=512):
    m, d = x2d.shape
    with_rope = positions is not None
    rows = lambda width: pl.BlockSpec((tm, width), lambda i: (i, 0))
    assert m // tm >= N_XBUF
    in_specs = [pl.BlockSpec(memory_space=pl.ANY), pl.BlockSpec((1, d), lambda i: (0, 0)),
                pl.BlockSpec((HEAD_DIM, d), lambda i: (GATE_ROW0 // HEAD_DIM, 0))]
    out_specs = [rows(d), rows(HEAD_DIM)]
    out_shape = [jax.ShapeDtypeStruct((m, d), jnp.bfloat16),
                 jax.ShapeDtypeStruct((m, HEAD_DIM), jnp.float32)]
    args = [x2d, w.reshape(1, d), w_t]
    if with_rope:
        inv = ROPE_THETA ** (-jnp.arange(HALF, dtype=jnp.float32) / HALF)
        in_specs += [rows(1), pl.BlockSpec((1, HEAD_DIM), lambda i: (0, 0))]
        out_specs += [rows(HEAD_DIM), rows(HEAD_DIM)]
        out_shape += [jax.ShapeDtypeStruct((m, HEAD_DIM), jnp.float32)] * 2
        args += [positions.reshape(m, 1), jnp.concatenate([inv, inv]).reshape(1, HEAD_DIM)]
    return pl.pallas_call(
        functools.partial(_rmsnorm_kernel, with_rope=with_rope),
        grid=(m // tm,),
        in_specs=in_specs,
        out_specs=out_specs,
        out_shape=out_shape,
        scratch_shapes=[pltpu.VMEM((N_XBUF, tm, d), jnp.float32),
                        pltpu.SemaphoreType.DMA((N_XBUF,))],
        compiler_params=_cparams(("arbitrary",)),
        name="pre_rmsnorm_gates",
    )(*args)


PROJ_TN = GROUP_A * HEAD_DIM
PROJ_FRONT = GATE_ROW0 // PROJ_TN


WOUT_CAST_ROWS = 64


def _in_proj_kernel(x_ref, w_ref, wo_ref, front_ref, back_ref, wo_bf16_ref):
    j = pl.program_id(1)

    def project(out_ref):
        wo_bf16_ref[...] = wo_ref[...].astype(wo_bf16_ref.dtype)
        out_ref[...] = _nt_dot(x_ref[...], w_ref[...].astype(jnp.bfloat16)).astype(out_ref.dtype)

    pl.when(j < PROJ_FRONT)(functools.partial(project, front_ref))
    pl.when(j >= PROJ_FRONT)(functools.partial(project, back_ref))


def _in_proj(h, w_t, w_out, tm=2048):
    m, k = h.shape
    n_back = N_BACK_HEADS * HEAD_DIM
    n_steps = PROJ_FRONT + n_back // PROJ_TN
    wo_blocks = w_out.shape[0] // WOUT_CAST_ROWS
    assert wo_blocks <= (m // tm) * n_steps

    def wo_block(i, j):
        return jnp.minimum(i * n_steps + j, wo_blocks - 1), 0

    def w_row(i, j):
        row = PROJ_TN * j + jnp.where(j >= PROJ_FRONT, BACK_ROW0 - GATE_ROW0, 0)
        return pl.multiple_of(row, F32_SUBLANES), 0

    wo_spec = pl.BlockSpec((WOUT_CAST_ROWS, w_out.shape[1]), wo_block)
    return pl.pallas_call(
        _in_proj_kernel,
        grid=(m // tm, n_steps),
        in_specs=[
            pl.BlockSpec((tm, k), lambda i, j: (i, 0), pipeline_mode=pl.Buffered(1)),
            pl.BlockSpec((pl.Element(PROJ_TN), pl.Element(k)), w_row),
            wo_spec],
        out_specs=[
            pl.BlockSpec((tm, PROJ_TN), lambda i, j: (i, jnp.minimum(j, PROJ_FRONT - 1))),
            pl.BlockSpec((tm, PROJ_TN), lambda i, j: (i, jnp.maximum(j - PROJ_FRONT, 0))),
            wo_spec],
        out_shape=[jax.ShapeDtypeStruct((m, N_FRONT_HEADS * HEAD_DIM), jnp.bfloat16),
                   jax.ShapeDtypeStruct((m, n_back), jnp.bfloat16),
                   jax.ShapeDtypeStruct(w_out.shape, jnp.bfloat16)],
        compiler_params=_cparams(("arbitrary", "arbitrary")),
        name="in_proj",
    )(h, w_t, w_out)


def _rope(xf, cos, sin_signed):
    return xf * cos + pltpu.roll(xf, HALF, 1) * sin_signed


CMP_PITCH = CMP_STRIDE + 4


def _compress_kernel(x_ref, pe_ref, w1_ref, w2_ref, o_ref, xs_ref):
    s = x_ref.shape[0]
    n_rows = s // CMP_STRIDE
    for hh in range(N_KV_A):
        cols = slice(hh * HEAD_DIM, (hh + 1) * HEAD_DIM)
        for g in range(n_rows):
            xs_ref[hh, pl.ds(g * CMP_PITCH, CMP_STRIDE), :] = (
                x_ref[pl.ds(g * CMP_STRIDE, CMP_STRIDE), cols].astype(jnp.float32))
        xs_ref[hh, pl.ds(n_rows * CMP_PITCH, CMP_STRIDE), :] = jnp.zeros(
            (CMP_STRIDE, HEAD_DIM), jnp.float32)

    def block_rows(hh, r):
        start = (r // CMP_STRIDE) * CMP_PITCH + r % CMP_STRIDE
        return xs_ref[hh, pl.ds(start, n_rows, stride=CMP_PITCH), :]

    flat = jnp.concatenate(
        [jnp.concatenate([(block_rows(hh, r) + pe_ref[pl.ds(r, 1), :]).astype(jnp.bfloat16)
                          for r in range(CMP_LEN)], axis=1)
         for hh in range(N_KV_A)], axis=0)
    hid = jax.nn.gelu(jnp.dot(flat, w1_ref[...], preferred_element_type=jnp.float32))
    out = jnp.dot(hid.astype(jnp.bfloat16), w2_ref[...], preferred_element_type=jnp.float32)
    for hh in range(N_KV_A):
        o_ref[hh] = out[hh * n_rows:(hh + 1) * n_rows, :].astype(o_ref.dtype)


def _compress(proj, batch, s, pe, w1, w2):
    n_rows = s // CMP_STRIDE
    assert H_KC % N_KV_A == 0 and H_VC == H_KC + N_KV_A
    return pl.pallas_call(
        _compress_kernel,
        grid=(batch, 2),
        in_specs=[
            pl.BlockSpec((s, N_KV_A * HEAD_DIM), lambda b, kv: (b, H_KC // N_KV_A + kv)),
            pl.BlockSpec((None, CMP_LEN, HEAD_DIM), lambda b, kv: (kv, 0, 0)),
            pl.BlockSpec((None, CMP_LEN * HEAD_DIM, CMP_HIDDEN), lambda b, kv: (kv, 0, 0)),
            pl.BlockSpec((None, CMP_HIDDEN, HEAD_DIM), lambda b, kv: (kv, 0, 0)),
        ],
        out_specs=pl.BlockSpec((None, N_KV_A, n_rows, HEAD_DIM), lambda b, kv: (b, kv, 0, 0)),
        out_shape=jax.ShapeDtypeStruct((batch, 2 * N_KV_A, n_rows, HEAD_DIM), jnp.bfloat16),
        scratch_shapes=[pltpu.VMEM((N_KV_A, (n_rows + 1) * CMP_PITCH, HEAD_DIM), jnp.float32)],
        compiler_params=_cparams(("parallel", "parallel")),
        name="compress",
    )(proj, pe, w1, w2)


def _kq(k, q):
    return lax.dot_general(k, q, (((1,), (1,)), ((), ())),
                           preferred_element_type=jnp.float32)


def _memo(fn):
    cache = []

    def get():
        if not cache:
            cache.append(fn())
        return cache[0]
    return get


class _AttnJob:
    def __init__(self, q_fn, chunks):
        self.q_fn, self.chunks = q_fn, chunks
        self.s, self.bias, self.m, self.acc = [], [], None, None

    def _score(self, c):
        k_fn, _, mask_fn = self.chunks[c][:3]
        s = mask_fn(_kq(k_fn(), self.q_fn()))
        bias = self.chunks[c][3]() if len(self.chunks[c]) > 3 else None
        self.s.append(s)
        self.bias.append(bias)
        if bias is None:
            cm = jnp.max(s, axis=0, keepdims=True)
        else:
            sub = s.shape[0] // len(bias)
            cm = functools.reduce(jnp.maximum, [
                jnp.max(s[a * sub:(a + 1) * sub, :], axis=0, keepdims=True) + b
                for a, b in enumerate(bias)])
        self.m = cm if self.m is None else jnp.maximum(self.m, cm)

    def _probabilities(self, c):
        s, bias = self.s[c], self.bias[c]
        if bias is None:
            return jnp.exp2(s - self.m).astype(jnp.bfloat16)
        sub = s.shape[0] // len(bias)
        return jnp.concatenate([jnp.exp2(s[a * sub:(a + 1) * sub, :] + (b - self.m))
                                for a, b in enumerate(bias)], axis=0).astype(jnp.bfloat16)

    def _value(self, cs):
        p = jnp.concatenate([self._probabilities(c) for c in cs], axis=0)
        vt = jnp.concatenate([self.chunks[c][1]() for c in cs], axis=1)
        part = jnp.dot(vt, p, preferred_element_type=jnp.float32)
        self.acc = part if self.acc is None else self.acc + part

    def score_tasks(self):
        return [functools.partial(self._score, c) for c in range(len(self.chunks))]

    def value_tasks(self):
        n = len(self.chunks)
        return [functools.partial(self._value, range(c, min(c + PV_GROUP, n)))
                for c in range(0, n, PV_GROUP)]

    def result(self):
        return self.acc[:HEAD_DIM, :] * (1.0 / self.acc[HEAD_DIM:HEAD_DIM + 1, :])


def _round_robin(task_lists):
    for i in range(max(len(t) for t in task_lists)):
        for tasks in task_lists:
            if i < len(tasks):
                tasks[i]()


def _run_pipelined(streams, on_done):
    n = len(streams[0])
    _round_robin([jobs[0].score_tasks() for jobs in streams])
    finish = []
    for k in range(n):
        lists = [finish] if finish else []
        for jobs in streams:
            lists.append(jobs[k].value_tasks())
            if k + 1 < n:
                lists.append(jobs[k + 1].score_tasks())
        _round_robin(lists)
        finish = [functools.partial(on_done, si, k, jobs[k]) for si, jobs in enumerate(streams)]
    _round_robin([finish])


def _store_values_transposed(src_ref, dst_ref):
    s_len = src_ref.shape[0]
    for c in range(s_len // TK):
        blk = src_ref[c * TK:(c + 1) * TK, :].astype(jnp.float32)
        dst_ref[0:HEAD_DIM, c * TK:(c + 1) * TK] = blk.T.astype(jnp.bfloat16)
    row = lax.broadcasted_iota(jnp.int32, (BF16_SUBLANES, s_len), 0)
    dst_ref[HEAD_DIM:, :] = jnp.where(row == 0, 1.0, 0.0).astype(jnp.bfloat16)


def _silu(z):
    return z * jax.nn.sigmoid(z)


def _topk_active(q0):
    return q0 + TQ > TOP_N * SEL_LEN


def _nsa_selection_bias(p_sum, q0, s_len):
    n_sel = s_len // SEL_LEN
    n_cmp = s_len // CMP_STRIDE - CMP_LEN // CMP_STRIDE + 1
    oj = lax.broadcasted_iota(jnp.int32, (n_sel, HEAD_DIM), 0)
    oc = lax.broadcasted_iota(jnp.int32, (n_sel, HEAD_DIM), 1)
    ovl = jnp.where(oc * CMP_STRIDE < oj * SEL_LEN + SEL_LEN,
                    jnp.where(oc * CMP_STRIDE + CMP_LEN > oj * SEL_LEN, 1.0, 0.0), 0.0)
    ovl = jnp.where(oc < n_cmp, ovl, 0.0).astype(jnp.bfloat16)
    p_hi = p_sum.astype(jnp.bfloat16)
    p_lo = (p_sum - p_hi.astype(jnp.float32)).astype(jnp.bfloat16)
    imp = (jnp.dot(ovl, p_hi, preferred_element_type=jnp.float32)
           + jnp.dot(ovl, p_lo, preferred_element_type=jnp.float32))
    j_row = lax.broadcasted_iota(jnp.int32, (n_sel, TQ), 0)
    t_sel = q0 + lax.broadcasted_iota(jnp.int32, (n_sel, TQ), 1)
    cur = lax.shift_right_arithmetic(t_sel, int(math.log2(SEL_LEN)))
    forced = (j_row == 0) | (j_row == cur) | (j_row == cur - 1)
    imp = jnp.where(forced, imp + FORCE_BONUS, imp)
    imp = jnp.where(j_row * SEL_LEN <= t_sel, imp, -jnp.inf)
    rank = jnp.zeros((n_sel, TQ), jnp.float32)
    for i in range(n_sel):
        vi = imp[i:i + 1, :]
        rank = rank + jnp.where(j_row > i, jnp.where(vi >= imp, 1.0, 0.0),
                                jnp.where(vi > imp, 1.0, 0.0))
    return jnp.where(rank < TOP_N, 0.0, NEG)


def _mixers_kernel(q_ref, z_ref, g_ref, bg_ref, kc_ref, vc_ref, ksl_ref, vsl_ref, kw_ref, vw_ref,
                   cos_ref, sin_ref, qd_ref, zd_ref, kd_ref, vd_ref, dbias_ref, o_ref, od_ref,
                   ksl_rot, kw_rot, vsl_t, vw_t, ocmp_s, selb_s, gate_s):
    s_len = ksl_ref.shape[0]
    n_q = s_len // TQ

    ksl_rot[...] = _rope(ksl_ref[...].astype(jnp.float32), cos_ref[...], sin_ref[...]).astype(jnp.bfloat16)
    kw_rot[...] = _rope(kw_ref[...].astype(jnp.float32), cos_ref[...], sin_ref[...]).astype(jnp.bfloat16)
    _store_values_transposed(vsl_ref, vsl_t)
    _store_values_transposed(vw_ref, vw_t)
    kc = (kc_ref[...].astype(jnp.float32) * QSCALE).astype(jnp.bfloat16)
    vc_t = vc_ref[...].astype(jnp.float32).T.astype(jnp.bfloat16)
    bg = bg_ref[...]
    kv_head = pl.program_id(1)

    c_row = lax.broadcasted_iota(jnp.int32, (HEAD_DIM, TQ), 0)
    lane_q = lax.broadcasted_iota(jnp.int32, (HEAD_DIM, TQ), 1)
    lane_one = lax.broadcasted_iota(jnp.int32, (1, TQ), 1)
    for qi in range(n_q):
        q0 = qi * TQ
        rows = slice(q0, q0 + TQ)
        n_vis = min(HEAD_DIM, (q0 + TQ) // CMP_STRIDE)
        hidden = jnp.zeros((HEAD_DIM - n_vis, TQ), jnp.float32)
        c_valid = (c_row * CMP_STRIDE + (CMP_LEN - 1) <= q0 + lane_q)[:n_vis, :]
        any_valid = jnp.where(q0 + lane_one >= CMP_LEN - 1, 1.0, 0.0)
        q4 = jnp.concatenate([q_ref[rows, g * HEAD_DIM:(g + 1) * HEAD_DIM]
                              for g in range(GROUP_A)], axis=0)
        sc4 = _kq(kc[:n_vis, :], q4)
        p_heads = []
        p_sum = jnp.zeros((HEAD_DIM, TQ), jnp.float32)
        for g in range(GROUP_A):
            sc = jnp.where(c_valid, sc4[:, g * TQ:(g + 1) * TQ], NEG)
            e = jnp.exp2(sc - jnp.max(sc, axis=0, keepdims=True))
            p = e * (any_valid / jnp.sum(e, axis=0, keepdims=True))
            if n_vis < HEAD_DIM:
                p = jnp.concatenate([p, hidden], axis=0)
            p_heads.append(p.astype(jnp.bfloat16))
            p_sum = p_sum + p
        o_cmp4 = jnp.dot(vc_t, jnp.concatenate(p_heads, axis=1),
                         preferred_element_type=jnp.float32)
        for g in range(GROUP_A):
            ocmp_s[g, :, rows] = o_cmp4[:, g * TQ:(g + 1) * TQ]
        if _topk_active(q0):
            selb_s[qi] = _nsa_selection_bias(p_sum, q0, s_len)
        gate_s[:, rows] = jax.nn.sigmoid(g_ref[rows, :] + bg).T

    diag_rc = (lax.broadcasted_iota(jnp.int32, (TK, TQ), 0)
               - lax.broadcasted_iota(jnp.int32, (TK, TQ), 1))

    def slc_mask(qi, c, s):
        if c == qi:
            s = jnp.where(diag_rc <= 0, s, NEG)
        return s

    def slc_row_bias(qi, c):
        return [selb_s[qi, c * SEL_PER_TK + a:c * SEL_PER_TK + a + 1, :] for a in range(SEL_PER_TK)]

    def win_mask(qi, c, s):
        if c == qi:
            return jnp.where(diag_rc <= 0, s, NEG)
        if (qi - c) * TK + TQ - 1 > WIN - 1:
            return jnp.where(diag_rc >= (qi - c) * TK - (WIN - 1), s, NEG)
        return s

    ones_rows = jnp.where(lax.broadcasted_iota(jnp.int32, (BF16_SUBLANES, TK), 0) == 0,
                          1.0, 0.0).astype(jnp.bfloat16)

    def head_body(g, carry):
        col = pl.ds(pl.multiple_of(g * HEAD_DIM, HEAD_DIM), HEAD_DIM)
        gate_row = GATE_PER_KV * kv_head + 3 * g
        blocks = [slice(c * TK, (c + 1) * TK) for c in range(n_q)]

        def roped(ref, rows, scale):
            xf = ref[rows, col].astype(jnp.float32)
            if scale != 1.0:
                xf = xf * scale
            return _rope(xf, cos_ref[rows, :], sin_ref[rows, :]).astype(jnp.bfloat16)

        def dil_value_t(rows):
            vt = vd_ref[rows, col].astype(jnp.float32).T.astype(jnp.bfloat16)
            return jnp.concatenate([vt, ones_rows], axis=0)

        def dil_bias(delta, s):
            r0 = (n_q - 1 - delta) * TK
            return s + dbias_ref[r0:r0 + TK, :]

        dil_k = [_memo(functools.partial(roped, kd_ref, rows, 1.0)) for rows in blocks]
        dil_v = [_memo(functools.partial(dil_value_t, rows)) for rows in blocks]
        slc_jobs, win_jobs, dil_jobs = [], [], []
        for qi in range(n_q):
            q = _memo(functools.partial(roped, q_ref, blocks[qi], QSCALE))
            first_win = max(0, (qi * TQ - (WIN - 1)) // TK)
            ranked = _topk_active(qi * TQ)
            for jobs, k_ref_, v_ref_, mask, row_bias, c_lo in (
                    (slc_jobs, ksl_rot, vsl_t, slc_mask, slc_row_bias if ranked else None, 0),
                    (win_jobs, kw_rot, vw_t, win_mask, None, first_win)):
                jobs.append(_AttnJob(q, [
                    (functools.partial(lambda r, c: r[c * TK:(c + 1) * TK, :], k_ref_, c),
                     functools.partial(lambda r, c: r[:, c * TK:(c + 1) * TK], v_ref_, c),
                     functools.partial(mask, qi, c))
                    + ((functools.partial(row_bias, qi, c),) if row_bias else ())
                    for c in range(c_lo, qi + 1)]))
            dil_jobs.append(_AttnJob(
                _memo(functools.partial(roped, qd_ref, blocks[qi], QSCALE)),
                [(dil_k[c], dil_v[c], functools.partial(dil_bias, qi - c)) for c in range(qi + 1)]))

        def on_done(si, k, job):
            kind, qi = si % 3, MIX_SPLIT * k + si // 3
            rows = blocks[qi]
            if kind == 1:
                o_t = (gate_s[pl.ds(gate_row, 1), rows] * ocmp_s[g, :, rows]
                       + gate_s[pl.ds(gate_row + 1, 1), rows] * slc_jobs[qi].result()
                       + gate_s[pl.ds(gate_row + 2, 1), rows] * job.result())
                z = z_ref[rows, col].astype(jnp.float32)
                o_ref[rows, col] = (o_t.T * _silu(z)).astype(o_ref.dtype)
            elif kind == 2:
                z = zd_ref[rows, col].astype(jnp.float32)
                od_ref[rows, col] = (job.result().T * _silu(z)).astype(od_ref.dtype)

        streams = []
        for r in range(MIX_SPLIT):
            streams += [slc_jobs[r::MIX_SPLIT], win_jobs[r::MIX_SPLIT], dil_jobs[r::MIX_SPLIT]]
        _run_pipelined(streams, on_done)
        return carry

    lax.fori_loop(0, GROUP_A, head_body, 0)


def _mixers(front, back, gates, cmp_kv, b_gate_row, cos, sin, dil_bias, batch, s):
    gw = GROUP_A * HEAD_DIM
    assert N_HEADS_B == N_HEADS_A
    full = lambda col0: pl.BlockSpec((s, HEAD_DIM), lambda b, h: (b, col0 + h))
    group = lambda head0: pl.BlockSpec((s, gw), lambda b, h: (b, head0 // GROUP_A + h))
    out = jax.ShapeDtypeStruct((batch * s, N_HEADS_A * HEAD_DIM), jnp.bfloat16)
    rarely_changes = dict(pipeline_mode=pl.Buffered(1))
    return pl.pallas_call(
        _mixers_kernel,
        grid=(batch, N_KV_A),
        in_specs=[
            group(H_QA), group(H_ZA),
            pl.BlockSpec((s, HEAD_DIM), lambda b, h: (b, 0)),
            pl.BlockSpec((1, HEAD_DIM), lambda b, h: (0, 0)),
            pl.BlockSpec((None, None, s // CMP_STRIDE, HEAD_DIM), lambda b, h: (b, h, 0, 0)),
            pl.BlockSpec((None, None, s // CMP_STRIDE, HEAD_DIM), lambda b, h: (b, N_KV_A + h, 0, 0)),
            full(H_KSL), full(H_VSL), full(H_KW), full(H_VW),
            pl.BlockSpec((None, s, HEAD_DIM), lambda b, h: (b, 0, 0), **rarely_changes),
            pl.BlockSpec((None, s, HEAD_DIM), lambda b, h: (b, 0, 0), **rarely_changes),
            group(H_QB), group(H_ZB), group(H_KB), group(H_VB),
            pl.BlockSpec((s, TQ), lambda b, h: (0, 0), **rarely_changes),
        ],
        out_specs=[pl.BlockSpec((s, gw), lambda b, h: (b, h)),
                   pl.BlockSpec((s, gw), lambda b, h: (b, h))],
        out_shape=[out, out],
        scratch_shapes=[pltpu.VMEM((s, HEAD_DIM), jnp.bfloat16),
                        pltpu.VMEM((s, HEAD_DIM), jnp.bfloat16),
                        pltpu.VMEM((V_ROWS, s), jnp.bfloat16),
                        pltpu.VMEM((V_ROWS, s), jnp.bfloat16),
                        pltpu.VMEM((GROUP_A, HEAD_DIM, s), jnp.float32),
                        pltpu.VMEM((s // TQ, s // SEL_LEN, TQ), jnp.float32),
                        pltpu.VMEM((HEAD_DIM, s), jnp.float32)],
        compiler_params=pltpu.CompilerParams(dimension_semantics=("parallel", "parallel"),
                                             vmem_limit_bytes=MIXERS_VMEM_LIMIT),
        name="token_mixers",
    )(front, front, gates, b_gate_row, cmp_kv, cmp_kv, back, back, back, back, cos, sin,
      back, back, back, back, dil_bias)


def _dilated_multiplicity(dist):
    cnt = jnp.zeros(dist.shape, jnp.float32)
    for window, dil in DILATIONS:
        cnt = cnt + jnp.where((dist & (dil - 1)) == 0,
                              jnp.where(dist <= (window // dil) * dil, 1.0, 0.0), 0.0)
    return jnp.where(dist >= 0, cnt, 0.0)


def _dilated_table_kernel(bias_ref):
    s_len = bias_ref.shape[0]
    dist = (s_len - TQ + lax.broadcasted_iota(jnp.int32, (s_len, TQ), 1)
            - lax.broadcasted_iota(jnp.int32, (s_len, TQ), 0))
    cnt = _dilated_multiplicity(dist)
    bias_ref[...] = jnp.where(cnt > 0.5, jnp.log2(jnp.maximum(cnt, 1.0)), NEG)


def _dilated_tables(s):
    return pl.pallas_call(
        _dilated_table_kernel,
        out_shape=jax.ShapeDtypeStruct((s, TQ), jnp.float32),
        compiler_params=pltpu.CompilerParams(vmem_limit_bytes=VMEM_LIMIT),
        name="dilated_tables",
    )()


OUT_PANELS = 8


def _out_proj_kernel(ma_ref, mb_ref, wa_ref, wb_ref, x_ref, nw_ref, o_ref, y_s, ssq_s):
    i = pl.program_id(0)
    n_tiles = pl.num_programs(0) - 1
    d = o_ref.shape[1]
    pn = d // OUT_PANELS

    def step(multiply, finish):
        if finish:
            scale = lax.rsqrt(ssq_s[...] * (1.0 / d) + EPS)
        ssq = None
        for p in range(OUT_PANELS):
            cols = slice(p * pn, (p + 1) * pn)
            if finish:
                o_ref[:, cols] = x_ref[:, cols] + y_s[:, cols] * scale * nw_ref[:, cols]
            if multiply:
                y = (jnp.dot(ma_ref[...], wa_ref[:, cols], preferred_element_type=jnp.float32)
                     + jnp.dot(mb_ref[...], wb_ref[:, cols], preferred_element_type=jnp.float32))
                y_s[:, cols] = y
                part = jnp.sum(y * y, axis=-1, keepdims=True)
                ssq = part if ssq is None else ssq + part
        if multiply:
            ssq_s[...] = ssq

    pl.when(i == 0)(functools.partial(step, True, False))
    pl.when((i > 0) & (i < n_tiles))(functools.partial(step, True, True))
    pl.when(i == n_tiles)(functools.partial(step, False, True))


def _out_proj(mix_a, mix_b, w, x2d, post_w, tm=256):
    m, d = x2d.shape
    ka, kb = mix_a.shape[1], mix_b.shape[1]
    assert ka == kb and w.shape[0] == ka + kb
    n_tiles = m // tm
    resident = dict(pipeline_mode=pl.Buffered(1))
    this_tile = lambda i: (jnp.minimum(i, n_tiles - 1), 0)
    prev_tile = lambda i: (jnp.maximum(i - 1, 0), 0)
    return pl.pallas_call(
        _out_proj_kernel,
        grid=(n_tiles + 1,),
        in_specs=[
            pl.BlockSpec((tm, ka), this_tile),
            pl.BlockSpec((tm, kb), this_tile),
            pl.BlockSpec((ka, d), lambda i: (0, 0), **resident),
            pl.BlockSpec((kb, d), lambda i: (1, 0), **resident),
            pl.BlockSpec((tm, d), prev_tile),
            pl.BlockSpec((1, d), lambda i: (0, 0)),
        ],
        out_specs=pl.BlockSpec((tm, d), prev_tile),
        out_shape=jax.ShapeDtypeStruct((m, d), jnp.float32),
        scratch_shapes=[pltpu.VMEM((tm, d), jnp.float32),
                        pltpu.VMEM((tm, 1), jnp.float32)],
        compiler_params=pltpu.CompilerParams(dimension_semantics=("arbitrary",),
                                             vmem_limit_bytes=OUT_PROJ_VMEM_LIMIT),
        name="out_proj_norm_residual",
    )(mix_a, mix_b, w, w, x2d, post_w.reshape(1, d))


def _layer(x, positions, rope, pre_w, post_w, w_in, b_gate, pe_k, pe_v, wk1, wk2, wv1, wv2, w_out):
    b, s, d = x.shape
    x2d = x.reshape(b * s, d)
    w_t = w_in.T
    if rope is None:
        h, gates, cos, sin = _rmsnorm(x2d, pre_w, w_t, positions.reshape(b * s))
        rope = (cos.reshape(b, s, HEAD_DIM), sin.reshape(b, s, HEAD_DIM))
    else:
        h, gates = _rmsnorm(x2d, pre_w, w_t)
    cos, sin = rope
    front, back, w_out_bf16 = _in_proj(h, w_t, w_out)

    pe = jnp.stack([pe_k, pe_v])
    w1 = jnp.stack([wk1, wv1]).astype(jnp.bfloat16)
    w2 = jnp.stack([wk2, wv2]).astype(jnp.bfloat16)
    cmp_kv = _compress(back, b, s, pe, w1, w2)

    bg = jnp.pad(b_gate.reshape(1, -1), ((0, 0), (0, HEAD_DIM - b_gate.size)))
    mix_a, mix_b = _mixers(front, back, gates, cmp_kv, bg, cos, sin, _dilated_tables(s), b, s)

    out = _out_proj(mix_a, mix_b, w_out_bf16, x2d, post_w)
    return out.reshape(b, s, d), rope


def kernel(x, positions, pre_norm_w, post_norm_w, w_in, b_gate, cmp_pe_k, cmp_pe_v,
           cmp_wk1, cmp_wk2, cmp_wv1, cmp_wv2, w_out):
    rope = None
    for l in range(pre_norm_w.shape[0]):
        x, rope = _layer(x, positions, rope, pre_norm_w[l], post_norm_w[l], w_in[l], b_gate[l],
                         cmp_pe_k[l], cmp_pe_v[l], cmp_wk1[l], cmp_wk2[l], cmp_wv1[l],
                         cmp_wv2[l], w_out[l])
    return x
```
